```python
import jax, jax.numpy as jnp
from jax import lax
import numpy as np


D_MODEL = 1024
BATCH = 8
SEQ = 4096
DEPTH = 1

ATTN_HEAD_DIM = 64
ATTN_WIDTH = D_MODEL // 2
ATTN_HEADS = ATTN_WIDTH // ATTN_HEAD_DIM
DILATION_PATTERNS = ((128, 1), (512, 4), (2048, 16))
ATTN_BLOCK = 128
REL_BUCKETS = 32
REL_MAX_DISTANCE = 2048

HGRN_HEAD_DIM = 128
HGRN_WIDTH = D_MODEL - ATTN_WIDTH
HGRN_HEADS = HGRN_WIDTH // HGRN_HEAD_DIM
HGRN_CHUNK = 64

MIX_WIDTH = ATTN_WIDTH + HGRN_WIDTH
IN_SPLITS = (ATTN_WIDTH, 2 * ATTN_WIDTH, 3 * ATTN_WIDTH,
             3 * ATTN_WIDTH + HGRN_WIDTH, 3 * ATTN_WIDTH + 2 * HGRN_WIDTH,
             3 * ATTN_WIDTH + 3 * HGRN_WIDTH)
IN_PROJ_WIDTH = 3 * ATTN_WIDTH + 4 * HGRN_WIDTH

N_EXPERTS = 32
TOP_K = 4
D_EXPERT = D_MODEL
SWIGLU_LIMIT = 7.0
SWIGLU_ALPHA = 1.702
EXPERT_BLOCK = 128
RMS_EPS = 1e-5

kernel_name = 'hybrid_dilated_attn_hgrn2_moe_block'


def rms_norm(x, w):
    xf = x.astype(jnp.float32)
    y = xf * lax.rsqrt(jnp.mean(xf * xf, axis=-1, keepdims=True) + RMS_EPS)
    return (y * w.astype(jnp.float32)).astype(x.dtype)


def t5_causal_bucket(dist):
    n = np.maximum(dist, 0)
    max_exact = REL_BUCKETS // 2
    large = max_exact + (np.log(np.maximum(n, 1) / max_exact)
                         / np.log(REL_MAX_DISTANCE / max_exact)
                         * (REL_BUCKETS - max_exact)).astype(np.int32)
    large = np.minimum(large, REL_BUCKETS - 1)
    return np.where(n < max_exact, n, large).astype(np.int32)


def dilated_window_attention(q, k, v, rel_bias, window, dilation):
    b, h, s, hd = q.shape
    blk = ATTN_BLOCK
    span = window // dilation
    unit = blk * dilation
    s_pad = -(-s // unit) * unit
    n_sub = s_pad // dilation
    nb = n_sub // blk

    def to_blocks(t):
        t = jnp.pad(t, ((0, 0), (0, 0), (0, s_pad - s), (0, 0)))
        t = t.reshape(b, h, n_sub, dilation, hd).transpose(0, 1, 3, 2, 4)
        return t.reshape(b, h, dilation, nb, blk, hd)

    def with_prev(t):
        prev = jnp.pad(t[:, :, :, :-1], ((0, 0), (0, 0), (0, 0), (1, 0), (0, 0), (0, 0)))
        return jnp.concatenate([prev, t], axis=-2)

    qb = to_blocks(q)
    kk = with_prev(to_blocks(k))
    vv = with_prev(to_blocks(v))

    qi = np.arange(blk)[:, None]
    kj = np.arange(2 * blk)[None, :]
    dist_sub = blk + qi - kj
    band = (dist_sub >= 0) & (dist_sub <= span)
    valid = band[None] & ((np.arange(nb)[:, None, None] > 0) | (kj[None] >= blk))
    buckets = t5_causal_bucket(dist_sub * dilation)
    bias = jnp.transpose(rel_bias[buckets], (2, 0, 1)).astype(jnp.float32)

    scores = jnp.einsum('bhrnqe,bhrnke->bhrnqk', qb, kk).astype(jnp.float32) * (hd ** -0.5)
    scores = scores + bias[None, :, None, None]
    scores = jnp.where(valid, scores, -1e30)
    m = jnp.max(scores, axis=-1, keepdims=True)
    p = jnp.exp(scores - m)
    z = jnp.sum(p, axis=-1, keepdims=True)
    out = jnp.einsum('bhrnqk,bhrnke->bhrnqe', p, vv.astype(jnp.float32)) / z
    lse = (m + jnp.log(z))[..., 0]

    out = out.reshape(b, h, dilation, n_sub, hd).transpose(0, 1, 3, 2, 4).reshape(b, h, s_pad, hd)[:, :, :s]
    lse = lse.reshape(b, h, dilation, n_sub).transpose(0, 1, 3, 2).reshape(b, h, s_pad)[:, :, :s]
    return out, lse


def hgrn2_chunkwise(q, f_logit, i, lower_bound):
    b, s, h, dk = q.shape
    dv = i.shape[-1]
    c = HGRN_CHUNK
    nc = s // c
    lb = lower_bound.astype(jnp.float32)
    f = lb + (1.0 - lb) * jax.nn.sigmoid(f_logit.astype(jnp.float32))
    log_f = jnp.log(f)
    key = 1.0 - f
    qf = jax.nn.silu(q.astype(jnp.float32)) * (dk ** -0.5)

    def chunks(t):
        return t.astype(jnp.float32).reshape(b, nc, c, h, t.shape[-1]).transpose(1, 0, 3, 2, 4)

    causal = jnp.tril(jnp.ones((c, c), dtype=bool))

    def step(state, inp):
        qc, kc, vc, gc = inp
        cum = jnp.cumsum(gc, axis=-2)
        o_inter = jnp.einsum('bhtk,bhkv->bhtv', qc * jnp.exp(cum), state)
        diff = cum[:, :, :, None, :] - cum[:, :, None, :, :]
        decay = jnp.exp(jnp.where(causal[:, :, None], diff, -jnp.inf))
        attn = jnp.einsum('bhtk,bhsk,bhtsk->bhts', qc, kc, decay)
        o_intra = jnp.einsum('bhts,bhsv->bhtv', attn, vc)
        last = cum[:, :, -1:, :]
        state = (jnp.exp(last[:, :, 0, :])[..., None] * state
                 + jnp.einsum('bhsk,bhsv->bhkv', kc * jnp.exp(last - cum), vc))
        return state, o_inter + o_intra

    s0 = jnp.zeros((b, h, dk, dv), jnp.float32)
    _, o = lax.scan(step, s0, (chunks(qf), chunks(key), chunks(i), chunks(log_f)))
    return o.transpose(1, 0, 3, 2, 4).reshape(b, s, h, dv)


def token_mixer(xn, w_in_l, rel_bias, lower_bound_l, hgrn_norm_w_l, w_out_l):
    b, s, _ = xn.shape
    proj = xn @ w_in_l
    q_a, k_a, v_a, q_h, f_h, i_h, g_h = jnp.split(proj, IN_SPLITS, axis=-1)

    def attn_heads(t):
        return t.reshape(b, s, ATTN_HEADS, ATTN_HEAD_DIM).transpose(0, 2, 1, 3)

    qa, ka, va = attn_heads(q_a), attn_heads(k_a), attn_heads(v_a)
    outs, lses = [], []
    for window, dilation in DILATION_PATTERNS:
        o, l = dilated_window_attention(qa, ka, va, rel_bias, window, dilation)
        outs.append(o)
        lses.append(l)
    weights = jax.nn.softmax(jnp.stack(lses), axis=0)
    attn = jnp.einsum('pbhs,pbhse->bshe', weights, jnp.stack(outs)).reshape(b, s, ATTN_WIDTH)

    def hgrn_heads(t):
        return t.reshape(b, s, HGRN_HEADS, HGRN_HEAD_DIM)

    hg = hgrn2_chunkwise(hgrn_heads(q_h), hgrn_heads(f_h), hgrn_heads(i_h),
                         lower_bound_l.reshape(HGRN_HEADS, HGRN_HEAD_DIM))
    hg = rms_norm(hg, hgrn_norm_w_l) * jax.nn.silu(hgrn_heads(g_h).astype(jnp.float32))

    mix = jnp.concatenate([attn.astype(xn.dtype), hg.reshape(b, s, HGRN_WIDTH).astype(xn.dtype)], axis=-1)
    return (mix @ w_out_l).astype(xn.dtype)


def expert_ffn(xn, router_w_l, router_b_l, w_gu_l, b_gu_l, w_dn_l, b_dn_l):
    b, s, d = xn.shape
    tokens = xn.reshape(b * s, d)
    n = b * s
    nk = n * TOP_K
    logits = (tokens @ router_w_l).astype(jnp.float32) + router_b_l.astype(jnp.float32)
    top_vals, top_idx = lax.top_k(logits, TOP_K)
    gates = jax.nn.softmax(top_vals, axis=-1)

    flat_e = top_idx.reshape(-1)
    flat_tok = jnp.arange(nk, dtype=jnp.int32) // TOP_K
    flat_g = gates.reshape(-1)
    order = jnp.argsort(flat_e)
    sorted_e = flat_e[order]
    counts = jnp.zeros((N_EXPERTS,), jnp.int32).at[flat_e].add(1)
    padded = (counts + EXPERT_BLOCK - 1) // EXPERT_BLOCK * EXPERT_BLOCK
    start = jnp.cumsum(counts) - counts
    pend = jnp.cumsum(padded)
    pstart = pend - padded
    dest = pstart[sorted_e] + (jnp.arange(nk, dtype=jnp.int32) - start[sorted_e])

    n_blocks = -(-nk // EXPERT_BLOCK) + N_EXPERTS
    cap = n_blocks * EXPERT_BLOCK
    tok_buf = jnp.full((cap,), n, jnp.int32).at[dest].set(flat_tok[order])
    gate_buf = jnp.zeros((cap,), jnp.float32).at[dest].set(flat_g[order])
    block_e = jnp.clip(jnp.searchsorted(pend, jnp.arange(n_blocks, dtype=jnp.int32) * EXPERT_BLOCK,
                                        side='right'), 0, N_EXPERTS - 1)
    tokens_pad = jnp.concatenate([tokens, jnp.zeros((1, d), tokens.dtype)], axis=0)

    def expert_block(args):
        ids, e = args
        xb = tokens_pad[ids]
        hgu = xb @ w_gu_l[e] + b_gu_l[e]
        gate = jnp.minimum(hgu[:, :D_EXPERT], SWIGLU_LIMIT)
        up = jnp.clip(hgu[:, D_EXPERT:], -SWIGLU_LIMIT, SWIGLU_LIMIT)
        act = gate * jax.nn.sigmoid(SWIGLU_ALPHA * gate) * (up + 1.0)
        return act @ w_dn_l[e] + b_dn_l[e]

    out_blocks = lax.map(expert_block, (tok_buf.reshape(n_blocks, EXPERT_BLOCK), block_e))
    y = jnp.zeros((n + 1, d), jnp.float32).at[tok_buf].add(
        out_blocks.reshape(cap, d).astype(jnp.float32) * gate_buf[:, None])
    return y[:n].reshape(b, s, d).astype(xn.dtype)


def setup_inputs(seed: int = 0) -> dict:
    key = jax.random.key(seed)
    ks = jax.random.split(key, 16)
    f32 = jnp.float32

    def nrm(k, shape, scale):
        return jax.random.normal(k, shape, f32) * scale

    return {
        'x': nrm(ks[0], (BATCH, SEQ, D_MODEL), 1.0),
        'norm_mix_w': 1.0 + nrm(ks[1], (DEPTH, D_MODEL), 0.02),
        'w_in': nrm(ks[2], (DEPTH, D_MODEL, IN_PROJ_WIDTH), D_MODEL ** -0.5),
        'rel_bias': nrm(ks[3], (REL_BUCKETS, ATTN_HEADS), 0.5),
        'hgrn_lb_logits': nrm(ks[4], (DEPTH + 1, HGRN_WIDTH), 1.0),
        'hgrn_norm_w': 1.0 + nrm(ks[5], (DEPTH, HGRN_HEAD_DIM), 0.02),
        'w_out': nrm(ks[6], (DEPTH, MIX_WIDTH, D_MODEL), MIX_WIDTH ** -0.5),
        'norm_ffn_w': 1.0 + nrm(ks[7], (DEPTH, D_MODEL), 0.02),
        'router_w': nrm(ks[8], (DEPTH, D_MODEL, N_EXPERTS), D_MODEL ** -0.5),
        'router_b': nrm(ks[9], (DEPTH, N_EXPERTS), 0.01),
        'w_gate_up': nrm(ks[10], (DEPTH, N_EXPERTS, D_MODEL, 2 * D_EXPERT), D_MODEL ** -0.5),
        'b_gate_up': nrm(ks[11], (DEPTH, N_EXPERTS, 2 * D_EXPERT), 0.01),
        'w_down': nrm(ks[12], (DEPTH, N_EXPERTS, D_EXPERT, D_MODEL), D_EXPERT ** -0.5),
        'b_down': nrm(ks[13], (DEPTH, N_EXPERTS, D_MODEL), 0.01),
        'final_norm_w': 1.0 + nrm(ks[14], (D_MODEL,), 0.02),
    }


def reference(x, norm_mix_w, w_in, rel_bias, hgrn_lb_logits, hgrn_norm_w, w_out, norm_ffn_w,
              router_w, router_b, w_gate_up, b_gate_up, w_down, b_down, final_norm_w):
    lower_bounds = jnp.cumsum(jax.nn.softmax(hgrn_lb_logits.astype(jnp.float32), axis=0), axis=0)
    h = x
    for layer in range(DEPTH):
        h = h + token_mixer(rms_norm(h, norm_mix_w[layer]), w_in[layer], rel_bias,
                            lower_bounds[layer], hgrn_norm_w[layer], w_out[layer])
        h = h + expert_ffn(rms_norm(h, norm_ffn_w[layer]), router_w[layer], router_b[layer],
                           w_gate_up[layer], b_gate_up[layer], w_down[layer], b_down[layer])
    return rms_norm(h, final_norm_w)
```

```python
import functools

import numpy as np
import jax
import jax.numpy as jnp
from jax import lax
from jax.experimental import pallas as pl
from jax.experimental.pallas import tpu as pltpu

F32 = jnp.float32
BF16 = jnp.bfloat16
HIGHEST = lax.Precision.HIGHEST

D_MODEL = 1024
ATTN_HEAD_DIM = 64
ATTN_WIDTH = 512
ATTN_HEADS = ATTN_WIDTH // ATTN_HEAD_DIM
DILATION_PATTERNS = ((128, 1), (512, 4), (2048, 16))
ATTN_BLOCK = 128
REL_BUCKETS = 32
REL_MAX_DISTANCE = 2048
HGRN_HEAD_DIM = 128
HGRN_WIDTH = 512
HGRN_HEADS = HGRN_WIDTH // HGRN_HEAD_DIM
HGRN_CHUNK = 64
HGRN_SUB = 8
N_EXPERTS = 32
TOP_K = 4
D_EXPERT = 1024
SWIGLU_LIMIT = 7.0
SWIGLU_ALPHA = 1.702
RMS_EPS = 1e-5
IN_PROJ_WIDTH = 3 * ATTN_WIDTH + 4 * HGRN_WIDTH
LANES = 128
HEAD_PAIRS = ATTN_WIDTH // LANES
QKV_COL_BLOCKS = 3 * ATTN_WIDTH // LANES

INPROJ_ROWS = 512
OUTPROJ_ROWS = 256
FFN_ROWS = 256
COMBINE_ROWS = 256
VMEM_LIMIT = 48 * 1024 * 1024

NT_DIMS = (((1,), (1,)), ((), ()))
TN_DIMS = (((0,), (0,)), ((), ()))


def _cparams(*sem):
    return pltpu.CompilerParams(dimension_semantics=sem, vmem_limit_bytes=VMEM_LIMIT)


def _inproj_kernel(x_ref, nw_ref, w_ref, qkv_ref, qh_ref, fh_ref, ih_ref, gh_ref):
    x = x_ref[...]
    xn = x * lax.rsqrt(jnp.mean(x * x, axis=-1, keepdims=True) + RMS_EPS) * nw_ref[...]
    xn = xn.astype(BF16)

    def proj(c0, width):
        return jnp.dot(xn, w_ref[:, c0:c0 + width], preferred_element_type=F32)

    for c in range(3):
        qkv_ref[:, c * ATTN_WIDTH:(c + 1) * ATTN_WIDTH] = proj(c * ATTN_WIDTH, ATTN_WIDTH).astype(BF16)
    base = 3 * ATTN_WIDTH
    qh_ref[...] = proj(base, HGRN_WIDTH).astype(BF16)
    fh_ref[...] = proj(base + HGRN_WIDTH, HGRN_WIDTH)
    ih_ref[...] = proj(base + 2 * HGRN_WIDTH, HGRN_WIDTH).astype(BF16)
    gh_ref[...] = proj(base + 3 * HGRN_WIDTH, HGRN_WIDTH)


def _inproj(x2, norm_w, w_in):
    n = x2.shape[0]
    tm = INPROJ_ROWS
    row = lambda width: pl.BlockSpec((tm, width), lambda i: (i, 0))
    return pl.pallas_call(
        _inproj_kernel,
        grid=(n // tm,),
        in_specs=[row(D_MODEL),
                  pl.BlockSpec((1, D_MODEL), lambda i: (0, 0)),
                  pl.BlockSpec((D_MODEL, IN_PROJ_WIDTH), lambda i: (0, 0))],
        out_specs=[row(3 * ATTN_WIDTH), row(HGRN_WIDTH), row(HGRN_WIDTH), row(HGRN_WIDTH), row(HGRN_WIDTH)],
        out_shape=[jax.ShapeDtypeStruct((n, 3 * ATTN_WIDTH), BF16),
                   jax.ShapeDtypeStruct((n, HGRN_WIDTH), BF16),
                   jax.ShapeDtypeStruct((n, HGRN_WIDTH), F32),
                   jax.ShapeDtypeStruct((n, HGRN_WIDTH), BF16),
                   jax.ShapeDtypeStruct((n, HGRN_WIDTH), F32)],
        compiler_params=_cparams("arbitrary"),
        name="inproj",
    )(x2, norm_w.reshape(1, D_MODEL), w_in.astype(BF16))


def _t5_causal_bucket(dist):
    n = np.maximum(dist, 0)
    max_exact = REL_BUCKETS // 2
    large = max_exact + (np.log(np.maximum(n, 1) / max_exact)
                         / np.log(REL_MAX_DISTANCE / max_exact)
                         * (REL_BUCKETS - max_exact)).astype(np.int32)
    large = np.minimum(large, REL_BUCKETS - 1)
    return np.where(n < max_exact, n, large).astype(np.int32)


def _masked_bias(rel_bias, window, dilation):
    blk = ATTN_BLOCK
    qi = np.arange(blk)[:, None]
    kj = np.arange(2 * blk)[None, :]
    dist_sub = blk + qi - kj
    band = (dist_sub >= 0) & (dist_sub <= window // dilation)
    buckets = _t5_causal_bucket(dist_sub * dilation)
    bias = jnp.transpose(rel_bias[buckets], (2, 0, 1)).astype(F32)
    bias = jnp.where(band[None], bias, -1e30)
    return bias.reshape(HEAD_PAIRS, 2, blk, 2 * blk)


def _attn_kernel(q_ref, k_ref, v_ref, bias_ref, o_ref, l_ref, *, nb):
    blk = ATTN_BLOCK
    lane = lax.broadcasted_iota(jnp.int32, (1, LANES), 1)
    first = lane < ATTN_HEAD_DIM
    scale = ATTN_HEAD_DIM ** -0.5
    qsel = (jnp.where(first, scale, 0.0).astype(BF16), jnp.where(first, 0.0, scale).astype(BF16))

    def one_block(qb, kk, vv, bias_of):
        outs, lses = [], []
        for hh in range(2):
            s = lax.dot_general(qb * qsel[hh], kk, NT_DIMS, preferred_element_type=F32) + bias_of(hh)
            m = jnp.max(s, axis=-1, keepdims=True)
            p = jnp.exp(s - m)
            z = jnp.sum(p, axis=-1, keepdims=True)
            o = jnp.dot(p.astype(BF16), vv, preferred_element_type=F32)
            outs.append(o / z)
            lses.append(m + jnp.log(z))
        return jnp.where(first, outs[0], outs[1]), jnp.where(first, lses[0], lses[1])

    out, lse = one_block(q_ref[0, 0:blk, :], k_ref[0, 0:blk, :], v_ref[0, 0:blk, :],
                         lambda hh: bias_ref[0, hh, :, blk:2 * blk])
    o_ref[0, 0:blk, :] = out
    l_ref[0, 0:blk, :] = lse

    def body(j, carry):
        r0 = pl.multiple_of(j * blk, blk)
        out, lse = one_block(q_ref[0, pl.ds(r0, blk), :],
                             k_ref[0, pl.ds(r0 - blk, 2 * blk), :],
                             v_ref[0, pl.ds(r0 - blk, 2 * blk), :],
                             lambda hh: bias_ref[0, hh, :, :])
        o_ref[0, pl.ds(r0, blk), :] = out
        l_ref[0, pl.ds(r0, blk), :] = lse
        return carry

    lax.fori_loop(1, nb, body, 0)


def _dilated_attention(qkv, bias, batch, seq, dilation):
    n_sub = seq // dilation
    nb = n_sub // ATTN_BLOCK
    view = qkv.reshape(batch, n_sub, dilation * 3 * ATTN_WIDTH)

    def in_spec(t):
        return pl.BlockSpec((1, n_sub, LANES),
                            lambda b, p, r: (b, 0, r * QKV_COL_BLOCKS + t * HEAD_PAIRS + p))

    out_spec = pl.BlockSpec((1, n_sub, LANES), lambda b, p, r: (b, 0, r * HEAD_PAIRS + p))
    out_sds = jax.ShapeDtypeStruct((batch, n_sub, dilation * ATTN_WIDTH), F32)
    o, l = pl.pallas_call(
        functools.partial(_attn_kernel, nb=nb),
        grid=(batch, HEAD_PAIRS, dilation),
        in_specs=[in_spec(0), in_spec(1), in_spec(2),
                  pl.BlockSpec((1, 2, ATTN_BLOCK, 2 * ATTN_BLOCK), lambda b, p, r: (p, 0, 0, 0))],
        out_specs=[out_spec, out_spec],
        out_shape=[out_sds, out_sds],
        compiler_params=_cparams("arbitrary", "arbitrary", "arbitrary"),
        name=f"dilated_attn_d{dilation}",
    )(view, view, view, bias)
    return o.reshape(batch * seq, ATTN_WIDTH), l.reshape(batch * seq, ATTN_WIDTH)


def _hgrn_kernel(q_ref, f_ref, i_ref, g_ref, lbl_ref, nw_ref, o_ref, st_ref, *, n_chunks):
    c, sub, dk = HGRN_CHUNK, HGRN_SUB, HGRN_HEAD_DIM
    nsub = c // sub
    lbl = lbl_ref[...]
    e = jnp.exp(lbl - jnp.max(lbl, axis=0, keepdims=True))
    lb = (e[0] / jnp.sum(e, axis=0))[0]
    nw = nw_ref[...]

    r64 = lax.broadcasted_iota(jnp.int32, (c, c), 0)
    c64 = lax.broadcasted_iota(jnp.int32, (c, c), 1)
    tril = (r64 >= c64).astype(F32)
    row_sub = r64 // sub
    col_sub = c64 // sub
    col_in_sub = c64 - row_sub * sub
    t_iota = lax.broadcasted_iota(jnp.int32, (nsub, sub, dk), 1)
    ones_rhs = jnp.ones((dk, c), BF16)

    st_ref[...] = jnp.zeros_like(st_ref)

    def body(ci, carry):
        r0 = pl.multiple_of(ci * c, c)
        rows = pl.ds(r0, c)
        q = q_ref[0, rows, :].astype(F32)
        v = i_ref[0, rows, :]
        g = g_ref[0, rows, :]
        f = lb + (1.0 - lb) * jax.nn.sigmoid(f_ref[0, rows, :])
        key = 1.0 - f
        cum = jnp.dot(tril, jnp.log(f), precision=HIGHEST, preferred_element_type=F32)
        qf = q * jax.nn.sigmoid(q) * (dk ** -0.5)
        last = cum[c - 1:c, :]
        st = st_ref[...]

        o = lax.dot_general((qf * jnp.exp(cum)).astype(BF16), st.astype(BF16), NT_DIMS,
                            preferred_element_type=F32)
        kd = (key * jnp.exp(last - cum)).astype(BF16)
        st_ref[...] = st * jnp.exp(last) + lax.dot_general(v, kd, TN_DIMS, preferred_element_type=F32)

        cum3 = cum.reshape(nsub, sub, dk)
        key3 = key.reshape(nsub, sub, dk)
        qf3 = qf.reshape(nsub, sub, dk)
        khat = (key3 * jnp.exp(cum3[:, sub - 1:sub, :] - cum3)).reshape(c, dk).astype(BF16)
        qs = [qf * jnp.exp(jnp.minimum(cum - cum[sub * j + sub - 1:sub * j + sub, :], 0.0))
              for j in range(nsub - 1)]
        prod = lax.dot_general(jnp.concatenate(qs, axis=0).astype(BF16), khat, NT_DIMS,
                               preferred_element_type=F32)
        a = jnp.zeros((c, c), F32)
        for j in range(nsub - 1):
            a = a + jnp.where((col_sub == j) & (row_sub > j), prod[j * c:(j + 1) * c, :], 0.0)

        ws = []
        for s in range(sub):
            dec = jnp.exp(jnp.where(t_iota >= s, cum3 - cum3[:, s:s + 1, :], -jnp.inf))
            ws.append((qf3 * key3[:, s:s + 1, :] * dec).reshape(c, dk))
        sums = jnp.dot(jnp.concatenate(ws, axis=0).astype(BF16), ones_rhs,
                       preferred_element_type=F32)
        for s in range(sub):
            a = a + jnp.where(col_in_sub == s, sums[s * c:(s + 1) * c, :], 0.0)

        o = o + jnp.dot(a.astype(BF16), v, preferred_element_type=F32)
        on = o * lax.rsqrt(jnp.mean(o * o, axis=-1, keepdims=True) + RMS_EPS) * nw
        o_ref[0, rows, :] = (on * (g * jax.nn.sigmoid(g))).astype(BF16)
        return carry

    lax.fori_loop(0, n_chunks, body, 0)


def _hgrn(qh, fh, ih, gh, lb_logits, norm_w, batch, seq):
    spec = pl.BlockSpec((1, seq, HGRN_HEAD_DIM), lambda b, h: (b, 0, h))
    slots = lb_logits.shape[0]
    shp = (batch, seq, HGRN_WIDTH)
    out = pl.pallas_call(
        functools.partial(_hgrn_kernel, n_chunks=seq // HGRN_CHUNK),
        grid=(batch, HGRN_HEADS),
        in_specs=[spec, spec, spec, spec,
                  pl.BlockSpec((slots, 1, 1, HGRN_HEAD_DIM), lambda b, h: (0, h, 0, 0)),
                  pl.BlockSpec((1, HGRN_HEAD_DIM), lambda b, h: (0, 0))],
        out_specs=spec,
        out_shape=jax.ShapeDtypeStruct(shp, BF16),
        scratch_shapes=[pltpu.VMEM((HGRN_HEAD_DIM, HGRN_HEAD_DIM), F32)],
        compiler_params=_cparams("arbitrary", "arbitrary"),
        name="hgrn2",
    )(qh.reshape(shp), fh.reshape(shp), ih.reshape(shp), gh.reshape(shp),
      lb_logits.astype(F32).reshape(slots, HGRN_HEADS, 1, HGRN_HEAD_DIM),
      norm_w.reshape(1, HGRN_HEAD_DIM))
    return out.reshape(batch * seq, HGRN_WIDTH)


def _outproj_kernel(o1_ref, o2_ref, o3_ref, l1_ref, l2_ref, l3_ref, hg_ref, x_ref, w_ref, nw_ref,
                    rw_ref, rb_ref, h_ref, xn_ref, idx_ref, gate_ref):
    l1, l2, l3 = l1_ref[...], l2_ref[...], l3_ref[...]
    m = jnp.maximum(jnp.maximum(l1, l2), l3)
    e1, e2, e3 = jnp.exp(l1 - m), jnp.exp(l2 - m), jnp.exp(l3 - m)
    attn = (e1 * o1_ref[...] + e2 * o2_ref[...] + e3 * o3_ref[...]) / (e1 + e2 + e3)
    y = jnp.dot(attn.astype(BF16), w_ref[0:ATTN_WIDTH, :], preferred_element_type=F32)
    y = y + jnp.dot(hg_ref[...], w_ref[ATTN_WIDTH:, :], preferred_element_type=F32)
    h = x_ref[...] + y
    h_ref[...] = h
    xn = h * lax.rsqrt(jnp.mean(h * h, axis=-1, keepdims=True) + RMS_EPS) * nw_ref[...]
    xn_ref[...] = xn.astype(BF16)

    logits = lax.dot_general(rw_ref[...], xn, NT_DIMS, precision=HIGHEST,
                             preferred_element_type=F32) + rb_ref[...]
    eid = lax.broadcasted_iota(jnp.int32, logits.shape, 0)
    vals, idxs = [], []
    for _ in range(TOP_K):
        mx = jnp.max(logits, axis=0, keepdims=True)
        ix = jnp.min(jnp.where(logits == mx, eid, N_EXPERTS), axis=0, keepdims=True)
        vals.append(mx)
        idxs.append(ix)
        logits = jnp.where(eid == ix, -jnp.inf, logits)
    es = [jnp.exp(v - vals[0]) for v in vals]
    den = es[0] + es[1] + es[2] + es[3]
    idx_ref[...] = jnp.concatenate(idxs, axis=0)
    gate_ref[...] = jnp.concatenate([e / den for e in es], axis=0)


def _outproj(o_l, hg, x2, w_out, norm_w, router_w, router_b):
    n = x2.shape[0]
    tm = OUTPROJ_ROWS
    row = lambda width: pl.BlockSpec((tm, width), lambda i: (i, 0))
    full = lambda a, b: pl.BlockSpec((a, b), lambda i: (0, 0))
    tok = pl.BlockSpec((TOP_K, tm), lambda i: (0, i))
    (o1, l1), (o2, l2), (o3, l3) = o_l
    return pl.pallas_call(
        _outproj_kernel,
        grid=(n // tm,),
        in_specs=[row(ATTN_WIDTH)] * 6 + [row(HGRN_WIDTH), row(D_MODEL), full(D_MODEL, D_MODEL),
                                           full(1, D_MODEL), full(N_EXPERTS, D_MODEL), full(N_EXPERTS, 1)],
        out_specs=[row(D_MODEL), row(D_MODEL), tok, tok],
        out_shape=[jax.ShapeDtypeStruct((n, D_MODEL), F32),
                   jax.ShapeDtypeStruct((n, D_MODEL), BF16),
                   jax.ShapeDtypeStruct((TOP_K, n), jnp.int32),
                   jax.ShapeDtypeStruct((TOP_K, n), F32)],
        compiler_params=_cparams("arbitrary"),
        name="outproj_router",
    )(o1, o2, o3, l1, l2, l3, hg, x2, w_out.astype(BF16), norm_w.reshape(1, D_MODEL),
      router_w.T.astype(F32), router_b.astype(F32).reshape(N_EXPERTS, 1))


def _ffn_kernel(be_ref, nv_ref, xs_ref, wgu_ref, bgu_ref, wdn_ref, bdn_ref, o_ref):
    i = pl.program_id(0)

    @pl.when(i < nv_ref[0])
    def _():
        hgu = jnp.dot(xs_ref[...], wgu_ref[0], preferred_element_type=F32) + bgu_ref[0]
        gate = jnp.minimum(hgu[:, :D_EXPERT], SWIGLU_LIMIT)
        up = jnp.clip(hgu[:, D_EXPERT:], -SWIGLU_LIMIT, SWIGLU_LIMIT)
        act = gate * jax.nn.sigmoid(SWIGLU_ALPHA * gate) * (up + 1.0)
        o_ref[...] = jnp.dot(act.astype(BF16), wdn_ref[0], preferred_element_type=F32) + bdn_ref[0]

    @pl.when(i >= nv_ref[0])
    def _():
        o_ref[...] = jnp.zeros_like(o_ref)


def _expert_ffn(block_e, n_valid, xs, w_gu, b_gu, w_dn, b_dn):
    cap = xs.shape[0]
    tm = FFN_ROWS
    grid_spec = pltpu.PrefetchScalarGridSpec(
        num_scalar_prefetch=2,
        grid=(cap // tm,),
        in_specs=[pl.BlockSpec((tm, D_MODEL), lambda i, be, nv: (i, 0)),
                  pl.BlockSpec((1, D_MODEL, 2 * D_EXPERT), lambda i, be, nv: (be[i], 0, 0)),
                  pl.BlockSpec((1, 1, 2 * D_EXPERT), lambda i, be, nv: (be[i], 0, 0)),
                  pl.BlockSpec((1, D_EXPERT, D_MODEL), lambda i, be, nv: (be[i], 0, 0)),
                  pl.BlockSpec((1, 1, D_MODEL), lambda i, be, nv: (be[i], 0, 0))],
        out_specs=pl.BlockSpec((tm, D_MODEL), lambda i, be, nv: (i, 0)),
    )
    return pl.pallas_call(
        _ffn_kernel,
        grid_spec=grid_spec,
        out_shape=jax.ShapeDtypeStruct((cap, D_MODEL), F32),
        compiler_params=_cparams("arbitrary"),
        name="expert_ffn",
    )(block_e, n_valid, xs, w_gu.astype(BF16), b_gu.reshape(N_EXPERTS, 1, 2 * D_EXPERT),
      w_dn.astype(BF16), b_dn.reshape(N_EXPERTS, 1, D_MODEL))


def _combine_kernel(h_ref, yp_ref, g_ref, fw_ref, o_ref):
    g = g_ref[...]
    y = h_ref[...]
    for k in range(TOP_K):
        y = y + yp_ref[k] * g[:, k:k + 1]
    o_ref[...] = y * lax.rsqrt(jnp.mean(y * y, axis=-1, keepdims=True) + RMS_EPS) * fw_ref[...]


def _combine(h, yp, gates_nk, final_w):
    n = h.shape[0]
    tm = COMBINE_ROWS
    return pl.pallas_call(
        _combine_kernel,
        grid=(n // tm,),
        in_specs=[pl.BlockSpec((tm, D_MODEL), lambda i: (i, 0)),
                  pl.BlockSpec((TOP_K, tm, D_MODEL), lambda i: (0, i, 0)),
                  pl.BlockSpec((tm, TOP_K), lambda i: (i, 0)),
                  pl.BlockSpec((1, D_MODEL), lambda i: (0, 0))],
        out_specs=pl.BlockSpec((tm, D_MODEL), lambda i: (i, 0)),
        out_shape=jax.ShapeDtypeStruct((n, D_MODEL), F32),
        compiler_params=_cparams("arbitrary"),
        name="combine_norm",
    )(h, yp, gates_nk, final_w.reshape(1, D_MODEL))


def kernel(x, norm_mix_w, w_in, rel_bias, hgrn_lb_logits, hgrn_norm_w, w_out, norm_ffn_w,
           router_w, router_b, w_gate_up, b_gate_up, w_down, b_down, final_norm_w):
    batch, seq, _ = x.shape
    n = batch * seq
    x2 = x.reshape(n, D_MODEL)

    qkv, qh, fh, ih, gh = _inproj(x2, norm_mix_w[0], w_in[0])
    o_l = [_dilated_attention(qkv, _masked_bias(rel_bias, w, d), batch, seq, d)
           for w, d in DILATION_PATTERNS]
    hg = _hgrn(qh, fh, ih, gh, hgrn_lb_logits, hgrn_norm_w[0], batch, seq)
    h, xn, idx_t, gate_t = _outproj(o_l, hg, x2, w_out[0], norm_ffn_w[0], router_w[0], router_b[0])

    tm = FFN_ROWS
    nk = n * TOP_K
    n_blocks = -(-nk // tm) + N_EXPERTS
    cap = n_blocks * tm
    flat_e = idx_t.reshape(nk)
    order = jnp.argsort(flat_e)
    sorted_e = flat_e[order]
    counts = jnp.zeros((N_EXPERTS,), jnp.int32).at[flat_e].add(1)
    padded = (counts + tm - 1) // tm * tm
    start = jnp.cumsum(counts) - counts
    pend = jnp.cumsum(padded)
    pstart = pend - padded
    dest = pstart[sorted_e] + (jnp.arange(nk, dtype=jnp.int32) - start[sorted_e])
    tok_buf = jnp.zeros((cap,), jnp.int32).at[dest].set((order % n).astype(jnp.int32))
    pos = jnp.zeros((nk,), jnp.int32).at[order].set(dest)
    block_e = jnp.clip(jnp.searchsorted(pend, jnp.arange(n_blocks, dtype=jnp.int32) * tm, side='right'),
                       0, N_EXPERTS - 1).astype(jnp.int32)
    n_valid = (pend[-1] // tm).astype(jnp.int32).reshape(1)

    xs = jnp.take(xn, tok_buf, axis=0)
    ys = _expert_ffn(block_e, n_valid, xs, w_gate_up[0], b_gate_up[0], w_down[0], b_down[0])
    yp = jnp.take(ys, pos, axis=0).reshape(TOP_K, n, D_MODEL)
    out = _combine(h, yp, gate_t.T, final_norm_w)
    return out.reshape(batch, seq, D_MODEL)
```

```python
import functools

import numpy as np
import jax
import jax.numpy as jnp
from jax import lax
from jax.experimental import pallas as pl
from jax.experimental.pallas import tpu as pltpu

F32 = jnp.float32
BF16 = jnp.bfloat16
HIGHEST = lax.Precision.HIGHEST

D_MODEL = 1024
ATTN_HEAD_DIM = 64
ATTN_WIDTH = 512
ATTN_HEADS = ATTN_WIDTH // ATTN_HEAD_DIM
DILATION_PATTERNS = ((128, 1), (512, 4), (2048, 16))
ATTN_BLOCK = 128
REL_BUCKETS = 32
REL_MAX_DISTANCE = 2048
HGRN_HEAD_DIM = 128
HGRN_WIDTH = 512
HGRN_HEADS = HGRN_WIDTH // HGRN_HEAD_DIM
HGRN_CHUNK = 64
HGRN_SUB = 8
N_EXPERTS = 32
TOP_K = 4
D_EXPERT = 1024
SWIGLU_LIMIT = 7.0
SWIGLU_ALPHA = 1.702
RMS_EPS = 1e-5
IN_PROJ_WIDTH = 3 * ATTN_WIDTH + 4 * HGRN_WIDTH
LANES = 128
HEAD_PAIRS = ATTN_WIDTH // LANES
QKV_COL_BLOCKS = 3 * ATTN_WIDTH // LANES

INPROJ_ROWS = 512
OUTPROJ_ROWS = 256
FFN_ROWS = 256
COMBINE_ROWS = 256
POS_TOKENS = 2048
MOVE_TOKENS = 512
MOVE_UNROLL = 8
VMEM_LIMIT = 48 * 1024 * 1024
FFN_VMEM_LIMIT = 56 * 1024 * 1024

NT_DIMS = (((1,), (1,)), ((), ()))
TN_DIMS = (((0,), (0,)), ((), ()))


def _cparams(*sem):
    return pltpu.CompilerParams(dimension_semantics=sem, vmem_limit_bytes=VMEM_LIMIT)


def _inproj_kernel(x_ref, nw_ref, w_ref, qkv_ref, qh_ref, fh_ref, ih_ref, gh_ref):
    x = x_ref[...]
    xn = x * lax.rsqrt(jnp.mean(x * x, axis=-1, keepdims=True) + RMS_EPS) * nw_ref[...]
    xn = xn.astype(BF16)

    def proj(c0, width):
        return jnp.dot(xn, w_ref[:, c0:c0 + width], preferred_element_type=F32)

    for c in range(3):
        qkv_ref[:, c * ATTN_WIDTH:(c + 1) * ATTN_WIDTH] = proj(c * ATTN_WIDTH, ATTN_WIDTH).astype(BF16)
    base = 3 * ATTN_WIDTH
    qh_ref[...] = proj(base, HGRN_WIDTH).astype(BF16)
    fh_ref[...] = proj(base + HGRN_WIDTH, HGRN_WIDTH)
    ih_ref[...] = proj(base + 2 * HGRN_WIDTH, HGRN_WIDTH).astype(BF16)
    gh_ref[...] = proj(base + 3 * HGRN_WIDTH, HGRN_WIDTH)


def _inproj(x2, norm_w, w_in):
    n = x2.shape[0]
    tm = INPROJ_ROWS
    row = lambda width: pl.BlockSpec((tm, width), lambda i: (i, 0))
    return pl.pallas_call(
        _inproj_kernel,
        grid=(n // tm,),
        in_specs=[row(D_MODEL),
                  pl.BlockSpec((1, D_MODEL), lambda i: (0, 0)),
                  pl.BlockSpec((D_MODEL, IN_PROJ_WIDTH), lambda i: (0, 0))],
        out_specs=[row(3 * ATTN_WIDTH), row(HGRN_WIDTH), row(HGRN_WIDTH), row(HGRN_WIDTH), row(HGRN_WIDTH)],
        out_shape=[jax.ShapeDtypeStruct((n, 3 * ATTN_WIDTH), BF16),
                   jax.ShapeDtypeStruct((n, HGRN_WIDTH), BF16),
                   jax.ShapeDtypeStruct((n, HGRN_WIDTH), F32),
                   jax.ShapeDtypeStruct((n, HGRN_WIDTH), BF16),
                   jax.ShapeDtypeStruct((n, HGRN_WIDTH), F32)],
        compiler_params=_cparams("arbitrary"),
        name="inproj",
    )(x2, norm_w.reshape(1, D_MODEL), w_in.astype(BF16))


def _t5_causal_bucket(dist):
    n = np.maximum(dist, 0)
    max_exact = REL_BUCKETS // 2
    large = max_exact + (np.log(np.maximum(n, 1) / max_exact)
                         / np.log(REL_MAX_DISTANCE / max_exact)
                         * (REL_BUCKETS - max_exact)).astype(np.int32)
    large = np.minimum(large, REL_BUCKETS - 1)
    return np.where(n < max_exact, n, large).astype(np.int32)


def _masked_bias(rel_bias, window, dilation):
    blk = ATTN_BLOCK
    qi = np.arange(blk)[:, None]
    kj = np.arange(2 * blk)[None, :]
    dist_sub = blk + qi - kj
    band = (dist_sub >= 0) & (dist_sub <= window // dilation)
    buckets = _t5_causal_bucket(dist_sub * dilation)
    bias = jnp.transpose(rel_bias[buckets], (2, 0, 1)).astype(F32)
    bias = jnp.where(band[None], bias, -1e30)
    return bias.reshape(HEAD_PAIRS, 2, blk, 2 * blk)


def _attn_kernel(q_ref, k_ref, v_ref, bias_ref, o_ref, l_ref, *, nb):
    blk = ATTN_BLOCK
    lane = lax.broadcasted_iota(jnp.int32, (1, LANES), 1)
    first = lane < ATTN_HEAD_DIM
    scale = ATTN_HEAD_DIM ** -0.5
    qsel = (jnp.where(first, scale, 0.0).astype(BF16), jnp.where(first, 0.0, scale).astype(BF16))

    def one_block(qb, kk, vv, bias_of):
        outs, lses = [], []
        for hh in range(2):
            s = lax.dot_general(qb * qsel[hh], kk, NT_DIMS, preferred_element_type=F32) + bias_of(hh)
            m = jnp.max(s, axis=-1, keepdims=True)
            p = jnp.exp(s - m)
            z = jnp.sum(p, axis=-1, keepdims=True)
            o = jnp.dot(p.astype(BF16), vv, preferred_element_type=F32)
            outs.append(o / z)
            lses.append(m + jnp.log(z))
        return jnp.where(first, outs[0], outs[1]), jnp.where(first, lses[0], lses[1])

    out, lse = one_block(q_ref[0, 0:blk, :], k_ref[0, 0:blk, :], v_ref[0, 0:blk, :],
                         lambda hh: bias_ref[0, hh, :, blk:2 * blk])
    o_ref[0, 0:blk, :] = out
    l_ref[0, 0:blk, :] = lse

    def body(j, carry):
        r0 = pl.multiple_of(j * blk, blk)
        out, lse = one_block(q_ref[0, pl.ds(r0, blk), :],
                             k_ref[0, pl.ds(r0 - blk, 2 * blk), :],
                             v_ref[0, pl.ds(r0 - blk, 2 * blk), :],
                             lambda hh: bias_ref[0, hh, :, :])
        o_ref[0, pl.ds(r0, blk), :] = out
        l_ref[0, pl.ds(r0, blk), :] = lse
        return carry

    lax.fori_loop(1, nb, body, 0)


def _dilated_attention(qkv, bias, batch, seq, dilation):
    n_sub = seq // dilation
    nb = n_sub // ATTN_BLOCK
    view = qkv.reshape(batch, n_sub, dilation * 3 * ATTN_WIDTH)

    def in_spec(t):
        return pl.BlockSpec((1, n_sub, LANES),
                            lambda b, p, r: (b, 0, r * QKV_COL_BLOCKS + t * HEAD_PAIRS + p))

    out_spec = pl.BlockSpec((1, n_sub, LANES), lambda b, p, r: (b, 0, r * HEAD_PAIRS + p))
    out_sds = jax.ShapeDtypeStruct((batch, n_sub, dilation * ATTN_WIDTH), F32)
    o, l = pl.pallas_call(
        functools.partial(_attn_kernel, nb=nb),
        grid=(batch, HEAD_PAIRS, dilation),
        in_specs=[in_spec(0), in_spec(1), in_spec(2),
                  pl.BlockSpec((1, 2, ATTN_BLOCK, 2 * ATTN_BLOCK), lambda b, p, r: (p, 0, 0, 0))],
        out_specs=[out_spec, out_spec],
        out_shape=[out_sds, out_sds],
        compiler_params=_cparams("arbitrary", "arbitrary", "arbitrary"),
        name=f"dilated_attn_d{dilation}",
    )(view, view, view, bias)
    return o.reshape(batch * seq, ATTN_WIDTH), l.reshape(batch * seq, ATTN_WIDTH)


def _hgrn_kernel(q_ref, f_ref, i_ref, g_ref, lbl_ref, nw_ref, o_ref, st_ref, *, n_chunks):
    c, sub, dk = HGRN_CHUNK, HGRN_SUB, HGRN_HEAD_DIM
    nsub = c // sub
    lbl = lbl_ref[...]
    e = jnp.exp(lbl - jnp.max(lbl, axis=0, keepdims=True))
    lb = (e[0] / jnp.sum(e, axis=0))[0]
    nw = nw_ref[...]

    r64 = lax.broadcasted_iota(jnp.int32, (c, c), 0)
    c64 = lax.broadcasted_iota(jnp.int32, (c, c), 1)
    tril = (r64 >= c64).astype(F32)
    row_sub = r64 // sub
    col_sub = c64 // sub
    col_in_sub = c64 - row_sub * sub
    t_iota = lax.broadcasted_iota(jnp.int32, (nsub, sub, dk), 1)
    ones_rhs = jnp.ones((dk, c), BF16)

    st_ref[...] = jnp.zeros_like(st_ref)

    def body(ci, carry):
        r0 = pl.multiple_of(ci * c, c)
        rows = pl.ds(r0, c)
        q = q_ref[0, rows, :].astype(F32)
        v = i_ref[0, rows, :]
        g = g_ref[0, rows, :]
        f = lb + (1.0 - lb) * jax.nn.sigmoid(f_ref[0, rows, :])
        key = 1.0 - f
        cum = jnp.dot(tril, jnp.log(f), precision=HIGHEST, preferred_element_type=F32)
        qf = q * jax.nn.sigmoid(q) * (dk ** -0.5)
        last = cum[c - 1:c, :]
        st = st_ref[...]

        o = lax.dot_general((qf * jnp.exp(cum)).astype(BF16), st.astype(BF16), NT_DIMS,
                            preferred_element_type=F32)
        kd = (key * jnp.exp(last - cum)).astype(BF16)
        st_ref[...] = st * jnp.exp(last) + lax.dot_general(v, kd, TN_DIMS, preferred_element_type=F32)

        cum3 = cum.reshape(nsub, sub, dk)
        key3 = key.reshape(nsub, sub, dk)
        qf3 = qf.reshape(nsub, sub, dk)
        khat = (key3 * jnp.exp(cum3[:, sub - 1:sub, :] - cum3)).reshape(c, dk).astype(BF16)
        qs = [qf * jnp.exp(jnp.minimum(cum - cum[sub * j + sub - 1:sub * j + sub, :], 0.0))
              for j in range(nsub - 1)]
        prod = lax.dot_general(jnp.concatenate(qs, axis=0).astype(BF16), khat, NT_DIMS,
                               preferred_element_type=F32)
        a = jnp.zeros((c, c), F32)
        for j in range(nsub - 1):
            a = a + jnp.where((col_sub == j) & (row_sub > j), prod[j * c:(j + 1) * c, :], 0.0)

        ws = []
        for s in range(sub):
            dec = jnp.exp(jnp.where(t_iota >= s, cum3 - cum3[:, s:s + 1, :], -jnp.inf))
            ws.append((qf3 * key3[:, s:s + 1, :] * dec).reshape(c, dk))
        sums = jnp.dot(jnp.concatenate(ws, axis=0).astype(BF16), ones_rhs,
                       preferred_element_type=F32)
        for s in range(sub):
            a = a + jnp.where(col_in_sub == s, sums[s * c:(s + 1) * c, :], 0.0)

        o = o + jnp.dot(a.astype(BF16), v, preferred_element_type=F32)
        on = o * lax.rsqrt(jnp.mean(o * o, axis=-1, keepdims=True) + RMS_EPS) * nw
        o_ref[0, rows, :] = (on * (g * jax.nn.sigmoid(g))).astype(BF16)
        return carry

    lax.fori_loop(0, n_chunks, body, 0)


def _hgrn(qh, fh, ih, gh, lb_logits, norm_w, batch, seq):
    spec = pl.BlockSpec((1, seq, HGRN_HEAD_DIM), lambda b, h: (b, 0, h))
    slots = lb_logits.shape[0]
    shp = (batch, seq, HGRN_WIDTH)
    out = pl.pallas_call(
        functools.partial(_hgrn_kernel, n_chunks=seq // HGRN_CHUNK),
        grid=(batch, HGRN_HEADS),
        in_specs=[spec, spec, spec, spec,
                  pl.BlockSpec((slots, 1, 1, HGRN_HEAD_DIM), lambda b, h: (0, h, 0, 0)),
                  pl.BlockSpec((1, HGRN_HEAD_DIM), lambda b, h: (0, 0))],
        out_specs=spec,
        out_shape=jax.ShapeDtypeStruct(shp, BF16),
        scratch_shapes=[pltpu.VMEM((HGRN_HEAD_DIM, HGRN_HEAD_DIM), F32)],
        compiler_params=_cparams("arbitrary", "arbitrary"),
        name="hgrn2",
    )(qh.reshape(shp), fh.reshape(shp), ih.reshape(shp), gh.reshape(shp),
      lb_logits.astype(F32).reshape(slots, HGRN_HEADS, 1, HGRN_HEAD_DIM),
      norm_w.reshape(1, HGRN_HEAD_DIM))
    return out.reshape(batch * seq, HGRN_WIDTH)


def _outproj_kernel(o1_ref, o2_ref, o3_ref, l1_ref, l2_ref, l3_ref, hg_ref, x_ref, w_ref, nw_ref,
                    rw_ref, rb_ref, h_ref, xn_ref, idx_ref, gate_ref, rank_ref, cnt_ref, run_ref):
    l1, l2, l3 = l1_ref[...], l2_ref[...], l3_ref[...]
    m = jnp.maximum(jnp.maximum(l1, l2), l3)
    e1, e2, e3 = jnp.exp(l1 - m), jnp.exp(l2 - m), jnp.exp(l3 - m)
    attn = (e1 * o1_ref[...] + e2 * o2_ref[...] + e3 * o3_ref[...]) / (e1 + e2 + e3)
    y = jnp.dot(attn.astype(BF16), w_ref[0:ATTN_WIDTH, :], preferred_element_type=F32)
    y = y + jnp.dot(hg_ref[...], w_ref[ATTN_WIDTH:, :], preferred_element_type=F32)
    h = x_ref[...] + y
    h_ref[...] = h
    xn = h * lax.rsqrt(jnp.mean(h * h, axis=-1, keepdims=True) + RMS_EPS) * nw_ref[...]
    xn_ref[...] = xn

    logits = lax.dot_general(rw_ref[...], xn, NT_DIMS, precision=HIGHEST,
                             preferred_element_type=F32) + rb_ref[...]
    eid = lax.broadcasted_iota(jnp.int32, logits.shape, 0)
    vals, idxs = [], []
    for _ in range(TOP_K):
        mx = jnp.max(logits, axis=0, keepdims=True)
        ix = jnp.min(jnp.where(logits == mx, eid, N_EXPERTS), axis=0, keepdims=True)
        vals.append(mx)
        idxs.append(ix)
        logits = jnp.where(eid == ix, -jnp.inf, logits)
    es = [jnp.exp(v - vals[0]) for v in vals]
    den = es[0] + es[1] + es[2] + es[3]
    idx_ref[...] = jnp.concatenate(idxs, axis=0)
    gate_ref[...] = jnp.concatenate([e / den for e in es], axis=0)

    @pl.when(pl.program_id(0) == 0)
    def _():
        run_ref[...] = jnp.zeros_like(run_ref)

    rows = logits.shape[1]
    onehots = [(eid == ix).astype(F32) for ix in idxs]
    routed = onehots[0] + onehots[1] + onehots[2] + onehots[3]
    earlier = (lax.broadcasted_iota(jnp.int32, (rows, rows), 0)
               < lax.broadcasted_iota(jnp.int32, (rows, rows), 1)).astype(BF16)
    before = jnp.dot(routed.astype(BF16), earlier, preferred_element_type=F32) + run_ref[...]
    rank_ref[...] = jnp.concatenate([jnp.sum(o * before, axis=0, keepdims=True) for o in onehots],
                                    axis=0).astype(jnp.int32)
    run_ref[...] = run_ref[...] + jnp.sum(routed, axis=1, keepdims=True)
    cnt_ref[...] = run_ref[...].astype(jnp.int32)


def _outproj(o_l, hg, x2, w_out, norm_w, router_w, router_b):
    n = x2.shape[0]
    tm = OUTPROJ_ROWS
    row = lambda width: pl.BlockSpec((tm, width), lambda i: (i, 0))
    full = lambda a, b: pl.BlockSpec((a, b), lambda i: (0, 0))
    tok = pl.BlockSpec((TOP_K, tm), lambda i: (0, i))
    (o1, l1), (o2, l2), (o3, l3) = o_l
    return pl.pallas_call(
        _outproj_kernel,
        grid=(n // tm,),
        in_specs=[row(ATTN_WIDTH)] * 6 + [row(HGRN_WIDTH), row(D_MODEL), full(D_MODEL, D_MODEL),
                                           full(1, D_MODEL), full(N_EXPERTS, D_MODEL), full(N_EXPERTS, 1)],
        out_specs=[row(D_MODEL), row(D_MODEL), tok, tok, tok, full(N_EXPERTS, 1)],
        out_shape=[jax.ShapeDtypeStruct((n, D_MODEL), F32),
                   jax.ShapeDtypeStruct((n, D_MODEL), F32),
                   jax.ShapeDtypeStruct((TOP_K, n), jnp.int32),
                   jax.ShapeDtypeStruct((TOP_K, n), F32),
                   jax.ShapeDtypeStruct((TOP_K, n), jnp.int32),
                   jax.ShapeDtypeStruct((N_EXPERTS, 1), jnp.int32)],
        scratch_shapes=[pltpu.VMEM((N_EXPERTS, 1), F32)],
        compiler_params=_cparams("arbitrary"),
        name="outproj_router",
    )(o1, o2, o3, l1, l2, l3, hg, x2, w_out.astype(BF16), norm_w.reshape(1, D_MODEL),
      router_w.T.astype(F32), router_b.astype(F32).reshape(N_EXPERTS, 1))


def _ffn_kernel(be_ref, rv_ref, xs_ref, wgu_ref, bgu_ref, wdn_ref, bdn_ref, o_ref, wgu_bf, wdn_bf):
    i = pl.program_id(0)
    rows_valid = rv_ref[i]
    new_expert = (i == 0) | (be_ref[i] != be_ref[jnp.maximum(i - 1, 0)])

    @pl.when(new_expert & (rows_valid > 0))
    def _():
        step = 128
        for r in range(0, D_MODEL, step):
            wgu_bf[r:r + step, :] = wgu_ref[0, r:r + step, :].astype(BF16)
        for r in range(0, D_EXPERT, step):
            wdn_bf[r:r + step, :] = wdn_ref[0, r:r + step, :].astype(BF16)

    @pl.when(rows_valid > 0)
    def _():
        live = lax.broadcasted_iota(jnp.int32, (xs_ref.shape[0], 1), 0) < rows_valid
        x = jnp.where(live, xs_ref[...], 0.0).astype(BF16)
        hgu = jnp.dot(x, wgu_bf[...], preferred_element_type=F32) + bgu_ref[0]
        gate = jnp.minimum(hgu[:, :D_EXPERT], SWIGLU_LIMIT)
        up = jnp.clip(hgu[:, D_EXPERT:], -SWIGLU_LIMIT, SWIGLU_LIMIT)
        act = gate * jax.nn.sigmoid(SWIGLU_ALPHA * gate) * (up + 1.0)
        o_ref[...] = jnp.dot(act.astype(BF16), wdn_bf[...], preferred_element_type=F32) + bdn_ref[0]

    @pl.when(rows_valid <= 0)
    def _():
        o_ref[...] = jnp.zeros_like(o_ref)


def _expert_ffn(block_e, rows_valid, xs, w_gu, b_gu, w_dn, b_dn):
    cap = xs.shape[0]
    tm = FFN_ROWS
    grid_spec = pltpu.PrefetchScalarGridSpec(
        num_scalar_prefetch=2,
        grid=(cap // tm,),
        in_specs=[pl.BlockSpec((tm, D_MODEL), lambda i, be, rv: (i, 0)),
                  pl.BlockSpec((1, D_MODEL, 2 * D_EXPERT), lambda i, be, rv: (be[i], 0, 0)),
                  pl.BlockSpec((1, 1, 2 * D_EXPERT), lambda i, be, rv: (be[i], 0, 0)),
                  pl.BlockSpec((1, D_EXPERT, D_MODEL), lambda i, be, rv: (be[i], 0, 0)),
                  pl.BlockSpec((1, 1, D_MODEL), lambda i, be, rv: (be[i], 0, 0))],
        out_specs=pl.BlockSpec((tm, D_MODEL), lambda i, be, rv: (i, 0)),
        scratch_shapes=[pltpu.VMEM((D_MODEL, 2 * D_EXPERT), BF16), pltpu.VMEM((D_EXPERT, D_MODEL), BF16)],
    )
    return pl.pallas_call(
        _ffn_kernel,
        grid_spec=grid_spec,
        out_shape=jax.ShapeDtypeStruct((cap, D_MODEL), F32),
        compiler_params=pltpu.CompilerParams(dimension_semantics=("arbitrary",),
                                             vmem_limit_bytes=FFN_VMEM_LIMIT),
        name="expert_ffn",
    )(block_e, rows_valid, xs, w_gu, b_gu.reshape(N_EXPERTS, 1, 2 * D_EXPERT),
      w_dn, b_dn.reshape(N_EXPERTS, 1, D_MODEL))


def _pos_kernel(idx_ref, rank_ref, pstart_ref, pos_ref):
    eid = lax.broadcasted_iota(jnp.int32, (N_EXPERTS, idx_ref.shape[1]), 0)
    pstart = pstart_ref[...]
    rows = [jnp.sum(jnp.where(eid == idx_ref[k:k + 1, :], pstart, 0), axis=0, keepdims=True)
            for k in range(TOP_K)]
    pos_ref[...] = jnp.concatenate(rows, axis=0) + rank_ref[...]


def _positions(idx_t, rank_t, pstart):
    n = idx_t.shape[1]
    tl = POS_TOKENS
    tok = pl.BlockSpec((TOP_K, tl), lambda i: (0, i))
    return pl.pallas_call(
        _pos_kernel,
        grid=(n // tl,),
        in_specs=[tok, tok, pl.BlockSpec((N_EXPERTS, 1), lambda i: (0, 0))],
        out_specs=tok,
        out_shape=jax.ShapeDtypeStruct((TOP_K, n), jnp.int32),
        compiler_params=_cparams("arbitrary"),
        name="dispatch_pos",
    )(idx_t, rank_t, pstart.reshape(N_EXPERTS, 1))


def _move_kernel(pos_ref, src_ref, dst_ref, sem, *, tokens, n, to_sorted):
    base = pl.program_id(0) * tokens

    def copy(t, k):
        p = pos_ref[0, k, t]
        if to_sorted:
            return pltpu.make_async_copy(src_ref.at[pl.ds(base + t, 1)], dst_ref.at[pl.ds(p, 1)], sem)
        return pltpu.make_async_copy(src_ref.at[pl.ds(p, 1)], dst_ref.at[pl.ds(k * n + base + t, 1)], sem)

    def issue(t, carry):
        for k in range(TOP_K):
            copy(t, k).start()
        return carry

    def drain(t, carry):
        for k in range(TOP_K):
            copy(t, k).wait()
        return carry

    lax.fori_loop(0, tokens, issue, 0, unroll=MOVE_UNROLL)
    lax.fori_loop(0, tokens, drain, 0, unroll=MOVE_UNROLL)


def _move_rows(pos, src, out_rows, to_sorted):
    n = pos.shape[1]
    tl = MOVE_TOKENS
    pos3 = pos.reshape(TOP_K, n // tl, tl).transpose(1, 0, 2)
    return pl.pallas_call(
        functools.partial(_move_kernel, tokens=tl, n=n, to_sorted=to_sorted),
        grid=(n // tl,),
        in_specs=[pl.BlockSpec((1, TOP_K, tl), lambda i: (i, 0, 0), memory_space=pltpu.SMEM),
                  pl.BlockSpec(memory_space=pl.ANY)],
        out_specs=pl.BlockSpec(memory_space=pl.ANY),
        out_shape=jax.ShapeDtypeStruct((out_rows, D_MODEL), src.dtype),
        scratch_shapes=[pltpu.SemaphoreType.DMA(())],
        compiler_params=_cparams("arbitrary"),
        name="dispatch_rows" if to_sorted else "collect_rows",
    )(pos3, src)


def _combine_kernel(h_ref, yp_ref, g_ref, fw_ref, o_ref):
    g = g_ref[...]
    y = h_ref[...]
    for k in range(TOP_K):
        y = y + yp_ref[k] * g[:, k:k + 1]
    o_ref[...] = y * lax.rsqrt(jnp.mean(y * y, axis=-1, keepdims=True) + RMS_EPS) * fw_ref[...]


def _combine(h, yp, gates_nk, final_w):
    n = h.shape[0]
    tm = COMBINE_ROWS
    return pl.pallas_call(
        _combine_kernel,
        grid=(n // tm,),
        in_specs=[pl.BlockSpec((tm, D_MODEL), lambda i: (i, 0)),
                  pl.BlockSpec((TOP_K, tm, D_MODEL), lambda i: (0, i, 0)),
                  pl.BlockSpec((tm, TOP_K), lambda i: (i, 0)),
                  pl.BlockSpec((1, D_MODEL), lambda i: (0, 0))],
        out_specs=pl.BlockSpec((tm, D_MODEL), lambda i: (i, 0)),
        out_shape=jax.ShapeDtypeStruct((n, D_MODEL), F32),
        compiler_params=_cparams("arbitrary"),
        name="combine_norm",
    )(h, yp, gates_nk, final_w.reshape(1, D_MODEL))


def kernel(x, norm_mix_w, w_in, rel_bias, hgrn_lb_logits, hgrn_norm_w, w_out, norm_ffn_w,
           router_w, router_b, w_gate_up, b_gate_up, w_down, b_down, final_norm_w):
    batch, seq, _ = x.shape
    n = batch * seq
    x2 = x.reshape(n, D_MODEL)

    qkv, qh, fh, ih, gh = _inproj(x2, norm_mix_w[0], w_in[0])
    o_l = [_dilated_attention(qkv, _masked_bias(rel_bias, w, d), batch, seq, d)
           for w, d in DILATION_PATTERNS]
    hg = _hgrn(qh, fh, ih, gh, hgrn_lb_logits, hgrn_norm_w[0], batch, seq)
    h, xn, idx_t, gate_t, rank_t, counts = _outproj(o_l, hg, x2, w_out[0], norm_ffn_w[0],
                                                    router_w[0], router_b[0])

    tm = FFN_ROWS
    n_blocks = -(-(n * TOP_K) // tm) + N_EXPERTS
    counts = counts.reshape(N_EXPERTS)
    padded = (counts + tm - 1) // tm * tm
    pend = jnp.cumsum(padded)
    pstart = pend - padded
    block_row0 = jnp.arange(n_blocks, dtype=jnp.int32) * tm
    block_e = jnp.clip(jnp.searchsorted(pend, block_row0, side='right'), 0, N_EXPERTS - 1).astype(jnp.int32)
    rows_valid = jnp.clip(pstart[block_e] + counts[block_e] - block_row0, 0, tm).astype(jnp.int32)

    pos = _positions(idx_t, rank_t, pstart.astype(jnp.int32))
    xs = _move_rows(pos, xn, n_blocks * tm, to_sorted=True)
    ys = _expert_ffn(block_e, rows_valid, xs, w_gate_up[0], b_gate_up[0], w_down[0], b_down[0])
    yp = _move_rows(pos, ys, TOP_K * n, to_sorted=False).reshape(TOP_K, n, D_MODEL)
    out = _combine(h, yp, gate_t.T, final_norm_w)
    return out.reshape(batch, seq, D_MODEL)
```

```python
import functools

import numpy as np
import jax
import jax.numpy as jnp
from jax import lax
from jax.experimental import pallas as pl
from jax.experimental.pallas import tpu as pltpu

F32 = jnp.float32
BF16 = jnp.bfloat16
HIGHEST = lax.Precision.HIGHEST

D_MODEL = 1024
ATTN_HEAD_DIM = 64
ATTN_WIDTH = 512
ATTN_HEADS = ATTN_WIDTH // ATTN_HEAD_DIM
DILATION_PATTERNS = ((128, 1), (512, 4), (2048, 16))
ATTN_BLOCK = 128
REL_BUCKETS = 32
REL_MAX_DISTANCE = 2048
HGRN_HEAD_DIM = 128
HGRN_WIDTH = 512
HGRN_HEADS = HGRN_WIDTH // HGRN_HEAD_DIM
HGRN_CHUNK = 64
HGRN_SUB = 8
N_EXPERTS = 32
TOP_K = 4
D_EXPERT = 1024
SWIGLU_LIMIT = 7.0
SWIGLU_ALPHA = 1.702
RMS_EPS = 1e-5
IN_PROJ_WIDTH = 3 * ATTN_WIDTH + 4 * HGRN_WIDTH
LANES = 128
HEAD_PAIRS = ATTN_WIDTH // LANES
QKV_COL_BLOCKS = 3 * ATTN_WIDTH // LANES

INPROJ_ROWS = 512
OUTPROJ_ROWS = 256
FFN_ROWS = 256
COMBINE_ROWS = 256
POS_TOKENS = 2048
MOVE_TOKENS = 512
MOVE_UNROLL = 8
VMEM_LIMIT = 48 * 1024 * 1024
FFN_VMEM_LIMIT = 56 * 1024 * 1024

NT_DIMS = (((1,), (1,)), ((), ()))
TN_DIMS = (((0,), (0,)), ((), ()))


def _cparams(*sem):
    return pltpu.CompilerParams(dimension_semantics=sem, vmem_limit_bytes=VMEM_LIMIT)


def _inproj_kernel(x_ref, nw_ref, w_ref, qkv_ref, qh_ref, fh_ref, ih_ref, gh_ref):
    x = x_ref[...]
    xn = x * lax.rsqrt(jnp.mean(x * x, axis=-1, keepdims=True) + RMS_EPS) * nw_ref[...]
    xn = xn.astype(BF16)

    def proj(c0, width):
        return jnp.dot(xn, w_ref[:, c0:c0 + width], preferred_element_type=F32)

    for c in range(3):
        qkv_ref[:, c * ATTN_WIDTH:(c + 1) * ATTN_WIDTH] = proj(c * ATTN_WIDTH, ATTN_WIDTH).astype(BF16)
    base = 3 * ATTN_WIDTH
    qh_ref[...] = proj(base, HGRN_WIDTH).astype(BF16)
    fh_ref[...] = proj(base + HGRN_WIDTH, HGRN_WIDTH)
    ih_ref[...] = proj(base + 2 * HGRN_WIDTH, HGRN_WIDTH).astype(BF16)
    gh_ref[...] = proj(base + 3 * HGRN_WIDTH, HGRN_WIDTH)


def _inproj(x2, norm_w, w_in):
    n = x2.shape[0]
    tm = INPROJ_ROWS
    row = lambda width: pl.BlockSpec((tm, width), lambda i: (i, 0))
    return pl.pallas_call(
        _inproj_kernel,
        grid=(n // tm,),
        in_specs=[row(D_MODEL),
                  pl.BlockSpec((1, D_MODEL), lambda i: (0, 0)),
                  pl.BlockSpec((D_MODEL, IN_PROJ_WIDTH), lambda i: (0, 0))],
        out_specs=[row(3 * ATTN_WIDTH), row(HGRN_WIDTH), row(HGRN_WIDTH), row(HGRN_WIDTH), row(HGRN_WIDTH)],
        out_shape=[jax.ShapeDtypeStruct((n, 3 * ATTN_WIDTH), BF16),
                   jax.ShapeDtypeStruct((n, HGRN_WIDTH), BF16),
                   jax.ShapeDtypeStruct((n, HGRN_WIDTH), F32),
                   jax.ShapeDtypeStruct((n, HGRN_WIDTH), BF16),
                   jax.ShapeDtypeStruct((n, HGRN_WIDTH), F32)],
        compiler_params=_cparams("arbitrary"),
        name="inproj",
    )(x2, norm_w.reshape(1, D_MODEL), w_in.astype(BF16))


def _t5_causal_bucket(dist):
    n = np.maximum(dist, 0)
    max_exact = REL_BUCKETS // 2
    large = max_exact + (np.log(np.maximum(n, 1) / max_exact)
                         / np.log(REL_MAX_DISTANCE / max_exact)
                         * (REL_BUCKETS - max_exact)).astype(np.int32)
    large = np.minimum(large, REL_BUCKETS - 1)
    return np.where(n < max_exact, n, large).astype(np.int32)


def _bucket_thresholds(max_dist):
    buckets = _t5_causal_bucket(np.arange(max_dist + 1))
    assert np.all(np.diff(buckets) >= 0)
    return [(b, int(np.argmax(buckets >= b))) for b in range(1, REL_BUCKETS) if np.any(buckets >= b)]


def _bias_kernel(rel_ref, o_ref, *, span, dilation):
    blk = ATTN_BLOCK
    h = pl.program_id(0)
    qi = lax.broadcasted_iota(jnp.int32, (blk, 2 * blk), 0)
    kj = lax.broadcasted_iota(jnp.int32, (blk, 2 * blk), 1)
    dist_sub = blk + qi - kj
    dist = dist_sub * dilation
    val = jnp.full((blk, 2 * blk), rel_ref[h], F32)
    for b, first_dist in _bucket_thresholds(span * dilation):
        val = jnp.where(dist >= first_dist, rel_ref[b * ATTN_HEADS + h], val)
    o_ref[0] = jnp.where((dist_sub >= 0) & (dist_sub <= span), val, -1e30)


def _masked_bias(rel_bias, window, dilation):
    blk = ATTN_BLOCK
    bias = pl.pallas_call(
        functools.partial(_bias_kernel, span=window // dilation, dilation=dilation),
        grid=(ATTN_HEADS,),
        in_specs=[pl.BlockSpec(memory_space=pltpu.SMEM)],
        out_specs=pl.BlockSpec((1, blk, 2 * blk), lambda h: (h, 0, 0)),
        out_shape=jax.ShapeDtypeStruct((ATTN_HEADS, blk, 2 * blk), F32),
        compiler_params=_cparams("arbitrary"),
        name=f"rel_bias_d{dilation}",
    )(rel_bias.astype(F32).reshape(REL_BUCKETS * ATTN_HEADS))
    return bias.reshape(HEAD_PAIRS, 2, blk, 2 * blk)


def _attn_kernel(q_ref, k_ref, v_ref, bias_ref, o_ref, l_ref, *, nb):
    blk = ATTN_BLOCK
    lane = lax.broadcasted_iota(jnp.int32, (1, LANES), 1)
    first = lane < ATTN_HEAD_DIM
    scale = ATTN_HEAD_DIM ** -0.5
    qsel = (jnp.where(first, scale, 0.0).astype(BF16), jnp.where(first, 0.0, scale).astype(BF16))

    def one_block(qb, kk, vv, bias_of):
        outs, lses = [], []
        for hh in range(2):
            s = lax.dot_general(qb * qsel[hh], kk, NT_DIMS, preferred_element_type=F32) + bias_of(hh)
            m = jnp.max(s, axis=-1, keepdims=True)
            p = jnp.exp(s - m)
            z = jnp.sum(p, axis=-1, keepdims=True)
            o = jnp.dot(p.astype(BF16), vv, preferred_element_type=F32)
            outs.append(o / z)
            lses.append(m + jnp.log(z))
        return jnp.where(first, outs[0], outs[1]), jnp.where(first, lses[0], lses[1])

    out, lse = one_block(q_ref[0, 0:blk, :], k_ref[0, 0:blk, :], v_ref[0, 0:blk, :],
                         lambda hh: bias_ref[0, hh, :, blk:2 * blk])
    o_ref[0, 0:blk, :] = out
    l_ref[0, 0:blk, :] = lse

    def body(j, carry):
        r0 = pl.multiple_of(j * blk, blk)
        out, lse = one_block(q_ref[0, pl.ds(r0, blk), :],
                             k_ref[0, pl.ds(r0 - blk, 2 * blk), :],
                             v_ref[0, pl.ds(r0 - blk, 2 * blk), :],
                             lambda hh: bias_ref[0, hh, :, :])
        o_ref[0, pl.ds(r0, blk), :] = out
        l_ref[0, pl.ds(r0, blk), :] = lse
        return carry

    lax.fori_loop(1, nb, body, 0)


def _dilated_attention(qkv, bias, batch, seq, dilation):
    n_sub = seq // dilation
    nb = n_sub // ATTN_BLOCK
    view = qkv.reshape(batch, n_sub, dilation * 3 * ATTN_WIDTH)

    def in_spec(t):
        return pl.BlockSpec((1, n_sub, LANES),
                            lambda b, p, r: (b, 0, r * QKV_COL_BLOCKS + t * HEAD_PAIRS + p))

    out_spec = pl.BlockSpec((1, n_sub, LANES), lambda b, p, r: (b, 0, r * HEAD_PAIRS + p))
    out_sds = jax.ShapeDtypeStruct((batch, n_sub, dilation * ATTN_WIDTH), F32)
    o, l = pl.pallas_call(
        functools.partial(_attn_kernel, nb=nb),
        grid=(batch, HEAD_PAIRS, dilation),
        in_specs=[in_spec(0), in_spec(1), in_spec(2),
                  pl.BlockSpec((1, 2, ATTN_BLOCK, 2 * ATTN_BLOCK), lambda b, p, r: (p, 0, 0, 0))],
        out_specs=[out_spec, out_spec],
        out_shape=[out_sds, out_sds],
        compiler_params=_cparams("arbitrary", "arbitrary", "arbitrary"),
        name=f"dilated_attn_d{dilation}",
    )(view, view, view, bias)
    return o.reshape(batch * seq, ATTN_WIDTH), l.reshape(batch * seq, ATTN_WIDTH)


def _hgrn_kernel(q_ref, f_ref, i_ref, g_ref, lbl_ref, nw_ref, o_ref, st_ref, *, n_chunks):
    c, sub, dk = HGRN_CHUNK, HGRN_SUB, HGRN_HEAD_DIM
    nsub = c // sub
    lbl = lbl_ref[...]
    e = jnp.exp(lbl - jnp.max(lbl, axis=0, keepdims=True))
    lb = (e[0] / jnp.sum(e, axis=0))[0]
    nw = nw_ref[...]

    r64 = lax.broadcasted_iota(jnp.int32, (c, c), 0)
    c64 = lax.broadcasted_iota(jnp.int32, (c, c), 1)
    tril = (r64 >= c64).astype(F32)
    row_sub = r64 // sub
    col_sub = c64 // sub
    col_in_sub = c64 - row_sub * sub
    t_iota = lax.broadcasted_iota(jnp.int32, (nsub, sub, dk), 1)
    ones_rhs = jnp.ones((dk, c), BF16)

    st_ref[...] = jnp.zeros_like(st_ref)

    def body(ci, carry):
        r0 = pl.multiple_of(ci * c, c)
        rows = pl.ds(r0, c)
        q = q_ref[0, rows, :].astype(F32)
        v = i_ref[0, rows, :]
        g = g_ref[0, rows, :]
        f = lb + (1.0 - lb) * jax.nn.sigmoid(f_ref[0, rows, :])
        key = 1.0 - f
        cum = jnp.dot(tril, jnp.log(f), precision=HIGHEST, preferred_element_type=F32)
        qf = q * jax.nn.sigmoid(q) * (dk ** -0.5)
        last = cum[c - 1:c, :]
        st = st_ref[...]

        o = lax.dot_general((qf * jnp.exp(cum)).astype(BF16), st.astype(BF16), NT_DIMS,
                            preferred_element_type=F32)
        kd = (key * jnp.exp(last - cum)).astype(BF16)
        st_ref[...] = st * jnp.exp(last) + lax.dot_general(v, kd, TN_DIMS, preferred_element_type=F32)

        cum3 = cum.reshape(nsub, sub, dk)
        key3 = key.reshape(nsub, sub, dk)
        qf3 = qf.reshape(nsub, sub, dk)
        khat = (key3 * jnp.exp(cum3[:, sub - 1:sub, :] - cum3)).reshape(c, dk).astype(BF16)
        qs = [qf * jnp.exp(jnp.minimum(cum - cum[sub * j + sub - 1:sub * j + sub, :], 0.0))
              for j in range(nsub - 1)]
        prod = lax.dot_general(jnp.concatenate(qs, axis=0).astype(BF16), khat, NT_DIMS,
                               preferred_element_type=F32)
        a = jnp.zeros((c, c), F32)
        for j in range(nsub - 1):
            a = a + jnp.where((col_sub == j) & (row_sub > j), prod[j * c:(j + 1) * c, :], 0.0)

        ws = []
        for s in range(sub):
            dec = jnp.exp(jnp.where(t_iota >= s, cum3 - cum3[:, s:s + 1, :], -jnp.inf))
            ws.append((qf3 * key3[:, s:s + 1, :] * dec).reshape(c, dk))
        sums = jnp.dot(jnp.concatenate(ws, axis=0).astype(BF16), ones_rhs,
                       preferred_element_type=F32)
        for s in range(sub):
            a = a + jnp.where(col_in_sub == s, sums[s * c:(s + 1) * c, :], 0.0)

        o = o + jnp.dot(a.astype(BF16), v, preferred_element_type=F32)
        on = o * lax.rsqrt(jnp.mean(o * o, axis=-1, keepdims=True) + RMS_EPS) * nw
        o_ref[0, rows, :] = (on * (g * jax.nn.sigmoid(g))).astype(BF16)
        return carry

    lax.fori_loop(0, n_chunks, body, 0)


def _hgrn(qh, fh, ih, gh, lb_logits, norm_w, batch, seq):
    spec = pl.BlockSpec((1, seq, HGRN_HEAD_DIM), lambda b, h: (b, 0, h))
    slots = lb_logits.shape[0]
    shp = (batch, seq, HGRN_WIDTH)
    out = pl.pallas_call(
        functools.partial(_hgrn_kernel, n_chunks=seq // HGRN_CHUNK),
        grid=(batch, HGRN_HEADS),
        in_specs=[spec, spec, spec, spec,
                  pl.BlockSpec((slots, 1, 1, HGRN_HEAD_DIM), lambda b, h: (0, h, 0, 0)),
                  pl.BlockSpec((1, HGRN_HEAD_DIM), lambda b, h: (0, 0))],
        out_specs=spec,
        out_shape=jax.ShapeDtypeStruct(shp, BF16),
        scratch_shapes=[pltpu.VMEM((HGRN_HEAD_DIM, HGRN_HEAD_DIM), F32)],
        compiler_params=_cparams("arbitrary", "arbitrary"),
        name="hgrn2",
    )(qh.reshape(shp), fh.reshape(shp), ih.reshape(shp), gh.reshape(shp),
      lb_logits.astype(F32).reshape(slots, HGRN_HEADS, 1, HGRN_HEAD_DIM),
      norm_w.reshape(1, HGRN_HEAD_DIM))
    return out.reshape(batch * seq, HGRN_WIDTH)


def _outproj_kernel(o1_ref, o2_ref, o3_ref, l1_ref, l2_ref, l3_ref, hg_ref, x_ref, w_ref, nw_ref,
                    rw_ref, rb_ref, h_ref, xn_ref, idx_ref, gate_ref, rank_ref, cnt_ref, run_ref):
    l1, l2, l3 = l1_ref[...], l2_ref[...], l3_ref[...]
    m = jnp.maximum(jnp.maximum(l1, l2), l3)
    e1, e2, e3 = jnp.exp(l1 - m), jnp.exp(l2 - m), jnp.exp(l3 - m)
    attn = (e1 * o1_ref[...] + e2 * o2_ref[...] + e3 * o3_ref[...]) / (e1 + e2 + e3)
    y = jnp.dot(attn.astype(BF16), w_ref[0:ATTN_WIDTH, :], preferred_element_type=F32)
    y = y + jnp.dot(hg_ref[...], w_ref[ATTN_WIDTH:, :], preferred_element_type=F32)
    h = x_ref[...] + y
    h_ref[...] = h
    xn = h * lax.rsqrt(jnp.mean(h * h, axis=-1, keepdims=True) + RMS_EPS) * nw_ref[...]
    xn_ref[...] = xn

    logits = lax.dot_general(rw_ref[...], xn, NT_DIMS, precision=HIGHEST,
                             preferred_element_type=F32) + rb_ref[...]
    eid = lax.broadcasted_iota(jnp.int32, logits.shape, 0)
    vals, idxs = [], []
    for _ in range(TOP_K):
        mx = jnp.max(logits, axis=0, keepdims=True)
        ix = jnp.min(jnp.where(logits == mx, eid, N_EXPERTS), axis=0, keepdims=True)
        vals.append(mx)
        idxs.append(ix)
        logits = jnp.where(eid == ix, -jnp.inf, logits)
    es = [jnp.exp(v - vals[0]) for v in vals]
    den = es[0] + es[1] + es[2] + es[3]
    idx_ref[...] = jnp.concatenate(idxs, axis=0)
    gate_ref[...] = jnp.concatenate([e / den for e in es], axis=0)

    @pl.when(pl.program_id(0) == 0)
    def _():
        run_ref[...] = jnp.zeros_like(run_ref)

    rows = logits.shape[1]
    onehots = [(eid == ix).astype(F32) for ix in idxs]
    routed = onehots[0] + onehots[1] + onehots[2] + onehots[3]
    earlier = (lax.broadcasted_iota(jnp.int32, (rows, rows), 0)
               < lax.broadcasted_iota(jnp.int32, (rows, rows), 1)).astype(BF16)
    before = jnp.dot(routed.astype(BF16), earlier, preferred_element_type=F32) + run_ref[...]
    rank_ref[...] = jnp.concatenate([jnp.sum(o * before, axis=0, keepdims=True) for o in onehots],
                                    axis=0).astype(jnp.int32)
    run_ref[...] = run_ref[...] + jnp.sum(routed, axis=1, keepdims=True)
    cnt_ref[...] = run_ref[...].astype(jnp.int32)


def _outproj(o_l, hg, x2, w_out, norm_w, router_w, router_b):
    n = x2.shape[0]
    tm = OUTPROJ_ROWS
    row = lambda width: pl.BlockSpec((tm, width), lambda i: (i, 0))
    full = lambda a, b: pl.BlockSpec((a, b), lambda i: (0, 0))
    tok = pl.BlockSpec((TOP_K, tm), lambda i: (0, i))
    (o1, l1), (o2, l2), (o3, l3) = o_l
    return pl.pallas_call(
        _outproj_kernel,
        grid=(n // tm,),
        in_specs=[row(ATTN_WIDTH)] * 6 + [row(HGRN_WIDTH), row(D_MODEL), full(D_MODEL, D_MODEL),
                                           full(1, D_MODEL), full(N_EXPERTS, D_MODEL), full(N_EXPERTS, 1)],
        out_specs=[row(D_MODEL), row(D_MODEL), tok, tok, tok, full(N_EXPERTS, 1)],
        out_shape=[jax.ShapeDtypeStruct((n, D_MODEL), F32),
                   jax.ShapeDtypeStruct((n, D_MODEL), F32),
                   jax.ShapeDtypeStruct((TOP_K, n), jnp.int32),
                   jax.ShapeDtypeStruct((TOP_K, n), F32),
                   jax.ShapeDtypeStruct((TOP_K, n), jnp.int32),
                   jax.ShapeDtypeStruct((N_EXPERTS, 1), jnp.int32)],
        scratch_shapes=[pltpu.VMEM((N_EXPERTS, 1), F32)],
        compiler_params=_cparams("arbitrary"),
        name="outproj_router",
    )(o1, o2, o3, l1, l2, l3, hg, x2, w_out.astype(BF16), norm_w.reshape(1, D_MODEL),
      router_w.T.astype(F32), router_b.astype(F32).reshape(N_EXPERTS, 1))


def _ffn_kernel(be_ref, rv_ref, xs_ref, wgu_ref, bgu_ref, wdn_ref, bdn_ref, o_ref, wgu_bf, wdn_bf):
    i = pl.program_id(0)
    rows_valid = rv_ref[i]
    new_expert = (i == 0) | (be_ref[i] != be_ref[jnp.maximum(i - 1, 0)])

    @pl.when(new_expert & (rows_valid > 0))
    def _():
        step = 128
        for r in range(0, D_MODEL, step):
            wgu_bf[r:r + step, :] = wgu_ref[0, r:r + step, :].astype(BF16)
        for r in range(0, D_EXPERT, step):
            wdn_bf[r:r + step, :] = wdn_ref[0, r:r + step, :].astype(BF16)

    @pl.when(rows_valid > 0)
    def _():
        live = lax.broadcasted_iota(jnp.int32, (xs_ref.shape[0], 1), 0) < rows_valid
        x = jnp.where(live, xs_ref[...], 0.0).astype(BF16)
        hgu = jnp.dot(x, wgu_bf[...], preferred_element_type=F32) + bgu_ref[0]
        gate = jnp.minimum(hgu[:, :D_EXPERT], SWIGLU_LIMIT)
        up = jnp.clip(hgu[:, D_EXPERT:], -SWIGLU_LIMIT, SWIGLU_LIMIT)
        act = gate * jax.nn.sigmoid(SWIGLU_ALPHA * gate) * (up + 1.0)
        o_ref[...] = jnp.dot(act.astype(BF16), wdn_bf[...], preferred_element_type=F32) + bdn_ref[0]

    @pl.when(rows_valid <= 0)
    def _():
        o_ref[...] = jnp.zeros_like(o_ref)


def _expert_ffn(block_e, rows_valid, xs, w_gu, b_gu, w_dn, b_dn):
    cap = xs.shape[0]
    tm = FFN_ROWS
    grid_spec = pltpu.PrefetchScalarGridSpec(
        num_scalar_prefetch=2,
        grid=(cap // tm,),
        in_specs=[pl.BlockSpec((tm, D_MODEL), lambda i, be, rv: (i, 0)),
                  pl.BlockSpec((1, D_MODEL, 2 * D_EXPERT), lambda i, be, rv: (be[i], 0, 0)),
                  pl.BlockSpec((1, 1, 2 * D_EXPERT), lambda i, be, rv: (be[i], 0, 0)),
                  pl.BlockSpec((1, D_EXPERT, D_MODEL), lambda i, be, rv: (be[i], 0, 0)),
                  pl.BlockSpec((1, 1, D_MODEL), lambda i, be, rv: (be[i], 0, 0))],
        out_specs=pl.BlockSpec((tm, D_MODEL), lambda i, be, rv: (i, 0)),
        scratch_shapes=[pltpu.VMEM((D_MODEL, 2 * D_EXPERT), BF16), pltpu.VMEM((D_EXPERT, D_MODEL), BF16)],
    )
    return pl.pallas_call(
        _ffn_kernel,
        grid_spec=grid_spec,
        out_shape=jax.ShapeDtypeStruct((cap, D_MODEL), F32),
        compiler_params=pltpu.CompilerParams(dimension_semantics=("arbitrary",),
                                             vmem_limit_bytes=FFN_VMEM_LIMIT),
        name="expert_ffn",
    )(block_e, rows_valid, xs, w_gu, b_gu.reshape(N_EXPERTS, 1, 2 * D_EXPERT),
      w_dn, b_dn.reshape(N_EXPERTS, 1, D_MODEL))


def _pos_kernel(idx_ref, rank_ref, pstart_ref, pos_ref):
    eid = lax.broadcasted_iota(jnp.int32, (N_EXPERTS, idx_ref.shape[1]), 0)
    pstart = pstart_ref[...]
    rows = [jnp.sum(jnp.where(eid == idx_ref[k:k + 1, :], pstart, 0), axis=0, keepdims=True)
            for k in range(TOP_K)]
    pos_ref[...] = jnp.concatenate(rows, axis=0) + rank_ref[...]


def _positions(idx_t, rank_t, pstart):
    n = idx_t.shape[1]
    tl = POS_TOKENS
    tok = pl.BlockSpec((TOP_K, tl), lambda i: (0, i))
    return pl.pallas_call(
        _pos_kernel,
        grid=(n // tl,),
        in_specs=[tok, tok, pl.BlockSpec((N_EXPERTS, 1), lambda i: (0, 0))],
        out_specs=tok,
        out_shape=jax.ShapeDtypeStruct((TOP_K, n), jnp.int32),
        compiler_params=_cparams("arbitrary"),
        name="dispatch_pos",
    )(idx_t, rank_t, pstart.reshape(N_EXPERTS, 1))


def _tile_positions(pos, tokens):
    n = pos.shape[1]
    return pos.reshape(TOP_K, n // tokens, tokens).transpose(1, 0, 2)


def _dispatch_kernel(pos_ref, x_ref, xs_ref, sem):
    tokens = x_ref.shape[0]

    def copy(t, k):
        return pltpu.make_async_copy(x_ref.at[pl.ds(t, 1)], xs_ref.at[pl.ds(pos_ref[0, k, t], 1)], sem)

    def issue(t, carry):
        for k in range(TOP_K):
            copy(t, k).start()
        return carry

    def drain(t, carry):
        for k in range(TOP_K):
            copy(t, k).wait()
        return carry

    lax.fori_loop(0, tokens, issue, 0, unroll=MOVE_UNROLL)
    lax.fori_loop(0, tokens, drain, 0, unroll=MOVE_UNROLL)


def _dispatch_rows(pos, xn, out_rows):
    n = xn.shape[0]
    tl = MOVE_TOKENS
    return pl.pallas_call(
        _dispatch_kernel,
        grid=(n // tl,),
        in_specs=[pl.BlockSpec((1, TOP_K, tl), lambda i: (i, 0, 0), memory_space=pltpu.SMEM),
                  pl.BlockSpec((tl, D_MODEL), lambda i: (i, 0))],
        out_specs=pl.BlockSpec(memory_space=pl.ANY),
        out_shape=jax.ShapeDtypeStruct((out_rows, D_MODEL), xn.dtype),
        scratch_shapes=[pltpu.SemaphoreType.DMA(())],
        compiler_params=_cparams("arbitrary"),
        name="dispatch_rows",
    )(_tile_positions(pos, tl), xn)


def _combine_kernel(pos_ref, pos_next_ref, h_ref, g_ref, fw_ref, ys_ref, o_ref, buf, sems):
    i = pl.program_id(0)
    steps = pl.num_programs(0)
    tokens = h_ref.shape[0]
    slot = lax.rem(i, 2)

    def gather(p_ref, s, start):
        def body(t, carry):
            for k in range(TOP_K):
                cp = pltpu.make_async_copy(ys_ref.at[pl.ds(p_ref[0, k, t], 1)],
                                           buf.at[s, k, pl.ds(t, 1)], sems.at[s])
                cp.start() if start else cp.wait()
            return carry
        lax.fori_loop(0, tokens, body, 0, unroll=MOVE_UNROLL)

    @pl.when(i == 0)
    def _():
        gather(pos_ref, 0, True)

    @pl.when(i + 1 < steps)
    def _():
        gather(pos_next_ref, 1 - slot, True)

    gather(pos_ref, slot, False)
    g = g_ref[...]
    y = h_ref[...]
    for k in range(TOP_K):
        y = y + buf[slot, k] * g[:, k:k + 1]
    o_ref[...] = y * lax.rsqrt(jnp.mean(y * y, axis=-1, keepdims=True) + RMS_EPS) * fw_ref[...]


def _combine(h, ys, pos, gates_nk, final_w):
    n = h.shape[0]
    tm = COMBINE_ROWS
    steps = n // tm
    pos_t = _tile_positions(pos, tm)
    smem = lambda imap: pl.BlockSpec((1, TOP_K, tm), imap, memory_space=pltpu.SMEM)
    return pl.pallas_call(
        _combine_kernel,
        grid=(steps,),
        in_specs=[smem(lambda i: (i, 0, 0)),
                  smem(lambda i: (jnp.minimum(i + 1, steps - 1), 0, 0)),
                  pl.BlockSpec((tm, D_MODEL), lambda i: (i, 0)),
                  pl.BlockSpec((tm, TOP_K), lambda i: (i, 0)),
                  pl.BlockSpec((1, D_MODEL), lambda i: (0, 0)),
                  pl.BlockSpec(memory_space=pl.ANY)],
        out_specs=pl.BlockSpec((tm, D_MODEL), lambda i: (i, 0)),
        out_shape=jax.ShapeDtypeStruct((n, D_MODEL), F32),
        scratch_shapes=[pltpu.VMEM((2, TOP_K, tm, D_MODEL), F32), pltpu.SemaphoreType.DMA((2,))],
        compiler_params=_cparams("arbitrary"),
        name="combine_norm",
    )(pos_t, pos_t, h, gates_nk, final_w.reshape(1, D_MODEL), ys)


def kernel(x, norm_mix_w, w_in, rel_bias, hgrn_lb_logits, hgrn_norm_w, w_out, norm_ffn_w,
           router_w, router_b, w_gate_up, b_gate_up, w_down, b_down, final_norm_w):
    batch, seq, _ = x.shape
    n = batch * seq
    x2 = x.reshape(n, D_MODEL)

    qkv, qh, fh, ih, gh = _inproj(x2, norm_mix_w[0], w_in[0])
    o_l = [_dilated_attention(qkv, _masked_bias(rel_bias, w, d), batch, seq, d)
           for w, d in DILATION_PATTERNS]
    hg = _hgrn(qh, fh, ih, gh, hgrn_lb_logits, hgrn_norm_w[0], batch, seq)
    h, xn, idx_t, gate_t, rank_t, counts = _outproj(o_l, hg, x2, w_out[0], norm_ffn_w[0],
                                                    router_w[0], router_b[0])

    tm = FFN_ROWS
    n_blocks = -(-(n * TOP_K) // tm) + N_EXPERTS
    counts = counts.reshape(N_EXPERTS)
    padded = (counts + tm - 1) // tm * tm
    pend = jnp.cumsum(padded)
    pstart = pend - padded
    block_row0 = jnp.arange(n_blocks, dtype=jnp.int32) * tm
    in_block = (block_row0[:, None] >= pstart[None, :]) & (block_row0[:, None] < pend[None, :])
    block_e = jnp.minimum(jnp.sum(pend[None, :] <= block_row0[:, None], axis=1), N_EXPERTS - 1).astype(jnp.int32)
    rows_valid = jnp.sum(jnp.where(in_block, jnp.clip(pstart + counts - block_row0[:, None], 0, tm), 0),
                         axis=1).astype(jnp.int32)

    pos = _positions(idx_t, rank_t, pstart.astype(jnp.int32))
    xs = _dispatch_rows(pos, xn, n_blocks * tm)
    ys = _expert_ffn(block_e, rows_valid, xs, w_gate_up[0], b_gate_up[0], w_down[0], b_down[0])
    out = _combine(h, ys, pos, gate_t.T, final_norm_w)
    return out.reshape(batch, seq, D_MODEL)
```

```python
import functools

import numpy as np
import jax
import jax.numpy as jnp
from jax import lax
from jax.experimental import pallas as pl
from jax.experimental.pallas import tpu as pltpu

F32 = jnp.float32
BF16 = jnp.bfloat16
HIGHEST = lax.Precision.HIGHEST

D_MODEL = 1024
ATTN_HEAD_DIM = 64
ATTN_WIDTH = 512
ATTN_HEADS = ATTN_WIDTH // ATTN_HEAD_DIM
DILATION_PATTERNS = ((128, 1), (512, 4), (2048, 16))
ATTN_BLOCK = 128
REL_BUCKETS = 32
REL_MAX_DISTANCE = 2048
HGRN_HEAD_DIM = 128
HGRN_WIDTH = 512
HGRN_HEADS = HGRN_WIDTH // HGRN_HEAD_DIM
HGRN_CHUNK = 64
HGRN_SUB = 8
N_EXPERTS = 32
TOP_K = 4
D_EXPERT = 1024
SWIGLU_LIMIT = 7.0
SWIGLU_ALPHA = 1.702
RMS_EPS = 1e-5
IN_PROJ_WIDTH = 3 * ATTN_WIDTH + 4 * HGRN_WIDTH
LANES = 128
HEAD_PAIRS = ATTN_WIDTH // LANES

INPROJ_ROWS = 512
OUTPROJ_ROWS = 256
FFN_ROWS = 256
COMBINE_ROWS = 256
ATTN_UNROLL = 4
POS_TOKENS = 2048
MOVE_TOKENS = 512
MOVE_UNROLL = 8
VMEM_LIMIT = 48 * 1024 * 1024
FFN_VMEM_LIMIT = 56 * 1024 * 1024

NT_DIMS = (((1,), (1,)), ((), ()))
TN_DIMS = (((0,), (0,)), ((), ()))


def _cparams(*sem):
    return pltpu.CompilerParams(dimension_semantics=sem, vmem_limit_bytes=VMEM_LIMIT)


def _inproj_kernel(x_ref, nw_ref, w_ref, qa_ref, ka_ref, va_ref, qh_ref, fh_ref, ih_ref, gh_ref):
    x = x_ref[...]
    xn = x * lax.rsqrt(jnp.mean(x * x, axis=-1, keepdims=True) + RMS_EPS) * nw_ref[...]
    xn = xn.astype(BF16)
    for c, ref in enumerate((qa_ref, ka_ref, va_ref, qh_ref, fh_ref, ih_ref, gh_ref)):
        y = jnp.dot(xn, w_ref[:, c * ATTN_WIDTH:(c + 1) * ATTN_WIDTH], preferred_element_type=F32)
        ref[...] = y.astype(ref.dtype)


def _inproj(x2, norm_w, w_in):
    n = x2.shape[0]
    tm = INPROJ_ROWS
    row = lambda width: pl.BlockSpec((tm, width), lambda i: (i, 0))
    dtypes = (F32, F32, F32, BF16, F32, BF16, F32)
    return pl.pallas_call(
        _inproj_kernel,
        grid=(n // tm,),
        in_specs=[row(D_MODEL),
                  pl.BlockSpec((1, D_MODEL), lambda i: (0, 0)),
                  pl.BlockSpec((D_MODEL, IN_PROJ_WIDTH), lambda i: (0, 0))],
        out_specs=[row(ATTN_WIDTH)] * 7,
        out_shape=[jax.ShapeDtypeStruct((n, ATTN_WIDTH), dt) for dt in dtypes],
        compiler_params=_cparams("arbitrary"),
        name="inproj",
    )(x2, norm_w.reshape(1, D_MODEL), w_in.astype(BF16))


def _t5_causal_bucket(dist):
    n = np.maximum(dist, 0)
    max_exact = REL_BUCKETS // 2
    large = max_exact + (np.log(np.maximum(n, 1) / max_exact)
                         / np.log(REL_MAX_DISTANCE / max_exact)
                         * (REL_BUCKETS - max_exact)).astype(np.int32)
    large = np.minimum(large, REL_BUCKETS - 1)
    return np.where(n < max_exact, n, large).astype(np.int32)


def _bucket_thresholds(max_dist):
    buckets = _t5_causal_bucket(np.arange(max_dist + 1))
    assert np.all(np.diff(buckets) >= 0)
    return [(b, int(np.argmax(buckets >= b))) for b in range(1, REL_BUCKETS) if np.any(buckets >= b)]


def _bias_kernel(rel_ref, o_ref):
    blk = ATTN_BLOCK
    h = pl.program_id(0)
    qi = lax.broadcasted_iota(jnp.int32, (blk, 2 * blk), 0)
    kj = lax.broadcasted_iota(jnp.int32, (blk, 2 * blk), 1)
    dist_sub = blk + qi - kj
    for pi, (window, dilation) in enumerate(DILATION_PATTERNS):
        span = window // dilation
        dist = dist_sub * dilation
        val = jnp.full((blk, 2 * blk), rel_ref[h], F32)
        for b, first_dist in _bucket_thresholds(window):
            val = jnp.where(dist >= first_dist, rel_ref[b * ATTN_HEADS + h], val)
        o_ref[pi, 0] = jnp.where((dist_sub >= 0) & (dist_sub <= span), val, -1e30)


def _masked_bias(rel_bias):
    blk = ATTN_BLOCK
    npat = len(DILATION_PATTERNS)
    bias = pl.pallas_call(
        _bias_kernel,
        grid=(ATTN_HEADS,),
        in_specs=[pl.BlockSpec(memory_space=pltpu.SMEM)],
        out_specs=pl.BlockSpec((npat, 1, blk, 2 * blk), lambda h: (0, h, 0, 0)),
        out_shape=jax.ShapeDtypeStruct((npat, ATTN_HEADS, blk, 2 * blk), F32),
        compiler_params=_cparams("arbitrary"),
        name="rel_bias",
    )(rel_bias.astype(F32).reshape(REL_BUCKETS * ATTN_HEADS))
    return bias.reshape(npat, HEAD_PAIRS, 2 * blk, 2 * blk)


def _attn_kernel(q_ref, k_ref, v_ref, bias_ref, o_ref, acc_ref, lse_ref, *, seq):
    blk = ATTN_BLOCK
    lane = lax.broadcasted_iota(jnp.int32, (1, LANES), 1)
    first = lane < ATTN_HEAD_DIM
    scale = ATTN_HEAD_DIM ** -0.5
    sel0 = jnp.where(first, scale, 0.0)
    sel1 = jnp.where(first, 0.0, scale)
    ones_cols = jnp.ones((2 * blk, LANES), BF16)

    def one_block(pi, dil, q_start, k_start, n_keys):
        qb = q_ref[0, pl.ds(q_start, blk, stride=dil), :]
        q2 = jnp.concatenate([qb * sel0, qb * sel1], axis=0).astype(BF16)
        kk = k_ref[0, pl.ds(k_start, n_keys, stride=dil), :].astype(BF16)
        vv = v_ref[0, pl.ds(k_start, n_keys, stride=dil), :].astype(BF16)
        s = lax.dot_general(q2, kk, NT_DIMS, preferred_element_type=F32)
        s = s + bias_ref[pi, 0, :, 2 * blk - n_keys:]
        m = jnp.max(s, axis=-1, keepdims=True)
        p = jnp.exp(s - m).astype(BF16)
        ov = jnp.dot(p, jnp.concatenate([vv, ones_cols[:n_keys]], axis=1), preferred_element_type=F32)
        z = ov[:, LANES:]
        o = ov[:, :LANES] / z
        lse = m + jnp.log(z)
        rows = pl.ds(q_start, blk, stride=dil)
        acc_ref[pi, rows, :] = jnp.where(first, o[:blk], o[blk:])
        lse_ref[pi, rows, :] = jnp.where(first, lse[:blk], lse[blk:])

    for pi, (_, dil) in enumerate(DILATION_PATTERNS):
        nb = seq // (dil * blk)

        def residue(r, carry, pi=pi, dil=dil, nb=nb):
            one_block(pi, dil, r, r, blk)

            def body(j, c):
                q_start = r + j * (blk * dil)
                one_block(pi, dil, q_start, q_start - blk * dil, 2 * blk)
                return c

            return lax.fori_loop(1, nb, body, carry, unroll=ATTN_UNROLL)

        lax.fori_loop(0, dil, residue, 0, unroll=ATTN_UNROLL if nb == 2 else 1)

    def merge(i, carry):
        rows = pl.ds(pl.multiple_of(i * blk, blk), blk)
        ls = [lse_ref[pi, rows, :] for pi in range(len(DILATION_PATTERNS))]
        m = functools.reduce(jnp.maximum, ls)
        es = [jnp.exp(l - m) for l in ls]
        num = sum(e * acc_ref[pi, rows, :] for pi, e in enumerate(es))
        o_ref[0, rows, :] = (num / sum(es)).astype(o_ref.dtype)
        return carry

    lax.fori_loop(0, seq // blk, merge, 0)


def _dilated_attention(qa, ka, va, bias, batch, seq):
    npat = len(DILATION_PATTERNS)
    shp = (batch, seq, ATTN_WIDTH)
    spec = pl.BlockSpec((1, seq, LANES), lambda b, p: (b, 0, p))
    out = pl.pallas_call(
        functools.partial(_attn_kernel, seq=seq),
        grid=(batch, HEAD_PAIRS),
        in_specs=[spec, spec, spec,
                  pl.BlockSpec((npat, 1, 2 * ATTN_BLOCK, 2 * ATTN_BLOCK), lambda b, p: (0, p, 0, 0))],
        out_specs=spec,
        out_shape=jax.ShapeDtypeStruct(shp, BF16),
        scratch_shapes=[pltpu.VMEM((npat, seq, LANES), F32), pltpu.VMEM((npat, seq, LANES), F32)],
        compiler_params=_cparams("arbitrary", "arbitrary"),
        name="dilated_attn",
    )(qa.reshape(shp), ka.reshape(shp), va.reshape(shp), bias)
    return out.reshape(batch * seq, ATTN_WIDTH)


def _hgrn_kernel(q_ref, f_ref, i_ref, g_ref, lbl_ref, nw_ref, o_ref, st_ref, *, n_chunks):
    c, sub, dk = HGRN_CHUNK, HGRN_SUB, HGRN_HEAD_DIM
    nsub = c // sub
    lbl = lbl_ref[...]
    e = jnp.exp(lbl - jnp.max(lbl, axis=0, keepdims=True))
    lb = (e[0] / jnp.sum(e, axis=0))[0]
    nw = nw_ref[...]

    r64 = lax.broadcasted_iota(jnp.int32, (c, c), 0)
    c64 = lax.broadcasted_iota(jnp.int32, (c, c), 1)
    tril = (r64 >= c64).astype(F32)
    row_sub = r64 // sub
    col_sub = c64 // sub
    col_in_sub = c64 - row_sub * sub
    t_iota = lax.broadcasted_iota(jnp.int32, (nsub, sub, dk), 1)
    ones_rhs = jnp.ones((dk, c), BF16)

    st_ref[...] = jnp.zeros_like(st_ref)

    def body(ci, carry):
        r0 = pl.multiple_of(ci * c, c)
        rows = pl.ds(r0, c)
        q = q_ref[0, rows, :].astype(F32)
        v = i_ref[0, rows, :]
        g = g_ref[0, rows, :]
        f = lb + (1.0 - lb) * jax.nn.sigmoid(f_ref[0, rows, :])
        key = 1.0 - f
        cum = jnp.dot(tril, jnp.log(f), precision=HIGHEST, preferred_element_type=F32)
        qf = q * jax.nn.sigmoid(q) * (dk ** -0.5)
        last = cum[c - 1:c, :]
        st = st_ref[...]

        o = lax.dot_general((qf * jnp.exp(cum)).astype(BF16), st.astype(BF16), NT_DIMS,
                            preferred_element_type=F32)
        kd = (key * jnp.exp(last - cum)).astype(BF16)
        st_ref[...] = st * jnp.exp(last) + lax.dot_general(v, kd, TN_DIMS, preferred_element_type=F32)

        cum3 = cum.reshape(nsub, sub, dk)
        key3 = key.reshape(nsub, sub, dk)
        qf3 = qf.reshape(nsub, sub, dk)
        khat = (key3 * jnp.exp(cum3[:, sub - 1:sub, :] - cum3)).reshape(c, dk).astype(BF16)
        qs = [qf * jnp.exp(jnp.minimum(cum - cum[sub * j + sub - 1:sub * j + sub, :], 0.0))
              for j in range(nsub - 1)]
        prod = lax.dot_general(jnp.concatenate(qs, axis=0).astype(BF16), khat, NT_DIMS,
                               preferred_element_type=F32)
        a = jnp.zeros((c, c), F32)
        for j in range(nsub - 1):
            a = a + jnp.where((col_sub == j) & (row_sub > j), prod[j * c:(j + 1) * c, :], 0.0)

        ws = []
        for s in range(sub):
            dec = jnp.exp(jnp.where(t_iota >= s, cum3 - cum3[:, s:s + 1, :], -jnp.inf))
            ws.append((qf3 * key3[:, s:s + 1, :] * dec).reshape(c, dk))
        sums = jnp.dot(jnp.concatenate(ws, axis=0).astype(BF16), ones_rhs,
                       preferred_element_type=F32)
        for s in range(sub):
            a = a + jnp.where(col_in_sub == s, sums[s * c:(s + 1) * c, :], 0.0)

        o = o + jnp.dot(a.astype(BF16), v, preferred_element_type=F32)
        on = o * lax.rsqrt(jnp.mean(o * o, axis=-1, keepdims=True) + RMS_EPS) * nw
        o_ref[0, rows, :] = (on * (g * jax.nn.sigmoid(g))).astype(BF16)
        return carry

    lax.fori_loop(0, n_chunks, body, 0)


def _hgrn(qh, fh, ih, gh, lb_logits, norm_w, batch, seq):
    spec = pl.BlockSpec((1, seq, HGRN_HEAD_DIM), lambda b, h: (b, 0, h))
    slots = lb_logits.shape[0]
    shp = (batch, seq, HGRN_WIDTH)
    out = pl.pallas_call(
        functools.partial(_hgrn_kernel, n_chunks=seq // HGRN_CHUNK),
        grid=(batch, HGRN_HEADS),
        in_specs=[spec, spec, spec, spec,
                  pl.BlockSpec((slots, 1, 1, HGRN_HEAD_DIM), lambda b, h: (0, h, 0, 0)),
                  pl.BlockSpec((1, HGRN_HEAD_DIM), lambda b, h: (0, 0))],
        out_specs=spec,
        out_shape=jax.ShapeDtypeStruct(shp, BF16),
        scratch_shapes=[pltpu.VMEM((HGRN_HEAD_DIM, HGRN_HEAD_DIM), F32)],
        compiler_params=_cparams("arbitrary", "arbitrary"),
        name="hgrn2",
    )(qh.reshape(shp), fh.reshape(shp), ih.reshape(shp), gh.reshape(shp),
      lb_logits.astype(F32).reshape(slots, HGRN_HEADS, 1, HGRN_HEAD_DIM),
      norm_w.reshape(1, HGRN_HEAD_DIM))
    return out.reshape(batch * seq, HGRN_WIDTH)


def _outproj_kernel(attn_ref, hg_ref, x_ref, w_ref, nw_ref, rw_ref, rb_ref,
                    h_ref, xn_ref, idx_ref, gate_ref, rank_ref, cnt_ref, run_ref):
    y = jnp.dot(attn_ref[...], w_ref[0:ATTN_WIDTH, :], preferred_element_type=F32)
    y = y + jnp.dot(hg_ref[...], w_ref[ATTN_WIDTH:, :], preferred_element_type=F32)
    h = x_ref[...] + y
    h_ref[...] = h
    xn = h * lax.rsqrt(jnp.mean(h * h, axis=-1, keepdims=True) + RMS_EPS) * nw_ref[...]
    xn_ref[...] = xn

    logits = lax.dot_general(rw_ref[...], xn, NT_DIMS, precision=HIGHEST,
                             preferred_element_type=F32) + rb_ref[...]
    eid = lax.broadcasted_iota(jnp.int32, logits.shape, 0)
    vals, idxs = [], []
    for _ in range(TOP_K):
        mx = jnp.max(logits, axis=0, keepdims=True)
        ix = jnp.min(jnp.where(logits == mx, eid, N_EXPERTS), axis=0, keepdims=True)
        vals.append(mx)
        idxs.append(ix)
        logits = jnp.where(eid == ix, -jnp.inf, logits)
    es = [jnp.exp(v - vals[0]) for v in vals]
    den = es[0] + es[1] + es[2] + es[3]
    idx_ref[...] = jnp.concatenate(idxs, axis=0)
    gate_ref[...] = jnp.concatenate([e / den for e in es], axis=0)

    @pl.when(pl.program_id(0) == 0)
    def _():
        run_ref[...] = jnp.zeros_like(run_ref)

    rows = logits.shape[1]
    onehots = [(eid == ix).astype(F32) for ix in idxs]
    routed = onehots[0] + onehots[1] + onehots[2] + onehots[3]
    earlier = (lax.broadcasted_iota(jnp.int32, (rows, rows), 0)
               < lax.broadcasted_iota(jnp.int32, (rows, rows), 1)).astype(BF16)
    before = jnp.dot(routed.astype(BF16), earlier, preferred_element_type=F32) + run_ref[...]
    rank_ref[...] = jnp.concatenate([jnp.sum(o * before, axis=0, keepdims=True) for o in onehots],
                                    axis=0).astype(jnp.int32)
    run_ref[...] = run_ref[...] + jnp.sum(routed, axis=1, keepdims=True)
    cnt_ref[...] = run_ref[...].astype(jnp.int32)


def _outproj(attn, hg, x2, w_out, norm_w, router_w, router_b):
    n = x2.shape[0]
    tm = OUTPROJ_ROWS
    row = lambda width: pl.BlockSpec((tm, width), lambda i: (i, 0))
    full = lambda a, b: pl.BlockSpec((a, b), lambda i: (0, 0))
    tok = pl.BlockSpec((TOP_K, tm), lambda i: (0, i))
    return pl.pallas_call(
        _outproj_kernel,
        grid=(n // tm,),
        in_specs=[row(ATTN_WIDTH), row(HGRN_WIDTH), row(D_MODEL), full(D_MODEL, D_MODEL),
                  full(1, D_MODEL), full(N_EXPERTS, D_MODEL), full(N_EXPERTS, 1)],
        out_specs=[row(D_MODEL), row(D_MODEL), tok, tok, tok, full(N_EXPERTS, 1)],
        out_shape=[jax.ShapeDtypeStruct((n, D_MODEL), F32),
                   jax.ShapeDtypeStruct((n, D_MODEL), F32),
                   jax.ShapeDtypeStruct((TOP_K, n), jnp.int32),
                   jax.ShapeDtypeStruct((TOP_K, n), F32),
                   jax.ShapeDtypeStruct((TOP_K, n), jnp.int32),
                   jax.ShapeDtypeStruct((N_EXPERTS, 1), jnp.int32)],
        scratch_shapes=[pltpu.VMEM((N_EXPERTS, 1), F32)],
        compiler_params=_cparams("arbitrary"),
        name="outproj_router",
    )(attn, hg, x2, w_out.astype(BF16), norm_w.reshape(1, D_MODEL),
      router_w.T.astype(F32), router_b.astype(F32).reshape(N_EXPERTS, 1))


def _ffn_kernel(be_ref, rv_ref, xs_ref, wgu_ref, bgu_ref, wdn_ref, bdn_ref, o_ref, wgu_bf, wdn_bf):
    i = pl.program_id(0)
    rows_valid = rv_ref[i]
    new_expert = (i == 0) | (be_ref[i] != be_ref[jnp.maximum(i - 1, 0)])

    @pl.when(new_expert & (rows_valid > 0))
    def _():
        step = 128
        for r in range(0, D_MODEL, step):
            wgu_bf[r:r + step, :] = wgu_ref[0, r:r + step, :].astype(BF16)
        for r in range(0, D_EXPERT, step):
            wdn_bf[r:r + step, :] = wdn_ref[0, r:r + step, :].astype(BF16)

    @pl.when(rows_valid > 0)
    def _():
        live = lax.broadcasted_iota(jnp.int32, (xs_ref.shape[0], 1), 0) < rows_valid
        x = jnp.where(live, xs_ref[...], 0.0).astype(BF16)
        hgu = jnp.dot(x, wgu_bf[...], preferred_element_type=F32) + bgu_ref[0]
        gate = jnp.minimum(hgu[:, :D_EXPERT], SWIGLU_LIMIT)
        up = jnp.clip(hgu[:, D_EXPERT:], -SWIGLU_LIMIT, SWIGLU_LIMIT)
        act = gate * jax.nn.sigmoid(SWIGLU_ALPHA * gate) * (up + 1.0)
        o_ref[...] = jnp.dot(act.astype(BF16), wdn_bf[...], preferred_element_type=F32) + bdn_ref[0]

    @pl.when(rows_valid <= 0)
    def _():
        o_ref[...] = jnp.zeros_like(o_ref)


def _expert_ffn(block_e, rows_valid, xs, w_gu, b_gu, w_dn, b_dn):
    cap = xs.shape[0]
    tm = FFN_ROWS
    grid_spec = pltpu.PrefetchScalarGridSpec(
        num_scalar_prefetch=2,
        grid=(cap // tm,),
        in_specs=[pl.BlockSpec((tm, D_MODEL), lambda i, be, rv: (i, 0)),
                  pl.BlockSpec((1, D_MODEL, 2 * D_EXPERT), lambda i, be, rv: (be[i], 0, 0)),
                  pl.BlockSpec((1, 1, 2 * D_EXPERT), lambda i, be, rv: (be[i], 0, 0)),
                  pl.BlockSpec((1, D_EXPERT, D_MODEL), lambda i, be, rv: (be[i], 0, 0)),
                  pl.BlockSpec((1, 1, D_MODEL), lambda i, be, rv: (be[i], 0, 0))],
        out_specs=pl.BlockSpec((tm, D_MODEL), lambda i, be, rv: (i, 0)),
        scratch_shapes=[pltpu.VMEM((D_MODEL, 2 * D_EXPERT), BF16), pltpu.VMEM((D_EXPERT, D_MODEL), BF16)],
    )
    return pl.pallas_call(
        _ffn_kernel,
        grid_spec=grid_spec,
        out_shape=jax.ShapeDtypeStruct((cap, D_MODEL), F32),
        compiler_params=pltpu.CompilerParams(dimension_semantics=("arbitrary",),
                                             vmem_limit_bytes=FFN_VMEM_LIMIT),
        name="expert_ffn",
    )(block_e, rows_valid, xs, w_gu, b_gu.reshape(N_EXPERTS, 1, 2 * D_EXPERT),
      w_dn, b_dn.reshape(N_EXPERTS, 1, D_MODEL))


def _pos_kernel(idx_ref, rank_ref, pstart_ref, pos_ref):
    eid = lax.broadcasted_iota(jnp.int32, (N_EXPERTS, idx_ref.shape[1]), 0)
    pstart = pstart_ref[...]
    rows = [jnp.sum(jnp.where(eid == idx_ref[k:k + 1, :], pstart, 0), axis=0, keepdims=True)
            for k in range(TOP_K)]
    pos_ref[...] = jnp.concatenate(rows, axis=0) + rank_ref[...]


def _positions(idx_t, rank_t, pstart):
    n = idx_t.shape[1]
    tl = POS_TOKENS
    tok = pl.BlockSpec((TOP_K, tl), lambda i: (0, i))
    return pl.pallas_call(
        _pos_kernel,
        grid=(n // tl,),
        in_specs=[tok, tok, pl.BlockSpec((N_EXPERTS, 1), lambda i: (0, 0))],
        out_specs=tok,
        out_shape=jax.ShapeDtypeStruct((TOP_K, n), jnp.int32),
        compiler_params=_cparams("arbitrary"),
        name="dispatch_pos",
    )(idx_t, rank_t, pstart.reshape(N_EXPERTS, 1))


def _tile_positions(pos, tokens):
    n = pos.shape[1]
    return pos.reshape(TOP_K, n // tokens, tokens).transpose(1, 0, 2)


def _dispatch_kernel(pos_ref, x_ref, xs_ref, sem):
    tokens = x_ref.shape[0]

    def copy(t, k):
        return pltpu.make_async_copy(x_ref.at[pl.ds(t, 1)], xs_ref.at[pl.ds(pos_ref[0, k, t], 1)], sem)

    def issue(t, carry):
        for k in range(TOP_K):
            copy(t, k).start()
        return carry

    def drain(t, carry):
        for k in range(TOP_K):
            copy(t, k).wait()
        return carry

    lax.fori_loop(0, tokens, issue, 0, unroll=MOVE_UNROLL)
    lax.fori_loop(0, tokens, drain, 0, unroll=MOVE_UNROLL)


def _dispatch_rows(pos, xn, out_rows):
    n = xn.shape[0]
    tl = MOVE_TOKENS
    return pl.pallas_call(
        _dispatch_kernel,
        grid=(n // tl,),
        in_specs=[pl.BlockSpec((1, TOP_K, tl), lambda i: (i, 0, 0), memory_space=pltpu.SMEM),
                  pl.BlockSpec((tl, D_MODEL), lambda i: (i, 0))],
        out_specs=pl.BlockSpec(memory_space=pl.ANY),
        out_shape=jax.ShapeDtypeStruct((out_rows, D_MODEL), xn.dtype),
        scratch_shapes=[pltpu.SemaphoreType.DMA(())],
        compiler_params=_cparams("arbitrary"),
        name="dispatch_rows",
    )(_tile_positions(pos, tl), xn)


def _combine_kernel(pos_ref, pos_next_ref, h_ref, g_ref, fw_ref, ys_ref, o_ref, buf, sems):
    i = pl.program_id(0)
    steps = pl.num_programs(0)
    tokens = h_ref.shape[0]
    slot = lax.rem(i, 2)

    def gather(p_ref, s, start):
        def body(t, carry):
            for k in range(TOP_K):
                cp = pltpu.make_async_copy(ys_ref.at[pl.ds(p_ref[0, k, t], 1)],
                                           buf.at[s, k, pl.ds(t, 1)], sems.at[s])
                cp.start() if start else cp.wait()
            return carry
        lax.fori_loop(0, tokens, body, 0, unroll=MOVE_UNROLL)

    @pl.when(i == 0)
    def _():
        gather(pos_ref, 0, True)

    @pl.when(i + 1 < steps)
    def _():
        gather(pos_next_ref, 1 - slot, True)

    gather(pos_ref, slot, False)
    g = g_ref[...]
    y = h_ref[...]
    for k in range(TOP_K):
        y = y + buf[slot, k] * g[:, k:k + 1]
    o_ref[...] = y * lax.rsqrt(jnp.mean(y * y, axis=-1, keepdims=True) + RMS_EPS) * fw_ref[...]


def _combine(h, ys, pos, gates_nk, final_w):
    n = h.shape[0]
    tm = COMBINE_ROWS
    steps = n // tm
    pos_t = _tile_positions(pos, tm)
    smem = lambda imap: pl.BlockSpec((1, TOP_K, tm), imap, memory_space=pltpu.SMEM)
    return pl.pallas_call(
        _combine_kernel,
        grid=(steps,),
        in_specs=[smem(lambda i: (i, 0, 0)),
                  smem(lambda i: (jnp.minimum(i + 1, steps - 1), 0, 0)),
                  pl.BlockSpec((tm, D_MODEL), lambda i: (i, 0)),
                  pl.BlockSpec((tm, TOP_K), lambda i: (i, 0)),
                  pl.BlockSpec((1, D_MODEL), lambda i: (0, 0)),
                  pl.BlockSpec(memory_space=pl.ANY)],
        out_specs=pl.BlockSpec((tm, D_MODEL), lambda i: (i, 0)),
        out_shape=jax.ShapeDtypeStruct((n, D_MODEL), F32),
        scratch_shapes=[pltpu.VMEM((2, TOP_K, tm, D_MODEL), F32), pltpu.SemaphoreType.DMA((2,))],
        compiler_params=_cparams("arbitrary"),
        name="combine_norm",
    )(pos_t, pos_t, h, gates_nk, final_w.reshape(1, D_MODEL), ys)


def kernel(x, norm_mix_w, w_in, rel_bias, hgrn_lb_logits, hgrn_norm_w, w_out, norm_ffn_w,
           router_w, router_b, w_gate_up, b_gate_up, w_down, b_down, final_norm_w):
    batch, seq, _ = x.shape
    n = batch * seq
    x2 = x.reshape(n, D_MODEL)

    qa, ka, va, qh, fh, ih, gh = _inproj(x2, norm_mix_w[0], w_in[0])
    attn = _dilated_attention(qa, ka, va, _masked_bias(rel_bias), batch, seq)
    hg = _hgrn(qh, fh, ih, gh, hgrn_lb_logits, hgrn_norm_w[0], batch, seq)
    h, xn, idx_t, gate_t, rank_t, counts = _outproj(attn, hg, x2, w_out[0], norm_ffn_w[0],
                                                    router_w[0], router_b[0])

    tm = FFN_ROWS
    n_blocks = -(-(n * TOP_K) // tm) + N_EXPERTS
    counts = counts.reshape(N_EXPERTS)
    padded = (counts + tm - 1) // tm * tm
    pend = jnp.cumsum(padded)
    pstart = pend - padded
    block_row0 = jnp.arange(n_blocks, dtype=jnp.int32) * tm
    in_block = (block_row0[:, None] >= pstart[None, :]) & (block_row0[:, None] < pend[None, :])
    block_e = jnp.minimum(jnp.sum(pend[None, :] <= block_row0[:, None], axis=1), N_EXPERTS - 1).astype(jnp.int32)
    rows_valid = jnp.sum(jnp.where(in_block, jnp.clip(pstart + counts - block_row0[:, None], 0, tm), 0),
                         axis=1).astype(jnp.int32)

    pos = _positions(idx_t, rank_t, pstart.astype(jnp.int32))
    xs = _dispatch_rows(pos, xn, n_blocks * tm)
    ys = _expert_ffn(block_e, rows_valid, xs, w_gate_up[0], b_gate_up[0], w_down[0], b_down[0])
    out = _combine(h, ys, pos, gate_t.T, final_norm_w)
    return out.reshape(batch, seq, D_MODEL)
```

```python
import functools

import numpy as np
import jax
import jax.numpy as jnp
from jax import lax
from jax.experimental import pallas as pl
from jax.experimental.pallas import tpu as pltpu

F32 = jnp.float32
BF16 = jnp.bfloat16
HIGHEST = lax.Precision.HIGHEST

D_MODEL = 1024
ATTN_HEAD_DIM = 64
ATTN_WIDTH = 512
ATTN_HEADS = ATTN_WIDTH // ATTN_HEAD_DIM
DILATION_PATTERNS = ((128, 1), (512, 4), (2048, 16))
ATTN_BLOCK = 128
REL_BUCKETS = 32
REL_MAX_DISTANCE = 2048
HGRN_HEAD_DIM = 128
HGRN_WIDTH = 512
HGRN_HEADS = HGRN_WIDTH // HGRN_HEAD_DIM
HGRN_CHUNK = 64
HGRN_SUB = 8
N_EXPERTS = 32
TOP_K = 4
D_EXPERT = 1024
SWIGLU_LIMIT = 7.0
SWIGLU_ALPHA = 1.702
RMS_EPS = 1e-5
IN_PROJ_WIDTH = 3 * ATTN_WIDTH + 4 * HGRN_WIDTH
LANES = 128
HEAD_PAIRS = ATTN_WIDTH // LANES

INPROJ_ROWS = 512
OUTPROJ_ROWS = 256
FFN_ROWS = 256
COMBINE_ROWS = 256
HGRN_SEQ_BLOCK = 1024
ATTN_UNROLL = 4
POS_TOKENS = 2048
MOVE_TOKENS = 512
MOVE_UNROLL = 8
VMEM_LIMIT = 48 * 1024 * 1024
FFN_VMEM_LIMIT = 56 * 1024 * 1024

NT_DIMS = (((1,), (1,)), ((), ()))
TN_DIMS = (((0,), (0,)), ((), ()))


def _cparams(*sem):
    return pltpu.CompilerParams(dimension_semantics=sem, vmem_limit_bytes=VMEM_LIMIT)


def _inproj_kernel(x_ref, nw_ref, w_ref, qa_ref, ka_ref, va_ref, qh_ref, fh_ref, ih_ref, gh_ref):
    x = x_ref[...]
    xn = x * lax.rsqrt(jnp.mean(x * x, axis=-1, keepdims=True) + RMS_EPS) * nw_ref[...]
    xn = xn.astype(BF16)
    for c, ref in enumerate((qa_ref, ka_ref, va_ref, qh_ref, fh_ref, ih_ref, gh_ref)):
        y = jnp.dot(xn, w_ref[:, c * ATTN_WIDTH:(c + 1) * ATTN_WIDTH], preferred_element_type=F32)
        ref[...] = y.astype(ref.dtype)


def _inproj(x2, norm_w, w_in):
    n = x2.shape[0]
    tm = INPROJ_ROWS
    row = lambda width: pl.BlockSpec((tm, width), lambda i: (i, 0))
    dtypes = (F32, F32, F32, BF16, F32, BF16, F32)
    return pl.pallas_call(
        _inproj_kernel,
        grid=(n // tm,),
        in_specs=[row(D_MODEL),
                  pl.BlockSpec((1, D_MODEL), lambda i: (0, 0)),
                  pl.BlockSpec((D_MODEL, IN_PROJ_WIDTH), lambda i: (0, 0))],
        out_specs=[row(ATTN_WIDTH)] * 7,
        out_shape=[jax.ShapeDtypeStruct((n, ATTN_WIDTH), dt) for dt in dtypes],
        compiler_params=_cparams("arbitrary"),
        name="inproj",
    )(x2, norm_w.reshape(1, D_MODEL), w_in.astype(BF16))


def _t5_causal_bucket(dist):
    n = np.maximum(dist, 0)
    max_exact = REL_BUCKETS // 2
    large = max_exact + (np.log(np.maximum(n, 1) / max_exact)
                         / np.log(REL_MAX_DISTANCE / max_exact)
                         * (REL_BUCKETS - max_exact)).astype(np.int32)
    large = np.minimum(large, REL_BUCKETS - 1)
    return np.where(n < max_exact, n, large).astype(np.int32)


def _bucket_thresholds(max_dist):
    buckets = _t5_causal_bucket(np.arange(max_dist + 1))
    assert np.all(np.diff(buckets) >= 0)
    return [(b, int(np.argmax(buckets >= b))) for b in range(1, REL_BUCKETS) if np.any(buckets >= b)]


def _bias_kernel(rel_ref, o_ref):
    blk = ATTN_BLOCK
    h = pl.program_id(0)
    qi = lax.broadcasted_iota(jnp.int32, (blk, 2 * blk), 0)
    kj = lax.broadcasted_iota(jnp.int32, (blk, 2 * blk), 1)
    dist_sub = blk + qi - kj
    for pi, (window, dilation) in enumerate(DILATION_PATTERNS):
        span = window // dilation
        dist = dist_sub * dilation
        val = jnp.full((blk, 2 * blk), rel_ref[h], F32)
        for b, first_dist in _bucket_thresholds(window):
            val = jnp.where(dist >= first_dist, rel_ref[b * ATTN_HEADS + h], val)
        o_ref[pi, 0] = jnp.where((dist_sub >= 0) & (dist_sub <= span), val, -1e30)


def _masked_bias(rel_bias):
    blk = ATTN_BLOCK
    npat = len(DILATION_PATTERNS)
    bias = pl.pallas_call(
        _bias_kernel,
        grid=(ATTN_HEADS,),
        in_specs=[pl.BlockSpec(memory_space=pltpu.SMEM)],
        out_specs=pl.BlockSpec((npat, 1, blk, 2 * blk), lambda h: (0, h, 0, 0)),
        out_shape=jax.ShapeDtypeStruct((npat, ATTN_HEADS, blk, 2 * blk), F32),
        compiler_params=_cparams("arbitrary"),
        name="rel_bias",
    )(rel_bias.astype(F32).reshape(REL_BUCKETS * ATTN_HEADS))
    return bias.reshape(npat, HEAD_PAIRS, 2 * blk, 2 * blk)


def _attn_kernel(q_ref, k_ref, v_ref, bias_ref, o_ref, acc_ref, lse_ref, *, seq):
    blk = ATTN_BLOCK
    lane = lax.broadcasted_iota(jnp.int32, (1, LANES), 1)
    first = lane < ATTN_HEAD_DIM
    scale = ATTN_HEAD_DIM ** -0.5
    sel0 = jnp.where(first, scale, 0.0)
    sel1 = jnp.where(first, 0.0, scale)
    ones_cols = jnp.ones((2 * blk, LANES), BF16)

    def one_block(pi, dil, q_start, k_start, n_keys):
        qb = q_ref[0, pl.ds(q_start, blk, stride=dil), :]
        q2 = jnp.concatenate([qb * sel0, qb * sel1], axis=0).astype(BF16)
        kk = k_ref[0, pl.ds(k_start, n_keys, stride=dil), :].astype(BF16)
        vv = v_ref[0, pl.ds(k_start, n_keys, stride=dil), :].astype(BF16)
        s = lax.dot_general(q2, kk, NT_DIMS, preferred_element_type=F32)
        s = s + bias_ref[pi, 0, :, 2 * blk - n_keys:]
        m = jnp.max(s, axis=-1, keepdims=True)
        p = jnp.exp(s - m).astype(BF16)
        ov = jnp.dot(p, jnp.concatenate([vv, ones_cols[:n_keys]], axis=1), preferred_element_type=F32)
        z = ov[:, LANES:]
        o = ov[:, :LANES] / z
        lse = m + jnp.log(z)
        rows = pl.ds(q_start, blk, stride=dil)
        acc_ref[pi, rows, :] = jnp.where(first, o[:blk], o[blk:])
        lse_ref[pi, rows, :] = jnp.where(first, lse[:blk], lse[blk:])

    for pi, (_, dil) in enumerate(DILATION_PATTERNS):
        nb = seq // (dil * blk)

        def residue(r, carry, pi=pi, dil=dil, nb=nb):
            one_block(pi, dil, r, r, blk)

            def body(j, c):
                q_start = r + j * (blk * dil)
                one_block(pi, dil, q_start, q_start - blk * dil, 2 * blk)
                return c

            return lax.fori_loop(1, nb, body, carry, unroll=ATTN_UNROLL)

        lax.fori_loop(0, dil, residue, 0, unroll=ATTN_UNROLL if nb == 2 else 1)

    def merge(i, carry):
        rows = pl.ds(pl.multiple_of(i * blk, blk), blk)
        ls = [lse_ref[pi, rows, :] for pi in range(len(DILATION_PATTERNS))]
        m = functools.reduce(jnp.maximum, ls)
        es = [jnp.exp(l - m) for l in ls]
        num = sum(e * acc_ref[pi, rows, :] for pi, e in enumerate(es))
        o_ref[0, rows, :] = (num / sum(es)).astype(o_ref.dtype)
        return carry

    lax.fori_loop(0, seq // blk, merge, 0)


def _dilated_attention(qa, ka, va, bias, batch, seq):
    npat = len(DILATION_PATTERNS)
    shp = (batch, seq, ATTN_WIDTH)
    spec = pl.BlockSpec((1, seq, LANES), lambda b, p: (b, 0, p))
    out = pl.pallas_call(
        functools.partial(_attn_kernel, seq=seq),
        grid=(batch, HEAD_PAIRS),
        in_specs=[spec, spec, spec,
                  pl.BlockSpec((npat, 1, 2 * ATTN_BLOCK, 2 * ATTN_BLOCK), lambda b, p: (0, p, 0, 0))],
        out_specs=spec,
        out_shape=jax.ShapeDtypeStruct(shp, BF16),
        scratch_shapes=[pltpu.VMEM((npat, seq, LANES), F32), pltpu.VMEM((npat, seq, LANES), F32)],
        compiler_params=_cparams("arbitrary", "arbitrary"),
        name="dilated_attn",
    )(qa.reshape(shp), ka.reshape(shp), va.reshape(shp), bias)
    return out.reshape(batch * seq, ATTN_WIDTH)


def _hgrn_kernel(q_ref, f_ref, i_ref, g_ref, lbl_ref, nw_ref, o_ref, st_ref, *, n_chunks):
    c, sub, dk = HGRN_CHUNK, HGRN_SUB, HGRN_HEAD_DIM
    nsub = c // sub
    lbl = lbl_ref[...]
    e = jnp.exp(lbl - jnp.max(lbl, axis=0, keepdims=True))
    lb_all = e[0:1] / jnp.sum(e, axis=0, keepdims=True)
    nw = nw_ref[...]

    r64 = lax.broadcasted_iota(jnp.int32, (c, c), 0)
    c64 = lax.broadcasted_iota(jnp.int32, (c, c), 1)
    tril = (r64 >= c64).astype(F32)
    col_sub = c64 // sub
    col_in_sub = c64 - (r64 // sub) * sub
    t_iota = lax.broadcasted_iota(jnp.int32, (nsub, sub, dk), 1)
    ones_rhs = jnp.ones((dk, c), BF16)

    @pl.when(pl.program_id(1) == 0)
    def _():
        st_ref[...] = jnp.zeros_like(st_ref)

    def cols_of(hd):
        return slice(hd * dk, (hd + 1) * dk)

    def gates(rows, hd):
        cols = cols_of(hd)
        lb = lb_all[:, cols]
        q = q_ref[0, rows, cols].astype(F32)
        f = lb + (1.0 - lb) * jax.nn.sigmoid(f_ref[0, rows, cols])
        cum = jnp.dot(tril, jnp.log(f), precision=HIGHEST, preferred_element_type=F32)
        return cum, 1.0 - f, q * jax.nn.sigmoid(q) * (dk ** -0.5)

    def products(rows, hd, cum, key, qf):
        v = i_ref[0, rows, cols_of(hd)]
        last = cum[c - 1:c, :]
        st = st_ref[hd]
        o = lax.dot_general((qf * jnp.exp(cum)).astype(BF16), st.astype(BF16), NT_DIMS,
                            preferred_element_type=F32)
        kd = (key * jnp.exp(last - cum)).astype(BF16)
        st_ref[hd] = st * jnp.exp(last) + lax.dot_general(v, kd, TN_DIMS, preferred_element_type=F32)

        cum3 = cum.reshape(nsub, sub, dk)
        key3 = key.reshape(nsub, sub, dk)
        qf3 = qf.reshape(nsub, sub, dk)
        khat = (key3 * jnp.exp(cum3[:, sub - 1:sub, :] - cum3)).reshape(c, dk).astype(BF16)
        qs = []
        for j in range(nsub - 1):
            t0 = sub * (j + 1)
            qs.append(qf[t0:] * jnp.exp(cum[t0:] - cum[t0 - 1:t0, :]))
        prod = lax.dot_general(jnp.concatenate(qs, axis=0).astype(BF16), khat, NT_DIMS,
                               preferred_element_type=F32)

        ws = []
        for s in range(sub):
            dec = jnp.exp(jnp.where(t_iota >= s, cum3 - cum3[:, s:s + 1, :], -jnp.inf))
            ws.append((qf3 * key3[:, s:s + 1, :] * dec).reshape(c, dk))
        sums = jnp.dot(jnp.concatenate(ws, axis=0).astype(BF16), ones_rhs,
                       preferred_element_type=F32)
        return o, prod, sums

    def finish(rows, hd, o, prod, sums):
        cols = cols_of(hd)
        a = jnp.zeros((c, c), F32)
        r = 0
        for j in range(nsub - 1):
            t0 = sub * (j + 1)
            pj = jnp.concatenate([jnp.zeros((t0, c), F32), prod[r:r + c - t0, :]], axis=0)
            a = a + jnp.where(col_sub == j, pj, 0.0)
            r += c - t0
        for s in range(sub):
            a = a + jnp.where(col_in_sub == s, sums[s * c:(s + 1) * c, :], 0.0)
        o = o + jnp.dot(a.astype(BF16), i_ref[0, rows, cols], preferred_element_type=F32)
        on = o * lax.rsqrt(jnp.mean(o * o, axis=-1, keepdims=True) + RMS_EPS) * nw
        g = g_ref[0, rows, cols]
        o_ref[0, rows, cols] = (on * (g * jax.nn.sigmoid(g))).astype(BF16)

    def body(ci, carry):
        rows = pl.ds(pl.multiple_of(ci * c, c), c)
        heads = range(HGRN_HEADS)
        stage1 = [gates(rows, hd) for hd in heads]
        stage2 = [products(rows, hd, *stage1[hd]) for hd in heads]
        for hd in heads:
            finish(rows, hd, *stage2[hd])
        return carry

    lax.fori_loop(0, n_chunks, body, 0)


def _hgrn(qh, fh, ih, gh, lb_logits, norm_w, batch, seq):
    sb = min(seq, HGRN_SEQ_BLOCK)
    spec = pl.BlockSpec((1, sb, HGRN_WIDTH), lambda b, s: (b, s, 0))
    slots = lb_logits.shape[0]
    shp = (batch, seq, HGRN_WIDTH)
    out = pl.pallas_call(
        functools.partial(_hgrn_kernel, n_chunks=sb // HGRN_CHUNK),
        grid=(batch, seq // sb),
        in_specs=[spec, spec, spec, spec,
                  pl.BlockSpec((slots, HGRN_WIDTH), lambda b, s: (0, 0)),
                  pl.BlockSpec((1, HGRN_HEAD_DIM), lambda b, s: (0, 0))],
        out_specs=spec,
        out_shape=jax.ShapeDtypeStruct(shp, BF16),
        scratch_shapes=[pltpu.VMEM((HGRN_HEADS, HGRN_HEAD_DIM, HGRN_HEAD_DIM), F32)],
        compiler_params=_cparams("arbitrary", "arbitrary"),
        name="hgrn2",
    )(qh.reshape(shp), fh.reshape(shp), ih.reshape(shp), gh.reshape(shp),
      lb_logits.astype(F32), norm_w.reshape(1, HGRN_HEAD_DIM))
    return out.reshape(batch * seq, HGRN_WIDTH)


def _outproj_kernel(attn_ref, hg_ref, x_ref, w_ref, nw_ref, rw_ref, rb_ref,
                    h_ref, xn_ref, idx_ref, gate_ref, rank_ref, cnt_ref, run_ref):
    y = jnp.dot(attn_ref[...], w_ref[0:ATTN_WIDTH, :], preferred_element_type=F32)
    y = y + jnp.dot(hg_ref[...], w_ref[ATTN_WIDTH:, :], preferred_element_type=F32)
    h = x_ref[...] + y
    h_ref[...] = h
    xn = h * lax.rsqrt(jnp.mean(h * h, axis=-1, keepdims=True) + RMS_EPS) * nw_ref[...]
    xn_ref[...] = xn

    logits = lax.dot_general(rw_ref[...], xn, NT_DIMS, precision=HIGHEST,
                             preferred_element_type=F32) + rb_ref[...]
    eid = lax.broadcasted_iota(jnp.int32, logits.shape, 0)
    vals, idxs = [], []
    for _ in range(TOP_K):
        mx = jnp.max(logits, axis=0, keepdims=True)
        ix = jnp.min(jnp.where(logits == mx, eid, N_EXPERTS), axis=0, keepdims=True)
        vals.append(mx)
        idxs.append(ix)
        logits = jnp.where(eid == ix, -jnp.inf, logits)
    es = [jnp.exp(v - vals[0]) for v in vals]
    den = es[0] + es[1] + es[2] + es[3]
    idx_ref[...] = jnp.concatenate(idxs, axis=0)
    gate_ref[...] = jnp.concatenate([e / den for e in es], axis=0)

    @pl.when(pl.program_id(0) == 0)
    def _():
        run_ref[...] = jnp.zeros_like(run_ref)

    rows = logits.shape[1]
    onehots = [(eid == ix).astype(F32) for ix in idxs]
    routed = onehots[0] + onehots[1] + onehots[2] + onehots[3]
    earlier = (lax.broadcasted_iota(jnp.int32, (rows, rows), 0)
               < lax.broadcasted_iota(jnp.int32, (rows, rows), 1)).astype(BF16)
    before = jnp.dot(routed.astype(BF16), earlier, preferred_element_type=F32) + run_ref[...]
    rank_ref[...] = jnp.concatenate([jnp.sum(o * before, axis=0, keepdims=True) for o in onehots],
                                    axis=0).astype(jnp.int32)
    run_ref[...] = run_ref[...] + jnp.sum(routed, axis=1, keepdims=True)
    cnt_ref[...] = run_ref[...].astype(jnp.int32)


def _outproj(attn, hg, x2, w_out, norm_w, router_w, router_b):
    n = x2.shape[0]
    tm = OUTPROJ_ROWS
    row = lambda width: pl.BlockSpec((tm, width), lambda i: (i, 0))
    full = lambda a, b: pl.BlockSpec((a, b), lambda i: (0, 0))
    tok = pl.BlockSpec((TOP_K, tm), lambda i: (0, i))
    return pl.pallas_call(
        _outproj_kernel,
        grid=(n // tm,),
        in_specs=[row(ATTN_WIDTH), row(HGRN_WIDTH), row(D_MODEL), full(D_MODEL, D_MODEL),
                  full(1, D_MODEL), full(N_EXPERTS, D_MODEL), full(N_EXPERTS, 1)],
        out_specs=[row(D_MODEL), row(D_MODEL), tok, tok, tok, full(N_EXPERTS, 1)],
        out_shape=[jax.ShapeDtypeStruct((n, D_MODEL), F32),
                   jax.ShapeDtypeStruct((n, D_MODEL), F32),
                   jax.ShapeDtypeStruct((TOP_K, n), jnp.int32),
                   jax.ShapeDtypeStruct((TOP_K, n), F32),
                   jax.ShapeDtypeStruct((TOP_K, n), jnp.int32),
                   jax.ShapeDtypeStruct((N_EXPERTS, 1), jnp.int32)],
        scratch_shapes=[pltpu.VMEM((N_EXPERTS, 1), F32)],
        compiler_params=_cparams("arbitrary"),
        name="outproj_router",
    )(attn, hg, x2, w_out.astype(BF16), norm_w.reshape(1, D_MODEL),
      router_w.T.astype(F32), router_b.astype(F32).reshape(N_EXPERTS, 1))


def _ffn_kernel(be_ref, rv_ref, xs_ref, wgu_ref, bgu_ref, wdn_ref, bdn_ref, o_ref, wgu_bf, wdn_bf):
    i = pl.program_id(0)
    rows_valid = rv_ref[i]
    new_expert = (i == 0) | (be_ref[i] != be_ref[jnp.maximum(i - 1, 0)])

    @pl.when(new_expert & (rows_valid > 0))
    def _():
        step = 128
        for r in range(0, D_MODEL, step):
            wgu_bf[r:r + step, :] = wgu_ref[0, r:r + step, :].astype(BF16)
        for r in range(0, D_EXPERT, step):
            wdn_bf[r:r + step, :] = wdn_ref[0, r:r + step, :].astype(BF16)

    @pl.when(rows_valid > 0)
    def _():
        live = lax.broadcasted_iota(jnp.int32, (xs_ref.shape[0], 1), 0) < rows_valid
        x = jnp.where(live, xs_ref[...], 0.0).astype(BF16)
        hgu = jnp.dot(x, wgu_bf[...], preferred_element_type=F32) + bgu_ref[0]
        gate = jnp.minimum(hgu[:, :D_EXPERT], SWIGLU_LIMIT)
        up = jnp.clip(hgu[:, D_EXPERT:], -SWIGLU_LIMIT, SWIGLU_LIMIT)
        act = gate * jax.nn.sigmoid(SWIGLU_ALPHA * gate) * (up + 1.0)
        o_ref[...] = jnp.dot(act.astype(BF16), wdn_bf[...], preferred_element_type=F32) + bdn_ref[0]

    @pl.when(rows_valid <= 0)
    def _():
        o_ref[...] = jnp.zeros_like(o_ref)


def _expert_ffn(block_e, rows_valid, xs, w_gu, b_gu, w_dn, b_dn):
    cap = xs.shape[0]
    tm = FFN_ROWS
    grid_spec = pltpu.PrefetchScalarGridSpec(
        num_scalar_prefetch=2,
        grid=(cap // tm,),
        in_specs=[pl.BlockSpec((tm, D_MODEL), lambda i, be, rv: (i, 0)),
                  pl.BlockSpec((1, D_MODEL, 2 * D_EXPERT), lambda i, be, rv: (be[i], 0, 0)),
                  pl.BlockSpec((1, 1, 2 * D_EXPERT), lambda i, be, rv: (be[i], 0, 0)),
                  pl.BlockSpec((1, D_EXPERT, D_MODEL), lambda i, be, rv: (be[i], 0, 0)),
                  pl.BlockSpec((1, 1, D_MODEL), lambda i, be, rv: (be[i], 0, 0))],
        out_specs=pl.BlockSpec((tm, D_MODEL), lambda i, be, rv: (i, 0)),
        scratch_shapes=[pltpu.VMEM((D_MODEL, 2 * D_EXPERT), BF16), pltpu.VMEM((D_EXPERT, D_MODEL), BF16)],
    )
    return pl.pallas_call(
        _ffn_kernel,
        grid_spec=grid_spec,
        out_shape=jax.ShapeDtypeStruct((cap, D_MODEL), F32),
        compiler_params=pltpu.CompilerParams(dimension_semantics=("arbitrary",),
                                             vmem_limit_bytes=FFN_VMEM_LIMIT),
        name="expert_ffn",
    )(block_e, rows_valid, xs, w_gu, b_gu.reshape(N_EXPERTS, 1, 2 * D_EXPERT),
      w_dn, b_dn.reshape(N_EXPERTS, 1, D_MODEL))


def _pos_kernel(idx_ref, rank_ref, pstart_ref, pos_ref):
    eid = lax.broadcasted_iota(jnp.int32, (N_EXPERTS, idx_ref.shape[1]), 0)
    pstart = pstart_ref[...]
    rows = [jnp.sum(jnp.where(eid == idx_ref[k:k + 1, :], pstart, 0), axis=0, keepdims=True)
            for k in range(TOP_K)]
    pos_ref[...] = jnp.concatenate(rows, axis=0) + rank_ref[...]


def _positions(idx_t, rank_t, pstart):
    n = idx_t.shape[1]
    tl = POS_TOKENS
    tok = pl.BlockSpec((TOP_K, tl), lambda i: (0, i))
    return pl.pallas_call(
        _pos_kernel,
        grid=(n // tl,),
        in_specs=[tok, tok, pl.BlockSpec((N_EXPERTS, 1), lambda i: (0, 0))],
        out_specs=tok,
        out_shape=jax.ShapeDtypeStruct((TOP_K, n), jnp.int32),
        compiler_params=_cparams("arbitrary"),
        name="dispatch_pos",
    )(idx_t, rank_t, pstart.reshape(N_EXPERTS, 1))


def _tile_positions(pos, tokens):
    n = pos.shape[1]
    return pos.reshape(TOP_K, n // tokens, tokens).transpose(1, 0, 2)


def _dispatch_kernel(pos_ref, x_ref, xs_ref, sem):
    tokens = x_ref.shape[0]

    def copy(t, k):
        return pltpu.make_async_copy(x_ref.at[pl.ds(t, 1)], xs_ref.at[pl.ds(pos_ref[0, k, t], 1)], sem)

    def issue(t, carry):
        for k in range(TOP_K):
            copy(t, k).start()
        return carry

    def drain(t, carry):
        for k in range(TOP_K):
            copy(t, k).wait()
        return carry

    lax.fori_loop(0, tokens, issue, 0, unroll=MOVE_UNROLL)
    lax.fori_loop(0, tokens, drain, 0, unroll=MOVE_UNROLL)


def _dispatch_rows(pos, xn, out_rows):
    n = xn.shape[0]
    tl = MOVE_TOKENS
    return pl.pallas_call(
        _dispatch_kernel,
        grid=(n // tl,),
        in_specs=[pl.BlockSpec((1, TOP_K, tl), lambda i: (i, 0, 0), memory_space=pltpu.SMEM),
                  pl.BlockSpec((tl, D_MODEL), lambda i: (i, 0))],
        out_specs=pl.BlockSpec(memory_space=pl.ANY),
        out_shape=jax.ShapeDtypeStruct((out_rows, D_MODEL), xn.dtype),
        scratch_shapes=[pltpu.SemaphoreType.DMA(())],
        compiler_params=_cparams("arbitrary"),
        name="dispatch_rows",
    )(_tile_positions(pos, tl), xn)


def _combine_kernel(pos_ref, pos_next_ref, h_ref, g_ref, fw_ref, ys_ref, o_ref, buf, sems):
    i = pl.program_id(0)
    steps = pl.num_programs(0)
    tokens = h_ref.shape[0]
    slot = lax.rem(i, 2)

    def gather(p_ref, s, start):
        def body(t, carry):
            for k in range(TOP_K):
                cp = pltpu.make_async_copy(ys_ref.at[pl.ds(p_ref[0, k, t], 1)],
                                           buf.at[s, k, pl.ds(t, 1)], sems.at[s])
                cp.start() if start else cp.wait()
            return carry
        lax.fori_loop(0, tokens, body, 0, unroll=MOVE_UNROLL)

    @pl.when(i == 0)
    def _():
        gather(pos_ref, 0, True)

    @pl.when(i + 1 < steps)
    def _():
        gather(pos_next_ref, 1 - slot, True)

    gather(pos_ref, slot, False)
    g = g_ref[...]
    y = h_ref[...]
    for k in range(TOP_K):
        y = y + buf[slot, k] * g[:, k:k + 1]
    o_ref[...] = y * lax.rsqrt(jnp.mean(y * y, axis=-1, keepdims=True) + RMS_EPS) * fw_ref[...]


def _combine(h, ys, pos, gates_nk, final_w):
    n = h.shape[0]
    tm = COMBINE_ROWS
    steps = n // tm
    pos_t = _tile_positions(pos, tm)
    smem = lambda imap: pl.BlockSpec((1, TOP_K, tm), imap, memory_space=pltpu.SMEM)
    return pl.pallas_call(
        _combine_kernel,
        grid=(steps,),
        in_specs=[smem(lambda i: (i, 0, 0)),
                  smem(lambda i: (jnp.minimum(i + 1, steps - 1), 0, 0)),
                  pl.BlockSpec((tm, D_MODEL), lambda i: (i, 0)),
                  pl.BlockSpec((tm, TOP_K), lambda i: (i, 0)),
                  pl.BlockSpec((1, D_MODEL), lambda i: (0, 0)),
                  pl.BlockSpec(memory_space=pl.ANY)],
        out_specs=pl.BlockSpec((tm, D_MODEL), lambda i: (i, 0)),
        out_shape=jax.ShapeDtypeStruct((n, D_MODEL), F32),
        scratch_shapes=[pltpu.VMEM((2, TOP_K, tm, D_MODEL), F32), pltpu.SemaphoreType.DMA((2,))],
        compiler_params=_cparams("arbitrary"),
        name="combine_norm",
    )(pos_t, pos_t, h, gates_nk, final_w.reshape(1, D_MODEL), ys)


def kernel(x, norm_mix_w, w_in, rel_bias, hgrn_lb_logits, hgrn_norm_w, w_out, norm_ffn_w,
           router_w, router_b, w_gate_up, b_gate_up, w_down, b_down, final_norm_w):
    batch, seq, _ = x.shape
    n = batch * seq
    x2 = x.reshape(n, D_MODEL)

    qa, ka, va, qh, fh, ih, gh = _inproj(x2, norm_mix_w[0], w_in[0])
    attn = _dilated_attention(qa, ka, va, _masked_bias(rel_bias), batch, seq)
    hg = _hgrn(qh, fh, ih, gh, hgrn_lb_logits, hgrn_norm_w[0], batch, seq)
    h, xn, idx_t, gate_t, rank_t, counts = _outproj(attn, hg, x2, w_out[0], norm_ffn_w[0],
                                                    router_w[0], router_b[0])

    tm = FFN_ROWS
    n_blocks = -(-(n * TOP_K) // tm) + N_EXPERTS
    counts = counts.reshape(N_EXPERTS)
    padded = (counts + tm - 1) // tm * tm
    pend = jnp.cumsum(padded)
    pstart = pend - padded
    block_row0 = jnp.arange(n_blocks, dtype=jnp.int32) * tm
    in_block = (block_row0[:, None] >= pstart[None, :]) & (block_row0[:, None] < pend[None, :])
    block_e = jnp.minimum(jnp.sum(pend[None, :] <= block_row0[:, None], axis=1), N_EXPERTS - 1).astype(jnp.int32)
    rows_valid = jnp.sum(jnp.where(in_block, jnp.clip(pstart + counts - block_row0[:, None], 0, tm), 0),
                         axis=1).astype(jnp.int32)

    pos = _positions(idx_t, rank_t, pstart.astype(jnp.int32))
    xs = _dispatch_rows(pos, xn, n_blocks * tm)
    ys = _expert_ffn(block_e, rows_valid, xs, w_gate_up[0], b_gate_up[0], w_down[0], b_down[0])
    out = _combine(h, ys, pos, gate_t.T, final_norm_w)
    return out.reshape(batch, seq, D_MODEL)
```

```python
import functools

import numpy as np
import jax
import jax.numpy as jnp
from jax import lax
from jax.experimental import pallas as pl
from jax.experimental.pallas import tpu as pltpu

F32 = jnp.float32
BF16 = jnp.bfloat16
HIGHEST = lax.Precision.HIGHEST

D_MODEL = 1024
ATTN_HEAD_DIM = 64
ATTN_WIDTH = 512
ATTN_HEADS = ATTN_WIDTH // ATTN_HEAD_DIM
DILATION_PATTERNS = ((128, 1), (512, 4), (2048, 16))
ATTN_BLOCK = 128
REL_BUCKETS = 32
REL_MAX_DISTANCE = 2048
HGRN_HEAD_DIM = 128
HGRN_WIDTH = 512
HGRN_HEADS = HGRN_WIDTH // HGRN_HEAD_DIM
HGRN_CHUNK = 64
HGRN_SUB = 8
N_EXPERTS = 32
TOP_K = 4
D_EXPERT = 1024
SWIGLU_LIMIT = 7.0
SWIGLU_ALPHA = 1.702
RMS_EPS = 1e-5
IN_PROJ_WIDTH = 3 * ATTN_WIDTH + 4 * HGRN_WIDTH
LANES = 128
HEAD_PAIRS = ATTN_WIDTH // LANES

INPROJ_ROWS = 512
OUTPROJ_ROWS = 256
FFN_ROWS = 512
FFN_COLS = 512
COMBINE_ROWS = 256
HGRN_SEQ_BLOCK = 1024
ATTN_UNROLL = 4
POS_TOKENS = 2048
MOVE_TOKENS = 512
MOVE_UNROLL = 8
VMEM_LIMIT = 48 * 1024 * 1024
FFN_VMEM_LIMIT = 56 * 1024 * 1024

NT_DIMS = (((1,), (1,)), ((), ()))
TN_DIMS = (((0,), (0,)), ((), ()))


def _cparams(*sem):
    return pltpu.CompilerParams(dimension_semantics=sem, vmem_limit_bytes=VMEM_LIMIT)


def _inproj_kernel(x_ref, nw_ref, w_ref, qa_ref, ka_ref, va_ref, qh_ref, fh_ref, ih_ref, gh_ref):
    x = x_ref[...]
    xn = x * lax.rsqrt(jnp.mean(x * x, axis=-1, keepdims=True) + RMS_EPS) * nw_ref[...]
    xn = xn.astype(BF16)
    for c, ref in enumerate((qa_ref, ka_ref, va_ref, qh_ref, fh_ref, ih_ref, gh_ref)):
        y = jnp.dot(xn, w_ref[:, c * ATTN_WIDTH:(c + 1) * ATTN_WIDTH], preferred_element_type=F32)
        ref[...] = y.astype(ref.dtype)


def _inproj(x2, norm_w, w_in):
    n = x2.shape[0]
    tm = INPROJ_ROWS
    row = lambda width: pl.BlockSpec((tm, width), lambda i: (i, 0))
    dtypes = (F32, F32, F32, BF16, F32, BF16, F32)
    return pl.pallas_call(
        _inproj_kernel,
        grid=(n // tm,),
        in_specs=[row(D_MODEL),
                  pl.BlockSpec((1, D_MODEL), lambda i: (0, 0)),
                  pl.BlockSpec((D_MODEL, IN_PROJ_WIDTH), lambda i: (0, 0))],
        out_specs=[row(ATTN_WIDTH)] * 7,
        out_shape=[jax.ShapeDtypeStruct((n, ATTN_WIDTH), dt) for dt in dtypes],
        compiler_params=_cparams("arbitrary"),
        name="inproj",
    )(x2, norm_w.reshape(1, D_MODEL), w_in.astype(BF16))


def _t5_causal_bucket(dist):
    n = np.maximum(dist, 0)
    max_exact = REL_BUCKETS // 2
    large = max_exact + (np.log(np.maximum(n, 1) / max_exact)
                         / np.log(REL_MAX_DISTANCE / max_exact)
                         * (REL_BUCKETS - max_exact)).astype(np.int32)
    large = np.minimum(large, REL_BUCKETS - 1)
    return np.where(n < max_exact, n, large).astype(np.int32)


def _bucket_thresholds(max_dist):
    buckets = _t5_causal_bucket(np.arange(max_dist + 1))
    assert np.all(np.diff(buckets) >= 0)
    return [(b, int(np.argmax(buckets >= b))) for b in range(1, REL_BUCKETS) if np.any(buckets >= b)]


def _bias_kernel(rel_ref, o_ref):
    blk = ATTN_BLOCK
    h = pl.program_id(0)
    qi = lax.broadcasted_iota(jnp.int32, (blk, 2 * blk), 0)
    kj = lax.broadcasted_iota(jnp.int32, (blk, 2 * blk), 1)
    dist_sub = blk + qi - kj
    for pi, (window, dilation) in enumerate(DILATION_PATTERNS):
        span = window // dilation
        dist = dist_sub * dilation
        val = jnp.full((blk, 2 * blk), rel_ref[h], F32)
        for b, first_dist in _bucket_thresholds(window):
            val = jnp.where(dist >= first_dist, rel_ref[b * ATTN_HEADS + h], val)
        o_ref[pi, 0] = jnp.where((dist_sub >= 0) & (dist_sub <= span), val, -1e30)


def _masked_bias(rel_bias):
    blk = ATTN_BLOCK
    npat = len(DILATION_PATTERNS)
    bias = pl.pallas_call(
        _bias_kernel,
        grid=(ATTN_HEADS,),
        in_specs=[pl.BlockSpec(memory_space=pltpu.SMEM)],
        out_specs=pl.BlockSpec((npat, 1, blk, 2 * blk), lambda h: (0, h, 0, 0)),
        out_shape=jax.ShapeDtypeStruct((npat, ATTN_HEADS, blk, 2 * blk), F32),
        compiler_params=_cparams("arbitrary"),
        name="rel_bias",
    )(rel_bias.astype(F32).reshape(REL_BUCKETS * ATTN_HEADS))
    return bias.reshape(npat, HEAD_PAIRS, 2 * blk, 2 * blk)


def _attn_kernel(q_ref, k_ref, v_ref, bias_ref, o_ref, acc_ref, lse_ref, *, seq):
    blk = ATTN_BLOCK
    lane = lax.broadcasted_iota(jnp.int32, (1, LANES), 1)
    first = lane < ATTN_HEAD_DIM
    scale = ATTN_HEAD_DIM ** -0.5
    sel0 = jnp.where(first, scale, 0.0)
    sel1 = jnp.where(first, 0.0, scale)
    ones_cols = jnp.ones((2 * blk, LANES), BF16)

    def one_block(pi, dil, q_start, k_start, n_keys):
        qb = q_ref[0, pl.ds(q_start, blk, stride=dil), :]
        q2 = jnp.concatenate([qb * sel0, qb * sel1], axis=0).astype(BF16)
        kk = k_ref[0, pl.ds(k_start, n_keys, stride=dil), :].astype(BF16)
        vv = v_ref[0, pl.ds(k_start, n_keys, stride=dil), :].astype(BF16)
        s = lax.dot_general(q2, kk, NT_DIMS, preferred_element_type=F32)
        s = s + bias_ref[pi, 0, :, 2 * blk - n_keys:]
        m = jnp.max(s, axis=-1, keepdims=True)
        p = jnp.exp(s - m).astype(BF16)
        ov = jnp.dot(p, jnp.concatenate([vv, ones_cols[:n_keys]], axis=1), preferred_element_type=F32)
        z = ov[:, LANES:]
        o = ov[:, :LANES] / z
        lse = m + jnp.log(z)
        rows = pl.ds(q_start, blk, stride=dil)
        acc_ref[pi, rows, :] = jnp.where(first, o[:blk], o[blk:])
        lse_ref[pi, rows, :] = jnp.where(first, lse[:blk], lse[blk:])

    for pi, (_, dil) in enumerate(DILATION_PATTERNS):
        nb = seq // (dil * blk)

        def residue(r, carry, pi=pi, dil=dil, nb=nb):
            one_block(pi, dil, r, r, blk)

            def body(j, c):
                q_start = r + j * (blk * dil)
                one_block(pi, dil, q_start, q_start - blk * dil, 2 * blk)
                return c

            return lax.fori_loop(1, nb, body, carry, unroll=ATTN_UNROLL)

        lax.fori_loop(0, dil, residue, 0, unroll=ATTN_UNROLL if nb == 2 else 1)

    def merge(i, carry):
        rows = pl.ds(pl.multiple_of(i * blk, blk), blk)
        ls = [lse_ref[pi, rows, :] for pi in range(len(DILATION_PATTERNS))]
        m = functools.reduce(jnp.maximum, ls)
        es = [jnp.exp(l - m) for l in ls]
        num = sum(e * acc_ref[pi, rows, :] for pi, e in enumerate(es))
        o_ref[0, rows, :] = (num / sum(es)).astype(o_ref.dtype)
        return carry

    lax.fori_loop(0, seq // blk, merge, 0)


def _dilated_attention(qa, ka, va, bias, batch, seq):
    npat = len(DILATION_PATTERNS)
    shp = (batch, seq, ATTN_WIDTH)
    spec = pl.BlockSpec((1, seq, LANES), lambda b, p: (b, 0, p))
    out = pl.pallas_call(
        functools.partial(_attn_kernel, seq=seq),
        grid=(batch, HEAD_PAIRS),
        in_specs=[spec, spec, spec,
                  pl.BlockSpec((npat, 1, 2 * ATTN_BLOCK, 2 * ATTN_BLOCK), lambda b, p: (0, p, 0, 0))],
        out_specs=spec,
        out_shape=jax.ShapeDtypeStruct(shp, BF16),
        scratch_shapes=[pltpu.VMEM((npat, seq, LANES), F32), pltpu.VMEM((npat, seq, LANES), F32)],
        compiler_params=_cparams("arbitrary", "arbitrary"),
        name="dilated_attn",
    )(qa.reshape(shp), ka.reshape(shp), va.reshape(shp), bias)
    return out.reshape(batch * seq, ATTN_WIDTH)


def _hgrn_kernel(q_ref, f_ref, i_ref, g_ref, lbl_ref, nw_ref, o_ref, st_ref, *, n_chunks):
    c, sub, dk = HGRN_CHUNK, HGRN_SUB, HGRN_HEAD_DIM
    nsub = c // sub
    lbl = lbl_ref[...]
    e = jnp.exp(lbl - jnp.max(lbl, axis=0, keepdims=True))
    lb_all = e[0:1] / jnp.sum(e, axis=0, keepdims=True)
    nw = nw_ref[...]

    r64 = lax.broadcasted_iota(jnp.int32, (c, c), 0)
    c64 = lax.broadcasted_iota(jnp.int32, (c, c), 1)
    tril = (r64 >= c64).astype(F32)
    col_sub = c64 // sub
    col_in_sub = c64 - (r64 // sub) * sub
    t_iota = lax.broadcasted_iota(jnp.int32, (nsub, sub, dk), 1)
    ones_rhs = jnp.ones((dk, c), BF16)

    @pl.when(pl.program_id(1) == 0)
    def _():
        st_ref[...] = jnp.zeros_like(st_ref)

    def cols_of(hd):
        return slice(hd * dk, (hd + 1) * dk)

    def gates(rows, hd):
        cols = cols_of(hd)
        lb = lb_all[:, cols]
        q = q_ref[0, rows, cols].astype(F32)
        f = lb + (1.0 - lb) * jax.nn.sigmoid(f_ref[0, rows, cols])
        cum = jnp.dot(tril, jnp.log(f), precision=HIGHEST, preferred_element_type=F32)
        return cum, 1.0 - f, q * jax.nn.sigmoid(q) * (dk ** -0.5)

    def products(rows, hd, cum, key, qf):
        v = i_ref[0, rows, cols_of(hd)]
        last = cum[c - 1:c, :]
        st = st_ref[hd]
        o = lax.dot_general((qf * jnp.exp(cum)).astype(BF16), st.astype(BF16), NT_DIMS,
                            preferred_element_type=F32)
        kd = (key * jnp.exp(last - cum)).astype(BF16)
        st_ref[hd] = st * jnp.exp(last) + lax.dot_general(v, kd, TN_DIMS, preferred_element_type=F32)

        cum3 = cum.reshape(nsub, sub, dk)
        key3 = key.reshape(nsub, sub, dk)
        qf3 = qf.reshape(nsub, sub, dk)
        khat = (key3 * jnp.exp(cum3[:, sub - 1:sub, :] - cum3)).reshape(c, dk).astype(BF16)
        qs = []
        for j in range(nsub - 1):
            t0 = sub * (j + 1)
            qs.append(qf[t0:] * jnp.exp(cum[t0:] - cum[t0 - 1:t0, :]))
        prod = lax.dot_general(jnp.concatenate(qs, axis=0).astype(BF16), khat, NT_DIMS,
                               preferred_element_type=F32)

        ws = []
        for s in range(sub):
            dec = jnp.exp(jnp.where(t_iota >= s, cum3 - cum3[:, s:s + 1, :], -jnp.inf))
            ws.append((qf3 * key3[:, s:s + 1, :] * dec).reshape(c, dk))
        sums = jnp.dot(jnp.concatenate(ws, axis=0).astype(BF16), ones_rhs,
                       preferred_element_type=F32)
        return o, prod, sums

    def finish(rows, hd, o, prod, sums):
        cols = cols_of(hd)
        a = jnp.zeros((c, c), F32)
        r = 0
        for j in range(nsub - 1):
            t0 = sub * (j + 1)
            pj = jnp.concatenate([jnp.zeros((t0, c), F32), prod[r:r + c - t0, :]], axis=0)
            a = a + jnp.where(col_sub == j, pj, 0.0)
            r += c - t0
        for s in range(sub):
            a = a + jnp.where(col_in_sub == s, sums[s * c:(s + 1) * c, :], 0.0)
        o = o + jnp.dot(a.astype(BF16), i_ref[0, rows, cols], preferred_element_type=F32)
        on = o * lax.rsqrt(jnp.mean(o * o, axis=-1, keepdims=True) + RMS_EPS) * nw
        g = g_ref[0, rows, cols]
        o_ref[0, rows, cols] = (on * (g * jax.nn.sigmoid(g))).astype(BF16)

    def body(ci, carry):
        rows = pl.ds(pl.multiple_of(ci * c, c), c)
        heads = range(HGRN_HEADS)
        stage1 = [gates(rows, hd) for hd in heads]
        stage2 = [products(rows, hd, *stage1[hd]) for hd in heads]
        for hd in heads:
            finish(rows, hd, *stage2[hd])
        return carry

    lax.fori_loop(0, n_chunks, body, 0)


def _hgrn(qh, fh, ih, gh, lb_logits, norm_w, batch, seq):
    sb = min(seq, HGRN_SEQ_BLOCK)
    spec = pl.BlockSpec((1, sb, HGRN_WIDTH), lambda b, s: (b, s, 0))
    slots = lb_logits.shape[0]
    shp = (batch, seq, HGRN_WIDTH)
    out = pl.pallas_call(
        functools.partial(_hgrn_kernel, n_chunks=sb // HGRN_CHUNK),
        grid=(batch, seq // sb),
        in_specs=[spec, spec, spec, spec,
                  pl.BlockSpec((slots, HGRN_WIDTH), lambda b, s: (0, 0)),
                  pl.BlockSpec((1, HGRN_HEAD_DIM), lambda b, s: (0, 0))],
        out_specs=spec,
        out_shape=jax.ShapeDtypeStruct(shp, BF16),
        scratch_shapes=[pltpu.VMEM((HGRN_HEADS, HGRN_HEAD_DIM, HGRN_HEAD_DIM), F32)],
        compiler_params=_cparams("arbitrary", "arbitrary"),
        name="hgrn2",
    )(qh.reshape(shp), fh.reshape(shp), ih.reshape(shp), gh.reshape(shp),
      lb_logits.astype(F32), norm_w.reshape(1, HGRN_HEAD_DIM))
    return out.reshape(batch * seq, HGRN_WIDTH)


def _outproj_kernel(attn_ref, hg_ref, x_ref, w_ref, nw_ref, rw_ref, rb_ref,
                    h_ref, xn_ref, idx_ref, gate_ref, rank_ref, cnt_ref, run_ref):
    y = jnp.dot(attn_ref[...], w_ref[0:ATTN_WIDTH, :], preferred_element_type=F32)
    y = y + jnp.dot(hg_ref[...], w_ref[ATTN_WIDTH:, :], preferred_element_type=F32)
    h = x_ref[...] + y
    h_ref[...] = h
    xn = h * lax.rsqrt(jnp.mean(h * h, axis=-1, keepdims=True) + RMS_EPS) * nw_ref[...]
    xn_ref[...] = xn

    logits = lax.dot_general(rw_ref[...], xn, NT_DIMS, precision=HIGHEST,
                             preferred_element_type=F32) + rb_ref[...]
    eid = lax.broadcasted_iota(jnp.int32, logits.shape, 0)
    vals, idxs = [], []
    for _ in range(TOP_K):
        mx = jnp.max(logits, axis=0, keepdims=True)
        ix = jnp.min(jnp.where(logits == mx, eid, N_EXPERTS), axis=0, keepdims=True)
        vals.append(mx)
        idxs.append(ix)
        logits = jnp.where(eid == ix, -jnp.inf, logits)
    es = [jnp.exp(v - vals[0]) for v in vals]
    den = es[0] + es[1] + es[2] + es[3]
    idx_ref[...] = jnp.concatenate(idxs, axis=0)
    gate_ref[...] = jnp.concatenate([e / den for e in es], axis=0)

    @pl.when(pl.program_id(0) == 0)
    def _():
        run_ref[...] = jnp.zeros_like(run_ref)

    rows = logits.shape[1]
    onehots = [(eid == ix).astype(F32) for ix in idxs]
    routed = onehots[0] + onehots[1] + onehots[2] + onehots[3]
    earlier = (lax.broadcasted_iota(jnp.int32, (rows, rows), 0)
               < lax.broadcasted_iota(jnp.int32, (rows, rows), 1)).astype(BF16)
    before = jnp.dot(routed.astype(BF16), earlier, preferred_element_type=F32) + run_ref[...]
    rank_ref[...] = jnp.concatenate([jnp.sum(o * before, axis=0, keepdims=True) for o in onehots],
                                    axis=0).astype(jnp.int32)
    run_ref[...] = run_ref[...] + jnp.sum(routed, axis=1, keepdims=True)
    cnt_ref[...] = run_ref[...].astype(jnp.int32)


def _outproj(attn, hg, x2, w_out, norm_w, router_w, router_b):
    n = x2.shape[0]
    tm = OUTPROJ_ROWS
    row = lambda width: pl.BlockSpec((tm, width), lambda i: (i, 0))
    full = lambda a, b: pl.BlockSpec((a, b), lambda i: (0, 0))
    tok = pl.BlockSpec((TOP_K, tm), lambda i: (0, i))
    return pl.pallas_call(
        _outproj_kernel,
        grid=(n // tm,),
        in_specs=[row(ATTN_WIDTH), row(HGRN_WIDTH), row(D_MODEL), full(D_MODEL, D_MODEL),
                  full(1, D_MODEL), full(N_EXPERTS, D_MODEL), full(N_EXPERTS, 1)],
        out_specs=[row(D_MODEL), row(D_MODEL), tok, tok, tok, full(N_EXPERTS, 1)],
        out_shape=[jax.ShapeDtypeStruct((n, D_MODEL), F32),
                   jax.ShapeDtypeStruct((n, D_MODEL), F32),
                   jax.ShapeDtypeStruct((TOP_K, n), jnp.int32),
                   jax.ShapeDtypeStruct((TOP_K, n), F32),
                   jax.ShapeDtypeStruct((TOP_K, n), jnp.int32),
                   jax.ShapeDtypeStruct((N_EXPERTS, 1), jnp.int32)],
        scratch_shapes=[pltpu.VMEM((N_EXPERTS, 1), F32)],
        compiler_params=_cparams("arbitrary"),
        name="outproj_router",
    )(attn, hg, x2, w_out.astype(BF16), norm_w.reshape(1, D_MODEL),
      router_w.T.astype(F32), router_b.astype(F32).reshape(N_EXPERTS, 1))


def _ffn_kernel(be_ref, rv_ref, xs_ref, wgu_ref, bgu_ref, wdn_ref, bdn_ref, o_ref, wgu_bf, wdn_bf):
    i = pl.program_id(0)
    rows_valid = rv_ref[i]
    new_expert = (i == 0) | (be_ref[i] != be_ref[jnp.maximum(i - 1, 0)])

    @pl.when(new_expert & (rows_valid > 0))
    def _():
        step = 128
        for r in range(0, D_MODEL, step):
            wgu_bf[r:r + step, :] = wgu_ref[0, r:r + step, :].astype(BF16)
        for r in range(0, D_EXPERT, step):
            wdn_bf[r:r + step, :] = wdn_ref[0, r:r + step, :].astype(BF16)

    @pl.when(rows_valid > 0)
    def _():
        live = lax.broadcasted_iota(jnp.int32, (xs_ref.shape[0], 1), 0) < rows_valid
        x = jnp.where(live, xs_ref[...], 0.0).astype(BF16)
        out = bdn_ref[0]
        for c0 in range(0, D_EXPERT, FFN_COLS):
            g_cols = slice(c0, c0 + FFN_COLS)
            u_cols = slice(D_EXPERT + c0, D_EXPERT + c0 + FFN_COLS)
            gate = jnp.dot(x, wgu_bf[:, g_cols], preferred_element_type=F32) + bgu_ref[0, :, g_cols]
            up = jnp.dot(x, wgu_bf[:, u_cols], preferred_element_type=F32) + bgu_ref[0, :, u_cols]
            gate = jnp.minimum(gate, SWIGLU_LIMIT)
            up = jnp.clip(up, -SWIGLU_LIMIT, SWIGLU_LIMIT)
            act = gate * jax.nn.sigmoid(SWIGLU_ALPHA * gate) * (up + 1.0)
            out = out + jnp.dot(act.astype(BF16), wdn_bf[g_cols, :], preferred_element_type=F32)
        o_ref[...] = out

    @pl.when(rows_valid <= 0)
    def _():
        o_ref[...] = jnp.zeros_like(o_ref)


def _expert_ffn(block_e, rows_valid, xs, w_gu, b_gu, w_dn, b_dn):
    cap = xs.shape[0]
    tm = FFN_ROWS
    grid_spec = pltpu.PrefetchScalarGridSpec(
        num_scalar_prefetch=2,
        grid=(cap // tm,),
        in_specs=[pl.BlockSpec((tm, D_MODEL), lambda i, be, rv: (i, 0)),
                  pl.BlockSpec((1, D_MODEL, 2 * D_EXPERT), lambda i, be, rv: (be[i], 0, 0)),
                  pl.BlockSpec((1, 1, 2 * D_EXPERT), lambda i, be, rv: (be[i], 0, 0)),
                  pl.BlockSpec((1, D_EXPERT, D_MODEL), lambda i, be, rv: (be[i], 0, 0)),
                  pl.BlockSpec((1, 1, D_MODEL), lambda i, be, rv: (be[i], 0, 0))],
        out_specs=pl.BlockSpec((tm, D_MODEL), lambda i, be, rv: (i, 0)),
        scratch_shapes=[pltpu.VMEM((D_MODEL, 2 * D_EXPERT), BF16), pltpu.VMEM((D_EXPERT, D_MODEL), BF16)],
    )
    return pl.pallas_call(
        _ffn_kernel,
        grid_spec=grid_spec,
        out_shape=jax.ShapeDtypeStruct((cap, D_MODEL), F32),
        compiler_params=pltpu.CompilerParams(dimension_semantics=("arbitrary",),
                                             vmem_limit_bytes=FFN_VMEM_LIMIT),
        name="expert_ffn",
    )(block_e, rows_valid, xs, w_gu, b_gu.reshape(N_EXPERTS, 1, 2 * D_EXPERT),
      w_dn, b_dn.reshape(N_EXPERTS, 1, D_MODEL))


def _pos_kernel(idx_ref, rank_ref, pstart_ref, pos_ref):
    eid = lax.broadcasted_iota(jnp.int32, (N_EXPERTS, idx_ref.shape[1]), 0)
    pstart = pstart_ref[...]
    rows = [jnp.sum(jnp.where(eid == idx_ref[k:k + 1, :], pstart, 0), axis=0, keepdims=True)
            for k in range(TOP_K)]
    pos_ref[...] = jnp.concatenate(rows, axis=0) + rank_ref[...]


def _positions(idx_t, rank_t, pstart):
    n = idx_t.shape[1]
    tl = POS_TOKENS
    tok = pl.BlockSpec((TOP_K, tl), lambda i: (0, i))
    return pl.pallas_call(
        _pos_kernel,
        grid=(n // tl,),
        in_specs=[tok, tok, pl.BlockSpec((N_EXPERTS, 1), lambda i: (0, 0))],
        out_specs=tok,
        out_shape=jax.ShapeDtypeStruct((TOP_K, n), jnp.int32),
        compiler_params=_cparams("arbitrary"),
        name="dispatch_pos",
    )(idx_t, rank_t, pstart.reshape(N_EXPERTS, 1))


def _tile_positions(pos, tokens):
    n = pos.shape[1]
    return pos.reshape(TOP_K, n // tokens, tokens).transpose(1, 0, 2)


def _dispatch_kernel(pos_ref, x_ref, xs_ref, sem):
    tokens = x_ref.shape[0]

    def copy(t, k):
        return pltpu.make_async_copy(x_ref.at[pl.ds(t, 1)], xs_ref.at[pl.ds(pos_ref[0, k, t], 1)], sem)

    def issue(t, carry):
        for k in range(TOP_K):
            copy(t, k).start(priority=k % 2)
        return carry

    def drain(t, carry):
        for k in range(TOP_K):
            copy(t, k).wait()
        return carry

    lax.fori_loop(0, tokens, issue, 0, unroll=MOVE_UNROLL)
    lax.fori_loop(0, tokens, drain, 0, unroll=MOVE_UNROLL)


def _dispatch_rows(pos, xn, out_rows):
    n = xn.shape[0]
    tl = MOVE_TOKENS
    return pl.pallas_call(
        _dispatch_kernel,
        grid=(n // tl,),
        in_specs=[pl.BlockSpec((1, TOP_K, tl), lambda i: (i, 0, 0), memory_space=pltpu.SMEM),
                  pl.BlockSpec((tl, D_MODEL), lambda i: (i, 0))],
        out_specs=pl.BlockSpec(memory_space=pl.ANY),
        out_shape=jax.ShapeDtypeStruct((out_rows, D_MODEL), xn.dtype),
        scratch_shapes=[pltpu.SemaphoreType.DMA(())],
        compiler_params=_cparams("arbitrary"),
        name="dispatch_rows",
    )(_tile_positions(pos, tl), xn)


def _combine_kernel(pos_ref, pos_next_ref, h_ref, g_ref, fw_ref, ys_ref, o_ref, buf, sems):
    i = pl.program_id(0)
    steps = pl.num_programs(0)
    tokens = h_ref.shape[0]
    slot = lax.rem(i, 2)

    def gather(p_ref, s, start):
        def body(t, carry):
            for k in range(TOP_K):
                cp = pltpu.make_async_copy(ys_ref.at[pl.ds(p_ref[0, k, t], 1)],
                                           buf.at[s, k, pl.ds(t, 1)], sems.at[s])
                cp.start(priority=k % 2) if start else cp.wait()
            return carry
        lax.fori_loop(0, tokens, body, 0, unroll=MOVE_UNROLL)

    @pl.when(i == 0)
    def _():
        gather(pos_ref, 0, True)

    @pl.when(i + 1 < steps)
    def _():
        gather(pos_next_ref, 1 - slot, True)

    gather(pos_ref, slot, False)
    g = g_ref[...]
    y = h_ref[...]
    for k in range(TOP_K):
        y = y + buf[slot, k] * g[:, k:k + 1]
    o_ref[...] = y * lax.rsqrt(jnp.mean(y * y, axis=-1, keepdims=True) + RMS_EPS) * fw_ref[...]


def _combine(h, ys, pos, gates_nk, final_w):
    n = h.shape[0]
    tm = COMBINE_ROWS
    steps = n // tm
    pos_t = _tile_positions(pos, tm)
    smem = lambda imap: pl.BlockSpec((1, TOP_K, tm), imap, memory_space=pltpu.SMEM)
    return pl.pallas_call(
        _combine_kernel,
        grid=(steps,),
        in_specs=[smem(lambda i: (i, 0, 0)),
                  smem(lambda i: (jnp.minimum(i + 1, steps - 1), 0, 0)),
                  pl.BlockSpec((tm, D_MODEL), lambda i: (i, 0)),
                  pl.BlockSpec((tm, TOP_K), lambda i: (i, 0)),
                  pl.BlockSpec((1, D_MODEL), lambda i: (0, 0)),
                  pl.BlockSpec(memory_space=pl.ANY)],
        out_specs=pl.BlockSpec((tm, D_MODEL), lambda i: (i, 0)),
        out_shape=jax.ShapeDtypeStruct((n, D_MODEL), F32),
        scratch_shapes=[pltpu.VMEM((2, TOP_K, tm, D_MODEL), F32), pltpu.SemaphoreType.DMA((2,))],
        compiler_params=_cparams("arbitrary"),
        name="combine_norm",
    )(pos_t, pos_t, h, gates_nk, final_w.reshape(1, D_MODEL), ys)


def kernel(x, norm_mix_w, w_in, rel_bias, hgrn_lb_logits, hgrn_norm_w, w_out, norm_ffn_w,
           router_w, router_b, w_gate_up, b_gate_up, w_down, b_down, final_norm_w):
    batch, seq, _ = x.shape
    n = batch * seq
    x2 = x.reshape(n, D_MODEL)

    qa, ka, va, qh, fh, ih, gh = _inproj(x2, norm_mix_w[0], w_in[0])
    attn = _dilated_attention(qa, ka, va, _masked_bias(rel_bias), batch, seq)
    hg = _hgrn(qh, fh, ih, gh, hgrn_lb_logits, hgrn_norm_w[0], batch, seq)
    h, xn, idx_t, gate_t, rank_t, counts = _outproj(attn, hg, x2, w_out[0], norm_ffn_w[0],
                                                    router_w[0], router_b[0])

    tm = FFN_ROWS
    n_blocks = -(-(n * TOP_K) // tm) + N_EXPERTS
    counts = counts.reshape(N_EXPERTS)
    padded = (counts + tm - 1) // tm * tm
    pend = jnp.cumsum(padded)
    pstart = pend - padded
    block_row0 = jnp.arange(n_blocks, dtype=jnp.int32) * tm
    in_block = (block_row0[:, None] >= pstart[None, :]) & (block_row0[:, None] < pend[None, :])
    block_e = jnp.minimum(jnp.sum(pend[None, :] <= block_row0[:, None], axis=1), N_EXPERTS - 1).astype(jnp.int32)
    rows_valid = jnp.sum(jnp.where(in_block, jnp.clip(pstart + counts - block_row0[:, None], 0, tm), 0),
                         axis=1).astype(jnp.int32)

    pos = _positions(idx_t, rank_t, pstart.astype(jnp.int32))
    xs = _dispatch_rows(pos, xn, n_blocks * tm)
    ys = _expert_ffn(block_e, rows_valid, xs, w_gate_up[0], b_gate_up[0], w_down[0], b_down[0])
    out = _combine(h, ys, pos, gate_t.T, final_norm_w)
    return out.reshape(batch, seq, D_MODEL)
```

```python
import functools

import numpy as np
import jax
import jax.numpy as jnp
from jax import lax
from jax.experimental import pallas as pl
from jax.experimental.pallas import tpu as pltpu

F32 = jnp.float32
BF16 = jnp.bfloat16
HIGHEST = lax.Precision.HIGHEST

D_MODEL = 1024
ATTN_HEAD_DIM = 64
ATTN_WIDTH = 512
ATTN_HEADS = ATTN_WIDTH // ATTN_HEAD_DIM
DILATION_PATTERNS = ((128, 1), (512, 4), (2048, 16))
ATTN_BLOCK = 128
REL_BUCKETS = 32
REL_MAX_DISTANCE = 2048
HGRN_HEAD_DIM = 128
HGRN_WIDTH = 512
HGRN_HEADS = HGRN_WIDTH // HGRN_HEAD_DIM
HGRN_CHUNK = 64
HGRN_SUB = 8
N_EXPERTS = 32
TOP_K = 4
D_EXPERT = 1024
SWIGLU_LIMIT = 7.0
SWIGLU_ALPHA = 1.702
RMS_EPS = 1e-5
IN_PROJ_WIDTH = 3 * ATTN_WIDTH + 4 * HGRN_WIDTH
LANES = 128
HEAD_PAIRS = ATTN_WIDTH // LANES
ROW_CHUNKS = D_MODEL // LANES

INPROJ_ROWS = 512
OUTPROJ_ROWS = 256
FFN_ROWS = 512
FFN_COLS = 512
COMBINE_ROWS = 256
HGRN_SEQ_BLOCK = 1024
ATTN_UNROLL = 4
POS_TOKENS = 2048
MOVE_TOKENS = 512
MOVE_UNROLL = 8
VMEM_LIMIT = 48 * 1024 * 1024
FFN_VMEM_LIMIT = 56 * 1024 * 1024

NT_DIMS = (((1,), (1,)), ((), ()))
TN_DIMS = (((0,), (0,)), ((), ()))


def _cparams(*sem):
    return pltpu.CompilerParams(dimension_semantics=sem, vmem_limit_bytes=VMEM_LIMIT)


def _store_row_tiles(ref, value):
    rows = value.shape[0]
    for c in range(ROW_CHUNKS):
        ref[pl.ds(c, rows, stride=ROW_CHUNKS), :] = value[:, c * LANES:(c + 1) * LANES]


def _load_row_tiles(ref, rows):
    return jnp.concatenate([ref[pl.ds(c, rows, stride=ROW_CHUNKS), :] for c in range(ROW_CHUNKS)], axis=1)


def _inproj_kernel(x_ref, nw_ref, w_ref, qa_ref, ka_ref, va_ref, qh_ref, fh_ref, ih_ref, gh_ref):
    x = x_ref[...]
    xn = x * lax.rsqrt(jnp.mean(x * x, axis=-1, keepdims=True) + RMS_EPS) * nw_ref[...]
    xn = xn.astype(BF16)
    for c, ref in enumerate((qa_ref, ka_ref, va_ref, qh_ref, fh_ref, ih_ref, gh_ref)):
        y = jnp.dot(xn, w_ref[:, c * ATTN_WIDTH:(c + 1) * ATTN_WIDTH], preferred_element_type=F32)
        ref[...] = y.astype(ref.dtype)


def _inproj(x2, norm_w, w_in):
    n = x2.shape[0]
    tm = INPROJ_ROWS
    row = lambda width: pl.BlockSpec((tm, width), lambda i: (i, 0))
    dtypes = (F32, F32, F32, BF16, F32, BF16, F32)
    return pl.pallas_call(
        _inproj_kernel,
        grid=(n // tm,),
        in_specs=[row(D_MODEL),
                  pl.BlockSpec((1, D_MODEL), lambda i: (0, 0)),
                  pl.BlockSpec((D_MODEL, IN_PROJ_WIDTH), lambda i: (0, 0))],
        out_specs=[row(ATTN_WIDTH)] * 7,
        out_shape=[jax.ShapeDtypeStruct((n, ATTN_WIDTH), dt) for dt in dtypes],
        compiler_params=_cparams("arbitrary"),
        name="inproj",
    )(x2, norm_w.reshape(1, D_MODEL), w_in.astype(BF16))


def _t5_causal_bucket(dist):
    n = np.maximum(dist, 0)
    max_exact = REL_BUCKETS // 2
    large = max_exact + (np.log(np.maximum(n, 1) / max_exact)
                         / np.log(REL_MAX_DISTANCE / max_exact)
                         * (REL_BUCKETS - max_exact)).astype(np.int32)
    large = np.minimum(large, REL_BUCKETS - 1)
    return np.where(n < max_exact, n, large).astype(np.int32)


def _bucket_thresholds(max_dist):
    buckets = _t5_causal_bucket(np.arange(max_dist + 1))
    assert np.all(np.diff(buckets) >= 0)
    return [(b, int(np.argmax(buckets >= b))) for b in range(1, REL_BUCKETS) if np.any(buckets >= b)]


def _bias_kernel(rel_ref, o_ref):
    blk = ATTN_BLOCK
    h = pl.program_id(0)
    qi = lax.broadcasted_iota(jnp.int32, (blk, 2 * blk), 0)
    kj = lax.broadcasted_iota(jnp.int32, (blk, 2 * blk), 1)
    dist_sub = blk + qi - kj
    for pi, (window, dilation) in enumerate(DILATION_PATTERNS):
        span = window // dilation
        dist = dist_sub * dilation
        val = jnp.full((blk, 2 * blk), rel_ref[h], F32)
        for b, first_dist in _bucket_thresholds(window):
            val = jnp.where(dist >= first_dist, rel_ref[b * ATTN_HEADS + h], val)
        o_ref[pi, 0] = jnp.where((dist_sub >= 0) & (dist_sub <= span), val, -1e30)


def _masked_bias(rel_bias):
    blk = ATTN_BLOCK
    npat = len(DILATION_PATTERNS)
    bias = pl.pallas_call(
        _bias_kernel,
        grid=(ATTN_HEADS,),
        in_specs=[pl.BlockSpec(memory_space=pltpu.SMEM)],
        out_specs=pl.BlockSpec((npat, 1, blk, 2 * blk), lambda h: (0, h, 0, 0)),
        out_shape=jax.ShapeDtypeStruct((npat, ATTN_HEADS, blk, 2 * blk), F32),
        compiler_params=_cparams("arbitrary"),
        name="rel_bias",
    )(rel_bias.astype(F32).reshape(REL_BUCKETS * ATTN_HEADS))
    return bias.reshape(npat, HEAD_PAIRS, 2 * blk, 2 * blk)


def _attn_kernel(q_ref, k_ref, v_ref, bias_ref, o_ref, acc_ref, lse_ref, *, seq):
    blk = ATTN_BLOCK
    lane = lax.broadcasted_iota(jnp.int32, (1, LANES), 1)
    first = lane < ATTN_HEAD_DIM
    scale = ATTN_HEAD_DIM ** -0.5
    sel0 = jnp.where(first, scale, 0.0)
    sel1 = jnp.where(first, 0.0, scale)
    ones_cols = jnp.ones((2 * blk, LANES), BF16)

    def one_block(pi, dil, q_start, k_start, n_keys):
        qb = q_ref[0, pl.ds(q_start, blk, stride=dil), :]
        q2 = jnp.concatenate([qb * sel0, qb * sel1], axis=0).astype(BF16)
        kk = k_ref[0, pl.ds(k_start, n_keys, stride=dil), :].astype(BF16)
        vv = v_ref[0, pl.ds(k_start, n_keys, stride=dil), :].astype(BF16)
        s = lax.dot_general(q2, kk, NT_DIMS, preferred_element_type=F32)
        s = s + bias_ref[pi, 0, :, 2 * blk - n_keys:]
        m = jnp.max(s, axis=-1, keepdims=True)
        p = jnp.exp(s - m).astype(BF16)
        ov = jnp.dot(p, jnp.concatenate([vv, ones_cols[:n_keys]], axis=1), preferred_element_type=F32)
        z = ov[:, LANES:]
        o = ov[:, :LANES] / z
        lse = m + jnp.log(z)
        rows = pl.ds(q_start, blk, stride=dil)
        acc_ref[pi, rows, :] = jnp.where(first, o[:blk], o[blk:])
        lse_ref[pi, rows, :] = jnp.where(first, lse[:blk], lse[blk:])

    for pi, (_, dil) in enumerate(DILATION_PATTERNS):
        nb = seq // (dil * blk)

        def residue(r, carry, pi=pi, dil=dil, nb=nb):
            one_block(pi, dil, r, r, blk)

            def body(j, c):
                q_start = r + j * (blk * dil)
                one_block(pi, dil, q_start, q_start - blk * dil, 2 * blk)
                return c

            return lax.fori_loop(1, nb, body, carry, unroll=ATTN_UNROLL)

        lax.fori_loop(0, dil, residue, 0, unroll=ATTN_UNROLL if nb == 2 else 1)

    def merge(i, carry):
        rows = pl.ds(pl.multiple_of(i * blk, blk), blk)
        ls = [lse_ref[pi, rows, :] for pi in range(len(DILATION_PATTERNS))]
        m = functools.reduce(jnp.maximum, ls)
        es = [jnp.exp(l - m) for l in ls]
        num = sum(e * acc_ref[pi, rows, :] for pi, e in enumerate(es))
        o_ref[0, rows, :] = (num / sum(es)).astype(o_ref.dtype)
        return carry

    lax.fori_loop(0, seq // blk, merge, 0)


def _dilated_attention(qa, ka, va, bias, batch, seq):
    npat = len(DILATION_PATTERNS)
    shp = (batch, seq, ATTN_WIDTH)
    spec = pl.BlockSpec((1, seq, LANES), lambda b, p: (b, 0, p))
    out = pl.pallas_call(
        functools.partial(_attn_kernel, seq=seq),
        grid=(batch, HEAD_PAIRS),
        in_specs=[spec, spec, spec,
                  pl.BlockSpec((npat, 1, 2 * ATTN_BLOCK, 2 * ATTN_BLOCK), lambda b, p: (0, p, 0, 0))],
        out_specs=spec,
        out_shape=jax.ShapeDtypeStruct(shp, BF16),
        scratch_shapes=[pltpu.VMEM((npat, seq, LANES), F32), pltpu.VMEM((npat, seq, LANES), F32)],
        compiler_params=_cparams("arbitrary", "arbitrary"),
        name="dilated_attn",
    )(qa.reshape(shp), ka.reshape(shp), va.reshape(shp), bias)
    return out.reshape(batch * seq, ATTN_WIDTH)


def _hgrn_kernel(q_ref, f_ref, i_ref, g_ref, lbl_ref, nw_ref, o_ref, st_ref, *, n_chunks):
    c, sub, dk = HGRN_CHUNK, HGRN_SUB, HGRN_HEAD_DIM
    nsub = c // sub
    lbl = lbl_ref[...]
    e = jnp.exp(lbl - jnp.max(lbl, axis=0, keepdims=True))
    lb_all = e[0:1] / jnp.sum(e, axis=0, keepdims=True)
    nw = nw_ref[...]

    r64 = lax.broadcasted_iota(jnp.int32, (c, c), 0)
    c64 = lax.broadcasted_iota(jnp.int32, (c, c), 1)
    tril = (r64 >= c64).astype(F32)
    col_sub = c64 // sub
    col_in_sub = c64 - (r64 // sub) * sub
    t_iota = lax.broadcasted_iota(jnp.int32, (nsub, sub, dk), 1)
    ones_rhs = jnp.ones((dk, c), BF16)

    @pl.when(pl.program_id(1) == 0)
    def _():
        st_ref[...] = jnp.zeros_like(st_ref)

    def cols_of(hd):
        return slice(hd * dk, (hd + 1) * dk)

    def gates(rows, hd):
        cols = cols_of(hd)
        lb = lb_all[:, cols]
        q = q_ref[0, rows, cols].astype(F32)
        f = lb + (1.0 - lb) * jax.nn.sigmoid(f_ref[0, rows, cols])
        cum = jnp.dot(tril, jnp.log(f), precision=HIGHEST, preferred_element_type=F32)
        return cum, 1.0 - f, q * jax.nn.sigmoid(q) * (dk ** -0.5)

    def products(rows, hd, cum, key, qf):
        v = i_ref[0, rows, cols_of(hd)]
        last = cum[c - 1:c, :]
        st = st_ref[hd]
        o = lax.dot_general((qf * jnp.exp(cum)).astype(BF16), st.astype(BF16), NT_DIMS,
                            preferred_element_type=F32)
        kd = (key * jnp.exp(last - cum)).astype(BF16)
        st_ref[hd] = st * jnp.exp(last) + lax.dot_general(v, kd, TN_DIMS, preferred_element_type=F32)

        cum3 = cum.reshape(nsub, sub, dk)
        key3 = key.reshape(nsub, sub, dk)
        qf3 = qf.reshape(nsub, sub, dk)
        khat = (key3 * jnp.exp(cum3[:, sub - 1:sub, :] - cum3)).reshape(c, dk).astype(BF16)
        qs = []
        for j in range(nsub - 1):
            t0 = sub * (j + 1)
            qs.append(qf[t0:] * jnp.exp(cum[t0:] - cum[t0 - 1:t0, :]))
        prod = lax.dot_general(jnp.concatenate(qs, axis=0).astype(BF16), khat, NT_DIMS,
                               preferred_element_type=F32)

        ws = []
        for s in range(sub):
            dec = jnp.exp(jnp.where(t_iota >= s, cum3 - cum3[:, s:s + 1, :], -jnp.inf))
            ws.append((qf3 * key3[:, s:s + 1, :] * dec).reshape(c, dk))
        sums = jnp.dot(jnp.concatenate(ws, axis=0).astype(BF16), ones_rhs,
                       preferred_element_type=F32)
        return o, prod, sums

    def finish(rows, hd, o, prod, sums):
        cols = cols_of(hd)
        a = jnp.zeros((c, c), F32)
        r = 0
        for j in range(nsub - 1):
            t0 = sub * (j + 1)
            pj = jnp.concatenate([jnp.zeros((t0, c), F32), prod[r:r + c - t0, :]], axis=0)
            a = a + jnp.where(col_sub == j, pj, 0.0)
            r += c - t0
        for s in range(sub):
            a = a + jnp.where(col_in_sub == s, sums[s * c:(s + 1) * c, :], 0.0)
        o = o + jnp.dot(a.astype(BF16), i_ref[0, rows, cols], preferred_element_type=F32)
        on = o * lax.rsqrt(jnp.mean(o * o, axis=-1, keepdims=True) + RMS_EPS) * nw
        g = g_ref[0, rows, cols]
        o_ref[0, rows, cols] = (on * (g * jax.nn.sigmoid(g))).astype(BF16)

    def body(ci, carry):
        rows = pl.ds(pl.multiple_of(ci * c, c), c)
        heads = range(HGRN_HEADS)
        stage1 = [gates(rows, hd) for hd in heads]
        stage2 = [products(rows, hd, *stage1[hd]) for hd in heads]
        for hd in heads:
            finish(rows, hd, *stage2[hd])
        return carry

    lax.fori_loop(0, n_chunks, body, 0)


def _hgrn(qh, fh, ih, gh, lb_logits, norm_w, batch, seq):
    sb = min(seq, HGRN_SEQ_BLOCK)
    spec = pl.BlockSpec((1, sb, HGRN_WIDTH), lambda b, s: (b, s, 0))
    slots = lb_logits.shape[0]
    shp = (batch, seq, HGRN_WIDTH)
    out = pl.pallas_call(
        functools.partial(_hgrn_kernel, n_chunks=sb // HGRN_CHUNK),
        grid=(batch, seq // sb),
        in_specs=[spec, spec, spec, spec,
                  pl.BlockSpec((slots, HGRN_WIDTH), lambda b, s: (0, 0)),
                  pl.BlockSpec((1, HGRN_HEAD_DIM), lambda b, s: (0, 0))],
        out_specs=spec,
        out_shape=jax.ShapeDtypeStruct(shp, BF16),
        scratch_shapes=[pltpu.VMEM((HGRN_HEADS, HGRN_HEAD_DIM, HGRN_HEAD_DIM), F32)],
        compiler_params=_cparams("arbitrary", "arbitrary"),
        name="hgrn2",
    )(qh.reshape(shp), fh.reshape(shp), ih.reshape(shp), gh.reshape(shp),
      lb_logits.astype(F32), norm_w.reshape(1, HGRN_HEAD_DIM))
    return out.reshape(batch * seq, HGRN_WIDTH)


def _outproj_kernel(attn_ref, hg_ref, x_ref, w_ref, nw_ref, rw_ref, rb_ref,
                    h_ref, xn_ref, idx_ref, gate_ref, rank_ref, cnt_ref, run_ref):
    y = jnp.dot(attn_ref[...], w_ref[0:ATTN_WIDTH, :], preferred_element_type=F32)
    y = y + jnp.dot(hg_ref[...], w_ref[ATTN_WIDTH:, :], preferred_element_type=F32)
    h = x_ref[...] + y
    h_ref[...] = h
    xn = h * lax.rsqrt(jnp.mean(h * h, axis=-1, keepdims=True) + RMS_EPS) * nw_ref[...]
    _store_row_tiles(xn_ref, xn)

    logits = lax.dot_general(rw_ref[...], xn, NT_DIMS, precision=HIGHEST,
                             preferred_element_type=F32) + rb_ref[...]
    eid = lax.broadcasted_iota(jnp.int32, logits.shape, 0)
    vals, idxs = [], []
    for _ in range(TOP_K):
        mx = jnp.max(logits, axis=0, keepdims=True)
        ix = jnp.min(jnp.where(logits == mx, eid, N_EXPERTS), axis=0, keepdims=True)
        vals.append(mx)
        idxs.append(ix)
        logits = jnp.where(eid == ix, -jnp.inf, logits)
    es = [jnp.exp(v - vals[0]) for v in vals]
    den = es[0] + es[1] + es[2] + es[3]
    idx_ref[...] = jnp.concatenate(idxs, axis=0)
    gate_ref[...] = jnp.concatenate([e / den for e in es], axis=0)

    @pl.when(pl.program_id(0) == 0)
    def _():
        run_ref[...] = jnp.zeros_like(run_ref)

    rows = logits.shape[1]
    onehots = [(eid == ix).astype(F32) for ix in idxs]
    routed = onehots[0] + onehots[1] + onehots[2] + onehots[3]
    earlier = (lax.broadcasted_iota(jnp.int32, (rows, rows), 0)
               < lax.broadcasted_iota(jnp.int32, (rows, rows), 1)).astype(BF16)
    before = jnp.dot(routed.astype(BF16), earlier, preferred_element_type=F32) + run_ref[...]
    rank_ref[...] = jnp.concatenate([jnp.sum(o * before, axis=0, keepdims=True) for o in onehots],
                                    axis=0).astype(jnp.int32)
    run_ref[...] = run_ref[...] + jnp.sum(routed, axis=1, keepdims=True)
    cnt_ref[...] = run_ref[...].astype(jnp.int32)


def _outproj(attn, hg, x2, w_out, norm_w, router_w, router_b):
    n = x2.shape[0]
    tm = OUTPROJ_ROWS
    row = lambda width: pl.BlockSpec((tm, width), lambda i: (i, 0))
    full = lambda a, b: pl.BlockSpec((a, b), lambda i: (0, 0))
    tok = pl.BlockSpec((TOP_K, tm), lambda i: (0, i))
    return pl.pallas_call(
        _outproj_kernel,
        grid=(n // tm,),
        in_specs=[row(ATTN_WIDTH), row(HGRN_WIDTH), row(D_MODEL), full(D_MODEL, D_MODEL),
                  full(1, D_MODEL), full(N_EXPERTS, D_MODEL), full(N_EXPERTS, 1)],
        out_specs=[row(D_MODEL), pl.BlockSpec((tm * ROW_CHUNKS, LANES), lambda i: (i, 0)),
                   tok, tok, tok, full(N_EXPERTS, 1)],
        out_shape=[jax.ShapeDtypeStruct((n, D_MODEL), F32),
                   jax.ShapeDtypeStruct((n * ROW_CHUNKS, LANES), F32),
                   jax.ShapeDtypeStruct((TOP_K, n), jnp.int32),
                   jax.ShapeDtypeStruct((TOP_K, n), F32),
                   jax.ShapeDtypeStruct((TOP_K, n), jnp.int32),
                   jax.ShapeDtypeStruct((N_EXPERTS, 1), jnp.int32)],
        scratch_shapes=[pltpu.VMEM((N_EXPERTS, 1), F32)],
        compiler_params=_cparams("arbitrary"),
        name="outproj_router",
    )(attn, hg, x2, w_out.astype(BF16), norm_w.reshape(1, D_MODEL),
      router_w.T.astype(F32), router_b.astype(F32).reshape(N_EXPERTS, 1))


def _ffn_kernel(be_ref, rv_ref, xs_ref, wgu_ref, bgu_ref, wdn_ref, bdn_ref, o_ref, wgu_bf, wdn_bf):
    i = pl.program_id(0)
    rows_valid = rv_ref[i]
    new_expert = (i == 0) | (be_ref[i] != be_ref[jnp.maximum(i - 1, 0)])

    @pl.when(new_expert & (rows_valid > 0))
    def _():
        step = 128
        for r in range(0, D_MODEL, step):
            wgu_bf[r:r + step, :] = wgu_ref[0, r:r + step, :].astype(BF16)
        for r in range(0, D_EXPERT, step):
            wdn_bf[r:r + step, :] = wdn_ref[0, r:r + step, :].astype(BF16)

    @pl.when(rows_valid > 0)
    def _():
        live = lax.broadcasted_iota(jnp.int32, (FFN_ROWS, 1), 0) < rows_valid
        x = jnp.where(live, _load_row_tiles(xs_ref, FFN_ROWS), 0.0).astype(BF16)
        out = bdn_ref[0]
        for c0 in range(0, D_EXPERT, FFN_COLS):
            g_cols = slice(c0, c0 + FFN_COLS)
            u_cols = slice(D_EXPERT + c0, D_EXPERT + c0 + FFN_COLS)
            gate = jnp.dot(x, wgu_bf[:, g_cols], preferred_element_type=F32) + bgu_ref[0, :, g_cols]
            up = jnp.dot(x, wgu_bf[:, u_cols], preferred_element_type=F32) + bgu_ref[0, :, u_cols]
            gate = jnp.minimum(gate, SWIGLU_LIMIT)
            up = jnp.clip(up, -SWIGLU_LIMIT, SWIGLU_LIMIT)
            act = gate * jax.nn.sigmoid(SWIGLU_ALPHA * gate) * (up + 1.0)
            out = out + jnp.dot(act.astype(BF16), wdn_bf[g_cols, :], preferred_element_type=F32)
        _store_row_tiles(o_ref, out)

    @pl.when(rows_valid <= 0)
    def _():
        o_ref[...] = jnp.zeros_like(o_ref)


def _expert_ffn(block_e, rows_valid, xs, w_gu, b_gu, w_dn, b_dn):
    tm = FFN_ROWS
    tiles = pl.BlockSpec((tm * ROW_CHUNKS, LANES), lambda i, be, rv: (i, 0))
    grid_spec = pltpu.PrefetchScalarGridSpec(
        num_scalar_prefetch=2,
        grid=(xs.shape[0] // (tm * ROW_CHUNKS),),
        in_specs=[tiles,
                  pl.BlockSpec((1, D_MODEL, 2 * D_EXPERT), lambda i, be, rv: (be[i], 0, 0)),
                  pl.BlockSpec((1, 1, 2 * D_EXPERT), lambda i, be, rv: (be[i], 0, 0)),
                  pl.BlockSpec((1, D_EXPERT, D_MODEL), lambda i, be, rv: (be[i], 0, 0)),
                  pl.BlockSpec((1, 1, D_MODEL), lambda i, be, rv: (be[i], 0, 0))],
        out_specs=tiles,
        scratch_shapes=[pltpu.VMEM((D_MODEL, 2 * D_EXPERT), BF16), pltpu.VMEM((D_EXPERT, D_MODEL), BF16)],
    )
    return pl.pallas_call(
        _ffn_kernel,
        grid_spec=grid_spec,
        out_shape=jax.ShapeDtypeStruct(xs.shape, F32),
        compiler_params=pltpu.CompilerParams(dimension_semantics=("arbitrary",),
                                             vmem_limit_bytes=FFN_VMEM_LIMIT),
        name="expert_ffn",
    )(block_e, rows_valid, xs, w_gu, b_gu.reshape(N_EXPERTS, 1, 2 * D_EXPERT),
      w_dn, b_dn.reshape(N_EXPERTS, 1, D_MODEL))


def _pos_kernel(idx_ref, rank_ref, pstart_ref, pos_ref):
    eid = lax.broadcasted_iota(jnp.int32, (N_EXPERTS, idx_ref.shape[1]), 0)
    pstart = pstart_ref[...]
    rows = [jnp.sum(jnp.where(eid == idx_ref[k:k + 1, :], pstart, 0), axis=0, keepdims=True)
            for k in range(TOP_K)]
    pos_ref[...] = (jnp.concatenate(rows, axis=0) + rank_ref[...]) * ROW_CHUNKS


def _positions(idx_t, rank_t, pstart):
    n = idx_t.shape[1]
    tl = POS_TOKENS
    tok = pl.BlockSpec((TOP_K, tl), lambda i: (0, i))
    return pl.pallas_call(
        _pos_kernel,
        grid=(n // tl,),
        in_specs=[tok, tok, pl.BlockSpec((N_EXPERTS, 1), lambda i: (0, 0))],
        out_specs=tok,
        out_shape=jax.ShapeDtypeStruct((TOP_K, n), jnp.int32),
        compiler_params=_cparams("arbitrary"),
        name="dispatch_pos",
    )(idx_t, rank_t, pstart.reshape(N_EXPERTS, 1))


def _tile_positions(pos, tokens):
    n = pos.shape[1]
    return pos.T.reshape(n // tokens, 1, tokens * TOP_K)


def _row_tile(ref, row_offset):
    return ref.at[pl.ds(pl.multiple_of(row_offset, ROW_CHUNKS), ROW_CHUNKS)]


def _dispatch_kernel(pos_ref, x_ref, xs_ref, sem):
    tokens = x_ref.shape[0] // ROW_CHUNKS

    def copy(t, k):
        return pltpu.make_async_copy(_row_tile(x_ref, t * ROW_CHUNKS),
                                     _row_tile(xs_ref, pos_ref[0, 0, t * TOP_K + k]), sem)

    def issue(t, carry):
        for k in range(TOP_K):
            copy(t, k).start(priority=k % 2)
        return carry

    def drain(t, carry):
        for k in range(TOP_K):
            copy(t, k).wait()
        return carry

    lax.fori_loop(0, tokens, issue, 0, unroll=MOVE_UNROLL)
    lax.fori_loop(0, tokens, drain, 0, unroll=MOVE_UNROLL)


def _dispatch_rows(pos, xn, out_rows):
    n = xn.shape[0] // ROW_CHUNKS
    tl = MOVE_TOKENS
    return pl.pallas_call(
        _dispatch_kernel,
        grid=(n // tl,),
        in_specs=[pl.BlockSpec((1, 1, tl * TOP_K), lambda i: (i, 0, 0), memory_space=pltpu.SMEM),
                  pl.BlockSpec((tl * ROW_CHUNKS, LANES), lambda i: (i, 0))],
        out_specs=pl.BlockSpec(memory_space=pl.ANY),
        out_shape=jax.ShapeDtypeStruct((out_rows * ROW_CHUNKS, LANES), xn.dtype),
        scratch_shapes=[pltpu.SemaphoreType.DMA(())],
        compiler_params=_cparams("arbitrary"),
        name="dispatch_rows",
    )(_tile_positions(pos, tl), xn)


def _combine_kernel(pos_ref, pos_next_ref, h_ref, g_ref, fw_ref, ys_ref, o_ref, buf, sems):
    i = pl.program_id(0)
    steps = pl.num_programs(0)
    tokens = h_ref.shape[0]
    slot = lax.rem(i, 2)

    def gather(p_ref, s, start):
        def body(t, carry):
            for k in range(TOP_K):
                cp = pltpu.make_async_copy(_row_tile(ys_ref, p_ref[0, 0, t * TOP_K + k]),
                                           _row_tile(buf.at[s, k], t * ROW_CHUNKS), sems.at[s])
                cp.start(priority=k % 2) if start else cp.wait()
            return carry
        lax.fori_loop(0, tokens, body, 0, unroll=MOVE_UNROLL)

    @pl.when(i == 0)
    def _():
        gather(pos_ref, 0, True)

    @pl.when(i + 1 < steps)
    def _():
        gather(pos_next_ref, 1 - slot, True)

    gather(pos_ref, slot, False)
    g = g_ref[...]
    y = h_ref[...]
    for k in range(TOP_K):
        y = y + _load_row_tiles(buf.at[slot, k], tokens) * g[:, k:k + 1]
    o_ref[...] = y * lax.rsqrt(jnp.mean(y * y, axis=-1, keepdims=True) + RMS_EPS) * fw_ref[...]


def _combine(h, ys, pos, gates_nk, final_w):
    n = h.shape[0]
    tm = COMBINE_ROWS
    steps = n // tm
    pos_t = _tile_positions(pos, tm)
    smem = lambda imap: pl.BlockSpec((1, 1, tm * TOP_K), imap, memory_space=pltpu.SMEM)
    return pl.pallas_call(
        _combine_kernel,
        grid=(steps,),
        in_specs=[smem(lambda i: (i, 0, 0)),
                  smem(lambda i: (jnp.minimum(i + 1, steps - 1), 0, 0)),
                  pl.BlockSpec((tm, D_MODEL), lambda i: (i, 0)),
                  pl.BlockSpec((tm, TOP_K), lambda i: (i, 0)),
                  pl.BlockSpec((1, D_MODEL), lambda i: (0, 0)),
                  pl.BlockSpec(memory_space=pl.ANY)],
        out_specs=pl.BlockSpec((tm, D_MODEL), lambda i: (i, 0)),
        out_shape=jax.ShapeDtypeStruct((n, D_MODEL), F32),
        scratch_shapes=[pltpu.VMEM((2, TOP_K, tm * ROW_CHUNKS, LANES), F32), pltpu.SemaphoreType.DMA((2,))],
        compiler_params=_cparams("arbitrary"),
        name="combine_norm",
    )(pos_t, pos_t, h, gates_nk, final_w.reshape(1, D_MODEL), ys)


def kernel(x, norm_mix_w, w_in, rel_bias, hgrn_lb_logits, hgrn_norm_w, w_out, norm_ffn_w,
           router_w, router_b, w_gate_up, b_gate_up, w_down, b_down, final_norm_w):
    batch, seq, _ = x.shape
    n = batch * seq
    x2 = x.reshape(n, D_MODEL)

    qa, ka, va, qh, fh, ih, gh = _inproj(x2, norm_mix_w[0], w_in[0])
    attn = _dilated_attention(qa, ka, va, _masked_bias(rel_bias), batch, seq)
    hg = _hgrn(qh, fh, ih, gh, hgrn_lb_logits, hgrn_norm_w[0], batch, seq)
    h, xn, idx_t, gate_t, rank_t, counts = _outproj(attn, hg, x2, w_out[0], norm_ffn_w[0],
                                                    router_w[0], router_b[0])

    tm = FFN_ROWS
    n_blocks = -(-(n * TOP_K) // tm) + N_EXPERTS
    counts = counts.reshape(N_EXPERTS)
    padded = (counts + tm - 1) // tm * tm
    pend = jnp.cumsum(padded)
    pstart = pend - padded
    block_row0 = jnp.arange(n_blocks, dtype=jnp.int32) * tm
    in_block = (block_row0[:, None] >= pstart[None, :]) & (block_row0[:, None] < pend[None, :])
    block_e = jnp.minimum(jnp.sum(pend[None, :] <= block_row0[:, None], axis=1), N_EXPERTS - 1).astype(jnp.int32)
    rows_valid = jnp.sum(jnp.where(in_block, jnp.clip(pstart + counts - block_row0[:, None], 0, tm), 0),
                         axis=1).astype(jnp.int32)

    pos = _positions(idx_t, rank_t, pstart.astype(jnp.int32))
    xs = _dispatch_rows(pos, xn, n_blocks * tm)
    ys = _expert_ffn(block_e, rows_valid, xs, w_gate_up[0], b_gate_up[0], w_down[0], b_down[0])
    out = _combine(h, ys, pos, gate_t.T, final_norm_w)
    return out.reshape(batch, seq, D_MODEL)
```

```python
import functools

import numpy as np
import jax
import jax.numpy as jnp
from jax import lax
from jax.experimental import pallas as pl
from jax.experimental.pallas import tpu as pltpu

F32 = jnp.float32
BF16 = jnp.bfloat16
HIGHEST = lax.Precision.HIGHEST

D_MODEL = 1024
ATTN_HEAD_DIM = 64
ATTN_WIDTH = 512
ATTN_HEADS = ATTN_WIDTH // ATTN_HEAD_DIM
DILATION_PATTERNS = ((128, 1), (512, 4), (2048, 16))
ATTN_BLOCK = 128
REL_BUCKETS = 32
REL_MAX_DISTANCE = 2048
HGRN_HEAD_DIM = 128
HGRN_WIDTH = 512
HGRN_HEADS = HGRN_WIDTH // HGRN_HEAD_DIM
HGRN_CHUNK = 64
HGRN_SUB = 8
N_EXPERTS = 32
TOP_K = 4
D_EXPERT = 1024
SWIGLU_LIMIT = 7.0
SWIGLU_ALPHA = 1.702
RMS_EPS = 1e-5
IN_PROJ_WIDTH = 3 * ATTN_WIDTH + 4 * HGRN_WIDTH
LANES = 128
HEAD_PAIRS = ATTN_WIDTH // LANES
ROW_CHUNKS = D_MODEL // LANES

INPROJ_ROWS = 512
OUTPROJ_ROWS = 1024
RANK_SPAN = 256
FFN_ROWS = 512
FFN_COLS = 512
COMBINE_ROWS = 256
HGRN_SEQ_BLOCK = 1024
ATTN_UNROLL = 4
POS_TOKENS = 2048
MOVE_TOKENS = 512
MOVE_UNROLL = 8
VMEM_LIMIT = 48 * 1024 * 1024
FFN_VMEM_LIMIT = 56 * 1024 * 1024

NT_DIMS = (((1,), (1,)), ((), ()))
TN_DIMS = (((0,), (0,)), ((), ()))


def _cparams(*sem):
    return pltpu.CompilerParams(dimension_semantics=sem, vmem_limit_bytes=VMEM_LIMIT)


def _store_row_tiles(ref, value):
    rows = value.shape[0]
    for c in range(ROW_CHUNKS):
        ref[pl.ds(c, rows, stride=ROW_CHUNKS), :] = value[:, c * LANES:(c + 1) * LANES]


def _load_row_tiles(ref, rows):
    return jnp.concatenate([ref[pl.ds(c, rows, stride=ROW_CHUNKS), :] for c in range(ROW_CHUNKS)], axis=1)


def _inproj_kernel(x_ref, nw_ref, w_ref, qa_ref, ka_ref, va_ref, qh_ref, fh_ref, ih_ref, gh_ref):
    x = x_ref[...]
    xn = x * lax.rsqrt(jnp.mean(x * x, axis=-1, keepdims=True) + RMS_EPS) * nw_ref[...]
    xn = xn.astype(BF16)
    for c, ref in enumerate((qa_ref, ka_ref, va_ref, qh_ref, fh_ref, ih_ref, gh_ref)):
        y = jnp.dot(xn, w_ref[:, c * ATTN_WIDTH:(c + 1) * ATTN_WIDTH], preferred_element_type=F32)
        ref[...] = y.astype(ref.dtype)


def _inproj(x2, norm_w, w_in):
    n = x2.shape[0]
    tm = INPROJ_ROWS
    row = lambda width: pl.BlockSpec((tm, width), lambda i: (i, 0))
    dtypes = (F32, F32, F32, BF16, F32, BF16, F32)
    return pl.pallas_call(
        _inproj_kernel,
        grid=(n // tm,),
        in_specs=[row(D_MODEL),
                  pl.BlockSpec((1, D_MODEL), lambda i: (0, 0)),
                  pl.BlockSpec((D_MODEL, IN_PROJ_WIDTH), lambda i: (0, 0))],
        out_specs=[row(ATTN_WIDTH)] * 7,
        out_shape=[jax.ShapeDtypeStruct((n, ATTN_WIDTH), dt) for dt in dtypes],
        compiler_params=_cparams("arbitrary"),
        name="inproj",
    )(x2, norm_w.reshape(1, D_MODEL), w_in.astype(BF16))


def _t5_causal_bucket(dist):
    n = np.maximum(dist, 0)
    max_exact = REL_BUCKETS // 2
    large = max_exact + (np.log(np.maximum(n, 1) / max_exact)
                         / np.log(REL_MAX_DISTANCE / max_exact)
                         * (REL_BUCKETS - max_exact)).astype(np.int32)
    large = np.minimum(large, REL_BUCKETS - 1)
    return np.where(n < max_exact, n, large).astype(np.int32)


def _bucket_thresholds(max_dist):
    buckets = _t5_causal_bucket(np.arange(max_dist + 1))
    assert np.all(np.diff(buckets) >= 0)
    return [(b, int(np.argmax(buckets >= b))) for b in range(1, REL_BUCKETS) if np.any(buckets >= b)]


def _bias_kernel(rel_ref, o_ref):
    blk = ATTN_BLOCK
    h = pl.program_id(0)
    qi = lax.broadcasted_iota(jnp.int32, (blk, 2 * blk), 0)
    kj = lax.broadcasted_iota(jnp.int32, (blk, 2 * blk), 1)
    dist_sub = blk + qi - kj
    for pi, (window, dilation) in enumerate(DILATION_PATTERNS):
        span = window // dilation
        dist = dist_sub * dilation
        val = jnp.full((blk, 2 * blk), rel_ref[h], F32)
        for b, first_dist in _bucket_thresholds(window):
            val = jnp.where(dist >= first_dist, rel_ref[b * ATTN_HEADS + h], val)
        o_ref[pi, 0] = jnp.where((dist_sub >= 0) & (dist_sub <= span), val, -1e30)


def _masked_bias(rel_bias):
    blk = ATTN_BLOCK
    npat = len(DILATION_PATTERNS)
    bias = pl.pallas_call(
        _bias_kernel,
        grid=(ATTN_HEADS,),
        in_specs=[pl.BlockSpec(memory_space=pltpu.SMEM)],
        out_specs=pl.BlockSpec((npat, 1, blk, 2 * blk), lambda h: (0, h, 0, 0)),
        out_shape=jax.ShapeDtypeStruct((npat, ATTN_HEADS, blk, 2 * blk), F32),
        compiler_params=_cparams("arbitrary"),
        name="rel_bias",
    )(rel_bias.astype(F32).reshape(REL_BUCKETS * ATTN_HEADS))
    return bias.reshape(npat, HEAD_PAIRS, 2 * blk, 2 * blk)


def _attn_kernel(q_ref, k_ref, v_ref, bias_ref, o_ref, acc_ref, lse_ref, *, seq):
    blk = ATTN_BLOCK
    lane = lax.broadcasted_iota(jnp.int32, (1, LANES), 1)
    first = lane < ATTN_HEAD_DIM
    scale = ATTN_HEAD_DIM ** -0.5
    sel0 = jnp.where(first, scale, 0.0)
    sel1 = jnp.where(first, 0.0, scale)
    ones_cols = jnp.ones((2 * blk, LANES), BF16)

    def one_block(pi, dil, q_start, k_start, n_keys):
        qb = q_ref[0, pl.ds(q_start, blk, stride=dil), :]
        q2 = jnp.concatenate([qb * sel0, qb * sel1], axis=0).astype(BF16)
        kk = k_ref[0, pl.ds(k_start, n_keys, stride=dil), :].astype(BF16)
        vv = v_ref[0, pl.ds(k_start, n_keys, stride=dil), :].astype(BF16)
        s = lax.dot_general(q2, kk, NT_DIMS, preferred_element_type=F32)
        s = s + bias_ref[pi, 0, :, 2 * blk - n_keys:]
        m = jnp.max(s, axis=-1, keepdims=True)
        p = jnp.exp(s - m).astype(BF16)
        ov = jnp.dot(p, jnp.concatenate([vv, ones_cols[:n_keys]], axis=1), preferred_element_type=F32)
        z = ov[:, LANES:]
        o = ov[:, :LANES] / z
        lse = m + jnp.log(z)
        rows = pl.ds(q_start, blk, stride=dil)
        acc_ref[pi, rows, :] = jnp.where(first, o[:blk], o[blk:])
        lse_ref[pi, rows, :] = jnp.where(first, lse[:blk], lse[blk:])

    for pi, (_, dil) in enumerate(DILATION_PATTERNS):
        nb = seq // (dil * blk)

        def residue(r, carry, pi=pi, dil=dil, nb=nb):
            one_block(pi, dil, r, r, blk)

            def body(j, c):
                q_start = r + j * (blk * dil)
                one_block(pi, dil, q_start, q_start - blk * dil, 2 * blk)
                return c

            return lax.fori_loop(1, nb, body, carry, unroll=ATTN_UNROLL)

        lax.fori_loop(0, dil, residue, 0, unroll=ATTN_UNROLL if nb == 2 else 1)

    def merge(i, carry):
        rows = pl.ds(pl.multiple_of(i * blk, blk), blk)
        ls = [lse_ref[pi, rows, :] for pi in range(len(DILATION_PATTERNS))]
        m = functools.reduce(jnp.maximum, ls)
        es = [jnp.exp(l - m) for l in ls]
        num = sum(e * acc_ref[pi, rows, :] for pi, e in enumerate(es))
        o_ref[0, rows, :] = (num / sum(es)).astype(o_ref.dtype)
        return carry

    lax.fori_loop(0, seq // blk, merge, 0)


def _dilated_attention(qa, ka, va, bias, batch, seq):
    npat = len(DILATION_PATTERNS)
    shp = (batch, seq, ATTN_WIDTH)
    spec = pl.BlockSpec((1, seq, LANES), lambda b, p: (b, 0, p))
    out = pl.pallas_call(
        functools.partial(_attn_kernel, seq=seq),
        grid=(batch, HEAD_PAIRS),
        in_specs=[spec, spec, spec,
                  pl.BlockSpec((npat, 1, 2 * ATTN_BLOCK, 2 * ATTN_BLOCK), lambda b, p: (0, p, 0, 0))],
        out_specs=spec,
        out_shape=jax.ShapeDtypeStruct(shp, BF16),
        scratch_shapes=[pltpu.VMEM((npat, seq, LANES), F32), pltpu.VMEM((npat, seq, LANES), F32)],
        compiler_params=_cparams("arbitrary", "arbitrary"),
        name="dilated_attn",
    )(qa.reshape(shp), ka.reshape(shp), va.reshape(shp), bias)
    return out.reshape(batch * seq, ATTN_WIDTH)


def _hgrn_kernel(q_ref, f_ref, i_ref, g_ref, lbl_ref, nw_ref, o_ref, st_ref, *, n_chunks):
    c, sub, dk = HGRN_CHUNK, HGRN_SUB, HGRN_HEAD_DIM
    nsub = c // sub
    lbl = lbl_ref[...]
    e = jnp.exp(lbl - jnp.max(lbl, axis=0, keepdims=True))
    lb_all = e[0:1] / jnp.sum(e, axis=0, keepdims=True)
    nw = nw_ref[...]

    r64 = lax.broadcasted_iota(jnp.int32, (c, c), 0)
    c64 = lax.broadcasted_iota(jnp.int32, (c, c), 1)
    sub_end = (r64 // sub) * sub + (sub - 1)
    decay_sums = jnp.concatenate([(c64 <= r64), (c64 > r64) & (c64 <= sub_end), (c64 > r64)],
                                 axis=0).astype(BF16)
    col_sub = c64 // sub
    col_in_sub = c64 - (r64 // sub) * sub
    t_iota = lax.broadcasted_iota(jnp.int32, (nsub, sub, dk), 1)
    ones_rhs = jnp.ones((dk, c), BF16)

    @pl.when(pl.program_id(1) == 0)
    def _():
        st_ref[...] = jnp.zeros_like(st_ref)

    def cols_of(hd):
        return slice(hd * dk, (hd + 1) * dk)

    def gates(rows, hd):
        cols = cols_of(hd)
        lb = lb_all[:, cols]
        q = q_ref[0, rows, cols].astype(F32)
        f = lb + (1.0 - lb) * jax.nn.sigmoid(f_ref[0, rows, cols])
        log_f = jnp.log(f)
        log_hi = log_f.astype(BF16)
        log_lo = (log_f - log_hi.astype(F32)).astype(BF16)
        sums = (jnp.dot(decay_sums, log_hi, preferred_element_type=F32)
                + jnp.dot(decay_sums, log_lo, preferred_element_type=F32))
        return sums, 1.0 - f, q * jax.nn.sigmoid(q) * (dk ** -0.5)

    def products(rows, hd, sums, key, qf):
        cum, to_sub_end, to_chunk_end = sums[:c], sums[c:2 * c], sums[2 * c:]
        v = i_ref[0, rows, cols_of(hd)]
        st = st_ref[hd]
        o = lax.dot_general((qf * jnp.exp(cum)).astype(BF16), st.astype(BF16), NT_DIMS,
                            preferred_element_type=F32)
        kd = (key * jnp.exp(to_chunk_end)).astype(BF16)
        st_ref[hd] = (st * jnp.exp(cum[c - 1:c, :])
                      + lax.dot_general(v, kd, TN_DIMS, preferred_element_type=F32))

        cum3 = cum.reshape(nsub, sub, dk)
        key3 = key.reshape(nsub, sub, dk)
        qf3 = qf.reshape(nsub, sub, dk)
        khat = (key * jnp.exp(to_sub_end)).astype(BF16)
        qs = []
        for j in range(nsub - 1):
            t0 = sub * (j + 1)
            qs.append(qf[t0:] * jnp.exp(cum[t0:] - cum[t0 - 1:t0, :]))
        prod = lax.dot_general(jnp.concatenate(qs, axis=0).astype(BF16), khat, NT_DIMS,
                               preferred_element_type=F32)

        ws = []
        for s in range(sub):
            dec = jnp.exp(jnp.where(t_iota >= s, cum3 - cum3[:, s:s + 1, :], -jnp.inf))
            ws.append((qf3 * key3[:, s:s + 1, :] * dec).reshape(c, dk))
        pair_sums = jnp.dot(jnp.concatenate(ws, axis=0).astype(BF16), ones_rhs,
                            preferred_element_type=F32)
        return o, prod, pair_sums

    def finish(rows, hd, o, prod, pair_sums):
        cols = cols_of(hd)
        a = jnp.zeros((c, c), F32)
        r = 0
        for j in range(nsub - 1):
            t0 = sub * (j + 1)
            pj = jnp.concatenate([jnp.zeros((t0, c), F32), prod[r:r + c - t0, :]], axis=0)
            a = jnp.where(col_sub == j, pj, a)
            r += c - t0
        for s in range(sub):
            a = jnp.where(col_in_sub == s, pair_sums[s * c:(s + 1) * c, :], a)
        o = o + jnp.dot(a.astype(BF16), i_ref[0, rows, cols], preferred_element_type=F32)
        on = o * lax.rsqrt(jnp.mean(o * o, axis=-1, keepdims=True) + RMS_EPS) * nw
        g = g_ref[0, rows, cols]
        o_ref[0, rows, cols] = (on * (g * jax.nn.sigmoid(g))).astype(BF16)

    def body(ci, carry):
        rows = pl.ds(pl.multiple_of(ci * c, c), c)
        heads = range(HGRN_HEADS)
        stage1 = [gates(rows, hd) for hd in heads]
        stage2 = [products(rows, hd, *stage1[hd]) for hd in heads]
        for hd in heads:
            finish(rows, hd, *stage2[hd])
        return carry

    lax.fori_loop(0, n_chunks, body, 0, unroll=2)


def _hgrn(qh, fh, ih, gh, lb_logits, norm_w, batch, seq):
    sb = min(seq, HGRN_SEQ_BLOCK)
    spec = pl.BlockSpec((1, sb, HGRN_WIDTH), lambda b, s: (b, s, 0))
    slots = lb_logits.shape[0]
    shp = (batch, seq, HGRN_WIDTH)
    out = pl.pallas_call(
        functools.partial(_hgrn_kernel, n_chunks=sb // HGRN_CHUNK),
        grid=(batch, seq // sb),
        in_specs=[spec, spec, spec, spec,
                  pl.BlockSpec((slots, HGRN_WIDTH), lambda b, s: (0, 0)),
                  pl.BlockSpec((1, HGRN_HEAD_DIM), lambda b, s: (0, 0))],
        out_specs=spec,
        out_shape=jax.ShapeDtypeStruct(shp, BF16),
        scratch_shapes=[pltpu.VMEM((HGRN_HEADS, HGRN_HEAD_DIM, HGRN_HEAD_DIM), F32)],
        compiler_params=_cparams("arbitrary", "arbitrary"),
        name="hgrn2",
    )(qh.reshape(shp), fh.reshape(shp), ih.reshape(shp), gh.reshape(shp),
      lb_logits.astype(F32), norm_w.reshape(1, HGRN_HEAD_DIM))
    return out.reshape(batch * seq, HGRN_WIDTH)


def _outproj_kernel(attn_ref, hg_ref, x_ref, w_ref, nw_ref, rw_ref, rb_ref,
                    h_ref, xn_ref, idx_ref, gate_ref, rank_ref, cnt_ref, run_ref):
    y = jnp.dot(attn_ref[...], w_ref[0:ATTN_WIDTH, :], preferred_element_type=F32)
    y = y + jnp.dot(hg_ref[...], w_ref[ATTN_WIDTH:, :], preferred_element_type=F32)
    h = x_ref[...] + y
    h_ref[...] = h
    xn = h * lax.rsqrt(jnp.mean(h * h, axis=-1, keepdims=True) + RMS_EPS) * nw_ref[...]
    _store_row_tiles(xn_ref, xn)

    logits = lax.dot_general(rw_ref[...], xn, NT_DIMS, precision=HIGHEST,
                             preferred_element_type=F32) + rb_ref[...]
    eid = lax.broadcasted_iota(jnp.int32, logits.shape, 0)
    vals, idxs = [], []
    for _ in range(TOP_K):
        mx = jnp.max(logits, axis=0, keepdims=True)
        ix = jnp.min(jnp.where(logits == mx, eid, N_EXPERTS), axis=0, keepdims=True)
        vals.append(mx)
        idxs.append(ix)
        logits = jnp.where(eid == ix, -jnp.inf, logits)
    es = [jnp.exp(v - vals[0]) for v in vals]
    den = es[0] + es[1] + es[2] + es[3]
    idx_ref[...] = jnp.concatenate(idxs, axis=0)
    gate_ref[...] = jnp.concatenate([e / den for e in es], axis=0)

    @pl.when(pl.program_id(0) == 0)
    def _():
        run_ref[...] = jnp.zeros_like(run_ref)

    span = RANK_SPAN
    onehots = [(eid == ix).astype(F32) for ix in idxs]
    routed = onehots[0] + onehots[1] + onehots[2] + onehots[3]
    earlier = (lax.broadcasted_iota(jnp.int32, (span, span), 0)
               < lax.broadcasted_iota(jnp.int32, (span, span), 1)).astype(BF16)
    run = run_ref[...]
    ranks = []
    for c0 in range(0, logits.shape[1], span):
        part = routed[:, c0:c0 + span]
        before = jnp.dot(part.astype(BF16), earlier, preferred_element_type=F32) + run
        ranks.append(jnp.concatenate([jnp.sum(o[:, c0:c0 + span] * before, axis=0, keepdims=True)
                                      for o in onehots], axis=0))
        run = run + jnp.sum(part, axis=1, keepdims=True)
    rank_ref[...] = jnp.concatenate(ranks, axis=1).astype(jnp.int32)
    run_ref[...] = run
    cnt_ref[...] = run.astype(jnp.int32)


def _outproj(attn, hg, x2, w_out, norm_w, router_w, router_b):
    n = x2.shape[0]
    tm = OUTPROJ_ROWS
    row = lambda width: pl.BlockSpec((tm, width), lambda i: (i, 0))
    full = lambda a, b: pl.BlockSpec((a, b), lambda i: (0, 0))
    tok = pl.BlockSpec((TOP_K, tm), lambda i: (0, i))
    return pl.pallas_call(
        _outproj_kernel,
        grid=(n // tm,),
        in_specs=[row(ATTN_WIDTH), row(HGRN_WIDTH), row(D_MODEL), full(D_MODEL, D_MODEL),
                  full(1, D_MODEL), full(N_EXPERTS, D_MODEL), full(N_EXPERTS, 1)],
        out_specs=[row(D_MODEL), pl.BlockSpec((tm * ROW_CHUNKS, LANES), lambda i: (i, 0)),
                   tok, tok, tok, full(N_EXPERTS, 1)],
        out_shape=[jax.ShapeDtypeStruct((n, D_MODEL), F32),
                   jax.ShapeDtypeStruct((n * ROW_CHUNKS, LANES), F32),
                   jax.ShapeDtypeStruct((TOP_K, n), jnp.int32),
                   jax.ShapeDtypeStruct((TOP_K, n), F32),
                   jax.ShapeDtypeStruct((TOP_K, n), jnp.int32),
                   jax.ShapeDtypeStruct((N_EXPERTS, 1), jnp.int32)],
        scratch_shapes=[pltpu.VMEM((N_EXPERTS, 1), F32)],
        compiler_params=_cparams("arbitrary"),
        name="outproj_router",
    )(attn, hg, x2, w_out.astype(BF16), norm_w.reshape(1, D_MODEL),
      router_w.T.astype(F32), router_b.astype(F32).reshape(N_EXPERTS, 1))


def _ffn_kernel(be_ref, rv_ref, xs_ref, wgu_ref, bgu_ref, wdn_ref, bdn_ref, o_ref, wgu_bf, wdn_bf):
    i = pl.program_id(0)
    rows_valid = rv_ref[i]
    new_expert = (i == 0) | (be_ref[i] != be_ref[jnp.maximum(i - 1, 0)])

    @pl.when(new_expert & (rows_valid > 0))
    def _():
        step = 128
        for r in range(0, D_MODEL, step):
            wgu_bf[r:r + step, :] = wgu_ref[0, r:r + step, :].astype(BF16)
        for r in range(0, D_EXPERT, step):
            wdn_bf[r:r + step, :] = wdn_ref[0, r:r + step, :].astype(BF16)

    @pl.when(rows_valid > 0)
    def _():
        live = lax.broadcasted_iota(jnp.int32, (FFN_ROWS, 1), 0) < rows_valid
        x = jnp.where(live, _load_row_tiles(xs_ref, FFN_ROWS), 0.0).astype(BF16)
        out = bdn_ref[0]
        for c0 in range(0, D_EXPERT, FFN_COLS):
            g_cols = slice(c0, c0 + FFN_COLS)
            u_cols = slice(D_EXPERT + c0, D_EXPERT + c0 + FFN_COLS)
            gate = jnp.dot(x, wgu_bf[:, g_cols], preferred_element_type=F32) + bgu_ref[0, :, g_cols]
            up = jnp.dot(x, wgu_bf[:, u_cols], preferred_element_type=F32) + bgu_ref[0, :, u_cols]
            gate = jnp.minimum(gate, SWIGLU_LIMIT)
            up = jnp.clip(up, -SWIGLU_LIMIT, SWIGLU_LIMIT)
            act = gate * jax.nn.sigmoid(SWIGLU_ALPHA * gate) * (up + 1.0)
            out = out + jnp.dot(act.astype(BF16), wdn_bf[g_cols, :], preferred_element_type=F32)
        _store_row_tiles(o_ref, out)

    @pl.when(rows_valid <= 0)
    def _():
        o_ref[...] = jnp.zeros_like(o_ref)


def _expert_ffn(block_e, rows_valid, xs, w_gu, b_gu, w_dn, b_dn):
    tm = FFN_ROWS
    tiles = pl.BlockSpec((tm * ROW_CHUNKS, LANES), lambda i, be, rv: (i, 0))
    grid_spec = pltpu.PrefetchScalarGridSpec(
        num_scalar_prefetch=2,
        grid=(xs.shape[0] // (tm * ROW_CHUNKS),),
        in_specs=[tiles,
                  pl.BlockSpec((1, D_MODEL, 2 * D_EXPERT), lambda i, be, rv: (be[i], 0, 0)),
                  pl.BlockSpec((1, 1, 2 * D_EXPERT), lambda i, be, rv: (be[i], 0, 0)),
                  pl.BlockSpec((1, D_EXPERT, D_MODEL), lambda i, be, rv: (be[i], 0, 0)),
                  pl.BlockSpec((1, 1, D_MODEL), lambda i, be, rv: (be[i], 0, 0))],
        out_specs=tiles,
        scratch_shapes=[pltpu.VMEM((D_MODEL, 2 * D_EXPERT), BF16), pltpu.VMEM((D_EXPERT, D_MODEL), BF16)],
    )
    return pl.pallas_call(
        _ffn_kernel,
        grid_spec=grid_spec,
        out_shape=jax.ShapeDtypeStruct(xs.shape, F32),
        compiler_params=pltpu.CompilerParams(dimension_semantics=("arbitrary",),
                                             vmem_limit_bytes=FFN_VMEM_LIMIT),
        name="expert_ffn",
    )(block_e, rows_valid, xs, w_gu, b_gu.reshape(N_EXPERTS, 1, 2 * D_EXPERT),
      w_dn, b_dn.reshape(N_EXPERTS, 1, D_MODEL))


def _pos_kernel(idx_ref, rank_ref, pstart_ref, pos_ref):
    eid = lax.broadcasted_iota(jnp.int32, (N_EXPERTS, idx_ref.shape[1]), 0)
    pstart = pstart_ref[...]
    rows = [jnp.sum(jnp.where(eid == idx_ref[k:k + 1, :], pstart, 0), axis=0, keepdims=True)
            for k in range(TOP_K)]
    pos_ref[...] = (jnp.concatenate(rows, axis=0) + rank_ref[...]) * ROW_CHUNKS


def _positions(idx_t, rank_t, pstart):
    n = idx_t.shape[1]
    tl = POS_TOKENS
    tok = pl.BlockSpec((TOP_K, tl), lambda i: (0, i))
    return pl.pallas_call(
        _pos_kernel,
        grid=(n // tl,),
        in_specs=[tok, tok, pl.BlockSpec((N_EXPERTS, 1), lambda i: (0, 0))],
        out_specs=tok,
        out_shape=jax.ShapeDtypeStruct((TOP_K, n), jnp.int32),
        compiler_params=_cparams("arbitrary"),
        name="dispatch_pos",
    )(idx_t, rank_t, pstart.reshape(N_EXPERTS, 1))


def _tile_positions(pos, tokens):
    n = pos.shape[1]
    return pos.T.reshape(n // tokens, 1, tokens * TOP_K)


def _row_tile(ref, row_offset):
    return ref.at[pl.ds(pl.multiple_of(row_offset, ROW_CHUNKS), ROW_CHUNKS)]


def _dispatch_kernel(pos_ref, x_ref, xs_ref, sem):
    tokens = x_ref.shape[0] // ROW_CHUNKS

    def copy(t, k):
        return pltpu.make_async_copy(_row_tile(x_ref, t * ROW_CHUNKS),
                                     _row_tile(xs_ref, pos_ref[0, 0, t * TOP_K + k]), sem)

    def issue(t, carry):
        for k in range(TOP_K):
            copy(t, k).start(priority=k % 2)
        return carry

    def drain(t, carry):
        for k in range(TOP_K):
            copy(t, k).wait()
        return carry

    lax.fori_loop(0, tokens, issue, 0, unroll=MOVE_UNROLL)
    lax.fori_loop(0, tokens, drain, 0, unroll=MOVE_UNROLL)


def _dispatch_rows(pos, xn, out_rows):
    n = xn.shape[0] // ROW_CHUNKS
    tl = MOVE_TOKENS
    return pl.pallas_call(
        _dispatch_kernel,
        grid=(n // tl,),
        in_specs=[pl.BlockSpec((1, 1, tl * TOP_K), lambda i: (i, 0, 0), memory_space=pltpu.SMEM),
                  pl.BlockSpec((tl * ROW_CHUNKS, LANES), lambda i: (i, 0))],
        out_specs=pl.BlockSpec(memory_space=pl.ANY),
        out_shape=jax.ShapeDtypeStruct((out_rows * ROW_CHUNKS, LANES), xn.dtype),
        scratch_shapes=[pltpu.SemaphoreType.DMA(())],
        compiler_params=_cparams("arbitrary"),
        name="dispatch_rows",
    )(_tile_positions(pos, tl), xn)


def _combine_kernel(pos_ref, pos_next_ref, h_ref, g_ref, fw_ref, ys_ref, o_ref, buf, sems):
    i = pl.program_id(0)
    steps = pl.num_programs(0)
    tokens = h_ref.shape[0]
    slot = lax.rem(i, 2)

    def gather(p_ref, s, start):
        def body(t, carry):
            for k in range(TOP_K):
                cp = pltpu.make_async_copy(_row_tile(ys_ref, p_ref[0, 0, t * TOP_K + k]),
                                           _row_tile(buf.at[s, k], t * ROW_CHUNKS), sems.at[s])
                cp.start(priority=k % 2) if start else cp.wait()
            return carry
        lax.fori_loop(0, tokens, body, 0, unroll=MOVE_UNROLL)

    @pl.when(i == 0)
    def _():
        gather(pos_ref, 0, True)

    @pl.when(i + 1 < steps)
    def _():
        gather(pos_next_ref, 1 - slot, True)

    gather(pos_ref, slot, False)
    g = g_ref[...]
    y = h_ref[...]
    for k in range(TOP_K):
        y = y + _load_row_tiles(buf.at[slot, k], tokens) * g[:, k:k + 1]
    o_ref[...] = y * lax.rsqrt(jnp.mean(y * y, axis=-1, keepdims=True) + RMS_EPS) * fw_ref[...]


def _combine(h, ys, pos, gates_nk, final_w):
    n = h.shape[0]
    tm = COMBINE_ROWS
    steps = n // tm
    pos_t = _tile_positions(pos, tm)
    smem = lambda imap: pl.BlockSpec((1, 1, tm * TOP_K), imap, memory_space=pltpu.SMEM)
    return pl.pallas_call(
        _combine_kernel,
        grid=(steps,),
        in_specs=[smem(lambda i: (i, 0, 0)),
                  smem(lambda i: (jnp.minimum(i + 1, steps - 1), 0, 0)),
                  pl.BlockSpec((tm, D_MODEL), lambda i: (i, 0)),
                  pl.BlockSpec((tm, TOP_K), lambda i: (i, 0)),
                  pl.BlockSpec((1, D_MODEL), lambda i: (0, 0)),
                  pl.BlockSpec(memory_space=pl.ANY)],
        out_specs=pl.BlockSpec((tm, D_MODEL), lambda i: (i, 0)),
        out_shape=jax.ShapeDtypeStruct((n, D_MODEL), F32),
        scratch_shapes=[pltpu.VMEM((2, TOP_K, tm * ROW_CHUNKS, LANES), F32), pltpu.SemaphoreType.DMA((2,))],
        compiler_params=_cparams("arbitrary"),
        name="combine_norm",
    )(pos_t, pos_t, h, gates_nk, final_w.reshape(1, D_MODEL), ys)


def kernel(x, norm_mix_w, w_in, rel_bias, hgrn_lb_logits, hgrn_norm_w, w_out, norm_ffn_w,
           router_w, router_b, w_gate_up, b_gate_up, w_down, b_down, final_norm_w):
    batch, seq, _ = x.shape
    n = batch * seq
    x2 = x.reshape(n, D_MODEL)

    qa, ka, va, qh, fh, ih, gh = _inproj(x2, norm_mix_w[0], w_in[0])
    attn = _dilated_attention(qa, ka, va, _masked_bias(rel_bias), batch, seq)
    hg = _hgrn(qh, fh, ih, gh, hgrn_lb_logits, hgrn_norm_w[0], batch, seq)
    h, xn, idx_t, gate_t, rank_t, counts = _outproj(attn, hg, x2, w_out[0], norm_ffn_w[0],
                                                    router_w[0], router_b[0])

    tm = FFN_ROWS
    n_blocks = -(-(n * TOP_K) // tm) + N_EXPERTS
    counts = counts.reshape(N_EXPERTS)
    padded = (counts + tm - 1) // tm * tm
    pend = jnp.cumsum(padded)
    pstart = pend - padded
    block_row0 = jnp.arange(n_blocks, dtype=jnp.int32) * tm
    in_block = (block_row0[:, None] >= pstart[None, :]) & (block_row0[:, None] < pend[None, :])
    block_e = jnp.minimum(jnp.sum(pend[None, :] <= block_row0[:, None], axis=1), N_EXPERTS - 1).astype(jnp.int32)
    rows_valid = jnp.sum(jnp.where(in_block, jnp.clip(pstart + counts - block_row0[:, None], 0, tm), 0),
                         axis=1).astype(jnp.int32)

    pos = _positions(idx_t, rank_t, pstart.astype(jnp.int32))
    xs = _dispatch_rows(pos, xn, n_blocks * tm)
    ys = _expert_ffn(block_e, rows_valid, xs, w_gate_up[0], b_gate_up[0], w_down[0], b_down[0])
    out = _combine(h, ys, pos, gate_t.T, final_norm_w)
    return out.reshape(batch, seq, D_MODEL)
```

```python
import functools

import numpy as np
import jax
import jax.numpy as jnp
from jax import lax
from jax.experimental import pallas as pl
from jax.experimental.pallas import tpu as pltpu

F32 = jnp.float32
BF16 = jnp.bfloat16
HIGHEST = lax.Precision.HIGHEST

D_MODEL = 1024
ATTN_HEAD_DIM = 64
ATTN_WIDTH = 512
ATTN_HEADS = ATTN_WIDTH // ATTN_HEAD_DIM
DILATION_PATTERNS = ((128, 1), (512, 4), (2048, 16))
ATTN_BLOCK = 128
REL_BUCKETS = 32
REL_MAX_DISTANCE = 2048
HGRN_HEAD_DIM = 128
HGRN_WIDTH = 512
HGRN_HEADS = HGRN_WIDTH // HGRN_HEAD_DIM
HGRN_CHUNK = 64
HGRN_SUB = 8
N_EXPERTS = 32
TOP_K = 4
D_EXPERT = 1024
SWIGLU_LIMIT = 7.0
SWIGLU_ALPHA = 1.702
RMS_EPS = 1e-5
IN_PROJ_WIDTH = 3 * ATTN_WIDTH + 4 * HGRN_WIDTH
LANES = 128
HEAD_PAIRS = ATTN_WIDTH // LANES
ROW_CHUNKS = D_MODEL // LANES

INPROJ_ROWS = 512
OUTPROJ_ROWS = 1024
RANK_SPAN = 256
FFN_ROWS = 512
FFN_COLS = 512
COMBINE_ROWS = 512
HGRN_SEQ_BLOCK = 1024
ATTN_UNROLL = 8
POS_TOKENS = 2048
MOVE_TOKENS = 2048
MOVE_UNROLL = 8
VMEM_LIMIT = 48 * 1024 * 1024

NT_DIMS = (((1,), (1,)), ((), ()))
TN_DIMS = (((0,), (0,)), ((), ()))


def _cparams(*sem):
    return pltpu.CompilerParams(dimension_semantics=sem, vmem_limit_bytes=VMEM_LIMIT)


def _store_row_tiles(ref, value):
    rows = value.shape[0]
    for c in range(ROW_CHUNKS):
        ref[pl.ds(c, rows, stride=ROW_CHUNKS), :] = value[:, c * LANES:(c + 1) * LANES]


def _load_row_tiles(ref, rows):
    return jnp.concatenate([ref[pl.ds(c, rows, stride=ROW_CHUNKS), :] for c in range(ROW_CHUNKS)], axis=1)


def _inproj_kernel(x_ref, nw_ref, w_ref, qa_ref, ka_ref, va_ref, qh_ref, fh_ref, ih_ref, gh_ref):
    x = x_ref[...]
    xn = x * lax.rsqrt(jnp.mean(x * x, axis=-1, keepdims=True) + RMS_EPS) * nw_ref[...]
    xn = xn.astype(BF16)
    for c, ref in enumerate((qa_ref, ka_ref, va_ref, qh_ref, fh_ref, ih_ref, gh_ref)):
        y = jnp.dot(xn, w_ref[:, c * ATTN_WIDTH:(c + 1) * ATTN_WIDTH], preferred_element_type=F32)
        ref[...] = y.astype(ref.dtype)


def _inproj(x2, norm_w, w_in):
    n = x2.shape[0]
    tm = INPROJ_ROWS
    row = lambda width: pl.BlockSpec((tm, width), lambda i: (i, 0))
    dtypes = (F32, F32, F32, BF16, F32, BF16, F32)
    return pl.pallas_call(
        _inproj_kernel,
        grid=(n // tm,),
        in_specs=[row(D_MODEL),
                  pl.BlockSpec((1, D_MODEL), lambda i: (0, 0)),
                  pl.BlockSpec((D_MODEL, IN_PROJ_WIDTH), lambda i: (0, 0))],
        out_specs=[row(ATTN_WIDTH)] * 7,
        out_shape=[jax.ShapeDtypeStruct((n, ATTN_WIDTH), dt) for dt in dtypes],
        compiler_params=_cparams("arbitrary"),
        name="inproj",
    )(x2, norm_w.reshape(1, D_MODEL), w_in.astype(BF16))


def _t5_causal_bucket(dist):
    n = np.maximum(dist, 0)
    max_exact = REL_BUCKETS // 2
    large = max_exact + (np.log(np.maximum(n, 1) / max_exact)
                         / np.log(REL_MAX_DISTANCE / max_exact)
                         * (REL_BUCKETS - max_exact)).astype(np.int32)
    large = np.minimum(large, REL_BUCKETS - 1)
    return np.where(n < max_exact, n, large).astype(np.int32)


def _bucket_thresholds(max_dist):
    buckets = _t5_causal_bucket(np.arange(max_dist + 1))
    assert np.all(np.diff(buckets) >= 0)
    return [(b, int(np.argmax(buckets >= b))) for b in range(1, REL_BUCKETS) if np.any(buckets >= b)]


def _bias_kernel(rel_ref, o_ref):
    blk = ATTN_BLOCK
    h = pl.program_id(0)
    qi = lax.broadcasted_iota(jnp.int32, (blk, 2 * blk), 0)
    kj = lax.broadcasted_iota(jnp.int32, (blk, 2 * blk), 1)
    dist_sub = blk + qi - kj
    for pi, (window, dilation) in enumerate(DILATION_PATTERNS):
        span = window // dilation
        dist = dist_sub * dilation
        val = jnp.full((blk, 2 * blk), rel_ref[h], F32)
        for b, first_dist in _bucket_thresholds(window):
            val = jnp.where(dist >= first_dist, rel_ref[b * ATTN_HEADS + h], val)
        o_ref[pi, 0] = jnp.where((dist_sub >= 0) & (dist_sub <= span), val, -1e30)


def _masked_bias(rel_bias):
    blk = ATTN_BLOCK
    npat = len(DILATION_PATTERNS)
    bias = pl.pallas_call(
        _bias_kernel,
        grid=(ATTN_HEADS,),
        in_specs=[pl.BlockSpec(memory_space=pltpu.SMEM)],
        out_specs=pl.BlockSpec((npat, 1, blk, 2 * blk), lambda h: (0, h, 0, 0)),
        out_shape=jax.ShapeDtypeStruct((npat, ATTN_HEADS, blk, 2 * blk), F32),
        compiler_params=_cparams("arbitrary"),
        name="rel_bias",
    )(rel_bias.astype(F32).reshape(REL_BUCKETS * ATTN_HEADS))
    return bias.reshape(npat, HEAD_PAIRS, 2 * blk, 2 * blk)


def _attn_kernel(q_ref, k_ref, v_ref, bias_ref, o_ref, acc_ref, lse_ref, *, seq):
    blk = ATTN_BLOCK
    lane = lax.broadcasted_iota(jnp.int32, (1, LANES), 1)
    first = lane < ATTN_HEAD_DIM
    scale = ATTN_HEAD_DIM ** -0.5
    sel0 = jnp.where(first, scale, 0.0)
    sel1 = jnp.where(first, 0.0, scale)
    ones_cols = jnp.ones((2 * blk, LANES), BF16)

    def one_block(pi, dil, q_start, k_start, n_keys):
        qb = q_ref[0, pl.ds(q_start, blk, stride=dil), :]
        q2 = jnp.concatenate([qb * sel0, qb * sel1], axis=0).astype(BF16)
        kk = k_ref[0, pl.ds(k_start, n_keys, stride=dil), :].astype(BF16)
        vv = v_ref[0, pl.ds(k_start, n_keys, stride=dil), :].astype(BF16)
        s = lax.dot_general(q2, kk, NT_DIMS, preferred_element_type=F32)
        s = s + bias_ref[pi, 0, :, 2 * blk - n_keys:]
        m = jnp.max(s, axis=-1, keepdims=True)
        p = jnp.exp(s - m).astype(BF16)
        ov = jnp.dot(p, jnp.concatenate([vv, ones_cols[:n_keys]], axis=1), preferred_element_type=F32)
        z = ov[:, LANES:]
        o = ov[:, :LANES] / z
        lse = m + jnp.log(z)
        rows = pl.ds(q_start, blk, stride=dil)
        acc_ref[pi, rows, :] = jnp.where(first, o[:blk], o[blk:])
        lse_ref[pi, rows, :] = jnp.where(first, lse[:blk], lse[blk:])

    for pi, (_, dil) in enumerate(DILATION_PATTERNS):
        nb = seq // (dil * blk)

        def residue(r, carry, pi=pi, dil=dil, nb=nb):
            one_block(pi, dil, r, r, blk)

            def body(j, c):
                q_start = r + j * (blk * dil)
                one_block(pi, dil, q_start, q_start - blk * dil, 2 * blk)
                return c

            return lax.fori_loop(1, nb, body, carry, unroll=ATTN_UNROLL)

        lax.fori_loop(0, dil, residue, 0, unroll=ATTN_UNROLL if nb == 2 else 1)

    def merge(i, carry):
        rows = pl.ds(pl.multiple_of(i * blk, blk), blk)
        ls = [lse_ref[pi, rows, :] for pi in range(len(DILATION_PATTERNS))]
        m = functools.reduce(jnp.maximum, ls)
        es = [jnp.exp(l - m) for l in ls]
        num = sum(e * acc_ref[pi, rows, :] for pi, e in enumerate(es))
        o_ref[0, rows, :] = (num / sum(es)).astype(o_ref.dtype)
        return carry

    lax.fori_loop(0, seq // blk, merge, 0)


def _dilated_attention(qa, ka, va, bias, batch, seq):
    npat = len(DILATION_PATTERNS)
    shp = (batch, seq, ATTN_WIDTH)
    spec = pl.BlockSpec((1, seq, LANES), lambda b, p: (b, 0, p))
    out = pl.pallas_call(
        functools.partial(_attn_kernel, seq=seq),
        grid=(batch, HEAD_PAIRS),
        in_specs=[spec, spec, spec,
                  pl.BlockSpec((npat, 1, 2 * ATTN_BLOCK, 2 * ATTN_BLOCK), lambda b, p: (0, p, 0, 0))],
        out_specs=spec,
        out_shape=jax.ShapeDtypeStruct(shp, BF16),
        scratch_shapes=[pltpu.VMEM((npat, seq, LANES), F32), pltpu.VMEM((npat, seq, LANES), F32)],
        compiler_params=_cparams("arbitrary", "arbitrary"),
        name="dilated_attn",
    )(qa.reshape(shp), ka.reshape(shp), va.reshape(shp), bias)
    return out.reshape(batch * seq, ATTN_WIDTH)


def _hgrn_kernel(q_ref, f_ref, i_ref, g_ref, lbl_ref, nw_ref, o_ref, st_ref, *, n_chunks):
    c, sub, dk = HGRN_CHUNK, HGRN_SUB, HGRN_HEAD_DIM
    nsub = c // sub
    lbl = lbl_ref[...]
    e = jnp.exp(lbl - jnp.max(lbl, axis=0, keepdims=True))
    lb_all = e[0:1] / jnp.sum(e, axis=0, keepdims=True)
    nw = nw_ref[...]

    r64 = lax.broadcasted_iota(jnp.int32, (c, c), 0)
    c64 = lax.broadcasted_iota(jnp.int32, (c, c), 1)
    sub_end = (r64 // sub) * sub + (sub - 1)
    decay_sums = jnp.concatenate([(c64 <= r64), (c64 > r64) & (c64 <= sub_end), (c64 > r64)],
                                 axis=0).astype(BF16)
    col_sub = c64 // sub
    col_in_sub = c64 - (r64 // sub) * sub
    t_iota = lax.broadcasted_iota(jnp.int32, (nsub, sub, dk), 1)
    ones_rhs = jnp.ones((dk, c), BF16)

    @pl.when(pl.program_id(1) == 0)
    def _():
        st_ref[...] = jnp.zeros_like(st_ref)

    def cols_of(hd):
        return slice(hd * dk, (hd + 1) * dk)

    def gates(rows, hd):
        cols = cols_of(hd)
        lb = lb_all[:, cols]
        q = q_ref[0, rows, cols].astype(F32)
        f = lb + (1.0 - lb) * jax.nn.sigmoid(f_ref[0, rows, cols])
        log_f = jnp.log(f)
        log_hi = log_f.astype(BF16)
        log_lo = (log_f - log_hi.astype(F32)).astype(BF16)
        sums = (jnp.dot(decay_sums, log_hi, preferred_element_type=F32)
                + jnp.dot(decay_sums, log_lo, preferred_element_type=F32))
        return sums, 1.0 - f, q * jax.nn.sigmoid(q) * (dk ** -0.5)

    def products(rows, hd, sums, key, qf):
        cum, to_sub_end, to_chunk_end = sums[:c], sums[c:2 * c], sums[2 * c:]
        v = i_ref[0, rows, cols_of(hd)]
        st = st_ref[hd]
        o = lax.dot_general((qf * jnp.exp(cum)).astype(BF16), st.astype(BF16), NT_DIMS,
                            preferred_element_type=F32)
        kd = (key * jnp.exp(to_chunk_end)).astype(BF16)
        st_ref[hd] = (st * jnp.exp(cum[c - 1:c, :])
                      + lax.dot_general(v, kd, TN_DIMS, preferred_element_type=F32))

        cum3 = cum.reshape(nsub, sub, dk)
        key3 = key.reshape(nsub, sub, dk)
        qf3 = qf.reshape(nsub, sub, dk)
        khat = (key * jnp.exp(to_sub_end)).astype(BF16)
        qs = []
        for j in range(nsub - 1):
            t0 = sub * (j + 1)
            qs.append(qf[t0:] * jnp.exp(cum[t0:] - cum[t0 - 1:t0, :]))
        prod = lax.dot_general(jnp.concatenate(qs, axis=0).astype(BF16), khat, NT_DIMS,
                               preferred_element_type=F32)

        ws = []
        for s in range(sub):
            dec = jnp.exp(jnp.where(t_iota >= s, cum3 - cum3[:, s:s + 1, :], -jnp.inf))
            ws.append((qf3 * key3[:, s:s + 1, :] * dec).reshape(c, dk))
        pair_sums = jnp.dot(jnp.concatenate(ws, axis=0).astype(BF16), ones_rhs,
                            preferred_element_type=F32)
        return o, prod, pair_sums

    def finish(rows, hd, o, prod, pair_sums):
        cols = cols_of(hd)
        a = jnp.zeros((c, c), F32)
        r = 0
        for j in range(nsub - 1):
            t0 = sub * (j + 1)
            pj = jnp.concatenate([jnp.zeros((t0, c), F32), prod[r:r + c - t0, :]], axis=0)
            a = jnp.where(col_sub == j, pj, a)
            r += c - t0
        for s in range(sub):
            a = jnp.where(col_in_sub == s, pair_sums[s * c:(s + 1) * c, :], a)
        o = o + jnp.dot(a.astype(BF16), i_ref[0, rows, cols], preferred_element_type=F32)
        on = o * lax.rsqrt(jnp.mean(o * o, axis=-1, keepdims=True) + RMS_EPS) * nw
        g = g_ref[0, rows, cols]
        o_ref[0, rows, cols] = (on * (g * jax.nn.sigmoid(g))).astype(BF16)

    def body(ci, carry):
        rows = pl.ds(pl.multiple_of(ci * c, c), c)
        heads = range(HGRN_HEADS)
        stage1 = [gates(rows, hd) for hd in heads]
        stage2 = [products(rows, hd, *stage1[hd]) for hd in heads]
        for hd in heads:
            finish(rows, hd, *stage2[hd])
        return carry

    lax.fori_loop(0, n_chunks, body, 0, unroll=2)


def _hgrn(qh, fh, ih, gh, lb_logits, norm_w, batch, seq):
    sb = min(seq, HGRN_SEQ_BLOCK)
    spec = pl.BlockSpec((1, sb, HGRN_WIDTH), lambda b, s: (b, s, 0))
    slots = lb_logits.shape[0]
    shp = (batch, seq, HGRN_WIDTH)
    out = pl.pallas_call(
        functools.partial(_hgrn_kernel, n_chunks=sb // HGRN_CHUNK),
        grid=(batch, seq // sb),
        in_specs=[spec, spec, spec, spec,
                  pl.BlockSpec((slots, HGRN_WIDTH), lambda b, s: (0, 0)),
                  pl.BlockSpec((1, HGRN_HEAD_DIM), lambda b, s: (0, 0))],
        out_specs=spec,
        out_shape=jax.ShapeDtypeStruct(shp, BF16),
        scratch_shapes=[pltpu.VMEM((HGRN_HEADS, HGRN_HEAD_DIM, HGRN_HEAD_DIM), F32)],
        compiler_params=_cparams("arbitrary", "arbitrary"),
        name="hgrn2",
    )(qh.reshape(shp), fh.reshape(shp), ih.reshape(shp), gh.reshape(shp),
      lb_logits.astype(F32), norm_w.reshape(1, HGRN_HEAD_DIM))
    return out.reshape(batch * seq, HGRN_WIDTH)


def _outproj_kernel(attn_ref, hg_ref, x_ref, w_ref, nw_ref, rw_ref, rb_ref,
                    h_ref, xn_ref, idx_ref, gate_ref, rank_ref, cnt_ref, run_ref):
    y = jnp.dot(attn_ref[...], w_ref[0:ATTN_WIDTH, :], preferred_element_type=F32)
    y = y + jnp.dot(hg_ref[...], w_ref[ATTN_WIDTH:, :], preferred_element_type=F32)
    h = x_ref[...] + y
    h_ref[...] = h
    xn = h * lax.rsqrt(jnp.mean(h * h, axis=-1, keepdims=True) + RMS_EPS) * nw_ref[...]
    _store_row_tiles(xn_ref, xn)

    logits = lax.dot_general(rw_ref[...], xn, NT_DIMS, precision=HIGHEST,
                             preferred_element_type=F32) + rb_ref[...]
    eid = lax.broadcasted_iota(jnp.int32, logits.shape, 0)
    vals, idxs = [], []
    for _ in range(TOP_K):
        mx = jnp.max(logits, axis=0, keepdims=True)
        ix = jnp.min(jnp.where(logits == mx, eid, N_EXPERTS), axis=0, keepdims=True)
        vals.append(mx)
        idxs.append(ix)
        logits = jnp.where(eid == ix, -jnp.inf, logits)
    es = [jnp.exp(v - vals[0]) for v in vals]
    den = es[0] + es[1] + es[2] + es[3]
    idx_ref[...] = jnp.concatenate(idxs, axis=0)
    gate_ref[...] = jnp.concatenate([e / den for e in es], axis=0)

    @pl.when(pl.program_id(0) == 0)
    def _():
        run_ref[...] = jnp.zeros_like(run_ref)

    span = RANK_SPAN
    onehots = [(eid == ix).astype(F32) for ix in idxs]
    routed = onehots[0] + onehots[1] + onehots[2] + onehots[3]
    earlier = (lax.broadcasted_iota(jnp.int32, (span, span), 0)
               < lax.broadcasted_iota(jnp.int32, (span, span), 1)).astype(BF16)
    run = run_ref[...]
    ranks = []
    for c0 in range(0, logits.shape[1], span):
        part = routed[:, c0:c0 + span]
        before = jnp.dot(part.astype(BF16), earlier, preferred_element_type=F32) + run
        ranks.append(jnp.concatenate([jnp.sum(o[:, c0:c0 + span] * before, axis=0, keepdims=True)
                                      for o in onehots], axis=0))
        run = run + jnp.sum(part, axis=1, keepdims=True)
    rank_ref[...] = jnp.concatenate(ranks, axis=1).astype(jnp.int32)
    run_ref[...] = run
    cnt_ref[...] = run.astype(jnp.int32)


def _outproj(attn, hg, x2, w_out, norm_w, router_w, router_b):
    n = x2.shape[0]
    tm = OUTPROJ_ROWS
    row = lambda width: pl.BlockSpec((tm, width), lambda i: (i, 0))
    full = lambda a, b: pl.BlockSpec((a, b), lambda i: (0, 0))
    tok = pl.BlockSpec((TOP_K, tm), lambda i: (0, i))
    return pl.pallas_call(
        _outproj_kernel,
        grid=(n // tm,),
        in_specs=[row(ATTN_WIDTH), row(HGRN_WIDTH), row(D_MODEL), full(D_MODEL, D_MODEL),
                  full(1, D_MODEL), full(N_EXPERTS, D_MODEL), full(N_EXPERTS, 1)],
        out_specs=[row(D_MODEL), pl.BlockSpec((tm * ROW_CHUNKS, LANES), lambda i: (i, 0)),
                   tok, tok, tok, full(N_EXPERTS, 1)],
        out_shape=[jax.ShapeDtypeStruct((n, D_MODEL), F32),
                   jax.ShapeDtypeStruct((n * ROW_CHUNKS, LANES), F32),
                   jax.ShapeDtypeStruct((TOP_K, n), jnp.int32),
                   jax.ShapeDtypeStruct((TOP_K, n), F32),
                   jax.ShapeDtypeStruct((TOP_K, n), jnp.int32),
                   jax.ShapeDtypeStruct((N_EXPERTS, 1), jnp.int32)],
        scratch_shapes=[pltpu.VMEM((N_EXPERTS, 1), F32)],
        compiler_params=_cparams("arbitrary"),
        name="outproj_router",
    )(attn, hg, x2, w_out.astype(BF16), norm_w.reshape(1, D_MODEL),
      router_w.T.astype(F32), router_b.astype(F32).reshape(N_EXPERTS, 1))


def _ffn_kernel(be_ref, rv_ref, nx_ref, xs_ref, wgu_hbm, bgu_ref, wdn_hbm, bdn_ref, o_ref,
                wgu_f32, wdn_f32, wgu_bf, wdn_bf, sems):
    i = pl.program_id(0)
    rows_valid = rv_ref[i]
    new_expert = (i == 0) | (be_ref[i] != be_ref[jnp.maximum(i - 1, 0)])

    def weight_copies(e):
        return (pltpu.make_async_copy(wgu_hbm.at[e], wgu_f32, sems.at[0]),
                pltpu.make_async_copy(wdn_hbm.at[e], wdn_f32, sems.at[1]))

    @pl.when(i == 0)
    def _():
        for cp in weight_copies(be_ref[0]):
            cp.start()

    @pl.when(new_expert & (rows_valid > 0))
    def _():
        for cp in weight_copies(be_ref[i]):
            cp.wait()
        step = 128
        for r in range(0, D_MODEL, step):
            wgu_bf[r:r + step, :] = wgu_f32[r:r + step, :].astype(BF16)
        for r in range(0, D_EXPERT, step):
            wdn_bf[r:r + step, :] = wdn_f32[r:r + step, :].astype(BF16)

        @pl.when(nx_ref[i] >= 0)
        def _():
            for cp in weight_copies(nx_ref[i]):
                cp.start()

    @pl.when(rows_valid > 0)
    def _():
        live = lax.broadcasted_iota(jnp.int32, (FFN_ROWS, 1), 0) < rows_valid
        x = jnp.where(live, _load_row_tiles(xs_ref, FFN_ROWS), 0.0).astype(BF16)
        out = bdn_ref[0]
        for c0 in range(0, D_EXPERT, FFN_COLS):
            g_cols = slice(c0, c0 + FFN_COLS)
            u_cols = slice(D_EXPERT + c0, D_EXPERT + c0 + FFN_COLS)
            gate = jnp.dot(x, wgu_bf[:, g_cols], preferred_element_type=F32) + bgu_ref[0, :, g_cols]
            up = jnp.dot(x, wgu_bf[:, u_cols], preferred_element_type=F32) + bgu_ref[0, :, u_cols]
            gate = jnp.minimum(gate, SWIGLU_LIMIT)
            up = jnp.clip(up, -SWIGLU_LIMIT, SWIGLU_LIMIT)
            act = gate * jax.nn.sigmoid(SWIGLU_ALPHA * gate) * (up + 1.0)
            out = out + jnp.dot(act.astype(BF16), wdn_bf[g_cols, :], preferred_element_type=F32)
        _store_row_tiles(o_ref, out)

    @pl.when(rows_valid <= 0)
    def _():
        o_ref[...] = jnp.zeros_like(o_ref)


def _expert_ffn(block_e, rows_valid, next_e, xs, w_gu, b_gu, w_dn, b_dn):
    tm = FFN_ROWS
    tiles = pl.BlockSpec((tm * ROW_CHUNKS, LANES), lambda i, be, rv, nx: (i, 0))
    grid_spec = pltpu.PrefetchScalarGridSpec(
        num_scalar_prefetch=3,
        grid=(xs.shape[0] // (tm * ROW_CHUNKS),),
        in_specs=[tiles,
                  pl.BlockSpec(memory_space=pl.ANY),
                  pl.BlockSpec((1, 1, 2 * D_EXPERT), lambda i, be, rv, nx: (be[i], 0, 0)),
                  pl.BlockSpec(memory_space=pl.ANY),
                  pl.BlockSpec((1, 1, D_MODEL), lambda i, be, rv, nx: (be[i], 0, 0))],
        out_specs=tiles,
        scratch_shapes=[pltpu.VMEM((D_MODEL, 2 * D_EXPERT), F32), pltpu.VMEM((D_EXPERT, D_MODEL), F32),
                        pltpu.VMEM((D_MODEL, 2 * D_EXPERT), BF16), pltpu.VMEM((D_EXPERT, D_MODEL), BF16),
                        pltpu.SemaphoreType.DMA((2,))],
    )
    return pl.pallas_call(
        _ffn_kernel,
        grid_spec=grid_spec,
        out_shape=jax.ShapeDtypeStruct(xs.shape, F32),
        compiler_params=_cparams("arbitrary"),
        name="expert_ffn",
    )(block_e, rows_valid, next_e, xs, w_gu, b_gu.reshape(N_EXPERTS, 1, 2 * D_EXPERT),
      w_dn, b_dn.reshape(N_EXPERTS, 1, D_MODEL))


def _pos_kernel(idx_ref, rank_ref, pstart_ref, pos_ref):
    eid = lax.broadcasted_iota(jnp.int32, (N_EXPERTS, idx_ref.shape[1]), 0)
    pstart = pstart_ref[...]
    rows = [jnp.sum(jnp.where(eid == idx_ref[k:k + 1, :], pstart, 0), axis=0, keepdims=True)
            for k in range(TOP_K)]
    pos_ref[...] = (jnp.concatenate(rows, axis=0) + rank_ref[...]) * ROW_CHUNKS


def _positions(idx_t, rank_t, pstart):
    n = idx_t.shape[1]
    tl = POS_TOKENS
    tok = pl.BlockSpec((TOP_K, tl), lambda i: (0, i))
    return pl.pallas_call(
        _pos_kernel,
        grid=(n // tl,),
        in_specs=[tok, tok, pl.BlockSpec((N_EXPERTS, 1), lambda i: (0, 0))],
        out_specs=tok,
        out_shape=jax.ShapeDtypeStruct((TOP_K, n), jnp.int32),
        compiler_params=_cparams("arbitrary"),
        name="dispatch_pos",
    )(idx_t, rank_t, pstart.reshape(N_EXPERTS, 1))


def _tile_positions(pos, tokens):
    n = pos.shape[1]
    return pos.T.reshape(n // tokens, 1, tokens * TOP_K)


def _row_tile(ref, row_offset):
    return ref.at[pl.ds(pl.multiple_of(row_offset, ROW_CHUNKS), ROW_CHUNKS)]


def _dispatch_kernel(pos_ref, x_ref, xs_ref, sem):
    tokens = x_ref.shape[0] // ROW_CHUNKS

    def copy(t, k):
        return pltpu.make_async_copy(_row_tile(x_ref, t * ROW_CHUNKS),
                                     _row_tile(xs_ref, pos_ref[0, 0, t * TOP_K + k]), sem)

    def issue(t, carry):
        for k in range(TOP_K):
            copy(t, k).start(priority=k % 2)
        return carry

    def drain(t, carry):
        for k in range(TOP_K):
            copy(t, k).wait()
        return carry

    lax.fori_loop(0, tokens, issue, 0, unroll=MOVE_UNROLL)
    lax.fori_loop(0, tokens, drain, 0, unroll=MOVE_UNROLL)


def _dispatch_rows(pos, xn, out_rows):
    n = xn.shape[0] // ROW_CHUNKS
    tl = MOVE_TOKENS
    return pl.pallas_call(
        _dispatch_kernel,
        grid=(n // tl,),
        in_specs=[pl.BlockSpec((1, 1, tl * TOP_K), lambda i: (i, 0, 0), memory_space=pltpu.SMEM),
                  pl.BlockSpec((tl * ROW_CHUNKS, LANES), lambda i: (i, 0))],
        out_specs=pl.BlockSpec(memory_space=pl.ANY),
        out_shape=jax.ShapeDtypeStruct((out_rows * ROW_CHUNKS, LANES), xn.dtype),
        scratch_shapes=[pltpu.SemaphoreType.DMA(())],
        compiler_params=_cparams("arbitrary"),
        name="dispatch_rows",
    )(_tile_positions(pos, tl), xn)


def _combine_kernel(pos_ref, pos_next_ref, h_ref, g_ref, fw_ref, ys_ref, o_ref, buf, sems):
    i = pl.program_id(0)
    steps = pl.num_programs(0)
    tokens = h_ref.shape[0]
    slot = lax.rem(i, 2)

    def gather(p_ref, s, start):
        def body(t, carry):
            for k in range(TOP_K):
                cp = pltpu.make_async_copy(_row_tile(ys_ref, p_ref[0, 0, t * TOP_K + k]),
                                           _row_tile(buf.at[s, k], t * ROW_CHUNKS), sems.at[s])
                cp.start(priority=k % 2) if start else cp.wait()
            return carry
        lax.fori_loop(0, tokens, body, 0, unroll=MOVE_UNROLL)

    @pl.when(i == 0)
    def _():
        gather(pos_ref, 0, True)

    @pl.when(i + 1 < steps)
    def _():
        gather(pos_next_ref, 1 - slot, True)

    gather(pos_ref, slot, False)
    g = g_ref[...]
    y = h_ref[...]
    for k in range(TOP_K):
        y = y + _load_row_tiles(buf.at[slot, k], tokens) * g[:, k:k + 1]
    o_ref[...] = y * lax.rsqrt(jnp.mean(y * y, axis=-1, keepdims=True) + RMS_EPS) * fw_ref[...]


def _combine(h, ys, pos, gates_nk, final_w):
    n = h.shape[0]
    tm = COMBINE_ROWS
    steps = n // tm
    pos_t = _tile_positions(pos, tm)
    smem = lambda imap: pl.BlockSpec((1, 1, tm * TOP_K), imap, memory_space=pltpu.SMEM)
    return pl.pallas_call(
        _combine_kernel,
        grid=(steps,),
        in_specs=[smem(lambda i: (i, 0, 0)),
                  smem(lambda i: (jnp.minimum(i + 1, steps - 1), 0, 0)),
                  pl.BlockSpec((tm, D_MODEL), lambda i: (i, 0)),
                  pl.BlockSpec((tm, TOP_K), lambda i: (i, 0)),
                  pl.BlockSpec((1, D_MODEL), lambda i: (0, 0)),
                  pl.BlockSpec(memory_space=pl.ANY)],
        out_specs=pl.BlockSpec((tm, D_MODEL), lambda i: (i, 0)),
        out_shape=jax.ShapeDtypeStruct((n, D_MODEL), F32),
        scratch_shapes=[pltpu.VMEM((2, TOP_K, tm * ROW_CHUNKS, LANES), F32), pltpu.SemaphoreType.DMA((2,))],
        compiler_params=_cparams("arbitrary"),
        name="combine_norm",
    )(pos_t, pos_t, h, gates_nk, final_w.reshape(1, D_MODEL), ys)


def kernel(x, norm_mix_w, w_in, rel_bias, hgrn_lb_logits, hgrn_norm_w, w_out, norm_ffn_w,
           router_w, router_b, w_gate_up, b_gate_up, w_down, b_down, final_norm_w):
    batch, seq, _ = x.shape
    n = batch * seq
    x2 = x.reshape(n, D_MODEL)

    qa, ka, va, qh, fh, ih, gh = _inproj(x2, norm_mix_w[0], w_in[0])
    attn = _dilated_attention(qa, ka, va, _masked_bias(rel_bias), batch, seq)
    hg = _hgrn(qh, fh, ih, gh, hgrn_lb_logits, hgrn_norm_w[0], batch, seq)
    h, xn, idx_t, gate_t, rank_t, counts = _outproj(attn, hg, x2, w_out[0], norm_ffn_w[0],
                                                    router_w[0], router_b[0])

    tm = FFN_ROWS
    n_blocks = -(-(n * TOP_K) // tm) + N_EXPERTS
    counts = counts.reshape(N_EXPERTS)
    padded = (counts + tm - 1) // tm * tm
    pend = jnp.cumsum(padded)
    pstart = pend - padded
    block_row0 = jnp.arange(n_blocks, dtype=jnp.int32) * tm
    in_block = (block_row0[:, None] >= pstart[None, :]) & (block_row0[:, None] < pend[None, :])
    block_e = jnp.minimum(jnp.sum(pend[None, :] <= block_row0[:, None], axis=1), N_EXPERTS - 1).astype(jnp.int32)
    rows_valid = jnp.sum(jnp.where(in_block, jnp.clip(pstart + counts - block_row0[:, None], 0, tm), 0),
                         axis=1).astype(jnp.int32)

    pos = _positions(idx_t, rank_t, pstart.astype(jnp.int32))
    xs = _dispatch_rows(pos, xn, n_blocks * tm)
    experts = jnp.arange(N_EXPERTS, dtype=jnp.int32)
    later = jnp.where((experts[None, :] > experts[:, None]) & (counts[None, :] > 0), experts[None, :], N_EXPERTS)
    next_of = jnp.min(later, axis=1)
    next_e = jnp.where(next_of < N_EXPERTS, next_of, -1)[block_e].astype(jnp.int32)
    ys = _expert_ffn(block_e, rows_valid, next_e, xs, w_gate_up[0], b_gate_up[0], w_down[0], b_down[0])
    out = _combine(h, ys, pos, gate_t.T, final_norm_w)
    return out.reshape(batch, seq, D_MODEL)
```

```python
import functools

import numpy as np
import jax
import jax.numpy as jnp
from jax import lax
from jax.experimental import pallas as pl
from jax.experimental.pallas import tpu as pltpu
from jax.experimental.pallas import tpu_sc as plsc

F32 = jnp.float32
BF16 = jnp.bfloat16
HIGHEST = lax.Precision.HIGHEST

D_MODEL = 1024
ATTN_HEAD_DIM = 64
ATTN_WIDTH = 512
ATTN_HEADS = ATTN_WIDTH // ATTN_HEAD_DIM
DILATION_PATTERNS = ((128, 1), (512, 4), (2048, 16))
ATTN_BLOCK = 128
REL_BUCKETS = 32
REL_MAX_DISTANCE = 2048
HGRN_HEAD_DIM = 128
HGRN_WIDTH = 512
HGRN_HEADS = HGRN_WIDTH // HGRN_HEAD_DIM
HGRN_CHUNK = 64
HGRN_SUB = 8
N_EXPERTS = 32
TOP_K = 4
D_EXPERT = 1024
SWIGLU_LIMIT = 7.0
SWIGLU_ALPHA = 1.702
RMS_EPS = 1e-5
IN_PROJ_WIDTH = 3 * ATTN_WIDTH + 4 * HGRN_WIDTH
LANES = 128
HEAD_PAIRS = ATTN_WIDTH // LANES
ROW_CHUNKS = D_MODEL // LANES

INPROJ_ROWS = 512
OUTPROJ_ROWS = 1024
RANK_SPAN = 256
FFN_ROWS = 512
FFN_COLS = 512
COMBINE_ROWS = 512
HGRN_SEQ_BLOCK = 1024
ATTN_UNROLL = 8
POS_TOKENS = 2048
MOVE_TOKENS = 2048
MOVE_UNROLL = 8
SC_GATHER_WINDOW = 128
VMEM_LIMIT = 48 * 1024 * 1024

NT_DIMS = (((1,), (1,)), ((), ()))
TN_DIMS = (((0,), (0,)), ((), ()))


def _cparams(*sem):
    return pltpu.CompilerParams(dimension_semantics=sem, vmem_limit_bytes=VMEM_LIMIT)


def _store_row_tiles(ref, value):
    rows = value.shape[0]
    for c in range(ROW_CHUNKS):
        ref[pl.ds(c, rows, stride=ROW_CHUNKS), :] = value[:, c * LANES:(c + 1) * LANES]


def _load_row_tiles(ref, rows):
    return jnp.concatenate([ref[pl.ds(c, rows, stride=ROW_CHUNKS), :] for c in range(ROW_CHUNKS)], axis=1)


def _inproj_kernel(x_ref, nw_ref, w_ref, qa_ref, ka_ref, va_ref, qh_ref, fh_ref, ih_ref, gh_ref):
    x = x_ref[...]
    xn = x * lax.rsqrt(jnp.mean(x * x, axis=-1, keepdims=True) + RMS_EPS) * nw_ref[...]
    xn = xn.astype(BF16)
    for c, ref in enumerate((qa_ref, ka_ref, va_ref, qh_ref, fh_ref, ih_ref, gh_ref)):
        y = jnp.dot(xn, w_ref[:, c * ATTN_WIDTH:(c + 1) * ATTN_WIDTH], preferred_element_type=F32)
        ref[...] = y.astype(ref.dtype)


def _inproj(x2, norm_w, w_in):
    n = x2.shape[0]
    tm = INPROJ_ROWS
    row = lambda width: pl.BlockSpec((tm, width), lambda i: (i, 0))
    dtypes = (F32, F32, F32, BF16, F32, BF16, F32)
    return pl.pallas_call(
        _inproj_kernel,
        grid=(n // tm,),
        in_specs=[row(D_MODEL),
                  pl.BlockSpec((1, D_MODEL), lambda i: (0, 0)),
                  pl.BlockSpec((D_MODEL, IN_PROJ_WIDTH), lambda i: (0, 0))],
        out_specs=[row(ATTN_WIDTH)] * 7,
        out_shape=[jax.ShapeDtypeStruct((n, ATTN_WIDTH), dt) for dt in dtypes],
        compiler_params=_cparams("arbitrary"),
        name="inproj",
    )(x2, norm_w.reshape(1, D_MODEL), w_in.astype(BF16))


def _t5_causal_bucket(dist):
    n = np.maximum(dist, 0)
    max_exact = REL_BUCKETS // 2
    large = max_exact + (np.log(np.maximum(n, 1) / max_exact)
                         / np.log(REL_MAX_DISTANCE / max_exact)
                         * (REL_BUCKETS - max_exact)).astype(np.int32)
    large = np.minimum(large, REL_BUCKETS - 1)
    return np.where(n < max_exact, n, large).astype(np.int32)


def _bucket_thresholds(max_dist):
    buckets = _t5_causal_bucket(np.arange(max_dist + 1))
    assert np.all(np.diff(buckets) >= 0)
    return [(b, int(np.argmax(buckets >= b))) for b in range(1, REL_BUCKETS) if np.any(buckets >= b)]


def _bias_kernel(rel_ref, o_ref):
    blk = ATTN_BLOCK
    h = pl.program_id(0)
    qi = lax.broadcasted_iota(jnp.int32, (blk, 2 * blk), 0)
    kj = lax.broadcasted_iota(jnp.int32, (blk, 2 * blk), 1)
    dist_sub = blk + qi - kj
    for pi, (window, dilation) in enumerate(DILATION_PATTERNS):
        span = window // dilation
        dist = dist_sub * dilation
        val = jnp.full((blk, 2 * blk), rel_ref[h], F32)
        for b, first_dist in _bucket_thresholds(window):
            val = jnp.where(dist >= first_dist, rel_ref[b * ATTN_HEADS + h], val)
        o_ref[pi, 0] = jnp.where((dist_sub >= 0) & (dist_sub <= span), val, -1e30)


def _masked_bias(rel_bias):
    blk = ATTN_BLOCK
    npat = len(DILATION_PATTERNS)
    bias = pl.pallas_call(
        _bias_kernel,
        grid=(ATTN_HEADS,),
        in_specs=[pl.BlockSpec(memory_space=pltpu.SMEM)],
        out_specs=pl.BlockSpec((npat, 1, blk, 2 * blk), lambda h: (0, h, 0, 0)),
        out_shape=jax.ShapeDtypeStruct((npat, ATTN_HEADS, blk, 2 * blk), F32),
        compiler_params=_cparams("arbitrary"),
        name="rel_bias",
    )(rel_bias.astype(F32).reshape(REL_BUCKETS * ATTN_HEADS))
    return bias.reshape(npat, HEAD_PAIRS, 2 * blk, 2 * blk)


def _attn_kernel(q_ref, k_ref, v_ref, bias_ref, o_ref, acc_ref, lse_ref, *, seq):
    blk = ATTN_BLOCK
    lane = lax.broadcasted_iota(jnp.int32, (1, LANES), 1)
    first = lane < ATTN_HEAD_DIM
    scale = ATTN_HEAD_DIM ** -0.5
    sel0 = jnp.where(first, scale, 0.0)
    sel1 = jnp.where(first, 0.0, scale)
    ones_cols = jnp.ones((2 * blk, LANES), BF16)

    def one_block(pi, dil, q_start, k_start, n_keys):
        qb = q_ref[0, pl.ds(q_start, blk, stride=dil), :]
        q2 = jnp.concatenate([qb * sel0, qb * sel1], axis=0).astype(BF16)
        kk = k_ref[0, pl.ds(k_start, n_keys, stride=dil), :].astype(BF16)
        vv = v_ref[0, pl.ds(k_start, n_keys, stride=dil), :].astype(BF16)
        s = lax.dot_general(q2, kk, NT_DIMS, preferred_element_type=F32)
        s = s + bias_ref[pi, 0, :, 2 * blk - n_keys:]
        m = jnp.max(s, axis=-1, keepdims=True)
        p = jnp.exp(s - m).astype(BF16)
        ov = jnp.dot(p, jnp.concatenate([vv, ones_cols[:n_keys]], axis=1), preferred_element_type=F32)
        z = ov[:, LANES:]
        o = ov[:, :LANES] / z
        lse = m + jnp.log(z)
        rows = pl.ds(q_start, blk, stride=dil)
        acc_ref[pi, rows, :] = jnp.where(first, o[:blk], o[blk:])
        lse_ref[pi, rows, :] = jnp.where(first, lse[:blk], lse[blk:])

    for pi, (_, dil) in enumerate(DILATION_PATTERNS):
        nb = seq // (dil * blk)

        def residue(r, carry, pi=pi, dil=dil, nb=nb):
            one_block(pi, dil, r, r, blk)

            def body(j, c):
                q_start = r + j * (blk * dil)
                one_block(pi, dil, q_start, q_start - blk * dil, 2 * blk)
                return c

            return lax.fori_loop(1, nb, body, carry, unroll=ATTN_UNROLL)

        lax.fori_loop(0, dil, residue, 0, unroll=ATTN_UNROLL if nb == 2 else 1)

    def merge(i, carry):
        rows = pl.ds(pl.multiple_of(i * blk, blk), blk)
        ls = [lse_ref[pi, rows, :] for pi in range(len(DILATION_PATTERNS))]
        m = functools.reduce(jnp.maximum, ls)
        es = [jnp.exp(l - m) for l in ls]
        num = sum(e * acc_ref[pi, rows, :] for pi, e in enumerate(es))
        o_ref[0, rows, :] = (num / sum(es)).astype(o_ref.dtype)
        return carry

    lax.fori_loop(0, seq // blk, merge, 0)


def _dilated_attention(qa, ka, va, bias, batch, seq):
    npat = len(DILATION_PATTERNS)
    shp = (batch, seq, ATTN_WIDTH)
    spec = pl.BlockSpec((1, seq, LANES), lambda b, p: (b, 0, p))
    out = pl.pallas_call(
        functools.partial(_attn_kernel, seq=seq),
        grid=(batch, HEAD_PAIRS),
        in_specs=[spec, spec, spec,
                  pl.BlockSpec((npat, 1, 2 * ATTN_BLOCK, 2 * ATTN_BLOCK), lambda b, p: (0, p, 0, 0))],
        out_specs=spec,
        out_shape=jax.ShapeDtypeStruct(shp, BF16),
        scratch_shapes=[pltpu.VMEM((npat, seq, LANES), F32), pltpu.VMEM((npat, seq, LANES), F32)],
        compiler_params=_cparams("arbitrary", "arbitrary"),
        name="dilated_attn",
    )(qa.reshape(shp), ka.reshape(shp), va.reshape(shp), bias)
    return out.reshape(batch * seq, ATTN_WIDTH)


def _hgrn_kernel(q_ref, f_ref, i_ref, g_ref, lbl_ref, nw_ref, o_ref, st_ref, *, n_chunks):
    c, sub, dk = HGRN_CHUNK, HGRN_SUB, HGRN_HEAD_DIM
    nsub = c // sub
    lbl = lbl_ref[...]
    e = jnp.exp(lbl - jnp.max(lbl, axis=0, keepdims=True))
    lb_all = e[0:1] / jnp.sum(e, axis=0, keepdims=True)
    nw = nw_ref[...]

    r64 = lax.broadcasted_iota(jnp.int32, (c, c), 0)
    c64 = lax.broadcasted_iota(jnp.int32, (c, c), 1)
    sub_end = (r64 // sub) * sub + (sub - 1)
    decay_sums = jnp.concatenate([(c64 <= r64), (c64 > r64) & (c64 <= sub_end), (c64 > r64)],
                                 axis=0).astype(BF16)
    col_sub = c64 // sub
    col_in_sub = c64 - (r64 // sub) * sub
    t_iota = lax.broadcasted_iota(jnp.int32, (nsub, sub, dk), 1)
    ones_rhs = jnp.ones((dk, c), BF16)

    @pl.when(pl.program_id(1) == 0)
    def _():
        st_ref[...] = jnp.zeros_like(st_ref)

    def cols_of(hd):
        return slice(hd * dk, (hd + 1) * dk)

    def gates(rows, hd):
        cols = cols_of(hd)
        lb = lb_all[:, cols]
        q = q_ref[0, rows, cols].astype(F32)
        f = lb + (1.0 - lb) * jax.nn.sigmoid(f_ref[0, rows, cols])
        log_f = jnp.log(f)
        log_hi = log_f.astype(BF16)
        log_lo = (log_f - log_hi.astype(F32)).astype(BF16)
        sums = (jnp.dot(decay_sums, log_hi, preferred_element_type=F32)
                + jnp.dot(decay_sums, log_lo, preferred_element_type=F32))
        return sums, 1.0 - f, q * jax.nn.sigmoid(q) * (dk ** -0.5)

    def products(rows, hd, sums, key, qf):
        cum, to_sub_end, to_chunk_end = sums[:c], sums[c:2 * c], sums[2 * c:]
        v = i_ref[0, rows, cols_of(hd)]
        st = st_ref[hd]
        o = lax.dot_general((qf * jnp.exp(cum)).astype(BF16), st.astype(BF16), NT_DIMS,
                            preferred_element_type=F32)
        kd = (key * jnp.exp(to_chunk_end)).astype(BF16)
        st_ref[hd] = (st * jnp.exp(cum[c - 1:c, :])
                      + lax.dot_general(v, kd, TN_DIMS, preferred_element_type=F32))

        cum3 = cum.reshape(nsub, sub, dk)
        key3 = key.reshape(nsub, sub, dk)
        qf3 = qf.reshape(nsub, sub, dk)
        khat = (key * jnp.exp(to_sub_end)).astype(BF16)
        qs = []
        for j in range(nsub - 1):
            t0 = sub * (j + 1)
            qs.append(qf[t0:] * jnp.exp(cum[t0:] - cum[t0 - 1:t0, :]))
        prod = lax.dot_general(jnp.concatenate(qs, axis=0).astype(BF16), khat, NT_DIMS,
                               preferred_element_type=F32)

        ws = []
        for s in range(sub):
            dec = jnp.exp(jnp.where(t_iota >= s, cum3 - cum3[:, s:s + 1, :], -jnp.inf))
            ws.append((qf3 * key3[:, s:s + 1, :] * dec).reshape(c, dk))
        pair_sums = jnp.dot(jnp.concatenate(ws, axis=0).astype(BF16), ones_rhs,
                            preferred_element_type=F32)
        return o, prod, pair_sums

    def finish(rows, hd, o, prod, pair_sums):
        cols = cols_of(hd)
        a = jnp.zeros((c, c), F32)
        r = 0
        for j in range(nsub - 1):
            t0 = sub * (j + 1)
            pj = jnp.concatenate([jnp.zeros((t0, c), F32), prod[r:r + c - t0, :]], axis=0)
            a = jnp.where(col_sub == j, pj, a)
            r += c - t0
        for s in range(sub):
            a = jnp.where(col_in_sub == s, pair_sums[s * c:(s + 1) * c, :], a)
        o = o + jnp.dot(a.astype(BF16), i_ref[0, rows, cols], preferred_element_type=F32)
        on = o * lax.rsqrt(jnp.mean(o * o, axis=-1, keepdims=True) + RMS_EPS) * nw
        g = g_ref[0, rows, cols]
        o_ref[0, rows, cols] = (on * (g * jax.nn.sigmoid(g))).astype(BF16)

    def body(ci, carry):
        rows = pl.ds(pl.multiple_of(ci * c, c), c)
        heads = range(HGRN_HEADS)
        stage1 = [gates(rows, hd) for hd in heads]
        stage2 = [products(rows, hd, *stage1[hd]) for hd in heads]
        for hd in heads:
            finish(rows, hd, *stage2[hd])
        return carry

    lax.fori_loop(0, n_chunks, body, 0, unroll=2)


def _hgrn(qh, fh, ih, gh, lb_logits, norm_w, batch, seq):
    sb = min(seq, HGRN_SEQ_BLOCK)
    spec = pl.BlockSpec((1, sb, HGRN_WIDTH), lambda b, s: (b, s, 0))
    slots = lb_logits.shape[0]
    shp = (batch, seq, HGRN_WIDTH)
    out = pl.pallas_call(
        functools.partial(_hgrn_kernel, n_chunks=sb // HGRN_CHUNK),
        grid=(batch, seq // sb),
        in_specs=[spec, spec, spec, spec,
                  pl.BlockSpec((slots, HGRN_WIDTH), lambda b, s: (0, 0)),
                  pl.BlockSpec((1, HGRN_HEAD_DIM), lambda b, s: (0, 0))],
        out_specs=spec,
        out_shape=jax.ShapeDtypeStruct(shp, BF16),
        scratch_shapes=[pltpu.VMEM((HGRN_HEADS, HGRN_HEAD_DIM, HGRN_HEAD_DIM), F32)],
        compiler_params=_cparams("arbitrary", "arbitrary"),
        name="hgrn2",
    )(qh.reshape(shp), fh.reshape(shp), ih.reshape(shp), gh.reshape(shp),
      lb_logits.astype(F32), norm_w.reshape(1, HGRN_HEAD_DIM))
    return out.reshape(batch * seq, HGRN_WIDTH)


def _outproj_kernel(attn_ref, hg_ref, x_ref, w_ref, nw_ref, rw_ref, rb_ref,
                    h_ref, xn_ref, idx_ref, gate_ref, rank_ref, cnt_ref, run_ref):
    y = jnp.dot(attn_ref[...], w_ref[0:ATTN_WIDTH, :], preferred_element_type=F32)
    y = y + jnp.dot(hg_ref[...], w_ref[ATTN_WIDTH:, :], preferred_element_type=F32)
    h = x_ref[...] + y
    h_ref[...] = h
    xn = h * lax.rsqrt(jnp.mean(h * h, axis=-1, keepdims=True) + RMS_EPS) * nw_ref[...]
    _store_row_tiles(xn_ref, xn)

    logits = lax.dot_general(rw_ref[...], xn, NT_DIMS, precision=HIGHEST,
                             preferred_element_type=F32) + rb_ref[...]
    eid = lax.broadcasted_iota(jnp.int32, logits.shape, 0)
    vals, idxs = [], []
    for _ in range(TOP_K):
        mx = jnp.max(logits, axis=0, keepdims=True)
        ix = jnp.min(jnp.where(logits == mx, eid, N_EXPERTS), axis=0, keepdims=True)
        vals.append(mx)
        idxs.append(ix)
        logits = jnp.where(eid == ix, -jnp.inf, logits)
    es = [jnp.exp(v - vals[0]) for v in vals]
    den = es[0] + es[1] + es[2] + es[3]
    idx_ref[...] = jnp.concatenate(idxs, axis=0)
    gate_ref[...] = jnp.concatenate([e / den for e in es], axis=0)

    @pl.when(pl.program_id(0) == 0)
    def _():
        run_ref[...] = jnp.zeros_like(run_ref)

    span = RANK_SPAN
    onehots = [(eid == ix).astype(F32) for ix in idxs]
    routed = onehots[0] + onehots[1] + onehots[2] + onehots[3]
    earlier = (lax.broadcasted_iota(jnp.int32, (span, span), 0)
               < lax.broadcasted_iota(jnp.int32, (span, span), 1)).astype(BF16)
    run = run_ref[...]
    ranks = []
    for c0 in range(0, logits.shape[1], span):
        part = routed[:, c0:c0 + span]
        before = jnp.dot(part.astype(BF16), earlier, preferred_element_type=F32) + run
        ranks.append(jnp.concatenate([jnp.sum(o[:, c0:c0 + span] * before, axis=0, keepdims=True)
                                      for o in onehots], axis=0))
        run = run + jnp.sum(part, axis=1, keepdims=True)
    rank_ref[...] = jnp.concatenate(ranks, axis=1).astype(jnp.int32)
    run_ref[...] = run
    cnt_ref[...] = run.astype(jnp.int32)


def _outproj(attn, hg, x2, w_out, norm_w, router_w, router_b):
    n = x2.shape[0]
    tm = OUTPROJ_ROWS
    row = lambda width: pl.BlockSpec((tm, width), lambda i: (i, 0))
    full = lambda a, b: pl.BlockSpec((a, b), lambda i: (0, 0))
    tok = pl.BlockSpec((TOP_K, tm), lambda i: (0, i))
    return pl.pallas_call(
        _outproj_kernel,
        grid=(n // tm,),
        in_specs=[row(ATTN_WIDTH), row(HGRN_WIDTH), row(D_MODEL), full(D_MODEL, D_MODEL),
                  full(1, D_MODEL), full(N_EXPERTS, D_MODEL), full(N_EXPERTS, 1)],
        out_specs=[row(D_MODEL), pl.BlockSpec((tm * ROW_CHUNKS, LANES), lambda i: (i, 0)),
                   tok, tok, tok, full(N_EXPERTS, 1)],
        out_shape=[jax.ShapeDtypeStruct((n, D_MODEL), F32),
                   jax.ShapeDtypeStruct((n * ROW_CHUNKS, LANES), F32),
                   jax.ShapeDtypeStruct((TOP_K, n), jnp.int32),
                   jax.ShapeDtypeStruct((TOP_K, n), F32),
                   jax.ShapeDtypeStruct((TOP_K, n), jnp.int32),
                   jax.ShapeDtypeStruct((N_EXPERTS, 1), jnp.int32)],
        scratch_shapes=[pltpu.VMEM((N_EXPERTS, 1), F32)],
        compiler_params=_cparams("arbitrary"),
        name="outproj_router",
    )(attn, hg, x2, w_out.astype(BF16), norm_w.reshape(1, D_MODEL),
      router_w.T.astype(F32), router_b.astype(F32).reshape(N_EXPERTS, 1))


def _ffn_kernel(be_ref, rv_ref, nx_ref, xs_ref, wgu_hbm, bgu_ref, wdn_hbm, bdn_ref, o_ref,
                wgu_f32, wdn_f32, wgu_bf, wdn_bf, sems):
    i = pl.program_id(0)
    rows_valid = rv_ref[i]
    new_expert = (i == 0) | (be_ref[i] != be_ref[jnp.maximum(i - 1, 0)])

    def weight_copies(e):
        return (pltpu.make_async_copy(wgu_hbm.at[e], wgu_f32, sems.at[0]),
                pltpu.make_async_copy(wdn_hbm.at[e], wdn_f32, sems.at[1]))

    @pl.when(i == 0)
    def _():
        for cp in weight_copies(be_ref[0]):
            cp.start()

    @pl.when(new_expert & (rows_valid > 0))
    def _():
        for cp in weight_copies(be_ref[i]):
            cp.wait()
        step = 128
        for r in range(0, D_MODEL, step):
            wgu_bf[r:r + step, :] = wgu_f32[r:r + step, :].astype(BF16)
        for r in range(0, D_EXPERT, step):
            wdn_bf[r:r + step, :] = wdn_f32[r:r + step, :].astype(BF16)

        @pl.when(nx_ref[i] >= 0)
        def _():
            for cp in weight_copies(nx_ref[i]):
                cp.start()

    @pl.when(rows_valid > 0)
    def _():
        live = lax.broadcasted_iota(jnp.int32, (FFN_ROWS, 1), 0) < rows_valid
        x = jnp.where(live, _load_row_tiles(xs_ref, FFN_ROWS), 0.0).astype(BF16)
        out = bdn_ref[0]
        for c0 in range(0, D_EXPERT, FFN_COLS):
            g_cols = slice(c0, c0 + FFN_COLS)
            u_cols = slice(D_EXPERT + c0, D_EXPERT + c0 + FFN_COLS)
            gate = jnp.dot(x, wgu_bf[:, g_cols], preferred_element_type=F32) + bgu_ref[0, :, g_cols]
            up = jnp.dot(x, wgu_bf[:, u_cols], preferred_element_type=F32) + bgu_ref[0, :, u_cols]
            gate = jnp.minimum(gate, SWIGLU_LIMIT)
            up = jnp.clip(up, -SWIGLU_LIMIT, SWIGLU_LIMIT)
            act = gate * jax.nn.sigmoid(SWIGLU_ALPHA * gate) * (up + 1.0)
            out = out + jnp.dot(act.astype(BF16), wdn_bf[g_cols, :], preferred_element_type=F32)
        _store_row_tiles(o_ref, out)

    @pl.when(rows_valid <= 0)
    def _():
        o_ref[...] = jnp.zeros_like(o_ref)


def _expert_ffn(block_e, rows_valid, next_e, xs, w_gu, b_gu, w_dn, b_dn):
    tm = FFN_ROWS
    tiles = pl.BlockSpec((tm * ROW_CHUNKS, LANES), lambda i, be, rv, nx: (i, 0))
    grid_spec = pltpu.PrefetchScalarGridSpec(
        num_scalar_prefetch=3,
        grid=(xs.shape[0] // (tm * ROW_CHUNKS),),
        in_specs=[tiles,
                  pl.BlockSpec(memory_space=pl.ANY),
                  pl.BlockSpec((1, 1, 2 * D_EXPERT), lambda i, be, rv, nx: (be[i], 0, 0)),
                  pl.BlockSpec(memory_space=pl.ANY),
                  pl.BlockSpec((1, 1, D_MODEL), lambda i, be, rv, nx: (be[i], 0, 0))],
        out_specs=tiles,
        scratch_shapes=[pltpu.VMEM((D_MODEL, 2 * D_EXPERT), F32), pltpu.VMEM((D_EXPERT, D_MODEL), F32),
                        pltpu.VMEM((D_MODEL, 2 * D_EXPERT), BF16), pltpu.VMEM((D_EXPERT, D_MODEL), BF16),
                        pltpu.SemaphoreType.DMA((2,))],
    )
    return pl.pallas_call(
        _ffn_kernel,
        grid_spec=grid_spec,
        out_shape=jax.ShapeDtypeStruct(xs.shape, F32),
        compiler_params=_cparams("arbitrary"),
        name="expert_ffn",
    )(block_e, rows_valid, next_e, xs, w_gu, b_gu.reshape(N_EXPERTS, 1, 2 * D_EXPERT),
      w_dn, b_dn.reshape(N_EXPERTS, 1, D_MODEL))


def _pos_kernel(idx_ref, rank_ref, pstart_ref, pos_ref):
    eid = lax.broadcasted_iota(jnp.int32, (N_EXPERTS, idx_ref.shape[1]), 0)
    pstart = pstart_ref[...]
    rows = [jnp.sum(jnp.where(eid == idx_ref[k:k + 1, :], pstart, 0), axis=0, keepdims=True)
            for k in range(TOP_K)]
    pos_ref[...] = (jnp.concatenate(rows, axis=0) + rank_ref[...]) * ROW_CHUNKS


def _positions(idx_t, rank_t, pstart):
    n = idx_t.shape[1]
    tl = POS_TOKENS
    tok = pl.BlockSpec((TOP_K, tl), lambda i: (0, i))
    return pl.pallas_call(
        _pos_kernel,
        grid=(n // tl,),
        in_specs=[tok, tok, pl.BlockSpec((N_EXPERTS, 1), lambda i: (0, 0))],
        out_specs=tok,
        out_shape=jax.ShapeDtypeStruct((TOP_K, n), jnp.int32),
        compiler_params=_cparams("arbitrary"),
        name="dispatch_pos",
    )(idx_t, rank_t, pstart.reshape(N_EXPERTS, 1))


def _tile_positions(pos, tokens):
    n = pos.shape[1]
    return pos.T.reshape(n // tokens, 1, tokens * TOP_K)


def _row_tile(ref, row_offset):
    return ref.at[pl.ds(pl.multiple_of(row_offset, ROW_CHUNKS), ROW_CHUNKS)]


def _dispatch_kernel(pos_ref, x_ref, xs_ref, sem):
    tokens = x_ref.shape[0] // ROW_CHUNKS

    def copy(t, k):
        return pltpu.make_async_copy(_row_tile(x_ref, t * ROW_CHUNKS),
                                     _row_tile(xs_ref, pos_ref[0, 0, t * TOP_K + k]), sem)

    def issue(t, carry):
        for k in range(TOP_K):
            copy(t, k).start(priority=k % 2)
        return carry

    def drain(t, carry):
        for k in range(TOP_K):
            copy(t, k).wait()
        return carry

    lax.fori_loop(0, tokens, issue, 0, unroll=MOVE_UNROLL)
    lax.fori_loop(0, tokens, drain, 0, unroll=MOVE_UNROLL)


def _dispatch_rows(pos, xn, out_rows):
    n = xn.shape[0] // ROW_CHUNKS
    tl = MOVE_TOKENS
    return pl.pallas_call(
        _dispatch_kernel,
        grid=(n // tl,),
        in_specs=[pl.BlockSpec((1, 1, tl * TOP_K), lambda i: (i, 0, 0), memory_space=pltpu.SMEM),
                  pl.BlockSpec((tl * ROW_CHUNKS, LANES), lambda i: (i, 0))],
        out_specs=pl.BlockSpec(memory_space=pl.ANY),
        out_shape=jax.ShapeDtypeStruct((out_rows * ROW_CHUNKS, LANES), xn.dtype),
        scratch_shapes=[pltpu.SemaphoreType.DMA(())],
        compiler_params=_cparams("arbitrary"),
        name="dispatch_rows",
    )(_tile_positions(pos, tl), xn)


def _combine_kernel(pos_ref, pos_next_ref, h_ref, g_ref, fw_ref, ys_ref, o_ref, buf, sems):
    i = pl.program_id(0)
    steps = pl.num_programs(0)
    tokens = h_ref.shape[0]
    slot = lax.rem(i, 2)

    def gather(p_ref, s, start):
        def body(t, carry):
            for k in range(TOP_K):
                cp = pltpu.make_async_copy(_row_tile(ys_ref, p_ref[0, 0, t * TOP_K + k]),
                                           _row_tile(buf.at[s, k], t * ROW_CHUNKS), sems.at[s])
                cp.start(priority=k % 2) if start else cp.wait()
            return carry
        lax.fori_loop(0, tokens, body, 0, unroll=MOVE_UNROLL)

    @pl.when(i == 0)
    def _():
        gather(pos_ref, 0, True)

    @pl.when(i + 1 < steps)
    def _():
        gather(pos_next_ref, 1 - slot, True)

    gather(pos_ref, slot, False)
    g = g_ref[...]
    y = h_ref[...]
    for k in range(TOP_K):
        y = y + _load_row_tiles(buf.at[slot, k], tokens) * g[:, k:k + 1]
    o_ref[...] = y * lax.rsqrt(jnp.mean(y * y, axis=-1, keepdims=True) + RMS_EPS) * fw_ref[...]


def _combine(h, ys, pos, gates_nk, final_w):
    n = h.shape[0]
    tm = COMBINE_ROWS
    steps = n // tm
    pos_t = _tile_positions(pos, tm)
    smem = lambda imap: pl.BlockSpec((1, 1, tm * TOP_K), imap, memory_space=pltpu.SMEM)
    return pl.pallas_call(
        _combine_kernel,
        grid=(steps,),
        in_specs=[smem(lambda i: (i, 0, 0)),
                  smem(lambda i: (jnp.minimum(i + 1, steps - 1), 0, 0)),
                  pl.BlockSpec((tm, D_MODEL), lambda i: (i, 0)),
                  pl.BlockSpec((tm, TOP_K), lambda i: (i, 0)),
                  pl.BlockSpec((1, D_MODEL), lambda i: (0, 0)),
                  pl.BlockSpec(memory_space=pl.ANY)],
        out_specs=pl.BlockSpec((tm, D_MODEL), lambda i: (i, 0)),
        out_shape=jax.ShapeDtypeStruct((n, D_MODEL), F32),
        scratch_shapes=[pltpu.VMEM((2, TOP_K, tm * ROW_CHUNKS, LANES), F32), pltpu.SemaphoreType.DMA((2,))],
        compiler_params=_cparams("arbitrary"),
        name="combine_norm",
    )(pos_t, pos_t, h, gates_nk, final_w.reshape(1, D_MODEL), ys)


def _sc_gather_rows(table, idx):
    m = idx.shape[0]
    w = SC_GATHER_WINDOW
    mesh = plsc.VectorSubcoreMesh(core_axis_name="core", subcore_axis_name="subcore")

    @functools.partial(pl.kernel, mesh=mesh,
                       out_type=jax.ShapeDtypeStruct((m, ROW_CHUNKS, LANES), table.dtype))
    def gather(x_hbm, i_hbm, o_hbm):
        part = 2
        for c in range(ROW_CHUNKS // part):
            def body(i_vmem, o_vmem, c=c):
                pltpu.sync_copy(x_hbm.at[i_vmem.at[0], pl.ds(c * part, part)], o_vmem)

            pltpu.emit_pipeline(
                body,
                grid=(m // w,),
                in_specs=[pl.BlockSpec((1, w), lambda i: (0, i))],
                out_specs=[pl.BlockSpec((w, part, LANES), lambda i, c=c: (i, c, 0))],
                core_axis_name=("core", "subcore"),
                dimension_semantics=(pltpu.PARALLEL,),
            )(i_hbm, o_hbm)

    return gather(table, idx.reshape(1, m))


def _combine_dense_kernel(h_ref, yp_ref, g_ref, fw_ref, o_ref):
    tokens = h_ref.shape[0]
    g = g_ref[...]
    y = h_ref[...]
    for k in range(TOP_K):
        y = y + _load_row_tiles(yp_ref.at[k], tokens) * g[:, k:k + 1]
    o_ref[...] = y * lax.rsqrt(jnp.mean(y * y, axis=-1, keepdims=True) + RMS_EPS) * fw_ref[...]


def _combine_dense(h, yp, gates_nk, final_w):
    n = h.shape[0]
    tm = COMBINE_ROWS
    return pl.pallas_call(
        _combine_dense_kernel,
        grid=(n // tm,),
        in_specs=[pl.BlockSpec((tm, D_MODEL), lambda i: (i, 0)),
                  pl.BlockSpec((TOP_K, tm * ROW_CHUNKS, LANES), lambda i: (0, i, 0)),
                  pl.BlockSpec((tm, TOP_K), lambda i: (i, 0)),
                  pl.BlockSpec((1, D_MODEL), lambda i: (0, 0))],
        out_specs=pl.BlockSpec((tm, D_MODEL), lambda i: (i, 0)),
        out_shape=jax.ShapeDtypeStruct((n, D_MODEL), F32),
        compiler_params=_cparams("arbitrary"),
        name="combine_norm",
    )(h, yp, gates_nk, final_w.reshape(1, D_MODEL))


def kernel(x, norm_mix_w, w_in, rel_bias, hgrn_lb_logits, hgrn_norm_w, w_out, norm_ffn_w,
           router_w, router_b, w_gate_up, b_gate_up, w_down, b_down, final_norm_w):
    batch, seq, _ = x.shape
    n = batch * seq
    x2 = x.reshape(n, D_MODEL)

    qa, ka, va, qh, fh, ih, gh = _inproj(x2, norm_mix_w[0], w_in[0])
    attn = _dilated_attention(qa, ka, va, _masked_bias(rel_bias), batch, seq)
    hg = _hgrn(qh, fh, ih, gh, hgrn_lb_logits, hgrn_norm_w[0], batch, seq)
    h, xn, idx_t, gate_t, rank_t, counts = _outproj(attn, hg, x2, w_out[0], norm_ffn_w[0],
                                                    router_w[0], router_b[0])

    tm = FFN_ROWS
    n_blocks = -(-(n * TOP_K) // tm) + N_EXPERTS
    counts = counts.reshape(N_EXPERTS)
    padded = (counts + tm - 1) // tm * tm
    pend = jnp.cumsum(padded)
    pstart = pend - padded
    block_row0 = jnp.arange(n_blocks, dtype=jnp.int32) * tm
    in_block = (block_row0[:, None] >= pstart[None, :]) & (block_row0[:, None] < pend[None, :])
    block_e = jnp.minimum(jnp.sum(pend[None, :] <= block_row0[:, None], axis=1), N_EXPERTS - 1).astype(jnp.int32)
    rows_valid = jnp.sum(jnp.where(in_block, jnp.clip(pstart + counts - block_row0[:, None], 0, tm), 0),
                         axis=1).astype(jnp.int32)

    pos = _positions(idx_t, rank_t, pstart.astype(jnp.int32))
    xs = _dispatch_rows(pos, xn, n_blocks * tm)
    experts = jnp.arange(N_EXPERTS, dtype=jnp.int32)
    later = jnp.where((experts[None, :] > experts[:, None]) & (counts[None, :] > 0), experts[None, :], N_EXPERTS)
    next_of = jnp.min(later, axis=1)
    next_e = jnp.where(next_of < N_EXPERTS, next_of, -1)[block_e].astype(jnp.int32)
    ys = _expert_ffn(block_e, rows_valid, next_e, xs, w_gate_up[0], b_gate_up[0], w_down[0], b_down[0])
    yp = _sc_gather_rows(ys.reshape(-1, ROW_CHUNKS, LANES), pos.reshape(-1) // ROW_CHUNKS)
    out = _combine_dense(h, yp.reshape(TOP_K, n * ROW_CHUNKS, LANES), gate_t.T, final_norm_w)
    return out.reshape(batch, seq, D_MODEL)
```

```python
import functools

import numpy as np
import jax
import jax.numpy as jnp
from jax import lax
from jax.experimental import pallas as pl
from jax.experimental.pallas import tpu as pltpu
from jax.experimental.pallas import tpu_sc as plsc

F32 = jnp.float32
BF16 = jnp.bfloat16
HIGHEST = lax.Precision.HIGHEST

D_MODEL = 1024
ATTN_HEAD_DIM = 64
ATTN_WIDTH = 512
ATTN_HEADS = ATTN_WIDTH // ATTN_HEAD_DIM
DILATION_PATTERNS = ((128, 1), (512, 4), (2048, 16))
ATTN_BLOCK = 128
REL_BUCKETS = 32
REL_MAX_DISTANCE = 2048
HGRN_HEAD_DIM = 128
HGRN_WIDTH = 512
HGRN_HEADS = HGRN_WIDTH // HGRN_HEAD_DIM
HGRN_CHUNK = 64
HGRN_SUB = 8
N_EXPERTS = 32
TOP_K = 4
D_EXPERT = 1024
SWIGLU_LIMIT = 7.0
SWIGLU_ALPHA = 1.702
RMS_EPS = 1e-5
IN_PROJ_WIDTH = 3 * ATTN_WIDTH + 4 * HGRN_WIDTH
LANES = 128
HEAD_PAIRS = ATTN_WIDTH // LANES
ROW_CHUNKS = D_MODEL // LANES

INPROJ_ROWS = 512
OUTPROJ_ROWS = 1024
RANK_SPAN = 256
FFN_ROWS = 512
FFN_COLS = 512
COMBINE_ROWS = 512
HGRN_SEQ_BLOCK = 1024
ATTN_UNROLL = 8
POS_TOKENS = 1024
MOVE_UNROLL = 8
SC_WINDOW = 128
SC_ROW_PART = 2
N_PARTS = 2
VMEM_LIMIT = 48 * 1024 * 1024

NT_DIMS = (((1,), (1,)), ((), ()))
TN_DIMS = (((0,), (0,)), ((), ()))


def _cparams(*sem):
    return pltpu.CompilerParams(dimension_semantics=sem, vmem_limit_bytes=VMEM_LIMIT)


def _store_row_tiles(ref, value):
    rows = value.shape[0]
    for c in range(ROW_CHUNKS):
        ref[pl.ds(c, rows, stride=ROW_CHUNKS), :] = value[:, c * LANES:(c + 1) * LANES]


def _load_row_tiles(ref, rows):
    return jnp.concatenate([ref[pl.ds(c, rows, stride=ROW_CHUNKS), :] for c in range(ROW_CHUNKS)], axis=1)


def _inproj_kernel(x_ref, nw_ref, w_ref, qa_ref, ka_ref, va_ref, qh_ref, fh_ref, ih_ref, gh_ref):
    x = x_ref[...]
    xn = x * lax.rsqrt(jnp.mean(x * x, axis=-1, keepdims=True) + RMS_EPS) * nw_ref[...]
    xn = xn.astype(BF16)
    for c, ref in enumerate((qa_ref, ka_ref, va_ref, qh_ref, fh_ref, ih_ref, gh_ref)):
        y = jnp.dot(xn, w_ref[:, c * ATTN_WIDTH:(c + 1) * ATTN_WIDTH], preferred_element_type=F32)
        ref[...] = y.astype(ref.dtype)


def _inproj(x2, norm_w, w_in):
    n = x2.shape[0]
    tm = INPROJ_ROWS
    row = lambda width: pl.BlockSpec((tm, width), lambda i: (i, 0))
    dtypes = (F32, F32, F32, BF16, F32, BF16, F32)
    return pl.pallas_call(
        _inproj_kernel,
        grid=(n // tm,),
        in_specs=[row(D_MODEL),
                  pl.BlockSpec((1, D_MODEL), lambda i: (0, 0)),
                  pl.BlockSpec((D_MODEL, IN_PROJ_WIDTH), lambda i: (0, 0))],
        out_specs=[row(ATTN_WIDTH)] * 7,
        out_shape=[jax.ShapeDtypeStruct((n, ATTN_WIDTH), dt) for dt in dtypes],
        compiler_params=_cparams("arbitrary"),
        name="inproj",
    )(x2, norm_w.reshape(1, D_MODEL), w_in.astype(BF16))


def _t5_causal_bucket(dist):
    n = np.maximum(dist, 0)
    max_exact = REL_BUCKETS // 2
    large = max_exact + (np.log(np.maximum(n, 1) / max_exact)
                         / np.log(REL_MAX_DISTANCE / max_exact)
                         * (REL_BUCKETS - max_exact)).astype(np.int32)
    large = np.minimum(large, REL_BUCKETS - 1)
    return np.where(n < max_exact, n, large).astype(np.int32)


def _bucket_thresholds(max_dist):
    buckets = _t5_causal_bucket(np.arange(max_dist + 1))
    assert np.all(np.diff(buckets) >= 0)
    return [(b, int(np.argmax(buckets >= b))) for b in range(1, REL_BUCKETS) if np.any(buckets >= b)]


def _bias_kernel(rel_ref, o_ref):
    blk = ATTN_BLOCK
    h = pl.program_id(0)
    qi = lax.broadcasted_iota(jnp.int32, (blk, 2 * blk), 0)
    kj = lax.broadcasted_iota(jnp.int32, (blk, 2 * blk), 1)
    dist_sub = blk + qi - kj
    for pi, (window, dilation) in enumerate(DILATION_PATTERNS):
        span = window // dilation
        dist = dist_sub * dilation
        val = jnp.full((blk, 2 * blk), rel_ref[h], F32)
        for b, first_dist in _bucket_thresholds(window):
            val = jnp.where(dist >= first_dist, rel_ref[b * ATTN_HEADS + h], val)
        o_ref[pi, 0] = jnp.where((dist_sub >= 0) & (dist_sub <= span), val, -1e30)


def _masked_bias(rel_bias):
    blk = ATTN_BLOCK
    npat = len(DILATION_PATTERNS)
    bias = pl.pallas_call(
        _bias_kernel,
        grid=(ATTN_HEADS,),
        in_specs=[pl.BlockSpec(memory_space=pltpu.SMEM)],
        out_specs=pl.BlockSpec((npat, 1, blk, 2 * blk), lambda h: (0, h, 0, 0)),
        out_shape=jax.ShapeDtypeStruct((npat, ATTN_HEADS, blk, 2 * blk), F32),
        compiler_params=_cparams("arbitrary"),
        name="rel_bias",
    )(rel_bias.astype(F32).reshape(REL_BUCKETS * ATTN_HEADS))
    return bias.reshape(npat, HEAD_PAIRS, 2 * blk, 2 * blk)


def _attn_kernel(q_ref, k_ref, v_ref, bias_ref, o_ref, acc_ref, lse_ref, *, seq):
    blk = ATTN_BLOCK
    lane = lax.broadcasted_iota(jnp.int32, (1, LANES), 1)
    first = lane < ATTN_HEAD_DIM
    scale = ATTN_HEAD_DIM ** -0.5
    sel0 = jnp.where(first, scale, 0.0)
    sel1 = jnp.where(first, 0.0, scale)
    ones_cols = jnp.ones((2 * blk, LANES), BF16)

    def one_block(pi, dil, q_start, k_start, n_keys):
        qb = q_ref[0, pl.ds(q_start, blk, stride=dil), :]
        q2 = jnp.concatenate([qb * sel0, qb * sel1], axis=0).astype(BF16)
        kk = k_ref[0, pl.ds(k_start, n_keys, stride=dil), :].astype(BF16)
        vv = v_ref[0, pl.ds(k_start, n_keys, stride=dil), :].astype(BF16)
        s = lax.dot_general(q2, kk, NT_DIMS, preferred_element_type=F32)
        s = s + bias_ref[pi, 0, :, 2 * blk - n_keys:]
        m = jnp.max(s, axis=-1, keepdims=True)
        p = jnp.exp(s - m).astype(BF16)
        ov = jnp.dot(p, jnp.concatenate([vv, ones_cols[:n_keys]], axis=1), preferred_element_type=F32)
        z = ov[:, LANES:]
        o = ov[:, :LANES] / z
        lse = m + jnp.log(z)
        rows = pl.ds(q_start, blk, stride=dil)
        acc_ref[pi, rows, :] = jnp.where(first, o[:blk], o[blk:])
        lse_ref[pi, rows, :] = jnp.where(first, lse[:blk], lse[blk:])

    for pi, (_, dil) in enumerate(DILATION_PATTERNS):
        nb = seq // (dil * blk)

        def residue(r, carry, pi=pi, dil=dil, nb=nb):
            one_block(pi, dil, r, r, blk)

            def body(j, c):
                q_start = r + j * (blk * dil)
                one_block(pi, dil, q_start, q_start - blk * dil, 2 * blk)
                return c

            return lax.fori_loop(1, nb, body, carry, unroll=ATTN_UNROLL)

        lax.fori_loop(0, dil, residue, 0, unroll=ATTN_UNROLL if nb == 2 else 1)

    def merge(i, carry):
        rows = pl.ds(pl.multiple_of(i * blk, blk), blk)
        ls = [lse_ref[pi, rows, :] for pi in range(len(DILATION_PATTERNS))]
        m = functools.reduce(jnp.maximum, ls)
        es = [jnp.exp(l - m) for l in ls]
        num = sum(e * acc_ref[pi, rows, :] for pi, e in enumerate(es))
        o_ref[0, rows, :] = (num / sum(es)).astype(o_ref.dtype)
        return carry

    lax.fori_loop(0, seq // blk, merge, 0)


def _dilated_attention(qa, ka, va, bias, batch, seq):
    npat = len(DILATION_PATTERNS)
    shp = (batch, seq, ATTN_WIDTH)
    spec = pl.BlockSpec((1, seq, LANES), lambda b, p: (b, 0, p))
    out = pl.pallas_call(
        functools.partial(_attn_kernel, seq=seq),
        grid=(batch, HEAD_PAIRS),
        in_specs=[spec, spec, spec,
                  pl.BlockSpec((npat, 1, 2 * ATTN_BLOCK, 2 * ATTN_BLOCK), lambda b, p: (0, p, 0, 0))],
        out_specs=spec,
        out_shape=jax.ShapeDtypeStruct(shp, BF16),
        scratch_shapes=[pltpu.VMEM((npat, seq, LANES), F32), pltpu.VMEM((npat, seq, LANES), F32)],
        compiler_params=_cparams("arbitrary", "arbitrary"),
        name="dilated_attn",
    )(qa.reshape(shp), ka.reshape(shp), va.reshape(shp), bias)
    return out.reshape(batch * seq, ATTN_WIDTH)


def _hgrn_kernel(q_ref, f_ref, i_ref, g_ref, lbl_ref, nw_ref, o_ref, st_ref, *, n_chunks):
    c, sub, dk = HGRN_CHUNK, HGRN_SUB, HGRN_HEAD_DIM
    nsub = c // sub
    lbl = lbl_ref[...]
    e = jnp.exp(lbl - jnp.max(lbl, axis=0, keepdims=True))
    lb_all = e[0:1] / jnp.sum(e, axis=0, keepdims=True)
    nw = nw_ref[...]

    r64 = lax.broadcasted_iota(jnp.int32, (c, c), 0)
    c64 = lax.broadcasted_iota(jnp.int32, (c, c), 1)
    sub_end = (r64 // sub) * sub + (sub - 1)
    decay_sums = jnp.concatenate([(c64 <= r64), (c64 > r64) & (c64 <= sub_end), (c64 > r64)],
                                 axis=0).astype(BF16)
    col_sub = c64 // sub
    col_in_sub = c64 - (r64 // sub) * sub
    t_iota = lax.broadcasted_iota(jnp.int32, (nsub, sub, dk), 1)
    ones_rhs = jnp.ones((dk, c), BF16)

    @pl.when(pl.program_id(1) == 0)
    def _():
        st_ref[...] = jnp.zeros_like(st_ref)

    def cols_of(hd):
        return slice(hd * dk, (hd + 1) * dk)

    def gates(rows, hd):
        cols = cols_of(hd)
        lb = lb_all[:, cols]
        q = q_ref[0, rows, cols].astype(F32)
        f = lb + (1.0 - lb) * jax.nn.sigmoid(f_ref[0, rows, cols])
        log_f = jnp.log(f)
        log_hi = log_f.astype(BF16)
        log_lo = (log_f - log_hi.astype(F32)).astype(BF16)
        sums = (jnp.dot(decay_sums, log_hi, preferred_element_type=F32)
                + jnp.dot(decay_sums, log_lo, preferred_element_type=F32))
        return sums, 1.0 - f, q * jax.nn.sigmoid(q) * (dk ** -0.5)

    def products(rows, hd, sums, key, qf):
        cum, to_sub_end, to_chunk_end = sums[:c], sums[c:2 * c], sums[2 * c:]
        v = i_ref[0, rows, cols_of(hd)]
        st = st_ref[hd]
        o = lax.dot_general((qf * jnp.exp(cum)).astype(BF16), st.astype(BF16), NT_DIMS,
                            preferred_element_type=F32)
        kd = (key * jnp.exp(to_chunk_end)).astype(BF16)
        st_ref[hd] = (st * jnp.exp(cum[c - 1:c, :])
                      + lax.dot_general(v, kd, TN_DIMS, preferred_element_type=F32))

        cum3 = cum.reshape(nsub, sub, dk)
        key3 = key.reshape(nsub, sub, dk)
        qf3 = qf.reshape(nsub, sub, dk)
        khat = (key * jnp.exp(to_sub_end)).astype(BF16)
        qs = []
        for j in range(nsub - 1):
            t0 = sub * (j + 1)
            qs.append(qf[t0:] * jnp.exp(cum[t0:] - cum[t0 - 1:t0, :]))
        prod = lax.dot_general(jnp.concatenate(qs, axis=0).astype(BF16), khat, NT_DIMS,
                               preferred_element_type=F32)

        ws = []
        for s in range(sub):
            dec = jnp.exp(jnp.where(t_iota >= s, cum3 - cum3[:, s:s + 1, :], -jnp.inf))
            ws.append((qf3 * key3[:, s:s + 1, :] * dec).reshape(c, dk))
        pair_sums = jnp.dot(jnp.concatenate(ws, axis=0).astype(BF16), ones_rhs,
                            preferred_element_type=F32)
        return o, prod, pair_sums

    def finish(rows, hd, o, prod, pair_sums):
        cols = cols_of(hd)
        a = jnp.zeros((c, c), F32)
        r = 0
        for j in range(nsub - 1):
            t0 = sub * (j + 1)
            pj = jnp.concatenate([jnp.zeros((t0, c), F32), prod[r:r + c - t0, :]], axis=0)
            a = jnp.where(col_sub == j, pj, a)
            r += c - t0
        for s in range(sub):
            a = jnp.where(col_in_sub == s, pair_sums[s * c:(s + 1) * c, :], a)
        o = o + jnp.dot(a.astype(BF16), i_ref[0, rows, cols], preferred_element_type=F32)
        on = o * lax.rsqrt(jnp.mean(o * o, axis=-1, keepdims=True) + RMS_EPS) * nw
        g = g_ref[0, rows, cols]
        o_ref[0, rows, cols] = (on * (g * jax.nn.sigmoid(g))).astype(BF16)

    def body(ci, carry):
        rows = pl.ds(pl.multiple_of(ci * c, c), c)
        heads = range(HGRN_HEADS)
        stage1 = [gates(rows, hd) for hd in heads]
        stage2 = [products(rows, hd, *stage1[hd]) for hd in heads]
        for hd in heads:
            finish(rows, hd, *stage2[hd])
        return carry

    lax.fori_loop(0, n_chunks, body, 0, unroll=2)


def _hgrn(qh, fh, ih, gh, lb_logits, norm_w, batch, seq):
    sb = min(seq, HGRN_SEQ_BLOCK)
    spec = pl.BlockSpec((1, sb, HGRN_WIDTH), lambda b, s: (b, s, 0))
    slots = lb_logits.shape[0]
    shp = (batch, seq, HGRN_WIDTH)
    out = pl.pallas_call(
        functools.partial(_hgrn_kernel, n_chunks=sb // HGRN_CHUNK),
        grid=(batch, seq // sb),
        in_specs=[spec, spec, spec, spec,
                  pl.BlockSpec((slots, HGRN_WIDTH), lambda b, s: (0, 0)),
                  pl.BlockSpec((1, HGRN_HEAD_DIM), lambda b, s: (0, 0))],
        out_specs=spec,
        out_shape=jax.ShapeDtypeStruct(shp, BF16),
        scratch_shapes=[pltpu.VMEM((HGRN_HEADS, HGRN_HEAD_DIM, HGRN_HEAD_DIM), F32)],
        compiler_params=_cparams("arbitrary", "arbitrary"),
        name="hgrn2",
    )(qh.reshape(shp), fh.reshape(shp), ih.reshape(shp), gh.reshape(shp),
      lb_logits.astype(F32), norm_w.reshape(1, HGRN_HEAD_DIM))
    return out.reshape(batch * seq, HGRN_WIDTH)


def _outproj_kernel(attn_ref, hg_ref, x_ref, w_ref, nw_ref, rw_ref, rb_ref,
                    h_ref, xn_ref, idx_ref, gate_ref, rank_ref, cnt_ref, run_ref):
    y = jnp.dot(attn_ref[...], w_ref[0:ATTN_WIDTH, :], preferred_element_type=F32)
    y = y + jnp.dot(hg_ref[...], w_ref[ATTN_WIDTH:, :], preferred_element_type=F32)
    h = x_ref[...] + y
    h_ref[...] = h
    xn = h * lax.rsqrt(jnp.mean(h * h, axis=-1, keepdims=True) + RMS_EPS) * nw_ref[...]
    _store_row_tiles(xn_ref, xn)

    logits = lax.dot_general(rw_ref[...], xn, NT_DIMS, precision=HIGHEST,
                             preferred_element_type=F32) + rb_ref[...]
    eid = lax.broadcasted_iota(jnp.int32, logits.shape, 0)
    vals, idxs = [], []
    for _ in range(TOP_K):
        mx = jnp.max(logits, axis=0, keepdims=True)
        ix = jnp.min(jnp.where(logits == mx, eid, N_EXPERTS), axis=0, keepdims=True)
        vals.append(mx)
        idxs.append(ix)
        logits = jnp.where(eid == ix, -jnp.inf, logits)
    es = [jnp.exp(v - vals[0]) for v in vals]
    den = es[0] + es[1] + es[2] + es[3]
    idx_ref[...] = jnp.concatenate(idxs, axis=0)
    gate_ref[...] = jnp.concatenate([e / den for e in es], axis=0)

    @pl.when(pl.program_id(0) == 0)
    def _():
        run_ref[...] = jnp.zeros_like(run_ref)

    span = RANK_SPAN
    onehots = [(eid == ix).astype(F32) for ix in idxs]
    routed = onehots[0] + onehots[1] + onehots[2] + onehots[3]
    earlier = (lax.broadcasted_iota(jnp.int32, (span, span), 0)
               < lax.broadcasted_iota(jnp.int32, (span, span), 1)).astype(BF16)
    run = run_ref[...]
    ranks = []
    for c0 in range(0, logits.shape[1], span):
        part = routed[:, c0:c0 + span]
        before = jnp.dot(part.astype(BF16), earlier, preferred_element_type=F32) + run
        ranks.append(jnp.concatenate([jnp.sum(o[:, c0:c0 + span] * before, axis=0, keepdims=True)
                                      for o in onehots], axis=0))
        run = run + jnp.sum(part, axis=1, keepdims=True)
    rank_ref[...] = jnp.concatenate(ranks, axis=1).astype(jnp.int32)
    run_ref[...] = run
    cnt_ref[...] = run.astype(jnp.int32)


def _outproj(attn, hg, x2, w_out, norm_w, router_w, router_b, part, n_part):
    tm = OUTPROJ_ROWS
    steps = n_part // tm
    row_in = lambda width: pl.BlockSpec((tm, width), lambda i: (i + part * steps, 0))
    row = lambda width: pl.BlockSpec((tm, width), lambda i: (i, 0))
    full = lambda a, b: pl.BlockSpec((a, b), lambda i: (0, 0))
    tok = pl.BlockSpec((TOP_K, tm), lambda i: (0, i))
    return pl.pallas_call(
        _outproj_kernel,
        grid=(steps,),
        in_specs=[row_in(ATTN_WIDTH), row_in(HGRN_WIDTH), row_in(D_MODEL), full(D_MODEL, D_MODEL),
                  full(1, D_MODEL), full(N_EXPERTS, D_MODEL), full(N_EXPERTS, 1)],
        out_specs=[row(D_MODEL), pl.BlockSpec((tm * ROW_CHUNKS, LANES), lambda i: (i, 0)),
                   tok, tok, tok, full(N_EXPERTS, 1)],
        out_shape=[jax.ShapeDtypeStruct((n_part, D_MODEL), F32),
                   jax.ShapeDtypeStruct((n_part * ROW_CHUNKS, LANES), F32),
                   jax.ShapeDtypeStruct((TOP_K, n_part), jnp.int32),
                   jax.ShapeDtypeStruct((TOP_K, n_part), F32),
                   jax.ShapeDtypeStruct((TOP_K, n_part), jnp.int32),
                   jax.ShapeDtypeStruct((N_EXPERTS, 1), jnp.int32)],
        scratch_shapes=[pltpu.VMEM((N_EXPERTS, 1), F32)],
        compiler_params=_cparams("arbitrary"),
        name="outproj_router",
    )(attn, hg, x2, w_out, norm_w.reshape(1, D_MODEL), router_w, router_b)


def _ffn_kernel(be_ref, rv_ref, nx_ref, xs_ref, wgu_hbm, bgu_ref, wdn_hbm, bdn_ref, o_ref,
                wgu_f32, wdn_f32, wgu_bf, wdn_bf, sems):
    i = pl.program_id(0)
    rows_valid = rv_ref[i]
    new_expert = (i == 0) | (be_ref[i] != be_ref[jnp.maximum(i - 1, 0)])

    def weight_copies(e):
        return (pltpu.make_async_copy(wgu_hbm.at[e], wgu_f32, sems.at[0]),
                pltpu.make_async_copy(wdn_hbm.at[e], wdn_f32, sems.at[1]))

    @pl.when(i == 0)
    def _():
        for cp in weight_copies(be_ref[0]):
            cp.start()

    @pl.when(new_expert & (rows_valid > 0))
    def _():
        for cp in weight_copies(be_ref[i]):
            cp.wait()
        step = 128
        for r in range(0, D_MODEL, step):
            wgu_bf[r:r + step, :] = wgu_f32[r:r + step, :].astype(BF16)
        for r in range(0, D_EXPERT, step):
            wdn_bf[r:r + step, :] = wdn_f32[r:r + step, :].astype(BF16)

        @pl.when(nx_ref[i] >= 0)
        def _():
            for cp in weight_copies(nx_ref[i]):
                cp.start()

    @pl.when(rows_valid > 0)
    def _():
        live = lax.broadcasted_iota(jnp.int32, (FFN_ROWS, 1), 0) < rows_valid
        x = jnp.where(live, _load_row_tiles(xs_ref, FFN_ROWS), 0.0).astype(BF16)
        out = bdn_ref[0]
        for c0 in range(0, D_EXPERT, FFN_COLS):
            g_cols = slice(c0, c0 + FFN_COLS)
            u_cols = slice(D_EXPERT + c0, D_EXPERT + c0 + FFN_COLS)
            gate = jnp.dot(x, wgu_bf[:, g_cols], preferred_element_type=F32) + bgu_ref[0, :, g_cols]
            up = jnp.dot(x, wgu_bf[:, u_cols], preferred_element_type=F32) + bgu_ref[0, :, u_cols]
            gate = jnp.minimum(gate, SWIGLU_LIMIT)
            up = jnp.clip(up, -SWIGLU_LIMIT, SWIGLU_LIMIT)
            act = gate * jax.nn.sigmoid(SWIGLU_ALPHA * gate) * (up + 1.0)
            out = out + jnp.dot(act.astype(BF16), wdn_bf[g_cols, :], preferred_element_type=F32)
        _store_row_tiles(o_ref, out)

    @pl.when(rows_valid <= 0)
    def _():
        o_ref[...] = jnp.zeros_like(o_ref)


def _expert_ffn(block_e, rows_valid, next_e, xs, w_gu, b_gu, w_dn, b_dn):
    tm = FFN_ROWS
    tiles = pl.BlockSpec((tm * ROW_CHUNKS, LANES), lambda i, be, rv, nx: (i, 0))
    grid_spec = pltpu.PrefetchScalarGridSpec(
        num_scalar_prefetch=3,
        grid=(xs.shape[0] // (tm * ROW_CHUNKS),),
        in_specs=[tiles,
                  pl.BlockSpec(memory_space=pl.ANY),
                  pl.BlockSpec((1, 1, 2 * D_EXPERT), lambda i, be, rv, nx: (be[i], 0, 0)),
                  pl.BlockSpec(memory_space=pl.ANY),
                  pl.BlockSpec((1, 1, D_MODEL), lambda i, be, rv, nx: (be[i], 0, 0))],
        out_specs=tiles,
        scratch_shapes=[pltpu.VMEM((D_MODEL, 2 * D_EXPERT), F32), pltpu.VMEM((D_EXPERT, D_MODEL), F32),
                        pltpu.VMEM((D_MODEL, 2 * D_EXPERT), BF16), pltpu.VMEM((D_EXPERT, D_MODEL), BF16),
                        pltpu.SemaphoreType.DMA((2,))],
    )
    return pl.pallas_call(
        _ffn_kernel,
        grid_spec=grid_spec,
        out_shape=jax.ShapeDtypeStruct(xs.shape, F32),
        compiler_params=_cparams("arbitrary"),
        name="expert_ffn",
    )(block_e, rows_valid, next_e, xs, w_gu, b_gu.reshape(N_EXPERTS, 1, 2 * D_EXPERT),
      w_dn, b_dn.reshape(N_EXPERTS, 1, D_MODEL))


def _pos_kernel(idx_ref, rank_ref, pstart_ref, pos_ref):
    eid = lax.broadcasted_iota(jnp.int32, (N_EXPERTS, idx_ref.shape[1]), 0)
    pstart = pstart_ref[...]
    rows = [jnp.sum(jnp.where(eid == idx_ref[k:k + 1, :], pstart, 0), axis=0, keepdims=True)
            for k in range(TOP_K)]
    pos_ref[...] = (jnp.concatenate(rows, axis=0) + rank_ref[...]) * ROW_CHUNKS


def _positions(idx_t, rank_t, pstart):
    n = idx_t.shape[1]
    tl = POS_TOKENS
    tok = pl.BlockSpec((TOP_K, tl), lambda i: (0, i))
    return pl.pallas_call(
        _pos_kernel,
        grid=(n // tl,),
        in_specs=[tok, tok, pl.BlockSpec((N_EXPERTS, 1), lambda i: (0, 0))],
        out_specs=tok,
        out_shape=jax.ShapeDtypeStruct((TOP_K, n), jnp.int32),
        compiler_params=_cparams("arbitrary"),
        name="dispatch_pos",
    )(idx_t, rank_t, pstart.reshape(N_EXPERTS, 1))


def _tile_positions(pos, tokens):
    n = pos.shape[1]
    return pos.T.reshape(n // tokens, 1, tokens * TOP_K)


def _row_tile(ref, row_offset):
    return ref.at[pl.ds(pl.multiple_of(row_offset, ROW_CHUNKS), ROW_CHUNKS)]


def _dispatch_rows(xn_tiles, rows, out_rows):
    tokens = xn_tiles.shape[0]
    m = rows.shape[0]
    w = SC_WINDOW
    mesh = plsc.VectorSubcoreMesh(core_axis_name="core", subcore_axis_name="subcore")

    @functools.partial(pl.kernel, mesh=mesh,
                       out_type=jax.ShapeDtypeStruct((out_rows, ROW_CHUNKS, LANES), xn_tiles.dtype))
    def scatter(x_hbm, i_hbm, o_hbm):
        for c in range(ROW_CHUNKS // SC_ROW_PART):
            def body(x_vmem, i_vmem, c=c):
                pltpu.sync_copy(x_vmem, o_hbm.at[i_vmem.at[0], pl.ds(c * SC_ROW_PART, SC_ROW_PART)])

            pltpu.emit_pipeline(
                body,
                grid=(m // w,),
                in_specs=[pl.BlockSpec((w, SC_ROW_PART, LANES), lambda i, c=c: (lax.rem(i, tokens // w), c, 0)),
                          pl.BlockSpec((1, w), lambda i: (0, i))],
                out_specs=[],
                core_axis_name=("core", "subcore"),
                dimension_semantics=(pltpu.PARALLEL,),
            )(x_hbm, i_hbm)

    return scatter(xn_tiles, rows.reshape(1, m))


def _combine_kernel(pos_ref, pos_next_ref, *refs, part_steps):
    h_refs, (g_ref, fw_ref), ys_refs = refs[:N_PARTS], refs[N_PARTS:N_PARTS + 2], refs[N_PARTS + 2:2 * N_PARTS + 2]
    o_ref, buf, sems = refs[2 * N_PARTS + 2:]
    i = pl.program_id(0)
    steps = pl.num_programs(0)
    tokens = o_ref.shape[0]
    slot = lax.rem(i, 2)

    def gather(p_ref, tile, s, start):
        for part, ys_ref in enumerate(ys_refs):
            @pl.when((tile >= part * part_steps) & (tile < (part + 1) * part_steps))
            def _(ys_ref=ys_ref):
                def body(t, carry):
                    for k in range(TOP_K):
                        cp = pltpu.make_async_copy(_row_tile(ys_ref, p_ref[0, 0, t * TOP_K + k]),
                                                   _row_tile(buf.at[s, k], t * ROW_CHUNKS), sems.at[s])
                        cp.start(priority=k % 2) if start else cp.wait()
                    return carry
                lax.fori_loop(0, tokens, body, 0, unroll=MOVE_UNROLL)

    @pl.when(i == 0)
    def _():
        gather(pos_ref, i, 0, True)

    @pl.when(i + 1 < steps)
    def _():
        gather(pos_next_ref, i + 1, 1 - slot, True)

    gather(pos_ref, i, slot, False)
    g = g_ref[...]
    y = h_refs[0][...]
    for part in range(1, N_PARTS):
        y = jnp.where(i >= part * part_steps, h_refs[part][...], y)
    for k in range(TOP_K):
        y = y + _load_row_tiles(buf.at[slot, k], tokens) * g[:, k:k + 1]
    o_ref[...] = y * lax.rsqrt(jnp.mean(y * y, axis=-1, keepdims=True) + RMS_EPS) * fw_ref[...]


def _combine(hs, yss, pos, gates_nk, final_w):
    n_part = hs[0].shape[0]
    n = n_part * N_PARTS
    tm = COMBINE_ROWS
    steps = n // tm
    part_steps = n_part // tm
    pos_t = _tile_positions(pos, tm)
    smem = lambda imap: pl.BlockSpec((1, 1, tm * TOP_K), imap, memory_space=pltpu.SMEM)
    h_spec = lambda part: pl.BlockSpec(
        (tm, D_MODEL), lambda i: (jnp.clip(i - part * part_steps, 0, part_steps - 1), 0))
    return pl.pallas_call(
        functools.partial(_combine_kernel, part_steps=part_steps),
        grid=(steps,),
        in_specs=[smem(lambda i: (i, 0, 0)),
                  smem(lambda i: (jnp.minimum(i + 1, steps - 1), 0, 0))]
                 + [h_spec(part) for part in range(N_PARTS)]
                 + [pl.BlockSpec((tm, TOP_K), lambda i: (i, 0)),
                    pl.BlockSpec((1, D_MODEL), lambda i: (0, 0))]
                 + [pl.BlockSpec(memory_space=pl.ANY)] * N_PARTS,
        out_specs=pl.BlockSpec((tm, D_MODEL), lambda i: (i, 0)),
        out_shape=jax.ShapeDtypeStruct((n, D_MODEL), F32),
        scratch_shapes=[pltpu.VMEM((2, TOP_K, tm * ROW_CHUNKS, LANES), F32), pltpu.SemaphoreType.DMA((2,))],
        compiler_params=_cparams("arbitrary"),
        name="combine_norm",
    )(pos_t, pos_t, *hs, gates_nk, final_w.reshape(1, D_MODEL), *yss)


def _block_plan(counts, n_blocks):
    tm = FFN_ROWS
    padded = (counts + tm - 1) // tm * tm
    pend = jnp.cumsum(padded)
    pstart = pend - padded
    block_row0 = jnp.arange(n_blocks, dtype=jnp.int32) * tm
    in_block = (block_row0[:, None] >= pstart[None, :]) & (block_row0[:, None] < pend[None, :])
    block_e = jnp.minimum(jnp.sum(pend[None, :] <= block_row0[:, None], axis=1), N_EXPERTS - 1).astype(jnp.int32)
    rows_valid = jnp.sum(jnp.where(in_block, jnp.clip(pstart + counts - block_row0[:, None], 0, tm), 0),
                         axis=1).astype(jnp.int32)
    experts = jnp.arange(N_EXPERTS, dtype=jnp.int32)
    later = jnp.where((experts[None, :] > experts[:, None]) & (counts[None, :] > 0), experts[None, :], N_EXPERTS)
    next_of = jnp.min(later, axis=1)
    next_e = jnp.where(next_of < N_EXPERTS, next_of, -1)[block_e].astype(jnp.int32)
    return pstart.astype(jnp.int32), block_e, rows_valid, next_e


def kernel(x, norm_mix_w, w_in, rel_bias, hgrn_lb_logits, hgrn_norm_w, w_out, norm_ffn_w,
           router_w, router_b, w_gate_up, b_gate_up, w_down, b_down, final_norm_w):
    batch, seq, _ = x.shape
    n = batch * seq
    x2 = x.reshape(n, D_MODEL)

    qa, ka, va, qh, fh, ih, gh = _inproj(x2, norm_mix_w[0], w_in[0])
    attn = _dilated_attention(qa, ka, va, _masked_bias(rel_bias), batch, seq)
    hg = _hgrn(qh, fh, ih, gh, hgrn_lb_logits, hgrn_norm_w[0], batch, seq)

    n_part = n // N_PARTS
    n_blocks = -(-(n_part * TOP_K) // FFN_ROWS) + N_EXPERTS
    w_out_bf = w_out[0].astype(BF16)
    router_wt = router_w[0].T.astype(F32)
    router_b2 = router_b[0].astype(F32).reshape(N_EXPERTS, 1)
    routed = [_outproj(attn, hg, x2, w_out_bf, norm_ffn_w[0], router_wt, router_b2, part, n_part)
              for part in range(N_PARTS)]
    hs, poss, gates, yss = [], [], [], []
    for h, xn, idx_t, gate_t, rank_t, counts in routed:
        pstart, block_e, rows_valid, next_e = _block_plan(counts.reshape(N_EXPERTS), n_blocks)
        pos = _positions(idx_t, rank_t, pstart)
        xs = _dispatch_rows(xn.reshape(n_part, ROW_CHUNKS, LANES), pos.reshape(-1) // ROW_CHUNKS,
                            n_blocks * FFN_ROWS)
        ys = _expert_ffn(block_e, rows_valid, next_e, xs.reshape(-1, LANES),
                         w_gate_up[0], b_gate_up[0], w_down[0], b_down[0])
        hs.append(h); poss.append(pos); gates.append(gate_t); yss.append(ys)
    out = _combine(hs, yss, jnp.concatenate(poss, axis=1), jnp.concatenate(gates, axis=1).T, final_norm_w)
    return out.reshape(batch, seq, D_MODEL)
```

```python
import functools

import numpy as np
import jax
import jax.numpy as jnp
from jax import lax
from jax.experimental import pallas as pl
from jax.experimental.pallas import tpu as pltpu

F32 = jnp.float32
BF16 = jnp.bfloat16
HIGHEST = lax.Precision.HIGHEST

D_MODEL = 1024
ATTN_HEAD_DIM = 64
ATTN_WIDTH = 512
ATTN_HEADS = ATTN_WIDTH // ATTN_HEAD_DIM
DILATION_PATTERNS = ((128, 1), (512, 4), (2048, 16))
ATTN_BLOCK = 128
REL_BUCKETS = 32
REL_MAX_DISTANCE = 2048
HGRN_HEAD_DIM = 128
HGRN_WIDTH = 512
HGRN_HEADS = HGRN_WIDTH // HGRN_HEAD_DIM
HGRN_CHUNK = 64
HGRN_SUB = 8
N_EXPERTS = 32
TOP_K = 4
D_EXPERT = 1024
SWIGLU_LIMIT = 7.0
SWIGLU_ALPHA = 1.702
RMS_EPS = 1e-5
LOG2_E = 1.4426950408889634
IN_PROJ_WIDTH = 3 * ATTN_WIDTH + 4 * HGRN_WIDTH
LANES = 128
HEAD_PAIRS = ATTN_WIDTH // LANES
ROW_CHUNKS = D_MODEL // LANES

INPROJ_ROWS = 512
OUTPROJ_ROWS = 1024
RANK_SPAN = 256
FFN_ROWS = 512
FFN_COLS = 512
COMBINE_ROWS = 256
HGRN_SEQ_BLOCK = 1024
ATTN_UNROLL = 8
MOVE_TOKENS = 2048
MOVE_UNROLL = 8
VMEM_LIMIT = 48 * 1024 * 1024

NT_DIMS = (((1,), (1,)), ((), ()))
TN_DIMS = (((0,), (0,)), ((), ()))


def _cparams(*sem):
    return pltpu.CompilerParams(dimension_semantics=sem, vmem_limit_bytes=VMEM_LIMIT)


def _store_row_tiles(ref, value):
    rows = value.shape[0]
    for c in range(ROW_CHUNKS):
        ref[pl.ds(c, rows, stride=ROW_CHUNKS), :] = value[:, c * LANES:(c + 1) * LANES]


def _load_row_tiles(ref, rows):
    return jnp.concatenate([ref[pl.ds(c, rows, stride=ROW_CHUNKS), :] for c in range(ROW_CHUNKS)], axis=1)


def _inproj_kernel(x_ref, nw_ref, w_ref, qa_ref, ka_ref, va_ref, qh_ref, fh_ref, ih_ref, gh_ref):
    x = x_ref[...]
    xn = x * lax.rsqrt(jnp.mean(x * x, axis=-1, keepdims=True) + RMS_EPS) * nw_ref[...]
    xn = xn.astype(BF16)
    for c, ref in enumerate((qa_ref, ka_ref, va_ref, qh_ref, fh_ref, ih_ref, gh_ref)):
        y = jnp.dot(xn, w_ref[:, c * ATTN_WIDTH:(c + 1) * ATTN_WIDTH], preferred_element_type=F32)
        ref[...] = y.astype(ref.dtype)


def _inproj(x2, norm_w, w_in):
    n = x2.shape[0]
    tm = INPROJ_ROWS
    row = lambda width: pl.BlockSpec((tm, width), lambda i: (i, 0))
    dtypes = (F32, F32, F32, BF16, F32, BF16, F32)
    return pl.pallas_call(
        _inproj_kernel,
        grid=(n // tm,),
        in_specs=[row(D_MODEL),
                  pl.BlockSpec((1, D_MODEL), lambda i: (0, 0)),
                  pl.BlockSpec((D_MODEL, IN_PROJ_WIDTH), lambda i: (0, 0))],
        out_specs=[row(ATTN_WIDTH)] * 7,
        out_shape=[jax.ShapeDtypeStruct((n, ATTN_WIDTH), dt) for dt in dtypes],
        compiler_params=_cparams("arbitrary"),
        name="inproj",
    )(x2, norm_w.reshape(1, D_MODEL), w_in.astype(BF16))


def _t5_causal_bucket(dist):
    n = np.maximum(dist, 0)
    max_exact = REL_BUCKETS // 2
    large = max_exact + (np.log(np.maximum(n, 1) / max_exact)
                         / np.log(REL_MAX_DISTANCE / max_exact)
                         * (REL_BUCKETS - max_exact)).astype(np.int32)
    large = np.minimum(large, REL_BUCKETS - 1)
    return np.where(n < max_exact, n, large).astype(np.int32)


def _bucket_thresholds(max_dist):
    buckets = _t5_causal_bucket(np.arange(max_dist + 1))
    assert np.all(np.diff(buckets) >= 0)
    return [(b, int(np.argmax(buckets >= b))) for b in range(1, REL_BUCKETS) if np.any(buckets >= b)]


def _bias_kernel(rel_ref, o_ref):
    blk = ATTN_BLOCK
    h = pl.program_id(0)
    qi = lax.broadcasted_iota(jnp.int32, (blk, 2 * blk), 0)
    kj = lax.broadcasted_iota(jnp.int32, (blk, 2 * blk), 1)
    dist_sub = blk + qi - kj
    for pi, (window, dilation) in enumerate(DILATION_PATTERNS):
        span = window // dilation
        dist = dist_sub * dilation
        val = jnp.full((blk, 2 * blk), rel_ref[h], F32)
        for b, first_dist in _bucket_thresholds(window):
            val = jnp.where(dist >= first_dist, rel_ref[b * ATTN_HEADS + h], val)
        o_ref[pi, 0] = jnp.where((dist_sub >= 0) & (dist_sub <= span), val * LOG2_E, -1e30)


def _masked_bias(rel_bias):
    blk = ATTN_BLOCK
    npat = len(DILATION_PATTERNS)
    bias = pl.pallas_call(
        _bias_kernel,
        grid=(ATTN_HEADS,),
        in_specs=[pl.BlockSpec(memory_space=pltpu.SMEM)],
        out_specs=pl.BlockSpec((npat, 1, blk, 2 * blk), lambda h: (0, h, 0, 0)),
        out_shape=jax.ShapeDtypeStruct((npat, ATTN_HEADS, blk, 2 * blk), F32),
        compiler_params=_cparams("arbitrary"),
        name="rel_bias",
    )(rel_bias.astype(F32).reshape(REL_BUCKETS * ATTN_HEADS))
    return bias.reshape(npat, HEAD_PAIRS, 2 * blk, 2 * blk)


def _attn_kernel(q_ref, k_ref, v_ref, bias_ref, o_ref, acc_ref, lse_ref, *, seq):
    blk = ATTN_BLOCK
    lane = lax.broadcasted_iota(jnp.int32, (1, LANES), 1)
    first = lane < ATTN_HEAD_DIM
    scale = ATTN_HEAD_DIM ** -0.5 * LOG2_E
    sel0 = jnp.where(first, scale, 0.0)
    sel1 = jnp.where(first, 0.0, scale)
    ones_cols = jnp.ones((2 * blk, LANES), BF16)

    def one_block(pi, dil, q_start, k_start, n_keys):
        qb = q_ref[0, pl.ds(q_start, blk, stride=dil), :]
        q2 = jnp.concatenate([qb * sel0, qb * sel1], axis=0).astype(BF16)
        kk = k_ref[0, pl.ds(k_start, n_keys, stride=dil), :].astype(BF16)
        vv = v_ref[0, pl.ds(k_start, n_keys, stride=dil), :].astype(BF16)
        s = lax.dot_general(q2, kk, NT_DIMS, preferred_element_type=F32)
        s = s + bias_ref[pi, 0, :, 2 * blk - n_keys:]
        m = jnp.max(s, axis=-1, keepdims=True)
        p = jnp.exp2(s - m).astype(BF16)
        ov = jnp.dot(p, jnp.concatenate([vv, ones_cols[:n_keys]], axis=1), preferred_element_type=F32)
        z = ov[:, LANES:]
        o = ov[:, :LANES] / z
        lse = m + jnp.log2(z)
        rows = pl.ds(q_start, blk, stride=dil)
        acc_ref[pi, rows, :] = jnp.where(first, o[:blk], o[blk:])
        lse_ref[pi, rows, :] = jnp.where(first, lse[:blk], lse[blk:])

    for pi, (_, dil) in enumerate(DILATION_PATTERNS):
        nb = seq // (dil * blk)

        def residue(r, carry, pi=pi, dil=dil, nb=nb):
            one_block(pi, dil, r, r, blk)

            def body(j, c):
                q_start = r + j * (blk * dil)
                one_block(pi, dil, q_start, q_start - blk * dil, 2 * blk)
                return c

            return lax.fori_loop(1, nb, body, carry, unroll=ATTN_UNROLL)

        lax.fori_loop(0, dil, residue, 0, unroll=ATTN_UNROLL if nb == 2 else 1)

    def merge(i, carry):
        rows = pl.ds(pl.multiple_of(i * blk, blk), blk)
        ls = [lse_ref[pi, rows, :] for pi in range(len(DILATION_PATTERNS))]
        m = functools.reduce(jnp.maximum, ls)
        es = [jnp.exp2(l - m) for l in ls]
        num = sum(e * acc_ref[pi, rows, :] for pi, e in enumerate(es))
        o_ref[0, rows, :] = (num / sum(es)).astype(o_ref.dtype)
        return carry

    lax.fori_loop(0, seq // blk, merge, 0)


def _dilated_attention(qa, ka, va, bias, batch, seq):
    npat = len(DILATION_PATTERNS)
    shp = (batch, seq, ATTN_WIDTH)
    spec = pl.BlockSpec((1, seq, LANES), lambda b, p: (b, 0, p))
    out = pl.pallas_call(
        functools.partial(_attn_kernel, seq=seq),
        grid=(batch, HEAD_PAIRS),
        in_specs=[spec, spec, spec,
                  pl.BlockSpec((npat, 1, 2 * ATTN_BLOCK, 2 * ATTN_BLOCK), lambda b, p: (0, p, 0, 0))],
        out_specs=spec,
        out_shape=jax.ShapeDtypeStruct(shp, BF16),
        scratch_shapes=[pltpu.VMEM((npat, seq, LANES), F32), pltpu.VMEM((npat, seq, LANES), F32)],
        compiler_params=_cparams("arbitrary", "arbitrary"),
        name="dilated_attn",
    )(qa.reshape(shp), ka.reshape(shp), va.reshape(shp), bias)
    return out.reshape(batch * seq, ATTN_WIDTH)


def _hgrn_kernel(q_ref, f_ref, i_ref, g_ref, lbl_ref, nw_ref, o_ref, st_ref, *, n_chunks):
    c, sub, dk = HGRN_CHUNK, HGRN_SUB, HGRN_HEAD_DIM
    nsub = c // sub
    lbl = lbl_ref[...]
    e = jnp.exp(lbl - jnp.max(lbl, axis=0, keepdims=True))
    lb_all = e[0:1] / jnp.sum(e, axis=0, keepdims=True)
    nw = nw_ref[...]

    r64 = lax.broadcasted_iota(jnp.int32, (c, c), 0)
    c64 = lax.broadcasted_iota(jnp.int32, (c, c), 1)
    sub_end = (r64 // sub) * sub + (sub - 1)
    decay_sums = jnp.concatenate([(c64 <= r64), (c64 > r64) & (c64 <= sub_end), (c64 > r64)],
                                 axis=0).astype(BF16)
    col_sub = c64 // sub
    col_in_sub = c64 - (r64 // sub) * sub
    t_iota = lax.broadcasted_iota(jnp.int32, (nsub, sub, dk), 1)
    ones_rhs = jnp.ones((dk, c), BF16)

    @pl.when(pl.program_id(1) == 0)
    def _():
        st_ref[...] = jnp.zeros_like(st_ref)

    def cols_of(hd):
        return slice(hd * dk, (hd + 1) * dk)

    def gates(rows, hd):
        cols = cols_of(hd)
        lb = lb_all[:, cols]
        q = q_ref[0, rows, cols].astype(F32)
        f = lb + (1.0 - lb) * jax.nn.sigmoid(f_ref[0, rows, cols])
        log_f = jnp.log(f)
        log_hi = log_f.astype(BF16)
        log_lo = (log_f - log_hi.astype(F32)).astype(BF16)
        sums = (jnp.dot(decay_sums, log_hi, preferred_element_type=F32)
                + jnp.dot(decay_sums, log_lo, preferred_element_type=F32))
        return sums, 1.0 - f, q * jax.nn.sigmoid(q) * (dk ** -0.5)

    def products(rows, hd, sums, key, qf):
        cum, to_sub_end, to_chunk_end = sums[:c], sums[c:2 * c], sums[2 * c:]
        v = i_ref[0, rows, cols_of(hd)]
        st = st_ref[hd]
        o = lax.dot_general((qf * jnp.exp(cum)).astype(BF16), st.astype(BF16), NT_DIMS,
                            preferred_element_type=F32)
        kd = (key * jnp.exp(to_chunk_end)).astype(BF16)
        st_ref[hd] = (st * jnp.exp(cum[c - 1:c, :])
                      + lax.dot_general(v, kd, TN_DIMS, preferred_element_type=F32))

        cum3 = cum.reshape(nsub, sub, dk)
        key3 = key.reshape(nsub, sub, dk)
        qf3 = qf.reshape(nsub, sub, dk)
        khat = (key * jnp.exp(to_sub_end)).astype(BF16)
        qs = []
        for j in range(nsub - 1):
            t0 = sub * (j + 1)
            qs.append(qf[t0:] * jnp.exp(cum[t0:] - cum[t0 - 1:t0, :]))
        prod = lax.dot_general(jnp.concatenate(qs, axis=0).astype(BF16), khat, NT_DIMS,
                               preferred_element_type=F32)

        ws = []
        for s in range(sub):
            dec = jnp.exp(jnp.where(t_iota >= s, cum3 - cum3[:, s:s + 1, :], -jnp.inf))
            ws.append((qf3 * key3[:, s:s + 1, :] * dec).reshape(c, dk))
        pair_sums = jnp.dot(jnp.concatenate(ws, axis=0).astype(BF16), ones_rhs,
                            preferred_element_type=F32)
        return o, prod, pair_sums

    def finish(rows, hd, o, prod, pair_sums):
        cols = cols_of(hd)
        a = jnp.zeros((c, c), F32)
        r = 0
        for j in range(nsub - 1):
            t0 = sub * (j + 1)
            pj = jnp.concatenate([jnp.zeros((t0, c), F32), prod[r:r + c - t0, :]], axis=0)
            a = jnp.where(col_sub == j, pj, a)
            r += c - t0
        for s in range(sub):
            a = jnp.where(col_in_sub == s, pair_sums[s * c:(s + 1) * c, :], a)
        o = o + jnp.dot(a.astype(BF16), i_ref[0, rows, cols], preferred_element_type=F32)
        on = o * lax.rsqrt(jnp.mean(o * o, axis=-1, keepdims=True) + RMS_EPS) * nw
        g = g_ref[0, rows, cols]
        o_ref[0, rows, cols] = (on * (g * jax.nn.sigmoid(g))).astype(BF16)

    def body(ci, carry):
        rows = pl.ds(pl.multiple_of(ci * c, c), c)
        heads = range(HGRN_HEADS)
        stage1 = [gates(rows, hd) for hd in heads]
        stage2 = [products(rows, hd, *stage1[hd]) for hd in heads]
        for hd in heads:
            finish(rows, hd, *stage2[hd])
        return carry

    lax.fori_loop(0, n_chunks, body, 0, unroll=2)


def _hgrn(qh, fh, ih, gh, lb_logits, norm_w, batch, seq):
    sb = min(seq, HGRN_SEQ_BLOCK)
    spec = pl.BlockSpec((1, sb, HGRN_WIDTH), lambda b, s: (b, s, 0))
    slots = lb_logits.shape[0]
    shp = (batch, seq, HGRN_WIDTH)
    out = pl.pallas_call(
        functools.partial(_hgrn_kernel, n_chunks=sb // HGRN_CHUNK),
        grid=(batch, seq // sb),
        in_specs=[spec, spec, spec, spec,
                  pl.BlockSpec((slots, HGRN_WIDTH), lambda b, s: (0, 0)),
                  pl.BlockSpec((1, HGRN_HEAD_DIM), lambda b, s: (0, 0))],
        out_specs=spec,
        out_shape=jax.ShapeDtypeStruct(shp, BF16),
        scratch_shapes=[pltpu.VMEM((HGRN_HEADS, HGRN_HEAD_DIM, HGRN_HEAD_DIM), F32)],
        compiler_params=_cparams("arbitrary", "arbitrary"),
        name="hgrn2",
    )(qh.reshape(shp), fh.reshape(shp), ih.reshape(shp), gh.reshape(shp),
      lb_logits.astype(F32), norm_w.reshape(1, HGRN_HEAD_DIM))
    return out.reshape(batch * seq, HGRN_WIDTH)


def _outproj_kernel(attn_ref, hg_ref, x_ref, w_ref, nw_ref, rw_ref, rb_ref,
                    h_ref, xn_ref, idx_ref, gate_ref, rank_ref, cnt_ref, run_ref):
    y = jnp.dot(attn_ref[...], w_ref[0:ATTN_WIDTH, :], preferred_element_type=F32)
    y = y + jnp.dot(hg_ref[...], w_ref[ATTN_WIDTH:, :], preferred_element_type=F32)
    h = x_ref[...] + y
    h_ref[...] = h
    xn = h * lax.rsqrt(jnp.mean(h * h, axis=-1, keepdims=True) + RMS_EPS) * nw_ref[...]
    _store_row_tiles(xn_ref, xn)

    logits = lax.dot_general(rw_ref[...], xn, NT_DIMS, precision=HIGHEST,
                             preferred_element_type=F32) + rb_ref[...]
    eid = lax.broadcasted_iota(jnp.int32, logits.shape, 0)
    vals, idxs = [], []
    for _ in range(TOP_K):
        mx = jnp.max(logits, axis=0, keepdims=True)
        ix = jnp.min(jnp.where(logits == mx, eid, N_EXPERTS), axis=0, keepdims=True)
        vals.append(mx)
        idxs.append(ix)
        logits = jnp.where(eid == ix, -jnp.inf, logits)
    es = [jnp.exp(v - vals[0]) for v in vals]
    den = es[0] + es[1] + es[2] + es[3]
    idx_ref[...] = jnp.concatenate(idxs, axis=0)
    gate_ref[...] = jnp.concatenate([e / den for e in es], axis=0)

    @pl.when(pl.program_id(0) == 0)
    def _():
        run_ref[...] = jnp.zeros_like(run_ref)

    span = RANK_SPAN
    onehots = [(eid == ix).astype(F32) for ix in idxs]
    routed = onehots[0] + onehots[1] + onehots[2] + onehots[3]
    earlier = (lax.broadcasted_iota(jnp.int32, (span, span), 0)
               < lax.broadcasted_iota(jnp.int32, (span, span), 1)).astype(BF16)
    run = run_ref[...]
    ranks = []
    for c0 in range(0, logits.shape[1], span):
        part = routed[:, c0:c0 + span]
        before = jnp.dot(part.astype(BF16), earlier, preferred_element_type=F32) + run
        ranks.append(jnp.concatenate([jnp.sum(o[:, c0:c0 + span] * before, axis=0, keepdims=True)
                                      for o in onehots], axis=0))
        run = run + jnp.sum(part, axis=1, keepdims=True)
    rank_ref[...] = jnp.concatenate(ranks, axis=1).astype(jnp.int32)
    run_ref[...] = run
    cnt_ref[...] = run.astype(jnp.int32)


def _outproj(attn, hg, x2, w_out, norm_w, router_w, router_b):
    n = x2.shape[0]
    tm = OUTPROJ_ROWS
    row = lambda width: pl.BlockSpec((tm, width), lambda i: (i, 0))
    full = lambda a, b: pl.BlockSpec((a, b), lambda i: (0, 0))
    tok = pl.BlockSpec((TOP_K, tm), lambda i: (0, i))
    return pl.pallas_call(
        _outproj_kernel,
        grid=(n // tm,),
        in_specs=[row(ATTN_WIDTH), row(HGRN_WIDTH), row(D_MODEL), full(D_MODEL, D_MODEL),
                  full(1, D_MODEL), full(N_EXPERTS, D_MODEL), full(N_EXPERTS, 1)],
        out_specs=[row(D_MODEL), pl.BlockSpec((tm * ROW_CHUNKS, LANES), lambda i: (i, 0)),
                   tok, tok, tok, full(N_EXPERTS, 1)],
        out_shape=[jax.ShapeDtypeStruct((n, D_MODEL), F32),
                   jax.ShapeDtypeStruct((n * ROW_CHUNKS, LANES), F32),
                   jax.ShapeDtypeStruct((TOP_K, n), jnp.int32),
                   jax.ShapeDtypeStruct((TOP_K, n), F32),
                   jax.ShapeDtypeStruct((TOP_K, n), jnp.int32),
                   jax.ShapeDtypeStruct((N_EXPERTS, 1), jnp.int32)],
        scratch_shapes=[pltpu.VMEM((N_EXPERTS, 1), F32)],
        compiler_params=_cparams("arbitrary"),
        name="outproj_router",
    )(attn, hg, x2, w_out.astype(BF16), norm_w.reshape(1, D_MODEL),
      router_w.T.astype(F32), router_b.astype(F32).reshape(N_EXPERTS, 1))


def _ffn_kernel(be_ref, rv_ref, nx_ref, xs_ref, wgu_hbm, bgu_ref, wdn_hbm, bdn_ref, o_ref,
                wgu_f32, wdn_f32, wgu_bf, wdn_bf, sems):
    i = pl.program_id(0)
    rows_valid = rv_ref[i]
    new_expert = (i == 0) | (be_ref[i] != be_ref[jnp.maximum(i - 1, 0)])

    def weight_copies(e):
        return (pltpu.make_async_copy(wgu_hbm.at[e], wgu_f32, sems.at[0]),
                pltpu.make_async_copy(wdn_hbm.at[e], wdn_f32, sems.at[1]))

    @pl.when(i == 0)
    def _():
        for cp in weight_copies(be_ref[0]):
            cp.start()

    @pl.when(new_expert & (rows_valid > 0))
    def _():
        for cp in weight_copies(be_ref[i]):
            cp.wait()
        step = 128
        for r in range(0, D_MODEL, step):
            wgu_bf[r:r + step, :] = wgu_f32[r:r + step, :].astype(BF16)
        for r in range(0, D_EXPERT, step):
            wdn_bf[r:r + step, :] = wdn_f32[r:r + step, :].astype(BF16)

        @pl.when(nx_ref[i] >= 0)
        def _():
            for cp in weight_copies(nx_ref[i]):
                cp.start()

    @pl.when(rows_valid > 0)
    def _():
        live = lax.broadcasted_iota(jnp.int32, (FFN_ROWS, 1), 0) < rows_valid
        x = jnp.where(live, _load_row_tiles(xs_ref, FFN_ROWS), 0.0).astype(BF16)
        out = bdn_ref[0]
        for c0 in range(0, D_EXPERT, FFN_COLS):
            g_cols = slice(c0, c0 + FFN_COLS)
            u_cols = slice(D_EXPERT + c0, D_EXPERT + c0 + FFN_COLS)
            gate = jnp.dot(x, wgu_bf[:, g_cols], preferred_element_type=F32) + bgu_ref[0, :, g_cols]
            up = jnp.dot(x, wgu_bf[:, u_cols], preferred_element_type=F32) + bgu_ref[0, :, u_cols]
            gate = jnp.minimum(gate, SWIGLU_LIMIT)
            up = jnp.clip(up, -SWIGLU_LIMIT, SWIGLU_LIMIT)
            act = gate * jax.nn.sigmoid(SWIGLU_ALPHA * gate) * (up + 1.0)
            out = out + jnp.dot(act.astype(BF16), wdn_bf[g_cols, :], preferred_element_type=F32)
        _store_row_tiles(o_ref, out)

    @pl.when(rows_valid <= 0)
    def _():
        o_ref[...] = jnp.zeros_like(o_ref)


def _expert_ffn(block_e, rows_valid, next_e, xs, w_gu, b_gu, w_dn, b_dn):
    tm = FFN_ROWS
    tiles = pl.BlockSpec((tm * ROW_CHUNKS, LANES), lambda i, be, rv, nx: (i, 0))
    grid_spec = pltpu.PrefetchScalarGridSpec(
        num_scalar_prefetch=3,
        grid=(xs.shape[0] // (tm * ROW_CHUNKS),),
        in_specs=[tiles,
                  pl.BlockSpec(memory_space=pl.ANY),
                  pl.BlockSpec((1, 1, 2 * D_EXPERT), lambda i, be, rv, nx: (be[i], 0, 0)),
                  pl.BlockSpec(memory_space=pl.ANY),
                  pl.BlockSpec((1, 1, D_MODEL), lambda i, be, rv, nx: (be[i], 0, 0))],
        out_specs=tiles,
        scratch_shapes=[pltpu.VMEM((D_MODEL, 2 * D_EXPERT), F32), pltpu.VMEM((D_EXPERT, D_MODEL), F32),
                        pltpu.VMEM((D_MODEL, 2 * D_EXPERT), BF16), pltpu.VMEM((D_EXPERT, D_MODEL), BF16),
                        pltpu.SemaphoreType.DMA((2,))],
    )
    return pl.pallas_call(
        _ffn_kernel,
        grid_spec=grid_spec,
        out_shape=jax.ShapeDtypeStruct(xs.shape, F32),
        compiler_params=_cparams("arbitrary"),
        name="expert_ffn",
    )(block_e, rows_valid, next_e, xs, w_gu, b_gu.reshape(N_EXPERTS, 1, 2 * D_EXPERT),
      w_dn, b_dn.reshape(N_EXPERTS, 1, D_MODEL))


def _pos_kernel(idx_ref, rank_ref, pstart_ref, move_ref, comb_ref):
    eid = lax.broadcasted_iota(jnp.int32, (N_EXPERTS, idx_ref.shape[1]), 0)
    pstart = pstart_ref[...]
    rows = [jnp.sum(jnp.where(eid == idx_ref[k:k + 1, :], pstart, 0), axis=0, keepdims=True)
            for k in range(TOP_K)]
    pos = (jnp.concatenate(rows, axis=0) + rank_ref[...]) * ROW_CHUNKS
    for k in range(TOP_K):
        move_ref[0, :, k * MOVE_TOKENS:(k + 1) * MOVE_TOKENS] = pos[k:k + 1, :]
        for j in range(MOVE_TOKENS // COMBINE_ROWS):
            comb_ref[j, :, k * COMBINE_ROWS:(k + 1) * COMBINE_ROWS] = pos[k:k + 1, j * COMBINE_ROWS:(j + 1) * COMBINE_ROWS]


def _positions(idx_t, rank_t, pstart):
    n = idx_t.shape[1]
    tl = MOVE_TOKENS
    per = tl // COMBINE_ROWS
    tok = pl.BlockSpec((TOP_K, tl), lambda i: (0, i))
    return pl.pallas_call(
        _pos_kernel,
        grid=(n // tl,),
        in_specs=[tok, tok, pl.BlockSpec((N_EXPERTS, 1), lambda i: (0, 0))],
        out_specs=[pl.BlockSpec((1, 1, TOP_K * tl), lambda i: (i, 0, 0)),
                   pl.BlockSpec((per, 1, TOP_K * COMBINE_ROWS), lambda i: (i, 0, 0))],
        out_shape=[jax.ShapeDtypeStruct((n // tl, 1, TOP_K * tl), jnp.int32),
                   jax.ShapeDtypeStruct((n // COMBINE_ROWS, 1, TOP_K * COMBINE_ROWS), jnp.int32)],
        compiler_params=_cparams("arbitrary"),
        name="dispatch_pos",
    )(idx_t, rank_t, pstart.reshape(N_EXPERTS, 1))


def _row_tile(ref, row_offset):
    return ref.at[pl.ds(pl.multiple_of(row_offset, ROW_CHUNKS), ROW_CHUNKS)]


def _dispatch_kernel(pos_ref, x_ref, xs_ref, sem):
    tokens = x_ref.shape[0] // ROW_CHUNKS

    def copy(t, k):
        return pltpu.make_async_copy(_row_tile(x_ref, t * ROW_CHUNKS),
                                     _row_tile(xs_ref, pos_ref[0, 0, k * tokens + t]), sem)

    def issue(t, carry):
        for k in range(TOP_K):
            copy(t, k).start(priority=k % 2)
        return carry

    def drain(t, carry):
        for k in range(TOP_K):
            copy(t, k).wait()
        return carry

    lax.fori_loop(0, tokens, issue, 0, unroll=MOVE_UNROLL)
    lax.fori_loop(0, tokens, drain, 0, unroll=MOVE_UNROLL)


def _dispatch_rows(pos_tiles, xn, out_rows):
    n = xn.shape[0] // ROW_CHUNKS
    tl = MOVE_TOKENS
    return pl.pallas_call(
        _dispatch_kernel,
        grid=(n // tl,),
        in_specs=[pl.BlockSpec((1, 1, tl * TOP_K), lambda i: (i, 0, 0), memory_space=pltpu.SMEM),
                  pl.BlockSpec((tl * ROW_CHUNKS, LANES), lambda i: (i, 0))],
        out_specs=pl.BlockSpec(memory_space=pl.ANY),
        out_shape=jax.ShapeDtypeStruct((out_rows * ROW_CHUNKS, LANES), xn.dtype),
        scratch_shapes=[pltpu.SemaphoreType.DMA(())],
        compiler_params=_cparams("arbitrary"),
        name="dispatch_rows",
    )(pos_tiles, xn)


def _combine_kernel(pos_ref, pos_next_ref, h_ref, g_ref, fw_ref, ys_ref, o_ref, buf, sems):
    i = pl.program_id(0)
    steps = pl.num_programs(0)
    tokens = h_ref.shape[0]
    slot = lax.rem(i, 2)

    def gather(p_ref, s, start):
        def body(t, carry):
            for k in range(TOP_K):
                cp = pltpu.make_async_copy(_row_tile(ys_ref, p_ref[0, 0, k * tokens + t]),
                                           _row_tile(buf.at[s, k], t * ROW_CHUNKS), sems.at[s])
                cp.start(priority=k % 2) if start else cp.wait()
            return carry
        lax.fori_loop(0, tokens, body, 0, unroll=MOVE_UNROLL)

    @pl.when(i == 0)
    def _():
        gather(pos_ref, 0, True)

    @pl.when(i + 1 < steps)
    def _():
        gather(pos_next_ref, 1 - slot, True)

    gather(pos_ref, slot, False)
    g = g_ref[...]
    y = h_ref[...]
    for k in range(TOP_K):
        y = y + _load_row_tiles(buf.at[slot, k], tokens) * g[:, k:k + 1]
    o_ref[...] = y * lax.rsqrt(jnp.mean(y * y, axis=-1, keepdims=True) + RMS_EPS) * fw_ref[...]


def _combine(h, ys, pos_t, gates_nk, final_w):
    n = h.shape[0]
    tm = COMBINE_ROWS
    steps = n // tm
    smem = lambda imap: pl.BlockSpec((1, 1, tm * TOP_K), imap, memory_space=pltpu.SMEM)
    return pl.pallas_call(
        _combine_kernel,
        grid=(steps,),
        in_specs=[smem(lambda i: (i, 0, 0)),
                  smem(lambda i: (jnp.minimum(i + 1, steps - 1), 0, 0)),
                  pl.BlockSpec((tm, D_MODEL), lambda i: (i, 0)),
                  pl.BlockSpec((tm, TOP_K), lambda i: (i, 0)),
                  pl.BlockSpec((1, D_MODEL), lambda i: (0, 0)),
                  pl.BlockSpec(memory_space=pl.ANY)],
        out_specs=pl.BlockSpec((tm, D_MODEL), lambda i: (i, 0)),
        out_shape=jax.ShapeDtypeStruct((n, D_MODEL), F32),
        scratch_shapes=[pltpu.VMEM((2, TOP_K, tm * ROW_CHUNKS, LANES), F32), pltpu.SemaphoreType.DMA((2,))],
        compiler_params=_cparams("arbitrary"),
        name="combine_norm",
    )(pos_t, pos_t, h, gates_nk, final_w.reshape(1, D_MODEL), ys)


def kernel(x, norm_mix_w, w_in, rel_bias, hgrn_lb_logits, hgrn_norm_w, w_out, norm_ffn_w,
           router_w, router_b, w_gate_up, b_gate_up, w_down, b_down, final_norm_w):
    batch, seq, _ = x.shape
    n = batch * seq
    x2 = x.reshape(n, D_MODEL)

    qa, ka, va, qh, fh, ih, gh = _inproj(x2, norm_mix_w[0], w_in[0])
    attn = _dilated_attention(qa, ka, va, _masked_bias(rel_bias), batch, seq)
    hg = _hgrn(qh, fh, ih, gh, hgrn_lb_logits, hgrn_norm_w[0], batch, seq)
    h, xn, idx_t, gate_t, rank_t, counts = _outproj(attn, hg, x2, w_out[0], norm_ffn_w[0],
                                                    router_w[0], router_b[0])

    tm = FFN_ROWS
    n_blocks = -(-(n * TOP_K) // tm) + N_EXPERTS
    counts = counts.reshape(N_EXPERTS)
    padded = (counts + tm - 1) // tm * tm
    pend = jnp.cumsum(padded)
    pstart = pend - padded
    block_row0 = jnp.arange(n_blocks, dtype=jnp.int32) * tm
    in_block = (block_row0[:, None] >= pstart[None, :]) & (block_row0[:, None] < pend[None, :])
    block_e = jnp.minimum(jnp.sum(pend[None, :] <= block_row0[:, None], axis=1), N_EXPERTS - 1).astype(jnp.int32)
    rows_valid = jnp.sum(jnp.where(in_block, jnp.clip(pstart + counts - block_row0[:, None], 0, tm), 0),
                         axis=1).astype(jnp.int32)

    pos_move, pos_comb = _positions(idx_t, rank_t, pstart.astype(jnp.int32))
    xs = _dispatch_rows(pos_move, xn, n_blocks * tm)
    experts = jnp.arange(N_EXPERTS, dtype=jnp.int32)
    later = jnp.where((experts[None, :] > experts[:, None]) & (counts[None, :] > 0), experts[None, :], N_EXPERTS)
    next_of = jnp.min(later, axis=1)
    next_e = jnp.where(next_of < N_EXPERTS, next_of, -1)[block_e].astype(jnp.int32)
    ys = _expert_ffn(block_e, rows_valid, next_e, xs, w_gate_up[0], b_gate_up[0], w_down[0], b_down[0])
    out = _combine(h, ys, pos_comb, gate_t.T, final_norm_w)
    return out.reshape(batch, seq, D_MODEL)
```

```python
import functools

import numpy as np
import jax
import jax.numpy as jnp
from jax import lax
from jax.experimental import pallas as pl
from jax.experimental.pallas import tpu as pltpu

F32 = jnp.float32
BF16 = jnp.bfloat16
HIGHEST = lax.Precision.HIGHEST

D_MODEL = 1024
ATTN_HEAD_DIM = 64
ATTN_WIDTH = 512
ATTN_HEADS = ATTN_WIDTH // ATTN_HEAD_DIM
DILATION_PATTERNS = ((128, 1), (512, 4), (2048, 16))
ATTN_BLOCK = 128
REL_BUCKETS = 32
REL_MAX_DISTANCE = 2048
HGRN_HEAD_DIM = 128
HGRN_WIDTH = 512
HGRN_HEADS = HGRN_WIDTH // HGRN_HEAD_DIM
HGRN_CHUNK = 64
HGRN_SUB = 8
N_EXPERTS = 32
TOP_K = 4
D_EXPERT = 1024
SWIGLU_LIMIT = 7.0
SWIGLU_ALPHA = 1.702
RMS_EPS = 1e-5
LOG2_E = 1.4426950408889634
IN_PROJ_WIDTH = 3 * ATTN_WIDTH + 4 * HGRN_WIDTH
LANES = 128
HEAD_PAIRS = ATTN_WIDTH // LANES
ROW_CHUNKS = D_MODEL // LANES

INPROJ_ROWS = 512
OUTPROJ_ROWS = 1024
RANK_SPAN = 256
FFN_ROWS = 512
FFN_COLS = 512
COMBINE_ROWS = 256
HGRN_SEQ_BLOCK = 1024
ATTN_UNROLL = 8
MOVE_TOKENS = 2048
MOVE_UNROLL = 8
VMEM_LIMIT = 48 * 1024 * 1024

NT_DIMS = (((1,), (1,)), ((), ()))
TN_DIMS = (((0,), (0,)), ((), ()))


def _cparams(*sem):
    return pltpu.CompilerParams(dimension_semantics=sem, vmem_limit_bytes=VMEM_LIMIT)


def _store_row_tiles(ref, value):
    rows = value.shape[0]
    for c in range(ROW_CHUNKS):
        ref[pl.ds(c, rows, stride=ROW_CHUNKS), :] = value[:, c * LANES:(c + 1) * LANES]


def _load_row_tiles(ref, rows):
    return jnp.concatenate([ref[pl.ds(c, rows, stride=ROW_CHUNKS), :] for c in range(ROW_CHUNKS)], axis=1)


def _inproj_kernel(x_ref, nw_ref, w_ref, qa_ref, ka_ref, va_ref, qh_ref, fh_ref, ih_ref, gh_ref):
    x = x_ref[...]
    xn = x * lax.rsqrt(jnp.mean(x * x, axis=-1, keepdims=True) + RMS_EPS) * nw_ref[...]
    xn = xn.astype(BF16)
    for c, ref in enumerate((qa_ref, ka_ref, va_ref, qh_ref, fh_ref, ih_ref, gh_ref)):
        y = jnp.dot(xn, w_ref[:, c * ATTN_WIDTH:(c + 1) * ATTN_WIDTH], preferred_element_type=F32)
        ref[...] = y.astype(ref.dtype)


def _inproj(x2, norm_w, w_in):
    n = x2.shape[0]
    tm = INPROJ_ROWS
    row = lambda width: pl.BlockSpec((tm, width), lambda i: (i, 0))
    dtypes = (F32, F32, F32, BF16, F32, BF16, F32)
    return pl.pallas_call(
        _inproj_kernel,
        grid=(n // tm,),
        in_specs=[row(D_MODEL),
                  pl.BlockSpec((1, D_MODEL), lambda i: (0, 0)),
                  pl.BlockSpec((D_MODEL, IN_PROJ_WIDTH), lambda i: (0, 0))],
        out_specs=[row(ATTN_WIDTH)] * 7,
        out_shape=[jax.ShapeDtypeStruct((n, ATTN_WIDTH), dt) for dt in dtypes],
        compiler_params=_cparams("arbitrary"),
        name="inproj",
    )(x2, norm_w.reshape(1, D_MODEL), w_in.astype(BF16))


def _t5_causal_bucket(dist):
    n = np.maximum(dist, 0)
    max_exact = REL_BUCKETS // 2
    large = max_exact + (np.log(np.maximum(n, 1) / max_exact)
                         / np.log(REL_MAX_DISTANCE / max_exact)
                         * (REL_BUCKETS - max_exact)).astype(np.int32)
    large = np.minimum(large, REL_BUCKETS - 1)
    return np.where(n < max_exact, n, large).astype(np.int32)


def _bucket_thresholds(max_dist):
    buckets = _t5_causal_bucket(np.arange(max_dist + 1))
    assert np.all(np.diff(buckets) >= 0)
    return [(b, int(np.argmax(buckets >= b))) for b in range(1, REL_BUCKETS) if np.any(buckets >= b)]


def _bias_kernel(rel_ref, o_ref):
    blk = ATTN_BLOCK
    h = pl.program_id(0)
    qi = lax.broadcasted_iota(jnp.int32, (blk, 2 * blk), 0)
    kj = lax.broadcasted_iota(jnp.int32, (blk, 2 * blk), 1)
    dist_sub = blk + qi - kj
    for pi, (window, dilation) in enumerate(DILATION_PATTERNS):
        span = window // dilation
        dist = dist_sub * dilation
        val = jnp.full((blk, 2 * blk), rel_ref[h], F32)
        for b, first_dist in _bucket_thresholds(window):
            val = jnp.where(dist >= first_dist, rel_ref[b * ATTN_HEADS + h], val)
        o_ref[pi, 0] = jnp.where((dist_sub >= 0) & (dist_sub <= span), val * LOG2_E, -1e30)


def _masked_bias(rel_bias):
    blk = ATTN_BLOCK
    npat = len(DILATION_PATTERNS)
    bias = pl.pallas_call(
        _bias_kernel,
        grid=(ATTN_HEADS,),
        in_specs=[pl.BlockSpec(memory_space=pltpu.SMEM)],
        out_specs=pl.BlockSpec((npat, 1, blk, 2 * blk), lambda h: (0, h, 0, 0)),
        out_shape=jax.ShapeDtypeStruct((npat, ATTN_HEADS, blk, 2 * blk), F32),
        compiler_params=_cparams("arbitrary"),
        name="rel_bias",
    )(rel_bias.astype(F32).reshape(REL_BUCKETS * ATTN_HEADS))
    return bias.reshape(npat, HEAD_PAIRS, 2 * blk, 2 * blk)


def _attn_kernel(q_ref, k_ref, v_ref, bias_ref, o_ref, acc_ref, lse_ref, *, seq):
    blk = ATTN_BLOCK
    lane = lax.broadcasted_iota(jnp.int32, (1, LANES), 1)
    first = lane < ATTN_HEAD_DIM
    scale = ATTN_HEAD_DIM ** -0.5 * LOG2_E
    sel0 = jnp.where(first, scale, 0.0)
    sel1 = jnp.where(first, 0.0, scale)
    ones_cols = jnp.ones((2 * blk, LANES), BF16)

    def one_block(pi, dil, q_start, k_start, n_keys):
        qb = q_ref[0, pl.ds(q_start, blk, stride=dil), :]
        q2 = jnp.concatenate([qb * sel0, qb * sel1], axis=0).astype(BF16)
        kk = k_ref[0, pl.ds(k_start, n_keys, stride=dil), :].astype(BF16)
        vv = v_ref[0, pl.ds(k_start, n_keys, stride=dil), :].astype(BF16)
        s = lax.dot_general(q2, kk, NT_DIMS, preferred_element_type=F32)
        s = s + bias_ref[pi, 0, :, 2 * blk - n_keys:]
        m = jnp.max(s, axis=-1, keepdims=True)
        p = jnp.exp2(s - m).astype(BF16)
        ov = jnp.dot(p, jnp.concatenate([vv, ones_cols[:n_keys]], axis=1), preferred_element_type=F32)
        z = ov[:, LANES:]
        o = ov[:, :LANES] / z
        lse = m + jnp.log2(z)
        rows = pl.ds(q_start, blk, stride=dil)
        acc_ref[pi, rows, :] = jnp.where(first, o[:blk], o[blk:])
        lse_ref[pi, rows, :] = jnp.where(first, lse[:blk], lse[blk:])

    for pi, (_, dil) in enumerate(DILATION_PATTERNS):
        nb = seq // (dil * blk)

        def residue(r, carry, pi=pi, dil=dil, nb=nb):
            one_block(pi, dil, r, r, blk)

            def body(j, c):
                q_start = r + j * (blk * dil)
                one_block(pi, dil, q_start, q_start - blk * dil, 2 * blk)
                return c

            return lax.fori_loop(1, nb, body, carry, unroll=ATTN_UNROLL)

        lax.fori_loop(0, dil, residue, 0, unroll=ATTN_UNROLL if nb == 2 else 1)

    def merge(i, carry):
        rows = pl.ds(pl.multiple_of(i * blk, blk), blk)
        ls = [lse_ref[pi, rows, :] for pi in range(len(DILATION_PATTERNS))]
        m = functools.reduce(jnp.maximum, ls)
        es = [jnp.exp2(l - m) for l in ls]
        num = sum(e * acc_ref[pi, rows, :] for pi, e in enumerate(es))
        o_ref[0, rows, :] = (num / sum(es)).astype(o_ref.dtype)
        return carry

    lax.fori_loop(0, seq // blk, merge, 0)


def _dilated_attention(qa, ka, va, bias, batch, seq):
    npat = len(DILATION_PATTERNS)
    shp = (batch, seq, ATTN_WIDTH)
    spec = pl.BlockSpec((1, seq, LANES), lambda b, p: (b, 0, p))
    out = pl.pallas_call(
        functools.partial(_attn_kernel, seq=seq),
        grid=(batch, HEAD_PAIRS),
        in_specs=[spec, spec, spec,
                  pl.BlockSpec((npat, 1, 2 * ATTN_BLOCK, 2 * ATTN_BLOCK), lambda b, p: (0, p, 0, 0))],
        out_specs=spec,
        out_shape=jax.ShapeDtypeStruct(shp, BF16),
        scratch_shapes=[pltpu.VMEM((npat, seq, LANES), F32), pltpu.VMEM((npat, seq, LANES), F32)],
        compiler_params=_cparams("arbitrary", "arbitrary"),
        name="dilated_attn",
    )(qa.reshape(shp), ka.reshape(shp), va.reshape(shp), bias)
    return out.reshape(batch * seq, ATTN_WIDTH)


def _hgrn_kernel(q_ref, f_ref, i_ref, g_ref, lbl_ref, nw_ref, o_ref, st_ref, *, n_chunks):
    c, sub, dk = HGRN_CHUNK, HGRN_SUB, HGRN_HEAD_DIM
    nsub = c // sub
    lbl = lbl_ref[...]
    e = jnp.exp(lbl - jnp.max(lbl, axis=0, keepdims=True))
    lb_all = e[0:1] / jnp.sum(e, axis=0, keepdims=True)
    nw = nw_ref[...]

    r64 = lax.broadcasted_iota(jnp.int32, (c, c), 0)
    c64 = lax.broadcasted_iota(jnp.int32, (c, c), 1)
    sub_end = (r64 // sub) * sub + (sub - 1)
    decay_sums = jnp.concatenate([(c64 <= r64), (c64 > r64) & (c64 <= sub_end), (c64 > r64)],
                                 axis=0).astype(BF16)
    col_sub = c64 // sub
    col_in_sub = c64 - (r64 // sub) * sub
    t_iota = lax.broadcasted_iota(jnp.int32, (nsub, sub, dk), 1)
    ones_rhs = jnp.ones((dk, c), BF16)

    @pl.when(pl.program_id(1) == 0)
    def _():
        st_ref[...] = jnp.zeros_like(st_ref)

    def cols_of(hd):
        return slice(hd * dk, (hd + 1) * dk)

    def gates(rows, hd):
        cols = cols_of(hd)
        lb = lb_all[:, cols]
        q = q_ref[0, rows, cols].astype(F32)
        f = lb + (1.0 - lb) * jax.nn.sigmoid(f_ref[0, rows, cols])
        log_f = jnp.log2(f)
        log_hi = log_f.astype(BF16)
        log_lo = (log_f - log_hi.astype(F32)).astype(BF16)
        sums = (jnp.dot(decay_sums, log_hi, preferred_element_type=F32)
                + jnp.dot(decay_sums, log_lo, preferred_element_type=F32))
        return sums, 1.0 - f, q * jax.nn.sigmoid(q) * (dk ** -0.5)

    def products(rows, hd, sums, key, qf):
        cum, to_sub_end, to_chunk_end = sums[:c], sums[c:2 * c], sums[2 * c:]
        v = i_ref[0, rows, cols_of(hd)]
        st = st_ref[hd]
        o = lax.dot_general((qf * jnp.exp2(cum)).astype(BF16), st.astype(BF16), NT_DIMS,
                            preferred_element_type=F32)
        kd = (key * jnp.exp2(to_chunk_end)).astype(BF16)
        st_ref[hd] = (st * jnp.exp2(cum[c - 1:c, :])
                      + lax.dot_general(v, kd, TN_DIMS, preferred_element_type=F32))

        cum3 = cum.reshape(nsub, sub, dk)
        key3 = key.reshape(nsub, sub, dk)
        qf3 = qf.reshape(nsub, sub, dk)
        khat = (key * jnp.exp2(to_sub_end)).astype(BF16)
        qs = []
        for j in range(nsub - 1):
            t0 = sub * (j + 1)
            qs.append(qf[t0:] * jnp.exp2(cum[t0:] - cum[t0 - 1:t0, :]))
        prod = lax.dot_general(jnp.concatenate(qs, axis=0).astype(BF16), khat, NT_DIMS,
                               preferred_element_type=F32)

        ws = []
        for s in range(sub):
            dec = jnp.exp2(jnp.where(t_iota >= s, cum3 - cum3[:, s:s + 1, :], -jnp.inf))
            ws.append((qf3 * key3[:, s:s + 1, :] * dec).reshape(c, dk))
        pair_sums = jnp.dot(jnp.concatenate(ws, axis=0).astype(BF16), ones_rhs,
                            preferred_element_type=F32)
        return o, prod, pair_sums

    def finish(rows, hd, o, prod, pair_sums):
        cols = cols_of(hd)
        a = jnp.zeros((c, c), F32)
        r = 0
        for j in range(nsub - 1):
            t0 = sub * (j + 1)
            pj = jnp.concatenate([jnp.zeros((t0, c), F32), prod[r:r + c - t0, :]], axis=0)
            a = jnp.where(col_sub == j, pj, a)
            r += c - t0
        for s in range(sub):
            a = jnp.where(col_in_sub == s, pair_sums[s * c:(s + 1) * c, :], a)
        o = o + jnp.dot(a.astype(BF16), i_ref[0, rows, cols], preferred_element_type=F32)
        on = o * lax.rsqrt(jnp.mean(o * o, axis=-1, keepdims=True) + RMS_EPS) * nw
        g = g_ref[0, rows, cols]
        o_ref[0, rows, cols] = (on * (g * jax.nn.sigmoid(g))).astype(BF16)

    def body(ci, carry):
        rows = pl.ds(pl.multiple_of(ci * c, c), c)
        heads = range(HGRN_HEADS)
        stage1 = [gates(rows, hd) for hd in heads]
        stage2 = [products(rows, hd, *stage1[hd]) for hd in heads]
        for hd in heads:
            finish(rows, hd, *stage2[hd])
        return carry

    lax.fori_loop(0, n_chunks, body, 0, unroll=2)


def _hgrn(qh, fh, ih, gh, lb_logits, norm_w, batch, seq):
    sb = min(seq, HGRN_SEQ_BLOCK)
    spec = pl.BlockSpec((1, sb, HGRN_WIDTH), lambda b, s: (b, s, 0))
    slots = lb_logits.shape[0]
    shp = (batch, seq, HGRN_WIDTH)
    out = pl.pallas_call(
        functools.partial(_hgrn_kernel, n_chunks=sb // HGRN_CHUNK),
        grid=(batch, seq // sb),
        in_specs=[spec, spec, spec, spec,
                  pl.BlockSpec((slots, HGRN_WIDTH), lambda b, s: (0, 0)),
                  pl.BlockSpec((1, HGRN_HEAD_DIM), lambda b, s: (0, 0))],
        out_specs=spec,
        out_shape=jax.ShapeDtypeStruct(shp, BF16),
        scratch_shapes=[pltpu.VMEM((HGRN_HEADS, HGRN_HEAD_DIM, HGRN_HEAD_DIM), F32)],
        compiler_params=_cparams("arbitrary", "arbitrary"),
        name="hgrn2",
    )(qh.reshape(shp), fh.reshape(shp), ih.reshape(shp), gh.reshape(shp),
      lb_logits.astype(F32), norm_w.reshape(1, HGRN_HEAD_DIM))
    return out.reshape(batch * seq, HGRN_WIDTH)


def _outproj_kernel(attn_ref, hg_ref, x_ref, w_ref, nw_ref, rw_ref, rb_ref,
                    h_ref, xn_ref, idx_ref, gate_ref, rank_ref, cnt_ref, run_ref):
    y = jnp.dot(attn_ref[...], w_ref[0:ATTN_WIDTH, :], preferred_element_type=F32)
    y = y + jnp.dot(hg_ref[...], w_ref[ATTN_WIDTH:, :], preferred_element_type=F32)
    h = x_ref[...] + y
    h_ref[...] = h
    xn = h * lax.rsqrt(jnp.mean(h * h, axis=-1, keepdims=True) + RMS_EPS) * nw_ref[...]
    _store_row_tiles(xn_ref, xn)

    logits = lax.dot_general(rw_ref[...], xn, NT_DIMS, precision=HIGHEST,
                             preferred_element_type=F32) + rb_ref[...]
    eid = lax.broadcasted_iota(jnp.int32, logits.shape, 0)
    vals, idxs = [], []
    for _ in range(TOP_K):
        mx = jnp.max(logits, axis=0, keepdims=True)
        ix = jnp.min(jnp.where(logits == mx, eid, N_EXPERTS), axis=0, keepdims=True)
        vals.append(mx)
        idxs.append(ix)
        logits = jnp.where(eid == ix, -jnp.inf, logits)
    es = [jnp.exp(v - vals[0]) for v in vals]
    den = es[0] + es[1] + es[2] + es[3]
    idx_ref[...] = jnp.concatenate(idxs, axis=0)
    gate_ref[...] = jnp.concatenate([e / den for e in es], axis=0)

    @pl.when(pl.program_id(0) == 0)
    def _():
        run_ref[...] = jnp.zeros_like(run_ref)

    span = RANK_SPAN
    onehots = [(eid == ix).astype(F32) for ix in idxs]
    routed = onehots[0] + onehots[1] + onehots[2] + onehots[3]
    earlier = (lax.broadcasted_iota(jnp.int32, (span, span), 0)
               < lax.broadcasted_iota(jnp.int32, (span, span), 1)).astype(BF16)
    run = run_ref[...]
    ranks = []
    for c0 in range(0, logits.shape[1], span):
        part = routed[:, c0:c0 + span]
        before = jnp.dot(part.astype(BF16), earlier, preferred_element_type=F32) + run
        ranks.append(jnp.concatenate([jnp.sum(o[:, c0:c0 + span] * before, axis=0, keepdims=True)
                                      for o in onehots], axis=0))
        run = run + jnp.sum(part, axis=1, keepdims=True)
    rank_ref[...] = jnp.concatenate(ranks, axis=1).astype(jnp.int32)
    run_ref[...] = run
    cnt_ref[...] = run.astype(jnp.int32)


def _outproj(attn, hg, x2, w_out, norm_w, router_w, router_b):
    n = x2.shape[0]
    tm = OUTPROJ_ROWS
    row = lambda width: pl.BlockSpec((tm, width), lambda i: (i, 0))
    full = lambda a, b: pl.BlockSpec((a, b), lambda i: (0, 0))
    tok = pl.BlockSpec((TOP_K, tm), lambda i: (0, i))
    return pl.pallas_call(
        _outproj_kernel,
        grid=(n // tm,),
        in_specs=[row(ATTN_WIDTH), row(HGRN_WIDTH), row(D_MODEL), full(D_MODEL, D_MODEL),
                  full(1, D_MODEL), full(N_EXPERTS, D_MODEL), full(N_EXPERTS, 1)],
        out_specs=[row(D_MODEL), pl.BlockSpec((tm * ROW_CHUNKS, LANES), lambda i: (i, 0)),
                   tok, tok, tok, full(N_EXPERTS, 1)],
        out_shape=[jax.ShapeDtypeStruct((n, D_MODEL), F32),
                   jax.ShapeDtypeStruct((n * ROW_CHUNKS, LANES), F32),
                   jax.ShapeDtypeStruct((TOP_K, n), jnp.int32),
                   jax.ShapeDtypeStruct((TOP_K, n), F32),
                   jax.ShapeDtypeStruct((TOP_K, n), jnp.int32),
                   jax.ShapeDtypeStruct((N_EXPERTS, 1), jnp.int32)],
        scratch_shapes=[pltpu.VMEM((N_EXPERTS, 1), F32)],
        compiler_params=_cparams("arbitrary"),
        name="outproj_router",
    )(attn, hg, x2, w_out.astype(BF16), norm_w.reshape(1, D_MODEL),
      router_w.T.astype(F32), router_b.astype(F32).reshape(N_EXPERTS, 1))


def _ffn_kernel(be_ref, rv_ref, nx_ref, xs_ref, wgu_hbm, bgu_ref, wdn_hbm, bdn_ref, o_ref,
                wgu_f32, wdn_f32, wgu_bf, wdn_bf, sems):
    i = pl.program_id(0)
    rows_valid = rv_ref[i]
    new_expert = (i == 0) | (be_ref[i] != be_ref[jnp.maximum(i - 1, 0)])

    def weight_copies(e):
        return (pltpu.make_async_copy(wgu_hbm.at[e], wgu_f32, sems.at[0]),
                pltpu.make_async_copy(wdn_hbm.at[e], wdn_f32, sems.at[1]))

    @pl.when(i == 0)
    def _():
        for cp in weight_copies(be_ref[0]):
            cp.start()

    @pl.when(new_expert & (rows_valid > 0))
    def _():
        for cp in weight_copies(be_ref[i]):
            cp.wait()
        step = 128
        for r in range(0, D_MODEL, step):
            wgu_bf[r:r + step, :] = wgu_f32[r:r + step, :].astype(BF16)
        for r in range(0, D_EXPERT, step):
            wdn_bf[r:r + step, :] = wdn_f32[r:r + step, :].astype(BF16)

        @pl.when(nx_ref[i] >= 0)
        def _():
            for cp in weight_copies(nx_ref[i]):
                cp.start()

    @pl.when(rows_valid > 0)
    def _():
        live = lax.broadcasted_iota(jnp.int32, (FFN_ROWS, 1), 0) < rows_valid
        x = jnp.where(live, _load_row_tiles(xs_ref, FFN_ROWS), 0.0).astype(BF16)
        out = bdn_ref[0]
        for c0 in range(0, D_EXPERT, FFN_COLS):
            g_cols = slice(c0, c0 + FFN_COLS)
            u_cols = slice(D_EXPERT + c0, D_EXPERT + c0 + FFN_COLS)
            gate = jnp.dot(x, wgu_bf[:, g_cols], preferred_element_type=F32) + bgu_ref[0, :, g_cols]
            up = jnp.dot(x, wgu_bf[:, u_cols], preferred_element_type=F32) + bgu_ref[0, :, u_cols]
            gate = jnp.minimum(gate, SWIGLU_LIMIT)
            up = jnp.clip(up, -SWIGLU_LIMIT, SWIGLU_LIMIT)
            act = gate * jax.nn.sigmoid(SWIGLU_ALPHA * gate) * (up + 1.0)
            out = out + jnp.dot(act.astype(BF16), wdn_bf[g_cols, :], preferred_element_type=F32)
        _store_row_tiles(o_ref, out)

    @pl.when(rows_valid <= 0)
    def _():
        o_ref[...] = jnp.zeros_like(o_ref)


def _expert_ffn(block_e, rows_valid, next_e, xs, w_gu, b_gu, w_dn, b_dn):
    tm = FFN_ROWS
    tiles = pl.BlockSpec((tm * ROW_CHUNKS, LANES), lambda i, be, rv, nx: (i, 0))
    grid_spec = pltpu.PrefetchScalarGridSpec(
        num_scalar_prefetch=3,
        grid=(xs.shape[0] // (tm * ROW_CHUNKS),),
        in_specs=[tiles,
                  pl.BlockSpec(memory_space=pl.ANY),
                  pl.BlockSpec((1, 1, 2 * D_EXPERT), lambda i, be, rv, nx: (be[i], 0, 0)),
                  pl.BlockSpec(memory_space=pl.ANY),
                  pl.BlockSpec((1, 1, D_MODEL), lambda i, be, rv, nx: (be[i], 0, 0))],
        out_specs=tiles,
        scratch_shapes=[pltpu.VMEM((D_MODEL, 2 * D_EXPERT), F32), pltpu.VMEM((D_EXPERT, D_MODEL), F32),
                        pltpu.VMEM((D_MODEL, 2 * D_EXPERT), BF16), pltpu.VMEM((D_EXPERT, D_MODEL), BF16),
                        pltpu.SemaphoreType.DMA((2,))],
    )
    return pl.pallas_call(
        _ffn_kernel,
        grid_spec=grid_spec,
        out_shape=jax.ShapeDtypeStruct(xs.shape, F32),
        compiler_params=_cparams("arbitrary"),
        name="expert_ffn",
    )(block_e, rows_valid, next_e, xs, w_gu, b_gu.reshape(N_EXPERTS, 1, 2 * D_EXPERT),
      w_dn, b_dn.reshape(N_EXPERTS, 1, D_MODEL))


def _pos_kernel(idx_ref, rank_ref, pstart_ref, move_ref, comb_ref):
    eid = lax.broadcasted_iota(jnp.int32, (N_EXPERTS, idx_ref.shape[1]), 0)
    pstart = pstart_ref[...]
    rows = [jnp.sum(jnp.where(eid == idx_ref[k:k + 1, :], pstart, 0), axis=0, keepdims=True)
            for k in range(TOP_K)]
    pos = (jnp.concatenate(rows, axis=0) + rank_ref[...]) * ROW_CHUNKS
    for k in range(TOP_K):
        move_ref[0, :, k * MOVE_TOKENS:(k + 1) * MOVE_TOKENS] = pos[k:k + 1, :]
        for j in range(MOVE_TOKENS // COMBINE_ROWS):
            comb_ref[j, :, k * COMBINE_ROWS:(k + 1) * COMBINE_ROWS] = pos[k:k + 1, j * COMBINE_ROWS:(j + 1) * COMBINE_ROWS]


def _positions(idx_t, rank_t, pstart):
    n = idx_t.shape[1]
    tl = MOVE_TOKENS
    per = tl // COMBINE_ROWS
    tok = pl.BlockSpec((TOP_K, tl), lambda i: (0, i))
    return pl.pallas_call(
        _pos_kernel,
        grid=(n // tl,),
        in_specs=[tok, tok, pl.BlockSpec((N_EXPERTS, 1), lambda i: (0, 0))],
        out_specs=[pl.BlockSpec((1, 1, TOP_K * tl), lambda i: (i, 0, 0)),
                   pl.BlockSpec((per, 1, TOP_K * COMBINE_ROWS), lambda i: (i, 0, 0))],
        out_shape=[jax.ShapeDtypeStruct((n // tl, 1, TOP_K * tl), jnp.int32),
                   jax.ShapeDtypeStruct((n // COMBINE_ROWS, 1, TOP_K * COMBINE_ROWS), jnp.int32)],
        compiler_params=_cparams("arbitrary"),
        name="dispatch_pos",
    )(idx_t, rank_t, pstart.reshape(N_EXPERTS, 1))


def _row_tile(ref, row_offset):
    return ref.at[pl.ds(pl.multiple_of(row_offset, ROW_CHUNKS), ROW_CHUNKS)]


def _dispatch_kernel(pos_ref, x_ref, xs_ref, sem):
    tokens = x_ref.shape[0] // ROW_CHUNKS

    def copy(t, k):
        return pltpu.make_async_copy(_row_tile(x_ref, t * ROW_CHUNKS),
                                     _row_tile(xs_ref, pos_ref[0, 0, k * tokens + t]), sem)

    def issue(t, carry):
        for k in range(TOP_K):
            copy(t, k).start(priority=k % 2)
        return carry

    def drain(t, carry):
        for k in range(TOP_K):
            copy(t, k).wait()
        return carry

    lax.fori_loop(0, tokens, issue, 0, unroll=MOVE_UNROLL)
    lax.fori_loop(0, tokens, drain, 0, unroll=MOVE_UNROLL)


def _dispatch_rows(pos_tiles, xn, out_rows):
    n = xn.shape[0] // ROW_CHUNKS
    tl = MOVE_TOKENS
    return pl.pallas_call(
        _dispatch_kernel,
        grid=(n // tl,),
        in_specs=[pl.BlockSpec((1, 1, tl * TOP_K), lambda i: (i, 0, 0), memory_space=pltpu.SMEM),
                  pl.BlockSpec((tl * ROW_CHUNKS, LANES), lambda i: (i, 0))],
        out_specs=pl.BlockSpec(memory_space=pl.ANY),
        out_shape=jax.ShapeDtypeStruct((out_rows * ROW_CHUNKS, LANES), xn.dtype),
        scratch_shapes=[pltpu.SemaphoreType.DMA(())],
        compiler_params=_cparams("arbitrary"),
        name="dispatch_rows",
    )(pos_tiles, xn)


def _combine_kernel(pos_ref, pos_next_ref, h_ref, g_ref, fw_ref, ys_ref, o_ref, buf, sems):
    i = pl.program_id(0)
    steps = pl.num_programs(0)
    tokens = h_ref.shape[0]
    slot = lax.rem(i, 2)

    def gather(p_ref, s, start):
        def body(t, carry):
            for k in range(TOP_K):
                cp = pltpu.make_async_copy(_row_tile(ys_ref, p_ref[0, 0, k * tokens + t]),
                                           _row_tile(buf.at[s, k], t * ROW_CHUNKS), sems.at[s])
                cp.start(priority=k % 2) if start else cp.wait()
            return carry
        lax.fori_loop(0, tokens, body, 0, unroll=MOVE_UNROLL)

    @pl.when(i == 0)
    def _():
        gather(pos_ref, 0, True)

    @pl.when(i + 1 < steps)
    def _():
        gather(pos_next_ref, 1 - slot, True)

    gather(pos_ref, slot, False)
    g = g_ref[...]
    y = h_ref[...]
    for k in range(TOP_K):
        y = y + _load_row_tiles(buf.at[slot, k], tokens) * g[:, k:k + 1]
    o_ref[...] = y * lax.rsqrt(jnp.mean(y * y, axis=-1, keepdims=True) + RMS_EPS) * fw_ref[...]


def _combine(h, ys, pos_t, gates_nk, final_w):
    n = h.shape[0]
    tm = COMBINE_ROWS
    steps = n // tm
    smem = lambda imap: pl.BlockSpec((1, 1, tm * TOP_K), imap, memory_space=pltpu.SMEM)
    return pl.pallas_call(
        _combine_kernel,
        grid=(steps,),
        in_specs=[smem(lambda i: (i, 0, 0)),
                  smem(lambda i: (jnp.minimum(i + 1, steps - 1), 0, 0)),
                  pl.BlockSpec((tm, D_MODEL), lambda i: (i, 0)),
                  pl.BlockSpec((tm, TOP_K), lambda i: (i, 0)),
                  pl.BlockSpec((1, D_MODEL), lambda i: (0, 0)),
                  pl.BlockSpec(memory_space=pl.ANY)],
        out_specs=pl.BlockSpec((tm, D_MODEL), lambda i: (i, 0)),
        out_shape=jax.ShapeDtypeStruct((n, D_MODEL), F32),
        scratch_shapes=[pltpu.VMEM((2, TOP_K, tm * ROW_CHUNKS, LANES), F32), pltpu.SemaphoreType.DMA((2,))],
        compiler_params=_cparams("arbitrary"),
        name="combine_norm",
    )(pos_t, pos_t, h, gates_nk, final_w.reshape(1, D_MODEL), ys)


def kernel(x, norm_mix_w, w_in, rel_bias, hgrn_lb_logits, hgrn_norm_w, w_out, norm_ffn_w,
           router_w, router_b, w_gate_up, b_gate_up, w_down, b_down, final_norm_w):
    batch, seq, _ = x.shape
    n = batch * seq
    x2 = x.reshape(n, D_MODEL)

    qa, ka, va, qh, fh, ih, gh = _inproj(x2, norm_mix_w[0], w_in[0])
    attn = _dilated_attention(qa, ka, va, _masked_bias(rel_bias), batch, seq)
    hg = _hgrn(qh, fh, ih, gh, hgrn_lb_logits, hgrn_norm_w[0], batch, seq)
    h, xn, idx_t, gate_t, rank_t, counts = _outproj(attn, hg, x2, w_out[0], norm_ffn_w[0],
                                                    router_w[0], router_b[0])

    tm = FFN_ROWS
    n_blocks = -(-(n * TOP_K) // tm) + N_EXPERTS
    counts = counts.reshape(N_EXPERTS)
    padded = (counts + tm - 1) // tm * tm
    pend = jnp.cumsum(padded)
    pstart = pend - padded
    block_row0 = jnp.arange(n_blocks, dtype=jnp.int32) * tm
    in_block = (block_row0[:, None] >= pstart[None, :]) & (block_row0[:, None] < pend[None, :])
    block_e = jnp.minimum(jnp.sum(pend[None, :] <= block_row0[:, None], axis=1), N_EXPERTS - 1).astype(jnp.int32)
    rows_valid = jnp.sum(jnp.where(in_block, jnp.clip(pstart + counts - block_row0[:, None], 0, tm), 0),
                         axis=1).astype(jnp.int32)

    pos_move, pos_comb = _positions(idx_t, rank_t, pstart.astype(jnp.int32))
    xs = _dispatch_rows(pos_move, xn, n_blocks * tm)
    experts = jnp.arange(N_EXPERTS, dtype=jnp.int32)
    later = jnp.where((experts[None, :] > experts[:, None]) & (counts[None, :] > 0), experts[None, :], N_EXPERTS)
    next_of = jnp.min(later, axis=1)
    next_e = jnp.where(next_of < N_EXPERTS, next_of, -1)[block_e].astype(jnp.int32)
    ys = _expert_ffn(block_e, rows_valid, next_e, xs, w_gate_up[0], b_gate_up[0], w_down[0], b_down[0])
    out = _combine(h, ys, pos_comb, gate_t.T, final_norm_w)
    return out.reshape(batch, seq, D_MODEL)
```

```python
import functools

import numpy as np
import jax
import jax.numpy as jnp
from jax import lax
from jax.experimental import pallas as pl
from jax.experimental.pallas import tpu as pltpu

F32 = jnp.float32
BF16 = jnp.bfloat16
HIGHEST = lax.Precision.HIGHEST

D_MODEL = 1024
ATTN_HEAD_DIM = 64
ATTN_WIDTH = 512
ATTN_HEADS = ATTN_WIDTH // ATTN_HEAD_DIM
DILATION_PATTERNS = ((128, 1), (512, 4), (2048, 16))
ATTN_BLOCK = 128
REL_BUCKETS = 32
REL_MAX_DISTANCE = 2048
HGRN_HEAD_DIM = 128
HGRN_WIDTH = 512
HGRN_HEADS = HGRN_WIDTH // HGRN_HEAD_DIM
HGRN_CHUNK = 64
HGRN_SUB = 8
N_EXPERTS = 32
TOP_K = 4
D_EXPERT = 1024
SWIGLU_LIMIT = 7.0
SWIGLU_ALPHA = 1.702
RMS_EPS = 1e-5
LOG2_E = 1.4426950408889634
IN_PROJ_WIDTH = 3 * ATTN_WIDTH + 4 * HGRN_WIDTH
LANES = 128
HEAD_PAIRS = ATTN_WIDTH // LANES
ROW_CHUNKS = D_MODEL // LANES

INPROJ_ROWS = 512
OUTPROJ_ROWS = 1024
RANK_SPAN = 256
FFN_ROWS = 512
FFN_COLS = 512
COMBINE_ROWS = 256
HGRN_SEQ_BLOCK = 1024
ATTN_UNROLL = 8
MOVE_TOKENS = 2048
MOVE_UNROLL = 8
VMEM_LIMIT = 48 * 1024 * 1024

NT_DIMS = (((1,), (1,)), ((), ()))
TN_DIMS = (((0,), (0,)), ((), ()))


def _cparams(*sem):
    return pltpu.CompilerParams(dimension_semantics=sem, vmem_limit_bytes=VMEM_LIMIT)


def _store_row_tiles(ref, value):
    rows = value.shape[0]
    for c in range(ROW_CHUNKS):
        ref[pl.ds(c, rows, stride=ROW_CHUNKS), :] = value[:, c * LANES:(c + 1) * LANES]


def _load_row_tiles(ref, rows):
    return jnp.concatenate([ref[pl.ds(c, rows, stride=ROW_CHUNKS), :] for c in range(ROW_CHUNKS)], axis=1)


def _inproj_kernel(x_ref, nw_ref, w_ref, qa_ref, ka_ref, va_ref, qh_ref, fh_ref, ih_ref, gh_ref):
    x = x_ref[...]
    xn = x * lax.rsqrt(jnp.mean(x * x, axis=-1, keepdims=True) + RMS_EPS) * nw_ref[...]
    xn = xn.astype(BF16)
    for c, ref in enumerate((qa_ref, ka_ref, va_ref, qh_ref, fh_ref, ih_ref, gh_ref)):
        y = jnp.dot(xn, w_ref[:, c * ATTN_WIDTH:(c + 1) * ATTN_WIDTH], preferred_element_type=F32)
        ref[...] = y.astype(ref.dtype)


def _inproj(x2, norm_w, w_in):
    n = x2.shape[0]
    tm = INPROJ_ROWS
    row = lambda width: pl.BlockSpec((tm, width), lambda i: (i, 0))
    dtypes = (F32, F32, F32, BF16, F32, BF16, F32)
    return pl.pallas_call(
        _inproj_kernel,
        grid=(n // tm,),
        in_specs=[row(D_MODEL),
                  pl.BlockSpec((1, D_MODEL), lambda i: (0, 0)),
                  pl.BlockSpec((D_MODEL, IN_PROJ_WIDTH), lambda i: (0, 0))],
        out_specs=[row(ATTN_WIDTH)] * 7,
        out_shape=[jax.ShapeDtypeStruct((n, ATTN_WIDTH), dt) for dt in dtypes],
        compiler_params=_cparams("arbitrary"),
        name="inproj",
    )(x2, norm_w.reshape(1, D_MODEL), w_in.astype(BF16))


def _t5_causal_bucket(dist):
    n = np.maximum(dist, 0)
    max_exact = REL_BUCKETS // 2
    large = max_exact + (np.log(np.maximum(n, 1) / max_exact)
                         / np.log(REL_MAX_DISTANCE / max_exact)
                         * (REL_BUCKETS - max_exact)).astype(np.int32)
    large = np.minimum(large, REL_BUCKETS - 1)
    return np.where(n < max_exact, n, large).astype(np.int32)


def _bucket_thresholds(max_dist):
    buckets = _t5_causal_bucket(np.arange(max_dist + 1))
    assert np.all(np.diff(buckets) >= 0)
    return [(b, int(np.argmax(buckets >= b))) for b in range(1, REL_BUCKETS) if np.any(buckets >= b)]


def _bias_kernel(rel_ref, o_ref):
    blk = ATTN_BLOCK
    h = pl.program_id(0)
    qi = lax.broadcasted_iota(jnp.int32, (blk, 2 * blk), 0)
    kj = lax.broadcasted_iota(jnp.int32, (blk, 2 * blk), 1)
    dist_sub = blk + qi - kj
    for pi, (window, dilation) in enumerate(DILATION_PATTERNS):
        span = window // dilation
        dist = dist_sub * dilation
        val = jnp.full((blk, 2 * blk), rel_ref[h], F32)
        for b, first_dist in _bucket_thresholds(window):
            val = jnp.where(dist >= first_dist, rel_ref[b * ATTN_HEADS + h], val)
        o_ref[pi, 0] = jnp.where((dist_sub >= 0) & (dist_sub <= span), val * LOG2_E, -1e30)


def _masked_bias(rel_bias):
    blk = ATTN_BLOCK
    npat = len(DILATION_PATTERNS)
    bias = pl.pallas_call(
        _bias_kernel,
        grid=(ATTN_HEADS,),
        in_specs=[pl.BlockSpec(memory_space=pltpu.SMEM)],
        out_specs=pl.BlockSpec((npat, 1, blk, 2 * blk), lambda h: (0, h, 0, 0)),
        out_shape=jax.ShapeDtypeStruct((npat, ATTN_HEADS, blk, 2 * blk), F32),
        compiler_params=_cparams("arbitrary"),
        name="rel_bias",
    )(rel_bias.astype(F32).reshape(REL_BUCKETS * ATTN_HEADS))
    return bias.reshape(npat, HEAD_PAIRS, 2 * blk, 2 * blk)


def _attn_kernel(q_ref, k_ref, v_ref, bias_ref, o_ref, acc_ref, lse_ref, *, seq):
    blk = ATTN_BLOCK
    lane = lax.broadcasted_iota(jnp.int32, (1, LANES), 1)
    first = lane < ATTN_HEAD_DIM
    scale = ATTN_HEAD_DIM ** -0.5 * LOG2_E
    sel0 = jnp.where(first, scale, 0.0)
    sel1 = jnp.where(first, 0.0, scale)
    ones_cols = jnp.ones((2 * blk, LANES), BF16)

    def key_rows(ref, dil, start):
        return ref[0, pl.ds(start, blk, stride=dil), :].astype(BF16)

    def one_block(pi, dil, q_start, kk, vv):
        n_keys = kk.shape[0]
        qb = q_ref[0, pl.ds(q_start, blk, stride=dil), :]
        q2 = jnp.concatenate([qb * sel0, qb * sel1], axis=0).astype(BF16)
        s = lax.dot_general(q2, kk, NT_DIMS, preferred_element_type=F32)
        s = s + bias_ref[pi, 0, :, 2 * blk - n_keys:]
        m = jnp.max(s, axis=-1, keepdims=True)
        p = jnp.exp2(s - m).astype(BF16)
        ov = jnp.dot(p, jnp.concatenate([vv, ones_cols[:n_keys]], axis=1), preferred_element_type=F32)
        z = ov[:, LANES:]
        o = ov[:, :LANES] / z
        lse = m + jnp.log2(z)
        rows = pl.ds(q_start, blk, stride=dil)
        acc_ref[pi, rows, :] = jnp.where(first, o[:blk], o[blk:])
        lse_ref[pi, rows, :] = jnp.where(first, lse[:blk], lse[blk:])

    for pi, (_, dil) in enumerate(DILATION_PATTERNS):
        nb = seq // (dil * blk)

        def residue(r, carry, pi=pi, dil=dil, nb=nb):
            k0, v0 = key_rows(k_ref, dil, r), key_rows(v_ref, dil, r)
            one_block(pi, dil, r, k0, v0)

            def body(j, prev):
                k_prev, v_prev = prev
                q_start = r + j * (blk * dil)
                k_cur, v_cur = key_rows(k_ref, dil, q_start), key_rows(v_ref, dil, q_start)
                one_block(pi, dil, q_start, jnp.concatenate([k_prev, k_cur], axis=0),
                          jnp.concatenate([v_prev, v_cur], axis=0))
                return k_cur, v_cur

            lax.fori_loop(1, nb, body, (k0, v0), unroll=ATTN_UNROLL)
            return carry

        lax.fori_loop(0, dil, residue, 0, unroll=ATTN_UNROLL if nb == 2 else 1)

    def merge(i, carry):
        rows = pl.ds(pl.multiple_of(i * blk, blk), blk)
        ls = [lse_ref[pi, rows, :] for pi in range(len(DILATION_PATTERNS))]
        m = functools.reduce(jnp.maximum, ls)
        es = [jnp.exp2(l - m) for l in ls]
        num = sum(e * acc_ref[pi, rows, :] for pi, e in enumerate(es))
        o_ref[0, rows, :] = (num / sum(es)).astype(o_ref.dtype)
        return carry

    lax.fori_loop(0, seq // blk, merge, 0)


def _dilated_attention(qa, ka, va, bias, batch, seq):
    npat = len(DILATION_PATTERNS)
    shp = (batch, seq, ATTN_WIDTH)
    spec = pl.BlockSpec((1, seq, LANES), lambda b, p: (b, 0, p))
    out = pl.pallas_call(
        functools.partial(_attn_kernel, seq=seq),
        grid=(batch, HEAD_PAIRS),
        in_specs=[spec, spec, spec,
                  pl.BlockSpec((npat, 1, 2 * ATTN_BLOCK, 2 * ATTN_BLOCK), lambda b, p: (0, p, 0, 0))],
        out_specs=spec,
        out_shape=jax.ShapeDtypeStruct(shp, BF16),
        scratch_shapes=[pltpu.VMEM((npat, seq, LANES), F32), pltpu.VMEM((npat, seq, LANES), F32)],
        compiler_params=_cparams("arbitrary", "arbitrary"),
        name="dilated_attn",
    )(qa.reshape(shp), ka.reshape(shp), va.reshape(shp), bias)
    return out.reshape(batch * seq, ATTN_WIDTH)


def _hgrn_kernel(q_ref, f_ref, i_ref, g_ref, lbl_ref, nw_ref, o_ref, st_ref, *, n_chunks):
    c, sub, dk = HGRN_CHUNK, HGRN_SUB, HGRN_HEAD_DIM
    nsub = c // sub
    lbl = lbl_ref[...]
    e = jnp.exp(lbl - jnp.max(lbl, axis=0, keepdims=True))
    lb_all = e[0:1] / jnp.sum(e, axis=0, keepdims=True)
    nw = nw_ref[...]

    r64 = lax.broadcasted_iota(jnp.int32, (c, c), 0)
    c64 = lax.broadcasted_iota(jnp.int32, (c, c), 1)
    sub_end = (r64 // sub) * sub + (sub - 1)
    decay_sums = jnp.concatenate([(c64 <= r64), (c64 > r64) & (c64 <= sub_end), (c64 > r64)],
                                 axis=0).astype(BF16)
    col_sub = c64 // sub
    col_in_sub = c64 - (r64 // sub) * sub
    t_iota = lax.broadcasted_iota(jnp.int32, (nsub, sub, dk), 1)
    ones_rhs = jnp.ones((dk, c), BF16)

    @pl.when(pl.program_id(1) == 0)
    def _():
        st_ref[...] = jnp.zeros_like(st_ref)

    def cols_of(hd):
        return slice(hd * dk, (hd + 1) * dk)

    def gates(rows, hd):
        cols = cols_of(hd)
        lb = lb_all[:, cols]
        q = q_ref[0, rows, cols].astype(F32)
        f = lb + (1.0 - lb) * jax.nn.sigmoid(f_ref[0, rows, cols])
        log_f = jnp.log2(f)
        log_hi = log_f.astype(BF16)
        log_lo = (log_f - log_hi.astype(F32)).astype(BF16)
        sums = (jnp.dot(decay_sums, log_hi, preferred_element_type=F32)
                + jnp.dot(decay_sums, log_lo, preferred_element_type=F32))
        return sums, 1.0 - f, q * jax.nn.sigmoid(q) * (dk ** -0.5)

    def products(rows, hd, sums, key, qf):
        cum, to_sub_end, to_chunk_end = sums[:c], sums[c:2 * c], sums[2 * c:]
        v = i_ref[0, rows, cols_of(hd)]
        st = st_ref[hd]
        o = lax.dot_general((qf * jnp.exp2(cum)).astype(BF16), st.astype(BF16), NT_DIMS,
                            preferred_element_type=F32)
        kd = (key * jnp.exp2(to_chunk_end)).astype(BF16)
        st_ref[hd] = (st * jnp.exp2(cum[c - 1:c, :])
                      + lax.dot_general(v, kd, TN_DIMS, preferred_element_type=F32))

        cum3 = cum.reshape(nsub, sub, dk)
        key3 = key.reshape(nsub, sub, dk)
        qf3 = qf.reshape(nsub, sub, dk)
        khat = (key * jnp.exp2(to_sub_end)).astype(BF16)
        qs = []
        for j in range(nsub - 1):
            t0 = sub * (j + 1)
            qs.append(qf[t0:] * jnp.exp2(cum[t0:] - cum[t0 - 1:t0, :]))
        prod = lax.dot_general(jnp.concatenate(qs, axis=0).astype(BF16), khat, NT_DIMS,
                               preferred_element_type=F32)

        ws = []
        for s in range(sub):
            dec = jnp.exp2(jnp.where(t_iota >= s, cum3 - cum3[:, s:s + 1, :], -jnp.inf))
            ws.append((qf3 * key3[:, s:s + 1, :] * dec).reshape(c, dk))
        pair_sums = jnp.dot(jnp.concatenate(ws, axis=0).astype(BF16), ones_rhs,
                            preferred_element_type=F32)
        return o, prod, pair_sums

    def finish(rows, hd, o, prod, pair_sums):
        cols = cols_of(hd)
        a = jnp.zeros((c, c), F32)
        r = 0
        for j in range(nsub - 1):
            t0 = sub * (j + 1)
            pj = jnp.concatenate([jnp.zeros((t0, c), F32), prod[r:r + c - t0, :]], axis=0)
            a = jnp.where(col_sub == j, pj, a)
            r += c - t0
        for s in range(sub):
            a = jnp.where(col_in_sub == s, pair_sums[s * c:(s + 1) * c, :], a)
        o = o + jnp.dot(a.astype(BF16), i_ref[0, rows, cols], preferred_element_type=F32)
        on = o * lax.rsqrt(jnp.mean(o * o, axis=-1, keepdims=True) + RMS_EPS) * nw
        g = g_ref[0, rows, cols]
        o_ref[0, rows, cols] = (on * (g * jax.nn.sigmoid(g))).astype(BF16)

    def body(ci, carry):
        rows = pl.ds(pl.multiple_of(ci * c, c), c)
        heads = range(HGRN_HEADS)
        stage1 = [gates(rows, hd) for hd in heads]
        stage2 = [products(rows, hd, *stage1[hd]) for hd in heads]
        for hd in heads:
            finish(rows, hd, *stage2[hd])
        return carry

    lax.fori_loop(0, n_chunks, body, 0, unroll=2)


def _hgrn(qh, fh, ih, gh, lb_logits, norm_w, batch, seq):
    sb = min(seq, HGRN_SEQ_BLOCK)
    spec = pl.BlockSpec((1, sb, HGRN_WIDTH), lambda b, s: (b, s, 0))
    slots = lb_logits.shape[0]
    shp = (batch, seq, HGRN_WIDTH)
    out = pl.pallas_call(
        functools.partial(_hgrn_kernel, n_chunks=sb // HGRN_CHUNK),
        grid=(batch, seq // sb),
        in_specs=[spec, spec, spec, spec,
                  pl.BlockSpec((slots, HGRN_WIDTH), lambda b, s: (0, 0)),
                  pl.BlockSpec((1, HGRN_HEAD_DIM), lambda b, s: (0, 0))],
        out_specs=spec,
        out_shape=jax.ShapeDtypeStruct(shp, BF16),
        scratch_shapes=[pltpu.VMEM((HGRN_HEADS, HGRN_HEAD_DIM, HGRN_HEAD_DIM), F32)],
        compiler_params=_cparams("arbitrary", "arbitrary"),
        name="hgrn2",
    )(qh.reshape(shp), fh.reshape(shp), ih.reshape(shp), gh.reshape(shp),
      lb_logits.astype(F32), norm_w.reshape(1, HGRN_HEAD_DIM))
    return out.reshape(batch * seq, HGRN_WIDTH)


def _outproj_kernel(attn_ref, hg_ref, x_ref, w_ref, nw_ref, rw_ref, rb_ref,
                    h_ref, xn_ref, idx_ref, gate_ref, rank_ref, cnt_ref, run_ref):
    y = jnp.dot(attn_ref[...], w_ref[0:ATTN_WIDTH, :], preferred_element_type=F32)
    y = y + jnp.dot(hg_ref[...], w_ref[ATTN_WIDTH:, :], preferred_element_type=F32)
    h = x_ref[...] + y
    h_ref[...] = h
    xn = h * lax.rsqrt(jnp.mean(h * h, axis=-1, keepdims=True) + RMS_EPS) * nw_ref[...]
    _store_row_tiles(xn_ref, xn)

    logits = lax.dot_general(rw_ref[...], xn, NT_DIMS, precision=HIGHEST,
                             preferred_element_type=F32) + rb_ref[...]
    eid = lax.broadcasted_iota(jnp.int32, logits.shape, 0)
    vals, idxs = [], []
    for _ in range(TOP_K):
        mx = jnp.max(logits, axis=0, keepdims=True)
        ix = jnp.min(jnp.where(logits == mx, eid, N_EXPERTS), axis=0, keepdims=True)
        vals.append(mx)
        idxs.append(ix)
        logits = jnp.where(eid == ix, -jnp.inf, logits)
    es = [jnp.exp(v - vals[0]) for v in vals]
    den = es[0] + es[1] + es[2] + es[3]
    idx_ref[...] = jnp.concatenate(idxs, axis=0)
    gate_ref[...] = jnp.concatenate([e / den for e in es], axis=0)

    @pl.when(pl.program_id(0) == 0)
    def _():
        run_ref[...] = jnp.zeros_like(run_ref)

    span = RANK_SPAN
    onehots = [(eid == ix).astype(F32) for ix in idxs]
    routed = onehots[0] + onehots[1] + onehots[2] + onehots[3]
    earlier = (lax.broadcasted_iota(jnp.int32, (span, span), 0)
               < lax.broadcasted_iota(jnp.int32, (span, span), 1)).astype(BF16)
    run = run_ref[...]
    ranks = []
    for c0 in range(0, logits.shape[1], span):
        part = routed[:, c0:c0 + span]
        before = jnp.dot(part.astype(BF16), earlier, preferred_element_type=F32) + run
        ranks.append(jnp.concatenate([jnp.sum(o[:, c0:c0 + span] * before, axis=0, keepdims=True)
                                      for o in onehots], axis=0))
        run = run + jnp.sum(part, axis=1, keepdims=True)
    rank_ref[...] = jnp.concatenate(ranks, axis=1).astype(jnp.int32)
    run_ref[...] = run
    cnt_ref[...] = run.astype(jnp.int32)


def _outproj(attn, hg, x2, w_out, norm_w, router_w, router_b):
    n = x2.shape[0]
    tm = OUTPROJ_ROWS
    row = lambda width: pl.BlockSpec((tm, width), lambda i: (i, 0))
    full = lambda a, b: pl.BlockSpec((a, b), lambda i: (0, 0))
    tok = pl.BlockSpec((TOP_K, tm), lambda i: (0, i))
    return pl.pallas_call(
        _outproj_kernel,
        grid=(n // tm,),
        in_specs=[row(ATTN_WIDTH), row(HGRN_WIDTH), row(D_MODEL), full(D_MODEL, D_MODEL),
                  full(1, D_MODEL), full(N_EXPERTS, D_MODEL), full(N_EXPERTS, 1)],
        out_specs=[row(D_MODEL), pl.BlockSpec((tm * ROW_CHUNKS, LANES), lambda i: (i, 0)),
                   tok, tok, tok, full(N_EXPERTS, 1)],
        out_shape=[jax.ShapeDtypeStruct((n, D_MODEL), F32),
                   jax.ShapeDtypeStruct((n * ROW_CHUNKS, LANES), F32),
                   jax.ShapeDtypeStruct((TOP_K, n), jnp.int32),
                   jax.ShapeDtypeStruct((TOP_K, n), F32),
                   jax.ShapeDtypeStruct((TOP_K, n), jnp.int32),
                   jax.ShapeDtypeStruct((N_EXPERTS, 1), jnp.int32)],
        scratch_shapes=[pltpu.VMEM((N_EXPERTS, 1), F32)],
        compiler_params=_cparams("arbitrary"),
        name="outproj_router",
    )(attn, hg, x2, w_out.astype(BF16), norm_w.reshape(1, D_MODEL),
      router_w.T.astype(F32), router_b.astype(F32).reshape(N_EXPERTS, 1))


def _ffn_kernel(be_ref, rv_ref, nx_ref, xs_ref, wgu_hbm, bgu_ref, wdn_hbm, bdn_ref, o_ref,
                wgu_f32, wdn_f32, wgu_bf, wdn_bf, sems):
    i = pl.program_id(0)
    rows_valid = rv_ref[i]
    new_expert = (i == 0) | (be_ref[i] != be_ref[jnp.maximum(i - 1, 0)])

    def weight_copies(e):
        return (pltpu.make_async_copy(wgu_hbm.at[e], wgu_f32, sems.at[0]),
                pltpu.make_async_copy(wdn_hbm.at[e], wdn_f32, sems.at[1]))

    @pl.when(i == 0)
    def _():
        for cp in weight_copies(be_ref[0]):
            cp.start()

    @pl.when(new_expert & (rows_valid > 0))
    def _():
        for cp in weight_copies(be_ref[i]):
            cp.wait()
        step = 128
        for r in range(0, D_MODEL, step):
            wgu_bf[r:r + step, :] = wgu_f32[r:r + step, :].astype(BF16)
        for r in range(0, D_EXPERT, step):
            wdn_bf[r:r + step, :] = wdn_f32[r:r + step, :].astype(BF16)

        @pl.when(nx_ref[i] >= 0)
        def _():
            for cp in weight_copies(nx_ref[i]):
                cp.start()

    @pl.when(rows_valid > 0)
    def _():
        live = lax.broadcasted_iota(jnp.int32, (FFN_ROWS, 1), 0) < rows_valid
        x = jnp.where(live, _load_row_tiles(xs_ref, FFN_ROWS), 0.0).astype(BF16)
        out = bdn_ref[0]
        for c0 in range(0, D_EXPERT, FFN_COLS):
            g_cols = slice(c0, c0 + FFN_COLS)
            u_cols = slice(D_EXPERT + c0, D_EXPERT + c0 + FFN_COLS)
            gate = jnp.dot(x, wgu_bf[:, g_cols], preferred_element_type=F32) + bgu_ref[0, :, g_cols]
            up = jnp.dot(x, wgu_bf[:, u_cols], preferred_element_type=F32) + bgu_ref[0, :, u_cols]
            gate = jnp.minimum(gate, SWIGLU_LIMIT)
            up = jnp.clip(up, -SWIGLU_LIMIT, SWIGLU_LIMIT)
            act = gate * jax.nn.sigmoid(SWIGLU_ALPHA * gate) * (up + 1.0)
            out = out + jnp.dot(act.astype(BF16), wdn_bf[g_cols, :], preferred_element_type=F32)
        _store_row_tiles(o_ref, out)

    @pl.when(rows_valid <= 0)
    def _():
        o_ref[...] = jnp.zeros_like(o_ref)


def _expert_ffn(block_e, rows_valid, next_e, xs, w_gu, b_gu, w_dn, b_dn):
    tm = FFN_ROWS
    tiles = pl.BlockSpec((tm * ROW_CHUNKS, LANES), lambda i, be, rv, nx: (i, 0))
    grid_spec = pltpu.PrefetchScalarGridSpec(
        num_scalar_prefetch=3,
        grid=(xs.shape[0] // (tm * ROW_CHUNKS),),
        in_specs=[tiles,
                  pl.BlockSpec(memory_space=pl.ANY),
                  pl.BlockSpec((1, 1, 2 * D_EXPERT), lambda i, be, rv, nx: (be[i], 0, 0)),
                  pl.BlockSpec(memory_space=pl.ANY),
                  pl.BlockSpec((1, 1, D_MODEL), lambda i, be, rv, nx: (be[i], 0, 0))],
        out_specs=tiles,
        scratch_shapes=[pltpu.VMEM((D_MODEL, 2 * D_EXPERT), F32), pltpu.VMEM((D_EXPERT, D_MODEL), F32),
                        pltpu.VMEM((D_MODEL, 2 * D_EXPERT), BF16), pltpu.VMEM((D_EXPERT, D_MODEL), BF16),
                        pltpu.SemaphoreType.DMA((2,))],
    )
    return pl.pallas_call(
        _ffn_kernel,
        grid_spec=grid_spec,
        out_shape=jax.ShapeDtypeStruct(xs.shape, F32),
        compiler_params=_cparams("arbitrary"),
        name="expert_ffn",
    )(block_e, rows_valid, next_e, xs, w_gu, b_gu.reshape(N_EXPERTS, 1, 2 * D_EXPERT),
      w_dn, b_dn.reshape(N_EXPERTS, 1, D_MODEL))


def _pos_kernel(idx_ref, rank_ref, pstart_ref, move_ref, comb_ref):
    eid = lax.broadcasted_iota(jnp.int32, (N_EXPERTS, idx_ref.shape[1]), 0)
    pstart = pstart_ref[...]
    rows = [jnp.sum(jnp.where(eid == idx_ref[k:k + 1, :], pstart, 0), axis=0, keepdims=True)
            for k in range(TOP_K)]
    pos = (jnp.concatenate(rows, axis=0) + rank_ref[...]) * ROW_CHUNKS
    for k in range(TOP_K):
        move_ref[0, :, k * MOVE_TOKENS:(k + 1) * MOVE_TOKENS] = pos[k:k + 1, :]
        for j in range(MOVE_TOKENS // COMBINE_ROWS):
            comb_ref[j, :, k * COMBINE_ROWS:(k + 1) * COMBINE_ROWS] = pos[k:k + 1, j * COMBINE_ROWS:(j + 1) * COMBINE_ROWS]


def _positions(idx_t, rank_t, pstart):
    n = idx_t.shape[1]
    tl = MOVE_TOKENS
    per = tl // COMBINE_ROWS
    tok = pl.BlockSpec((TOP_K, tl), lambda i: (0, i))
    return pl.pallas_call(
        _pos_kernel,
        grid=(n // tl,),
        in_specs=[tok, tok, pl.BlockSpec((N_EXPERTS, 1), lambda i: (0, 0))],
        out_specs=[pl.BlockSpec((1, 1, TOP_K * tl), lambda i: (i, 0, 0)),
                   pl.BlockSpec((per, 1, TOP_K * COMBINE_ROWS), lambda i: (i, 0, 0))],
        out_shape=[jax.ShapeDtypeStruct((n // tl, 1, TOP_K * tl), jnp.int32),
                   jax.ShapeDtypeStruct((n // COMBINE_ROWS, 1, TOP_K * COMBINE_ROWS), jnp.int32)],
        compiler_params=_cparams("arbitrary"),
        name="dispatch_pos",
    )(idx_t, rank_t, pstart.reshape(N_EXPERTS, 1))


def _row_tile(ref, row_offset):
    return ref.at[pl.ds(pl.multiple_of(row_offset, ROW_CHUNKS), ROW_CHUNKS)]


def _dispatch_kernel(pos_ref, x_ref, xs_ref, sem):
    tokens = x_ref.shape[0] // ROW_CHUNKS

    def copy(t, k):
        return pltpu.make_async_copy(_row_tile(x_ref, t * ROW_CHUNKS),
                                     _row_tile(xs_ref, pos_ref[0, 0, k * tokens + t]), sem)

    def issue(t, carry):
        for k in range(TOP_K):
            copy(t, k).start(priority=k % 2)
        return carry

    def drain(t, carry):
        for k in range(TOP_K):
            copy(t, k).wait()
        return carry

    lax.fori_loop(0, tokens, issue, 0, unroll=MOVE_UNROLL)
    lax.fori_loop(0, tokens, drain, 0, unroll=MOVE_UNROLL)


def _dispatch_rows(pos_tiles, xn, out_rows):
    n = xn.shape[0] // ROW_CHUNKS
    tl = MOVE_TOKENS
    return pl.pallas_call(
        _dispatch_kernel,
        grid=(n // tl,),
        in_specs=[pl.BlockSpec((1, 1, tl * TOP_K), lambda i: (i, 0, 0), memory_space=pltpu.SMEM),
                  pl.BlockSpec((tl * ROW_CHUNKS, LANES), lambda i: (i, 0))],
        out_specs=pl.BlockSpec(memory_space=pl.ANY),
        out_shape=jax.ShapeDtypeStruct((out_rows * ROW_CHUNKS, LANES), xn.dtype),
        scratch_shapes=[pltpu.SemaphoreType.DMA(())],
        compiler_params=_cparams("arbitrary"),
        name="dispatch_rows",
    )(pos_tiles, xn)


def _combine_kernel(pos_ref, pos_next_ref, h_ref, g_ref, fw_ref, ys_ref, o_ref, buf, sems):
    i = pl.program_id(0)
    steps = pl.num_programs(0)
    tokens = h_ref.shape[0]
    slot = lax.rem(i, 2)

    def gather(p_ref, s, start):
        def body(t, carry):
            for k in range(TOP_K):
                cp = pltpu.make_async_copy(_row_tile(ys_ref, p_ref[0, 0, k * tokens + t]),
                                           _row_tile(buf.at[s, k], t * ROW_CHUNKS), sems.at[s])
                cp.start(priority=k % 2) if start else cp.wait()
            return carry
        lax.fori_loop(0, tokens, body, 0, unroll=MOVE_UNROLL)

    @pl.when(i == 0)
    def _():
        gather(pos_ref, 0, True)

    @pl.when(i + 1 < steps)
    def _():
        gather(pos_next_ref, 1 - slot, True)

    gather(pos_ref, slot, False)
    g = g_ref[...]
    y = h_ref[...]
    for k in range(TOP_K):
        y = y + _load_row_tiles(buf.at[slot, k], tokens) * g[:, k:k + 1]
    o_ref[...] = y * lax.rsqrt(jnp.mean(y * y, axis=-1, keepdims=True) + RMS_EPS) * fw_ref[...]


def _combine(h, ys, pos_t, gates_nk, final_w):
    n = h.shape[0]
    tm = COMBINE_ROWS
    steps = n // tm
    smem = lambda imap: pl.BlockSpec((1, 1, tm * TOP_K), imap, memory_space=pltpu.SMEM)
    return pl.pallas_call(
        _combine_kernel,
        grid=(steps,),
        in_specs=[smem(lambda i: (i, 0, 0)),
                  smem(lambda i: (jnp.minimum(i + 1, steps - 1), 0, 0)),
                  pl.BlockSpec((tm, D_MODEL), lambda i: (i, 0)),
                  pl.BlockSpec((tm, TOP_K), lambda i: (i, 0)),
                  pl.BlockSpec((1, D_MODEL), lambda i: (0, 0)),
                  pl.BlockSpec(memory_space=pl.ANY)],
        out_specs=pl.BlockSpec((tm, D_MODEL), lambda i: (i, 0)),
        out_shape=jax.ShapeDtypeStruct((n, D_MODEL), F32),
        scratch_shapes=[pltpu.VMEM((2, TOP_K, tm * ROW_CHUNKS, LANES), F32), pltpu.SemaphoreType.DMA((2,))],
        compiler_params=_cparams("arbitrary"),
        name="combine_norm",
    )(pos_t, pos_t, h, gates_nk, final_w.reshape(1, D_MODEL), ys)


def kernel(x, norm_mix_w, w_in, rel_bias, hgrn_lb_logits, hgrn_norm_w, w_out, norm_ffn_w,
           router_w, router_b, w_gate_up, b_gate_up, w_down, b_down, final_norm_w):
    batch, seq, _ = x.shape
    n = batch * seq
    x2 = x.reshape(n, D_MODEL)

    qa, ka, va, qh, fh, ih, gh = _inproj(x2, norm_mix_w[0], w_in[0])
    attn = _dilated_attention(qa, ka, va, _masked_bias(rel_bias), batch, seq)
    hg = _hgrn(qh, fh, ih, gh, hgrn_lb_logits, hgrn_norm_w[0], batch, seq)
    h, xn, idx_t, gate_t, rank_t, counts = _outproj(attn, hg, x2, w_out[0], norm_ffn_w[0],
                                                    router_w[0], router_b[0])

    tm = FFN_ROWS
    n_blocks = -(-(n * TOP_K) // tm) + N_EXPERTS
    counts = counts.reshape(N_EXPERTS)
    padded = (counts + tm - 1) // tm * tm
    pend = jnp.cumsum(padded)
    pstart = pend - padded
    block_row0 = jnp.arange(n_blocks, dtype=jnp.int32) * tm
    in_block = (block_row0[:, None] >= pstart[None, :]) & (block_row0[:, None] < pend[None, :])
    block_e = jnp.minimum(jnp.sum(pend[None, :] <= block_row0[:, None], axis=1), N_EXPERTS - 1).astype(jnp.int32)
    rows_valid = jnp.sum(jnp.where(in_block, jnp.clip(pstart + counts - block_row0[:, None], 0, tm), 0),
                         axis=1).astype(jnp.int32)

    pos_move, pos_comb = _positions(idx_t, rank_t, pstart.astype(jnp.int32))
    xs = _dispatch_rows(pos_move, xn, n_blocks * tm)
    experts = jnp.arange(N_EXPERTS, dtype=jnp.int32)
    later = jnp.where((experts[None, :] > experts[:, None]) & (counts[None, :] > 0), experts[None, :], N_EXPERTS)
    next_of = jnp.min(later, axis=1)
    next_e = jnp.where(next_of < N_EXPERTS, next_of, -1)[block_e].astype(jnp.int32)
    ys = _expert_ffn(block_e, rows_valid, next_e, xs, w_gate_up[0], b_gate_up[0], w_down[0], b_down[0])
    out = _combine(h, ys, pos_comb, gate_t.T, final_norm_w)
    return out.reshape(batch, seq, D_MODEL)
```

```python
import functools

import numpy as np
import jax
import jax.numpy as jnp
from jax import lax
from jax.experimental import pallas as pl
from jax.experimental.pallas import tpu as pltpu

F32 = jnp.float32
BF16 = jnp.bfloat16
HIGHEST = lax.Precision.HIGHEST

D_MODEL = 1024
ATTN_HEAD_DIM = 64
ATTN_WIDTH = 512
ATTN_HEADS = ATTN_WIDTH // ATTN_HEAD_DIM
DILATION_PATTERNS = ((128, 1), (512, 4), (2048, 16))
ATTN_BLOCK = 128
REL_BUCKETS = 32
REL_MAX_DISTANCE = 2048
HGRN_HEAD_DIM = 128
HGRN_WIDTH = 512
HGRN_HEADS = HGRN_WIDTH // HGRN_HEAD_DIM
HGRN_CHUNK = 64
HGRN_SUB = 8
N_EXPERTS = 32
TOP_K = 4
D_EXPERT = 1024
SWIGLU_LIMIT = 7.0
SWIGLU_ALPHA = 1.702
RMS_EPS = 1e-5
LOG2_E = 1.4426950408889634
IN_PROJ_WIDTH = 3 * ATTN_WIDTH + 4 * HGRN_WIDTH
LANES = 128
HEAD_PAIRS = ATTN_WIDTH // LANES
ROW_CHUNKS = D_MODEL // LANES

INPROJ_ROWS = 512
OUTPROJ_ROWS = 1024
RANK_SPAN = 256
FFN_ROWS = 512
FFN_COLS = 256
COMBINE_ROWS = 256
HGRN_SEQ_BLOCK = 1024
ATTN_UNROLL = 8
MOVE_TOKENS = 2048
MOVE_UNROLL = 8
VMEM_LIMIT = 48 * 1024 * 1024

NT_DIMS = (((1,), (1,)), ((), ()))
TN_DIMS = (((0,), (0,)), ((), ()))


def _cparams(*sem):
    return pltpu.CompilerParams(dimension_semantics=sem, vmem_limit_bytes=VMEM_LIMIT)


def _store_row_tiles(ref, value):
    rows = value.shape[0]
    for c in range(ROW_CHUNKS):
        ref[pl.ds(c, rows, stride=ROW_CHUNKS), :] = value[:, c * LANES:(c + 1) * LANES]


def _load_row_tiles(ref, rows):
    return jnp.concatenate([ref[pl.ds(c, rows, stride=ROW_CHUNKS), :] for c in range(ROW_CHUNKS)], axis=1)


def _inproj_kernel(x_ref, nw_ref, w_ref, qa_ref, ka_ref, va_ref, qh_ref, fh_ref, ih_ref, gh_ref):
    x = x_ref[...]
    xn = x * lax.rsqrt(jnp.mean(x * x, axis=-1, keepdims=True) + RMS_EPS) * nw_ref[...]
    xn = xn.astype(BF16)
    for c, ref in enumerate((qa_ref, ka_ref, va_ref, qh_ref, fh_ref, ih_ref, gh_ref)):
        y = jnp.dot(xn, w_ref[:, c * ATTN_WIDTH:(c + 1) * ATTN_WIDTH], preferred_element_type=F32)
        ref[...] = y.astype(ref.dtype)


def _inproj(x2, norm_w, w_in):
    n = x2.shape[0]
    tm = INPROJ_ROWS
    row = lambda width: pl.BlockSpec((tm, width), lambda i: (i, 0))
    dtypes = (F32, F32, F32, BF16, F32, BF16, F32)
    return pl.pallas_call(
        _inproj_kernel,
        grid=(n // tm,),
        in_specs=[row(D_MODEL),
                  pl.BlockSpec((1, D_MODEL), lambda i: (0, 0)),
                  pl.BlockSpec((D_MODEL, IN_PROJ_WIDTH), lambda i: (0, 0))],
        out_specs=[row(ATTN_WIDTH)] * 7,
        out_shape=[jax.ShapeDtypeStruct((n, ATTN_WIDTH), dt) for dt in dtypes],
        compiler_params=_cparams("arbitrary"),
        name="inproj",
    )(x2, norm_w.reshape(1, D_MODEL), w_in.astype(BF16))


def _t5_causal_bucket(dist):
    n = np.maximum(dist, 0)
    max_exact = REL_BUCKETS // 2
    large = max_exact + (np.log(np.maximum(n, 1) / max_exact)
                         / np.log(REL_MAX_DISTANCE / max_exact)
                         * (REL_BUCKETS - max_exact)).astype(np.int32)
    large = np.minimum(large, REL_BUCKETS - 1)
    return np.where(n < max_exact, n, large).astype(np.int32)


def _bucket_thresholds(max_dist):
    buckets = _t5_causal_bucket(np.arange(max_dist + 1))
    assert np.all(np.diff(buckets) >= 0)
    return [(b, int(np.argmax(buckets >= b))) for b in range(1, REL_BUCKETS) if np.any(buckets >= b)]


def _bias_kernel(rel_ref, o_ref):
    blk = ATTN_BLOCK
    h = pl.program_id(0)
    qi = lax.broadcasted_iota(jnp.int32, (blk, 2 * blk), 0)
    kj = lax.broadcasted_iota(jnp.int32, (blk, 2 * blk), 1)
    dist_sub = blk + qi - kj
    for pi, (window, dilation) in enumerate(DILATION_PATTERNS):
        span = window // dilation
        dist = dist_sub * dilation
        val = jnp.full((blk, 2 * blk), rel_ref[h], F32)
        for b, first_dist in _bucket_thresholds(window):
            val = jnp.where(dist >= first_dist, rel_ref[b * ATTN_HEADS + h], val)
        o_ref[pi, 0] = jnp.where((dist_sub >= 0) & (dist_sub <= span), val * LOG2_E, -1e30)


def _masked_bias(rel_bias):
    blk = ATTN_BLOCK
    npat = len(DILATION_PATTERNS)
    bias = pl.pallas_call(
        _bias_kernel,
        grid=(ATTN_HEADS,),
        in_specs=[pl.BlockSpec(memory_space=pltpu.SMEM)],
        out_specs=pl.BlockSpec((npat, 1, blk, 2 * blk), lambda h: (0, h, 0, 0)),
        out_shape=jax.ShapeDtypeStruct((npat, ATTN_HEADS, blk, 2 * blk), F32),
        compiler_params=_cparams("arbitrary"),
        name="rel_bias",
    )(rel_bias.astype(F32).reshape(REL_BUCKETS * ATTN_HEADS))
    return bias.reshape(npat, HEAD_PAIRS, 2 * blk, 2 * blk)


def _attn_kernel(q_ref, k_ref, v_ref, bias_ref, o_ref, acc_ref, lse_ref, *, seq):
    blk = ATTN_BLOCK
    lane = lax.broadcasted_iota(jnp.int32, (1, LANES), 1)
    first = lane < ATTN_HEAD_DIM
    scale = ATTN_HEAD_DIM ** -0.5 * LOG2_E
    sel0 = jnp.where(first, scale, 0.0)
    sel1 = jnp.where(first, 0.0, scale)
    ones_cols = jnp.ones((2 * blk, LANES), BF16)

    def key_rows(ref, dil, start):
        return ref[0, pl.ds(start, blk, stride=dil), :].astype(BF16)

    def one_block(pi, dil, q_start, kk, vv):
        n_keys = kk.shape[0]
        qb = q_ref[0, pl.ds(q_start, blk, stride=dil), :]
        q2 = jnp.concatenate([qb * sel0, qb * sel1], axis=0).astype(BF16)
        s = lax.dot_general(q2, kk, NT_DIMS, preferred_element_type=F32)
        s = s + bias_ref[pi, 0, :, 2 * blk - n_keys:]
        m = jnp.max(s, axis=-1, keepdims=True)
        p = jnp.exp2(s - m).astype(BF16)
        ov = jnp.dot(p, jnp.concatenate([vv, ones_cols[:n_keys]], axis=1), preferred_element_type=F32)
        z = ov[:, LANES:]
        o = ov[:, :LANES] / z
        lse = m + jnp.log2(z)
        rows = pl.ds(q_start, blk, stride=dil)
        acc_ref[pi, rows, :] = jnp.where(first, o[:blk], o[blk:])
        lse_ref[pi, rows, :] = jnp.where(first, lse[:blk], lse[blk:])

    for pi, (_, dil) in enumerate(DILATION_PATTERNS):
        nb = seq // (dil * blk)

        def residue(r, carry, pi=pi, dil=dil, nb=nb):
            k0, v0 = key_rows(k_ref, dil, r), key_rows(v_ref, dil, r)
            one_block(pi, dil, r, k0, v0)

            def body(j, prev):
                k_prev, v_prev = prev
                q_start = r + j * (blk * dil)
                k_cur, v_cur = key_rows(k_ref, dil, q_start), key_rows(v_ref, dil, q_start)
                one_block(pi, dil, q_start, jnp.concatenate([k_prev, k_cur], axis=0),
                          jnp.concatenate([v_prev, v_cur], axis=0))
                return k_cur, v_cur

            lax.fori_loop(1, nb, body, (k0, v0), unroll=ATTN_UNROLL)
            return carry

        lax.fori_loop(0, dil, residue, 0, unroll=ATTN_UNROLL if nb == 2 else 1)

    def merge(i, carry):
        rows = pl.ds(pl.multiple_of(i * blk, blk), blk)
        ls = [lse_ref[pi, rows, :] for pi in range(len(DILATION_PATTERNS))]
        m = functools.reduce(jnp.maximum, ls)
        es = [jnp.exp2(l - m) for l in ls]
        num = sum(e * acc_ref[pi, rows, :] for pi, e in enumerate(es))
        o_ref[0, rows, :] = (num / sum(es)).astype(o_ref.dtype)
        return carry

    lax.fori_loop(0, seq // blk, merge, 0)


def _dilated_attention(qa, ka, va, bias, batch, seq):
    npat = len(DILATION_PATTERNS)
    shp = (batch, seq, ATTN_WIDTH)
    spec = pl.BlockSpec((1, seq, LANES), lambda b, p: (b, 0, p))
    out = pl.pallas_call(
        functools.partial(_attn_kernel, seq=seq),
        grid=(batch, HEAD_PAIRS),
        in_specs=[spec, spec, spec,
                  pl.BlockSpec((npat, 1, 2 * ATTN_BLOCK, 2 * ATTN_BLOCK), lambda b, p: (0, p, 0, 0))],
        out_specs=spec,
        out_shape=jax.ShapeDtypeStruct(shp, BF16),
        scratch_shapes=[pltpu.VMEM((npat, seq, LANES), F32), pltpu.VMEM((npat, seq, LANES), F32)],
        compiler_params=_cparams("arbitrary", "arbitrary"),
        name="dilated_attn",
    )(qa.reshape(shp), ka.reshape(shp), va.reshape(shp), bias)
    return out.reshape(batch * seq, ATTN_WIDTH)


def _hgrn_kernel(q_ref, f_ref, i_ref, g_ref, lbl_ref, nw_ref, o_ref, st_ref, *, n_chunks):
    c, sub, dk = HGRN_CHUNK, HGRN_SUB, HGRN_HEAD_DIM
    nsub = c // sub
    lbl = lbl_ref[...]
    e = jnp.exp(lbl - jnp.max(lbl, axis=0, keepdims=True))
    lb_all = e[0:1] / jnp.sum(e, axis=0, keepdims=True)
    nw = nw_ref[...]

    r64 = lax.broadcasted_iota(jnp.int32, (c, c), 0)
    c64 = lax.broadcasted_iota(jnp.int32, (c, c), 1)
    sub_end = (r64 // sub) * sub + (sub - 1)
    decay_sums = jnp.concatenate([(c64 <= r64), (c64 > r64) & (c64 <= sub_end), (c64 > r64)],
                                 axis=0).astype(BF16)
    col_sub = c64 // sub
    col_in_sub = c64 - (r64 // sub) * sub
    t_iota = lax.broadcasted_iota(jnp.int32, (nsub, sub, dk), 1)
    ones_rhs = jnp.ones((dk, c), BF16)

    @pl.when(pl.program_id(1) == 0)
    def _():
        st_ref[...] = jnp.zeros_like(st_ref)

    def cols_of(hd):
        return slice(hd * dk, (hd + 1) * dk)

    def gates(rows, hd):
        cols = cols_of(hd)
        lb = lb_all[:, cols]
        q = q_ref[0, rows, cols].astype(F32)
        f = lb + (1.0 - lb) * jax.nn.sigmoid(f_ref[0, rows, cols])
        log_f = jnp.log2(f)
        log_hi = log_f.astype(BF16)
        log_lo = (log_f - log_hi.astype(F32)).astype(BF16)
        sums = (jnp.dot(decay_sums, log_hi, preferred_element_type=F32)
                + jnp.dot(decay_sums, log_lo, preferred_element_type=F32))
        return sums, 1.0 - f, q * jax.nn.sigmoid(q) * (dk ** -0.5)

    def products(rows, hd, sums, key, qf):
        cum, to_sub_end, to_chunk_end = sums[:c], sums[c:2 * c], sums[2 * c:]
        v = i_ref[0, rows, cols_of(hd)]
        st = st_ref[hd]
        o = lax.dot_general((qf * jnp.exp2(cum)).astype(BF16), st.astype(BF16), NT_DIMS,
                            preferred_element_type=F32)
        kd = (key * jnp.exp2(to_chunk_end)).astype(BF16)
        st_ref[hd] = (st * jnp.exp2(cum[c - 1:c, :])
                      + lax.dot_general(v, kd, TN_DIMS, preferred_element_type=F32))

        cum3 = cum.reshape(nsub, sub, dk)
        key3 = key.reshape(nsub, sub, dk)
        qf3 = qf.reshape(nsub, sub, dk)
        khat = (key * jnp.exp2(to_sub_end)).astype(BF16)
        qs = []
        for j in range(nsub - 1):
            t0 = sub * (j + 1)
            qs.append(qf[t0:] * jnp.exp2(cum[t0:] - cum[t0 - 1:t0, :]))
        prod = lax.dot_general(jnp.concatenate(qs, axis=0).astype(BF16), khat, NT_DIMS,
                               preferred_element_type=F32)

        ws = []
        for s in range(sub):
            dec = jnp.exp2(jnp.where(t_iota >= s, cum3 - cum3[:, s:s + 1, :], -jnp.inf))
            ws.append((qf3 * key3[:, s:s + 1, :] * dec).reshape(c, dk))
        pair_sums = jnp.dot(jnp.concatenate(ws, axis=0).astype(BF16), ones_rhs,
                            preferred_element_type=F32)
        return o, prod, pair_sums

    def finish(rows, hd, o, prod, pair_sums):
        cols = cols_of(hd)
        a = jnp.zeros((c, c), F32)
        r = 0
        for j in range(nsub - 1):
            t0 = sub * (j + 1)
            pj = jnp.concatenate([jnp.zeros((t0, c), F32), prod[r:r + c - t0, :]], axis=0)
            a = jnp.where(col_sub == j, pj, a)
            r += c - t0
        for s in range(sub):
            a = jnp.where(col_in_sub == s, pair_sums[s * c:(s + 1) * c, :], a)
        o = o + jnp.dot(a.astype(BF16), i_ref[0, rows, cols], preferred_element_type=F32)
        on = o * lax.rsqrt(jnp.mean(o * o, axis=-1, keepdims=True) + RMS_EPS) * nw
        g = g_ref[0, rows, cols]
        o_ref[0, rows, cols] = (on * (g * jax.nn.sigmoid(g))).astype(BF16)

    def body(ci, carry):
        rows = pl.ds(pl.multiple_of(ci * c, c), c)
        heads = range(HGRN_HEADS)
        stage1 = [gates(rows, hd) for hd in heads]
        stage2 = [products(rows, hd, *stage1[hd]) for hd in heads]
        for hd in heads:
            finish(rows, hd, *stage2[hd])
        return carry

    lax.fori_loop(0, n_chunks, body, 0, unroll=2)


def _hgrn(qh, fh, ih, gh, lb_logits, norm_w, batch, seq):
    sb = min(seq, HGRN_SEQ_BLOCK)
    spec = pl.BlockSpec((1, sb, HGRN_WIDTH), lambda b, s: (b, s, 0))
    slots = lb_logits.shape[0]
    shp = (batch, seq, HGRN_WIDTH)
    out = pl.pallas_call(
        functools.partial(_hgrn_kernel, n_chunks=sb // HGRN_CHUNK),
        grid=(batch, seq // sb),
        in_specs=[spec, spec, spec, spec,
                  pl.BlockSpec((slots, HGRN_WIDTH), lambda b, s: (0, 0)),
                  pl.BlockSpec((1, HGRN_HEAD_DIM), lambda b, s: (0, 0))],
        out_specs=spec,
        out_shape=jax.ShapeDtypeStruct(shp, BF16),
        scratch_shapes=[pltpu.VMEM((HGRN_HEADS, HGRN_HEAD_DIM, HGRN_HEAD_DIM), F32)],
        compiler_params=_cparams("arbitrary", "arbitrary"),
        name="hgrn2",
    )(qh.reshape(shp), fh.reshape(shp), ih.reshape(shp), gh.reshape(shp),
      lb_logits.astype(F32), norm_w.reshape(1, HGRN_HEAD_DIM))
    return out.reshape(batch * seq, HGRN_WIDTH)


def _outproj_kernel(attn_ref, hg_ref, x_ref, w_ref, nw_ref, rw_ref, rb_ref,
                    h_ref, xn_ref, idx_ref, gate_ref, rank_ref, cnt_ref, run_ref):
    y = jnp.dot(attn_ref[...], w_ref[0:ATTN_WIDTH, :], preferred_element_type=F32)
    y = y + jnp.dot(hg_ref[...], w_ref[ATTN_WIDTH:, :], preferred_element_type=F32)
    h = x_ref[...] + y
    h_ref[...] = h
    xn = h * lax.rsqrt(jnp.mean(h * h, axis=-1, keepdims=True) + RMS_EPS) * nw_ref[...]
    _store_row_tiles(xn_ref, xn)

    logits = lax.dot_general(rw_ref[...], xn, NT_DIMS, precision=HIGHEST,
                             preferred_element_type=F32) + rb_ref[...]
    eid = lax.broadcasted_iota(jnp.int32, logits.shape, 0)
    vals, idxs = [], []
    for _ in range(TOP_K):
        mx = jnp.max(logits, axis=0, keepdims=True)
        ix = jnp.min(jnp.where(logits == mx, eid, N_EXPERTS), axis=0, keepdims=True)
        vals.append(mx)
        idxs.append(ix)
        logits = jnp.where(eid == ix, -jnp.inf, logits)
    es = [jnp.exp(v - vals[0]) for v in vals]
    den = es[0] + es[1] + es[2] + es[3]
    idx_ref[...] = jnp.concatenate(idxs, axis=0)
    gate_ref[...] = jnp.concatenate([e / den for e in es], axis=0)

    @pl.when(pl.program_id(0) == 0)
    def _():
        run_ref[...] = jnp.zeros_like(run_ref)

    span = RANK_SPAN
    onehots = [(eid == ix).astype(F32) for ix in idxs]
    routed = onehots[0] + onehots[1] + onehots[2] + onehots[3]
    earlier = (lax.broadcasted_iota(jnp.int32, (span, span), 0)
               < lax.broadcasted_iota(jnp.int32, (span, span), 1)).astype(BF16)
    run = run_ref[...]
    ranks = []
    for c0 in range(0, logits.shape[1], span):
        part = routed[:, c0:c0 + span]
        before = jnp.dot(part.astype(BF16), earlier, preferred_element_type=F32) + run
        ranks.append(jnp.concatenate([jnp.sum(o[:, c0:c0 + span] * before, axis=0, keepdims=True)
                                      for o in onehots], axis=0))
        run = run + jnp.sum(part, axis=1, keepdims=True)
    rank_ref[...] = jnp.concatenate(ranks, axis=1).astype(jnp.int32)
    run_ref[...] = run
    cnt_ref[...] = run.astype(jnp.int32)


def _outproj(attn, hg, x2, w_out, norm_w, router_w, router_b):
    n = x2.shape[0]
    tm = OUTPROJ_ROWS
    row = lambda width: pl.BlockSpec((tm, width), lambda i: (i, 0))
    full = lambda a, b: pl.BlockSpec((a, b), lambda i: (0, 0))
    tok = pl.BlockSpec((TOP_K, tm), lambda i: (0, i))
    return pl.pallas_call(
        _outproj_kernel,
        grid=(n // tm,),
        in_specs=[row(ATTN_WIDTH), row(HGRN_WIDTH), row(D_MODEL), full(D_MODEL, D_MODEL),
                  full(1, D_MODEL), full(N_EXPERTS, D_MODEL), full(N_EXPERTS, 1)],
        out_specs=[row(D_MODEL), pl.BlockSpec((tm * ROW_CHUNKS, LANES), lambda i: (i, 0)),
                   tok, tok, tok, full(N_EXPERTS, 1)],
        out_shape=[jax.ShapeDtypeStruct((n, D_MODEL), F32),
                   jax.ShapeDtypeStruct((n * ROW_CHUNKS, LANES), F32),
                   jax.ShapeDtypeStruct((TOP_K, n), jnp.int32),
                   jax.ShapeDtypeStruct((TOP_K, n), F32),
                   jax.ShapeDtypeStruct((TOP_K, n), jnp.int32),
                   jax.ShapeDtypeStruct((N_EXPERTS, 1), jnp.int32)],
        scratch_shapes=[pltpu.VMEM((N_EXPERTS, 1), F32)],
        compiler_params=_cparams("arbitrary"),
        name="outproj_router",
    )(attn, hg, x2, w_out.astype(BF16), norm_w.reshape(1, D_MODEL),
      router_w.T.astype(F32), router_b.astype(F32).reshape(N_EXPERTS, 1))


def _ffn_kernel(be_ref, rv_ref, nx_ref, xs_ref, wgu_hbm, bgu_ref, wdn_hbm, bdn_ref, o_ref,
                wgu_f32, wdn_f32, wgu_bf, wdn_bf, sems):
    i = pl.program_id(0)
    rows_valid = rv_ref[i]
    new_expert = (i == 0) | (be_ref[i] != be_ref[jnp.maximum(i - 1, 0)])

    def weight_copies(e):
        return (pltpu.make_async_copy(wgu_hbm.at[e], wgu_f32, sems.at[0]),
                pltpu.make_async_copy(wdn_hbm.at[e], wdn_f32, sems.at[1]))

    @pl.when(i == 0)
    def _():
        for cp in weight_copies(be_ref[0]):
            cp.start()

    @pl.when(new_expert & (rows_valid > 0))
    def _():
        for cp in weight_copies(be_ref[i]):
            cp.wait()
        step = 128
        for r in range(0, D_MODEL, step):
            wgu_bf[r:r + step, :] = wgu_f32[r:r + step, :].astype(BF16)
        for r in range(0, D_EXPERT, step):
            wdn_bf[r:r + step, :] = wdn_f32[r:r + step, :].astype(BF16)

        @pl.when(nx_ref[i] >= 0)
        def _():
            for cp in weight_copies(nx_ref[i]):
                cp.start()

    @pl.when(rows_valid > 0)
    def _():
        live = lax.broadcasted_iota(jnp.int32, (FFN_ROWS, 1), 0) < rows_valid
        x = jnp.where(live, _load_row_tiles(xs_ref, FFN_ROWS), 0.0).astype(BF16)
        def gate_up(c0):
            g_cols = slice(c0, c0 + FFN_COLS)
            u_cols = slice(D_EXPERT + c0, D_EXPERT + c0 + FFN_COLS)
            gate = jnp.dot(x, wgu_bf[:, g_cols], preferred_element_type=F32) + bgu_ref[0, :, g_cols]
            up = jnp.dot(x, wgu_bf[:, u_cols], preferred_element_type=F32) + bgu_ref[0, :, u_cols]
            return gate, up

        chunks = list(range(0, D_EXPERT, FFN_COLS))
        out = bdn_ref[0]
        pending = gate_up(chunks[0])
        for n, c0 in enumerate(chunks):
            gate, up = pending
            if n + 1 < len(chunks):
                pending = gate_up(chunks[n + 1])
            gate = jnp.minimum(gate, SWIGLU_LIMIT)
            up = jnp.clip(up, -SWIGLU_LIMIT, SWIGLU_LIMIT)
            act = gate * jax.nn.sigmoid(SWIGLU_ALPHA * gate) * (up + 1.0)
            out = out + jnp.dot(act.astype(BF16), wdn_bf[c0:c0 + FFN_COLS, :], preferred_element_type=F32)
        _store_row_tiles(o_ref, out)

    @pl.when(rows_valid <= 0)
    def _():
        o_ref[...] = jnp.zeros_like(o_ref)


def _expert_ffn(block_e, rows_valid, next_e, xs, w_gu, b_gu, w_dn, b_dn):
    tm = FFN_ROWS
    tiles = pl.BlockSpec((tm * ROW_CHUNKS, LANES), lambda i, be, rv, nx: (i, 0))
    grid_spec = pltpu.PrefetchScalarGridSpec(
        num_scalar_prefetch=3,
        grid=(xs.shape[0] // (tm * ROW_CHUNKS),),
        in_specs=[tiles,
                  pl.BlockSpec(memory_space=pl.ANY),
                  pl.BlockSpec((1, 1, 2 * D_EXPERT), lambda i, be, rv, nx: (be[i], 0, 0)),
                  pl.BlockSpec(memory_space=pl.ANY),
                  pl.BlockSpec((1, 1, D_MODEL), lambda i, be, rv, nx: (be[i], 0, 0))],
        out_specs=tiles,
        scratch_shapes=[pltpu.VMEM((D_MODEL, 2 * D_EXPERT), F32), pltpu.VMEM((D_EXPERT, D_MODEL), F32),
                        pltpu.VMEM((D_MODEL, 2 * D_EXPERT), BF16), pltpu.VMEM((D_EXPERT, D_MODEL), BF16),
                        pltpu.SemaphoreType.DMA((2,))],
    )
    return pl.pallas_call(
        _ffn_kernel,
        grid_spec=grid_spec,
        out_shape=jax.ShapeDtypeStruct(xs.shape, F32),
        compiler_params=_cparams("arbitrary"),
        name="expert_ffn",
    )(block_e, rows_valid, next_e, xs, w_gu, b_gu.reshape(N_EXPERTS, 1, 2 * D_EXPERT),
      w_dn, b_dn.reshape(N_EXPERTS, 1, D_MODEL))


def _pos_kernel(idx_ref, rank_ref, pstart_ref, move_ref, comb_ref):
    eid = lax.broadcasted_iota(jnp.int32, (N_EXPERTS, idx_ref.shape[1]), 0)
    pstart = pstart_ref[...]
    rows = [jnp.sum(jnp.where(eid == idx_ref[k:k + 1, :], pstart, 0), axis=0, keepdims=True)
            for k in range(TOP_K)]
    pos = (jnp.concatenate(rows, axis=0) + rank_ref[...]) * ROW_CHUNKS
    for k in range(TOP_K):
        move_ref[0, :, k * MOVE_TOKENS:(k + 1) * MOVE_TOKENS] = pos[k:k + 1, :]
        for j in range(MOVE_TOKENS // COMBINE_ROWS):
            comb_ref[j, :, k * COMBINE_ROWS:(k + 1) * COMBINE_ROWS] = pos[k:k + 1, j * COMBINE_ROWS:(j + 1) * COMBINE_ROWS]


def _positions(idx_t, rank_t, pstart):
    n = idx_t.shape[1]
    tl = MOVE_TOKENS
    per = tl // COMBINE_ROWS
    tok = pl.BlockSpec((TOP_K, tl), lambda i: (0, i))
    return pl.pallas_call(
        _pos_kernel,
        grid=(n // tl,),
        in_specs=[tok, tok, pl.BlockSpec((N_EXPERTS, 1), lambda i: (0, 0))],
        out_specs=[pl.BlockSpec((1, 1, TOP_K * tl), lambda i: (i, 0, 0)),
                   pl.BlockSpec((per, 1, TOP_K * COMBINE_ROWS), lambda i: (i, 0, 0))],
        out_shape=[jax.ShapeDtypeStruct((n // tl, 1, TOP_K * tl), jnp.int32),
                   jax.ShapeDtypeStruct((n // COMBINE_ROWS, 1, TOP_K * COMBINE_ROWS), jnp.int32)],
        compiler_params=_cparams("arbitrary"),
        name="dispatch_pos",
    )(idx_t, rank_t, pstart.reshape(N_EXPERTS, 1))


def _row_tile(ref, row_offset):
    return ref.at[pl.ds(pl.multiple_of(row_offset, ROW_CHUNKS), ROW_CHUNKS)]


def _dispatch_kernel(pos_ref, x_ref, xs_ref, sem):
    tokens = x_ref.shape[0] // ROW_CHUNKS

    def copy(t, k):
        return pltpu.make_async_copy(_row_tile(x_ref, t * ROW_CHUNKS),
                                     _row_tile(xs_ref, pos_ref[0, 0, k * tokens + t]), sem)

    def issue(t, carry):
        for k in range(TOP_K):
            copy(t, k).start(priority=k % 2)
        return carry

    def drain(t, carry):
        for k in range(TOP_K):
            copy(t, k).wait()
        return carry

    lax.fori_loop(0, tokens, issue, 0, unroll=MOVE_UNROLL)
    lax.fori_loop(0, tokens, drain, 0, unroll=MOVE_UNROLL)


def _dispatch_rows(pos_tiles, xn, out_rows):
    n = xn.shape[0] // ROW_CHUNKS
    tl = MOVE_TOKENS
    return pl.pallas_call(
        _dispatch_kernel,
        grid=(n // tl,),
        in_specs=[pl.BlockSpec((1, 1, tl * TOP_K), lambda i: (i, 0, 0), memory_space=pltpu.SMEM),
                  pl.BlockSpec((tl * ROW_CHUNKS, LANES), lambda i: (i, 0))],
        out_specs=pl.BlockSpec(memory_space=pl.ANY),
        out_shape=jax.ShapeDtypeStruct((out_rows * ROW_CHUNKS, LANES), xn.dtype),
        scratch_shapes=[pltpu.SemaphoreType.DMA(())],
        compiler_params=_cparams("arbitrary"),
        name="dispatch_rows",
    )(pos_tiles, xn)


def _combine_kernel(pos_ref, pos_next_ref, h_ref, g_ref, fw_ref, ys_ref, o_ref, buf, sems):
    i = pl.program_id(0)
    steps = pl.num_programs(0)
    tokens = h_ref.shape[0]
    slot = lax.rem(i, 2)

    def gather(p_ref, s, start):
        def body(t, carry):
            for k in range(TOP_K):
                cp = pltpu.make_async_copy(_row_tile(ys_ref, p_ref[0, 0, k * tokens + t]),
                                           _row_tile(buf.at[s, k], t * ROW_CHUNKS), sems.at[s])
                cp.start(priority=k % 2) if start else cp.wait()
            return carry
        lax.fori_loop(0, tokens, body, 0, unroll=MOVE_UNROLL)

    @pl.when(i == 0)
    def _():
        gather(pos_ref, 0, True)

    @pl.when(i + 1 < steps)
    def _():
        gather(pos_next_ref, 1 - slot, True)

    gather(pos_ref, slot, False)
    g = g_ref[...]
    y = h_ref[...]
    for k in range(TOP_K):
        y = y + _load_row_tiles(buf.at[slot, k], tokens) * g[:, k:k + 1]
    o_ref[...] = y * lax.rsqrt(jnp.mean(y * y, axis=-1, keepdims=True) + RMS_EPS) * fw_ref[...]


def _combine(h, ys, pos_t, gates_nk, final_w):
    n = h.shape[0]
    tm = COMBINE_ROWS
    steps = n // tm
    smem = lambda imap: pl.BlockSpec((1, 1, tm * TOP_K), imap, memory_space=pltpu.SMEM)
    return pl.pallas_call(
        _combine_kernel,
        grid=(steps,),
        in_specs=[smem(lambda i: (i, 0, 0)),
                  smem(lambda i: (jnp.minimum(i + 1, steps - 1), 0, 0)),
                  pl.BlockSpec((tm, D_MODEL), lambda i: (i, 0)),
                  pl.BlockSpec((tm, TOP_K), lambda i: (i, 0)),
                  pl.BlockSpec((1, D_MODEL), lambda i: (0, 0)),
                  pl.BlockSpec(memory_space=pl.ANY)],
        out_specs=pl.BlockSpec((tm, D_MODEL), lambda i: (i, 0)),
        out_shape=jax.ShapeDtypeStruct((n, D_MODEL), F32),
        scratch_shapes=[pltpu.VMEM((2, TOP_K, tm * ROW_CHUNKS, LANES), F32), pltpu.SemaphoreType.DMA((2,))],
        compiler_params=_cparams("arbitrary"),
        name="combine_norm",
    )(pos_t, pos_t, h, gates_nk, final_w.reshape(1, D_MODEL), ys)


def kernel(x, norm_mix_w, w_in, rel_bias, hgrn_lb_logits, hgrn_norm_w, w_out, norm_ffn_w,
           router_w, router_b, w_gate_up, b_gate_up, w_down, b_down, final_norm_w):
    batch, seq, _ = x.shape
    n = batch * seq
    x2 = x.reshape(n, D_MODEL)

    qa, ka, va, qh, fh, ih, gh = _inproj(x2, norm_mix_w[0], w_in[0])
    attn = _dilated_attention(qa, ka, va, _masked_bias(rel_bias), batch, seq)
    hg = _hgrn(qh, fh, ih, gh, hgrn_lb_logits, hgrn_norm_w[0], batch, seq)
    h, xn, idx_t, gate_t, rank_t, counts = _outproj(attn, hg, x2, w_out[0], norm_ffn_w[0],
                                                    router_w[0], router_b[0])

    tm = FFN_ROWS
    n_blocks = -(-(n * TOP_K) // tm) + N_EXPERTS
    counts = counts.reshape(N_EXPERTS)
    padded = (counts + tm - 1) // tm * tm
    pend = jnp.cumsum(padded)
    pstart = pend - padded
    block_row0 = jnp.arange(n_blocks, dtype=jnp.int32) * tm
    in_block = (block_row0[:, None] >= pstart[None, :]) & (block_row0[:, None] < pend[None, :])
    block_e = jnp.minimum(jnp.sum(pend[None, :] <= block_row0[:, None], axis=1), N_EXPERTS - 1).astype(jnp.int32)
    rows_valid = jnp.sum(jnp.where(in_block, jnp.clip(pstart + counts - block_row0[:, None], 0, tm), 0),
                         axis=1).astype(jnp.int32)

    pos_move, pos_comb = _positions(idx_t, rank_t, pstart.astype(jnp.int32))
    xs = _dispatch_rows(pos_move, xn, n_blocks * tm)
    experts = jnp.arange(N_EXPERTS, dtype=jnp.int32)
    later = jnp.where((experts[None, :] > experts[:, None]) & (counts[None, :] > 0), experts[None, :], N_EXPERTS)
    next_of = jnp.min(later, axis=1)
    next_e = jnp.where(next_of < N_EXPERTS, next_of, -1)[block_e].astype(jnp.int32)
    ys = _expert_ffn(block_e, rows_valid, next_e, xs, w_gate_up[0], b_gate_up[0], w_down[0], b_down[0])
    out = _combine(h, ys, pos_comb, gate_t.T, final_norm_w)
    return out.reshape(batch, seq, D_MODEL)
```

```python
import functools

import numpy as np
import jax
import jax.numpy as jnp
from jax import lax
from jax.experimental import pallas as pl
from jax.experimental.pallas import tpu as pltpu

F32 = jnp.float32
BF16 = jnp.bfloat16
HIGHEST = lax.Precision.HIGHEST

D_MODEL = 1024
ATTN_HEAD_DIM = 64
ATTN_WIDTH = 512
ATTN_HEADS = ATTN_WIDTH // ATTN_HEAD_DIM
DILATION_PATTERNS = ((128, 1), (512, 4), (2048, 16))
ATTN_BLOCK = 128
REL_BUCKETS = 32
REL_MAX_DISTANCE = 2048
HGRN_HEAD_DIM = 128
HGRN_WIDTH = 512
HGRN_HEADS = HGRN_WIDTH // HGRN_HEAD_DIM
HGRN_CHUNK = 64
HGRN_SUB = 8
N_EXPERTS = 32
TOP_K = 4
D_EXPERT = 1024
SWIGLU_LIMIT = 7.0
SWIGLU_ALPHA = 1.702
RMS_EPS = 1e-5
LOG2_E = 1.4426950408889634
IN_PROJ_WIDTH = 3 * ATTN_WIDTH + 4 * HGRN_WIDTH
LANES = 128
HEAD_PAIRS = ATTN_WIDTH // LANES
ROW_CHUNKS = D_MODEL // LANES

INPROJ_ROWS = 512
OUTPROJ_ROWS = 1024
RANK_SPAN = 256
FFN_ROWS = 512
FFN_COLS = 512
COMBINE_ROWS = 256
HGRN_SEQ_BLOCK = 1024
ATTN_UNROLL = 8
MOVE_TOKENS = 2048
MOVE_UNROLL = 8
VMEM_LIMIT = 48 * 1024 * 1024

NT_DIMS = (((1,), (1,)), ((), ()))
TN_DIMS = (((0,), (0,)), ((), ()))


def _cparams(*sem):
    return pltpu.CompilerParams(dimension_semantics=sem, vmem_limit_bytes=VMEM_LIMIT)


def _store_row_tiles(ref, value):
    rows = value.shape[0]
    for c in range(ROW_CHUNKS):
        ref[pl.ds(c, rows, stride=ROW_CHUNKS), :] = value[:, c * LANES:(c + 1) * LANES]


def _load_row_tiles(ref, rows):
    return jnp.concatenate([ref[pl.ds(c, rows, stride=ROW_CHUNKS), :] for c in range(ROW_CHUNKS)], axis=1)


def _inproj_kernel(x_ref, nw_ref, w_ref, qa_ref, ka_ref, va_ref, qh_ref, fh_ref, ih_ref, gh_ref):
    x = x_ref[...]
    xn = x * lax.rsqrt(jnp.mean(x * x, axis=-1, keepdims=True) + RMS_EPS) * nw_ref[...]
    xn = xn.astype(BF16)
    for c, ref in enumerate((qa_ref, ka_ref, va_ref, qh_ref, fh_ref, ih_ref, gh_ref)):
        y = jnp.dot(xn, w_ref[:, c * ATTN_WIDTH:(c + 1) * ATTN_WIDTH], preferred_element_type=F32)
        ref[...] = y.astype(ref.dtype)


def _inproj(x2, norm_w, w_in):
    n = x2.shape[0]
    tm = INPROJ_ROWS
    row = lambda width: pl.BlockSpec((tm, width), lambda i: (i, 0))
    dtypes = (F32, F32, F32, BF16, F32, BF16, F32)
    return pl.pallas_call(
        _inproj_kernel,
        grid=(n // tm,),
        in_specs=[row(D_MODEL),
                  pl.BlockSpec((1, D_MODEL), lambda i: (0, 0)),
                  pl.BlockSpec((D_MODEL, IN_PROJ_WIDTH), lambda i: (0, 0))],
        out_specs=[row(ATTN_WIDTH)] * 7,
        out_shape=[jax.ShapeDtypeStruct((n, ATTN_WIDTH), dt) for dt in dtypes],
        compiler_params=_cparams("arbitrary"),
        name="inproj",
    )(x2, norm_w.reshape(1, D_MODEL), w_in.astype(BF16))


def _t5_causal_bucket(dist):
    n = np.maximum(dist, 0)
    max_exact = REL_BUCKETS // 2
    large = max_exact + (np.log(np.maximum(n, 1) / max_exact)
                         / np.log(REL_MAX_DISTANCE / max_exact)
                         * (REL_BUCKETS - max_exact)).astype(np.int32)
    large = np.minimum(large, REL_BUCKETS - 1)
    return np.where(n < max_exact, n, large).astype(np.int32)


def _bucket_thresholds(max_dist):
    buckets = _t5_causal_bucket(np.arange(max_dist + 1))
    assert np.all(np.diff(buckets) >= 0)
    return [(b, int(np.argmax(buckets >= b))) for b in range(1, REL_BUCKETS) if np.any(buckets >= b)]


def _bias_kernel(rel_ref, o_ref):
    blk = ATTN_BLOCK
    h = pl.program_id(0)
    qi = lax.broadcasted_iota(jnp.int32, (blk, 2 * blk), 0)
    kj = lax.broadcasted_iota(jnp.int32, (blk, 2 * blk), 1)
    dist_sub = blk + qi - kj
    for pi, (window, dilation) in enumerate(DILATION_PATTERNS):
        span = window // dilation
        dist = dist_sub * dilation
        val = jnp.full((blk, 2 * blk), rel_ref[h], F32)
        for b, first_dist in _bucket_thresholds(window):
            val = jnp.where(dist >= first_dist, rel_ref[b * ATTN_HEADS + h], val)
        o_ref[pi, 0] = jnp.where((dist_sub >= 0) & (dist_sub <= span), val * LOG2_E, -1e30)


def _masked_bias(rel_bias):
    blk = ATTN_BLOCK
    npat = len(DILATION_PATTERNS)
    bias = pl.pallas_call(
        _bias_kernel,
        grid=(ATTN_HEADS,),
        in_specs=[pl.BlockSpec(memory_space=pltpu.SMEM)],
        out_specs=pl.BlockSpec((npat, 1, blk, 2 * blk), lambda h: (0, h, 0, 0)),
        out_shape=jax.ShapeDtypeStruct((npat, ATTN_HEADS, blk, 2 * blk), F32),
        compiler_params=_cparams("arbitrary"),
        name="rel_bias",
    )(rel_bias.astype(F32).reshape(REL_BUCKETS * ATTN_HEADS))
    return bias.reshape(npat, HEAD_PAIRS, 2 * blk, 2 * blk)


def _attn_kernel(q_ref, k_ref, v_ref, bias_ref, o_ref, acc_ref, lse_ref, *, seq):
    blk = ATTN_BLOCK
    lane = lax.broadcasted_iota(jnp.int32, (1, LANES), 1)
    first = lane < ATTN_HEAD_DIM
    scale = ATTN_HEAD_DIM ** -0.5 * LOG2_E
    sel0 = jnp.where(first, scale, 0.0)
    sel1 = jnp.where(first, 0.0, scale)
    ones_cols = jnp.ones((2 * blk, LANES), BF16)

    def key_rows(ref, dil, start):
        return ref[0, pl.ds(start, blk, stride=dil), :].astype(BF16)

    def one_block(pi, dil, q_start, kk, vv):
        n_keys = kk.shape[0]
        qb = q_ref[0, pl.ds(q_start, blk, stride=dil), :]
        q2 = jnp.concatenate([qb * sel0, qb * sel1], axis=0).astype(BF16)
        s = lax.dot_general(q2, kk, NT_DIMS, preferred_element_type=F32)
        s = s + bias_ref[pi, 0, :, 2 * blk - n_keys:]
        m = jnp.max(s, axis=-1, keepdims=True)
        p = jnp.exp2(s - m).astype(BF16)
        ov = jnp.dot(p, jnp.concatenate([vv, ones_cols[:n_keys]], axis=1), preferred_element_type=F32)
        z = ov[:, LANES:]
        o = ov[:, :LANES] / z
        lse = m + jnp.log2(z)
        rows = pl.ds(q_start, blk, stride=dil)
        acc_ref[pi, rows, :] = jnp.where(first, o[:blk], o[blk:])
        lse_ref[pi, rows, :] = jnp.where(first, lse[:blk], lse[blk:])

    for pi, (_, dil) in enumerate(DILATION_PATTERNS):
        nb = seq // (dil * blk)

        def residue(r, carry, pi=pi, dil=dil, nb=nb):
            k0, v0 = key_rows(k_ref, dil, r), key_rows(v_ref, dil, r)
            one_block(pi, dil, r, k0, v0)

            def body(j, prev):
                k_prev, v_prev = prev
                q_start = r + j * (blk * dil)
                k_cur, v_cur = key_rows(k_ref, dil, q_start), key_rows(v_ref, dil, q_start)
                one_block(pi, dil, q_start, jnp.concatenate([k_prev, k_cur], axis=0),
                          jnp.concatenate([v_prev, v_cur], axis=0))
                return k_cur, v_cur

            lax.fori_loop(1, nb, body, (k0, v0), unroll=ATTN_UNROLL)
            return carry

        lax.fori_loop(0, dil, residue, 0, unroll=ATTN_UNROLL if nb == 2 else 1)

    def merge(i, carry):
        rows = pl.ds(pl.multiple_of(i * blk, blk), blk)
        ls = [lse_ref[pi, rows, :] for pi in range(len(DILATION_PATTERNS))]
        m = functools.reduce(jnp.maximum, ls)
        es = [jnp.exp2(l - m) for l in ls]
        num = sum(e * acc_ref[pi, rows, :] for pi, e in enumerate(es))
        o_ref[0, rows, :] = (num / sum(es)).astype(o_ref.dtype)
        return carry

    lax.fori_loop(0, seq // blk, merge, 0)


def _dilated_attention(qa, ka, va, bias, batch, seq):
    npat = len(DILATION_PATTERNS)
    shp = (batch, seq, ATTN_WIDTH)
    spec = pl.BlockSpec((1, seq, LANES), lambda b, p: (b, 0, p))
    out = pl.pallas_call(
        functools.partial(_attn_kernel, seq=seq),
        grid=(batch, HEAD_PAIRS),
        in_specs=[spec, spec, spec,
                  pl.BlockSpec((npat, 1, 2 * ATTN_BLOCK, 2 * ATTN_BLOCK), lambda b, p: (0, p, 0, 0))],
        out_specs=spec,
        out_shape=jax.ShapeDtypeStruct(shp, BF16),
        scratch_shapes=[pltpu.VMEM((npat, seq, LANES), F32), pltpu.VMEM((npat, seq, LANES), F32)],
        compiler_params=_cparams("arbitrary", "arbitrary"),
        name="dilated_attn",
    )(qa.reshape(shp), ka.reshape(shp), va.reshape(shp), bias)
    return out.reshape(batch * seq, ATTN_WIDTH)


def _hgrn_kernel(q_ref, f_ref, i_ref, g_ref, lbl_ref, nw_ref, o_ref, st_ref, *, n_chunks):
    c, sub, dk = HGRN_CHUNK, HGRN_SUB, HGRN_HEAD_DIM
    nsub = c // sub
    lbl = lbl_ref[...]
    e = jnp.exp(lbl - jnp.max(lbl, axis=0, keepdims=True))
    lb_all = e[0:1] / jnp.sum(e, axis=0, keepdims=True)
    nw = nw_ref[...]

    r64 = lax.broadcasted_iota(jnp.int32, (c, c), 0)
    c64 = lax.broadcasted_iota(jnp.int32, (c, c), 1)
    sub_end = (r64 // sub) * sub + (sub - 1)
    decay_sums = jnp.concatenate([(c64 <= r64), (c64 > r64) & (c64 <= sub_end), (c64 > r64)],
                                 axis=0).astype(BF16)
    col_sub = c64 // sub
    col_in_sub = c64 - (r64 // sub) * sub
    t_iota = lax.broadcasted_iota(jnp.int32, (nsub, sub, dk), 1)
    ones_rhs = jnp.ones((dk, c), BF16)

    @pl.when(pl.program_id(1) == 0)
    def _():
        st_ref[...] = jnp.zeros_like(st_ref)

    def cols_of(hd):
        return slice(hd * dk, (hd + 1) * dk)

    def gates(rows, hd):
        cols = cols_of(hd)
        lb = lb_all[:, cols]
        q = q_ref[0, rows, cols].astype(F32)
        f = lb + (1.0 - lb) * jax.nn.sigmoid(f_ref[0, rows, cols])
        log_f = jnp.log2(f)
        log_hi = log_f.astype(BF16)
        log_lo = (log_f - log_hi.astype(F32)).astype(BF16)
        sums = (jnp.dot(decay_sums, log_hi, preferred_element_type=F32)
                + jnp.dot(decay_sums, log_lo, preferred_element_type=F32))
        return sums, 1.0 - f, q * jax.nn.sigmoid(q) * (dk ** -0.5)

    def products(rows, hd, sums, key, qf):
        cum, to_sub_end, to_chunk_end = sums[:c], sums[c:2 * c], sums[2 * c:]
        v = i_ref[0, rows, cols_of(hd)]
        st = st_ref[hd]
        o = lax.dot_general((qf * jnp.exp2(cum)).astype(BF16), st.astype(BF16), NT_DIMS,
                            preferred_element_type=F32)
        kd = (key * jnp.exp2(to_chunk_end)).astype(BF16)
        st_ref[hd] = (st * jnp.exp2(cum[c - 1:c, :])
                      + lax.dot_general(v, kd, TN_DIMS, preferred_element_type=F32))

        cum3 = cum.reshape(nsub, sub, dk)
        key3 = key.reshape(nsub, sub, dk)
        qf3 = qf.reshape(nsub, sub, dk)
        khat = (key * jnp.exp2(to_sub_end)).astype(BF16)
        qs = []
        for j in range(nsub - 1):
            t0 = sub * (j + 1)
            qs.append(qf[t0:] * jnp.exp2(cum[t0:] - cum[t0 - 1:t0, :]))
        prod = lax.dot_general(jnp.concatenate(qs, axis=0).astype(BF16), khat, NT_DIMS,
                               preferred_element_type=F32)

        ws = []
        for s in range(sub):
            dec = jnp.exp2(jnp.where(t_iota >= s, cum3 - cum3[:, s:s + 1, :], -jnp.inf))
            ws.append((qf3 * key3[:, s:s + 1, :] * dec).reshape(c, dk))
        pair_sums = jnp.dot(jnp.concatenate(ws, axis=0).astype(BF16), ones_rhs,
                            preferred_element_type=F32)
        return o, prod, pair_sums

    def finish(rows, hd, o, prod, pair_sums):
        cols = cols_of(hd)
        a = jnp.zeros((c, c), F32)
        r = 0
        for j in range(nsub - 1):
            t0 = sub * (j + 1)
            pj = jnp.concatenate([jnp.zeros((t0, c), F32), prod[r:r + c - t0, :]], axis=0)
            a = jnp.where(col_sub == j, pj, a)
            r += c - t0
        for s in range(sub):
            a = jnp.where(col_in_sub == s, pair_sums[s * c:(s + 1) * c, :], a)
        o = o + jnp.dot(a.astype(BF16), i_ref[0, rows, cols], preferred_element_type=F32)
        on = o * lax.rsqrt(jnp.mean(o * o, axis=-1, keepdims=True) + RMS_EPS) * nw
        g = g_ref[0, rows, cols]
        o_ref[0, rows, cols] = (on * (g * jax.nn.sigmoid(g))).astype(BF16)

    def body(ci, carry):
        rows = pl.ds(pl.multiple_of(ci * c, c), c)
        heads = range(HGRN_HEADS)
        stage1 = [gates(rows, hd) for hd in heads]
        stage2 = [products(rows, hd, *stage1[hd]) for hd in heads]
        for hd in heads:
            finish(rows, hd, *stage2[hd])
        return carry

    lax.fori_loop(0, n_chunks, body, 0, unroll=4)


def _hgrn(qh, fh, ih, gh, lb_logits, norm_w, batch, seq):
    sb = min(seq, HGRN_SEQ_BLOCK)
    spec = pl.BlockSpec((1, sb, HGRN_WIDTH), lambda b, s: (b, s, 0))
    slots = lb_logits.shape[0]
    shp = (batch, seq, HGRN_WIDTH)
    out = pl.pallas_call(
        functools.partial(_hgrn_kernel, n_chunks=sb // HGRN_CHUNK),
        grid=(batch, seq // sb),
        in_specs=[spec, spec, spec, spec,
                  pl.BlockSpec((slots, HGRN_WIDTH), lambda b, s: (0, 0)),
                  pl.BlockSpec((1, HGRN_HEAD_DIM), lambda b, s: (0, 0))],
        out_specs=spec,
        out_shape=jax.ShapeDtypeStruct(shp, BF16),
        scratch_shapes=[pltpu.VMEM((HGRN_HEADS, HGRN_HEAD_DIM, HGRN_HEAD_DIM), F32)],
        compiler_params=_cparams("arbitrary", "arbitrary"),
        name="hgrn2",
    )(qh.reshape(shp), fh.reshape(shp), ih.reshape(shp), gh.reshape(shp),
      lb_logits.astype(F32), norm_w.reshape(1, HGRN_HEAD_DIM))
    return out.reshape(batch * seq, HGRN_WIDTH)


def _outproj_kernel(attn_ref, hg_ref, x_ref, w_ref, nw_ref, rw_ref, rb_ref,
                    h_ref, xn_ref, idx_ref, gate_ref, rank_ref, cnt_ref, run_ref):
    y = jnp.dot(attn_ref[...], w_ref[0:ATTN_WIDTH, :], preferred_element_type=F32)
    y = y + jnp.dot(hg_ref[...], w_ref[ATTN_WIDTH:, :], preferred_element_type=F32)
    h = x_ref[...] + y
    h_ref[...] = h
    xn = h * lax.rsqrt(jnp.mean(h * h, axis=-1, keepdims=True) + RMS_EPS) * nw_ref[...]
    _store_row_tiles(xn_ref, xn)

    logits = lax.dot_general(rw_ref[...], xn, NT_DIMS, precision=HIGHEST,
                             preferred_element_type=F32) + rb_ref[...]
    eid = lax.broadcasted_iota(jnp.int32, logits.shape, 0)
    vals, idxs = [], []
    for _ in range(TOP_K):
        mx = jnp.max(logits, axis=0, keepdims=True)
        ix = jnp.min(jnp.where(logits == mx, eid, N_EXPERTS), axis=0, keepdims=True)
        vals.append(mx)
        idxs.append(ix)
        logits = jnp.where(eid == ix, -jnp.inf, logits)
    es = [jnp.exp(v - vals[0]) for v in vals]
    den = es[0] + es[1] + es[2] + es[3]
    idx_ref[...] = jnp.concatenate(idxs, axis=0)
    gate_ref[...] = jnp.concatenate([e / den for e in es], axis=0)

    @pl.when(pl.program_id(0) == 0)
    def _():
        run_ref[...] = jnp.zeros_like(run_ref)

    span = RANK_SPAN
    onehots = [(eid == ix).astype(F32) for ix in idxs]
    routed = onehots[0] + onehots[1] + onehots[2] + onehots[3]
    earlier = (lax.broadcasted_iota(jnp.int32, (span, span), 0)
               < lax.broadcasted_iota(jnp.int32, (span, span), 1)).astype(BF16)
    run = run_ref[...]
    ranks = []
    for c0 in range(0, logits.shape[1], span):
        part = routed[:, c0:c0 + span]
        before = jnp.dot(part.astype(BF16), earlier, preferred_element_type=F32) + run
        ranks.append(jnp.concatenate([jnp.sum(o[:, c0:c0 + span] * before, axis=0, keepdims=True)
                                      for o in onehots], axis=0))
        run = run + jnp.sum(part, axis=1, keepdims=True)
    rank_ref[...] = jnp.concatenate(ranks, axis=1).astype(jnp.int32)
    run_ref[...] = run
    cnt_ref[...] = run.astype(jnp.int32)


def _outproj(attn, hg, x2, w_out, norm_w, router_w, router_b):
    n = x2.shape[0]
    tm = OUTPROJ_ROWS
    row = lambda width: pl.BlockSpec((tm, width), lambda i: (i, 0))
    full = lambda a, b: pl.BlockSpec((a, b), lambda i: (0, 0))
    tok = pl.BlockSpec((TOP_K, tm), lambda i: (0, i))
    return pl.pallas_call(
        _outproj_kernel,
        grid=(n // tm,),
        in_specs=[row(ATTN_WIDTH), row(HGRN_WIDTH), row(D_MODEL), full(D_MODEL, D_MODEL),
                  full(1, D_MODEL), full(N_EXPERTS, D_MODEL), full(N_EXPERTS, 1)],
        out_specs=[row(D_MODEL), pl.BlockSpec((tm * ROW_CHUNKS, LANES), lambda i: (i, 0)),
                   tok, tok, tok, full(N_EXPERTS, 1)],
        out_shape=[jax.ShapeDtypeStruct((n, D_MODEL), F32),
                   jax.ShapeDtypeStruct((n * ROW_CHUNKS, LANES), F32),
                   jax.ShapeDtypeStruct((TOP_K, n), jnp.int32),
                   jax.ShapeDtypeStruct((TOP_K, n), F32),
                   jax.ShapeDtypeStruct((TOP_K, n), jnp.int32),
                   jax.ShapeDtypeStruct((N_EXPERTS, 1), jnp.int32)],
        scratch_shapes=[pltpu.VMEM((N_EXPERTS, 1), F32)],
        compiler_params=_cparams("arbitrary"),
        name="outproj_router",
    )(attn, hg, x2, w_out.astype(BF16), norm_w.reshape(1, D_MODEL),
      router_w.T.astype(F32), router_b.astype(F32).reshape(N_EXPERTS, 1))


def _ffn_kernel(be_ref, rv_ref, nx_ref, xs_ref, wgu_hbm, bgu_ref, wdn_hbm, bdn_ref, o_ref,
                wgu_f32, wdn_f32, wgu_bf, wdn_bf, sems):
    i = pl.program_id(0)
    rows_valid = rv_ref[i]
    new_expert = (i == 0) | (be_ref[i] != be_ref[jnp.maximum(i - 1, 0)])

    def weight_copies(e):
        return (pltpu.make_async_copy(wgu_hbm.at[e], wgu_f32, sems.at[0]),
                pltpu.make_async_copy(wdn_hbm.at[e], wdn_f32, sems.at[1]))

    @pl.when(i == 0)
    def _():
        for cp in weight_copies(be_ref[0]):
            cp.start()

    @pl.when(new_expert & (rows_valid > 0))
    def _():
        for cp in weight_copies(be_ref[i]):
            cp.wait()
        step = 128
        for r in range(0, D_MODEL, step):
            wgu_bf[r:r + step, :] = wgu_f32[r:r + step, :].astype(BF16)
        for r in range(0, D_EXPERT, step):
            wdn_bf[r:r + step, :] = wdn_f32[r:r + step, :].astype(BF16)

        @pl.when(nx_ref[i] >= 0)
        def _():
            for cp in weight_copies(nx_ref[i]):
                cp.start()

    @pl.when(rows_valid > 0)
    def _():
        live = lax.broadcasted_iota(jnp.int32, (FFN_ROWS, 1), 0) < rows_valid
        x = jnp.where(live, _load_row_tiles(xs_ref, FFN_ROWS), 0.0).astype(BF16)
        out = bdn_ref[0]
        for c0 in range(0, D_EXPERT, FFN_COLS):
            g_cols = slice(c0, c0 + FFN_COLS)
            u_cols = slice(D_EXPERT + c0, D_EXPERT + c0 + FFN_COLS)
            gate = jnp.dot(x, wgu_bf[:, g_cols], preferred_element_type=F32) + bgu_ref[0, :, g_cols]
            up = jnp.dot(x, wgu_bf[:, u_cols], preferred_element_type=F32) + bgu_ref[0, :, u_cols]
            gate = jnp.minimum(gate, SWIGLU_LIMIT)
            up = jnp.clip(up, -SWIGLU_LIMIT, SWIGLU_LIMIT)
            act = gate * jax.nn.sigmoid(SWIGLU_ALPHA * gate) * (up + 1.0)
            out = out + jnp.dot(act.astype(BF16), wdn_bf[g_cols, :], preferred_element_type=F32)
        _store_row_tiles(o_ref, out)

    @pl.when(rows_valid <= 0)
    def _():
        o_ref[...] = jnp.zeros_like(o_ref)


def _expert_ffn(block_e, rows_valid, next_e, xs, w_gu, b_gu, w_dn, b_dn):
    tm = FFN_ROWS
    tiles = pl.BlockSpec((tm * ROW_CHUNKS, LANES), lambda i, be, rv, nx: (i, 0))
    grid_spec = pltpu.PrefetchScalarGridSpec(
        num_scalar_prefetch=3,
        grid=(xs.shape[0] // (tm * ROW_CHUNKS),),
        in_specs=[tiles,
                  pl.BlockSpec(memory_space=pl.ANY),
                  pl.BlockSpec((1, 1, 2 * D_EXPERT), lambda i, be, rv, nx: (be[i], 0, 0)),
                  pl.BlockSpec(memory_space=pl.ANY),
                  pl.BlockSpec((1, 1, D_MODEL), lambda i, be, rv, nx: (be[i], 0, 0))],
        out_specs=tiles,
        scratch_shapes=[pltpu.VMEM((D_MODEL, 2 * D_EXPERT), F32), pltpu.VMEM((D_EXPERT, D_MODEL), F32),
                        pltpu.VMEM((D_MODEL, 2 * D_EXPERT), BF16), pltpu.VMEM((D_EXPERT, D_MODEL), BF16),
                        pltpu.SemaphoreType.DMA((2,))],
    )
    return pl.pallas_call(
        _ffn_kernel,
        grid_spec=grid_spec,
        out_shape=jax.ShapeDtypeStruct(xs.shape, F32),
        compiler_params=_cparams("arbitrary"),
        name="expert_ffn",
    )(block_e, rows_valid, next_e, xs, w_gu, b_gu.reshape(N_EXPERTS, 1, 2 * D_EXPERT),
      w_dn, b_dn.reshape(N_EXPERTS, 1, D_MODEL))


def _pos_kernel(idx_ref, rank_ref, pstart_ref, move_ref, comb_ref):
    eid = lax.broadcasted_iota(jnp.int32, (N_EXPERTS, idx_ref.shape[1]), 0)
    pstart = pstart_ref[...]
    rows = [jnp.sum(jnp.where(eid == idx_ref[k:k + 1, :], pstart, 0), axis=0, keepdims=True)
            for k in range(TOP_K)]
    pos = (jnp.concatenate(rows, axis=0) + rank_ref[...]) * ROW_CHUNKS
    for k in range(TOP_K):
        move_ref[0, :, k * MOVE_TOKENS:(k + 1) * MOVE_TOKENS] = pos[k:k + 1, :]
        for j in range(MOVE_TOKENS // COMBINE_ROWS):
            comb_ref[j, :, k * COMBINE_ROWS:(k + 1) * COMBINE_ROWS] = pos[k:k + 1, j * COMBINE_ROWS:(j + 1) * COMBINE_ROWS]


def _positions(idx_t, rank_t, pstart):
    n = idx_t.shape[1]
    tl = MOVE_TOKENS
    per = tl // COMBINE_ROWS
    tok = pl.BlockSpec((TOP_K, tl), lambda i: (0, i))
    return pl.pallas_call(
        _pos_kernel,
        grid=(n // tl,),
        in_specs=[tok, tok, pl.BlockSpec((N_EXPERTS, 1), lambda i: (0, 0))],
        out_specs=[pl.BlockSpec((1, 1, TOP_K * tl), lambda i: (i, 0, 0)),
                   pl.BlockSpec((per, 1, TOP_K * COMBINE_ROWS), lambda i: (i, 0, 0))],
        out_shape=[jax.ShapeDtypeStruct((n // tl, 1, TOP_K * tl), jnp.int32),
                   jax.ShapeDtypeStruct((n // COMBINE_ROWS, 1, TOP_K * COMBINE_ROWS), jnp.int32)],
        compiler_params=_cparams("arbitrary"),
        name="dispatch_pos",
    )(idx_t, rank_t, pstart.reshape(N_EXPERTS, 1))


def _row_tile(ref, row_offset):
    return ref.at[pl.ds(pl.multiple_of(row_offset, ROW_CHUNKS), ROW_CHUNKS)]


def _dispatch_kernel(pos_ref, x_ref, xs_ref, sem):
    tokens = x_ref.shape[0] // ROW_CHUNKS

    def copy(t, k):
        return pltpu.make_async_copy(_row_tile(x_ref, t * ROW_CHUNKS),
                                     _row_tile(xs_ref, pos_ref[0, 0, k * tokens + t]), sem)

    def issue(t, carry):
        for k in range(TOP_K):
            copy(t, k).start(priority=k % 2)
        return carry

    def drain(t, carry):
        for k in range(TOP_K):
            copy(t, k).wait()
        return carry

    lax.fori_loop(0, tokens, issue, 0, unroll=MOVE_UNROLL)
    lax.fori_loop(0, tokens, drain, 0, unroll=MOVE_UNROLL)


def _dispatch_rows(pos_tiles, xn, out_rows):
    n = xn.shape[0] // ROW_CHUNKS
    tl = MOVE_TOKENS
    return pl.pallas_call(
        _dispatch_kernel,
        grid=(n // tl,),
        in_specs=[pl.BlockSpec((1, 1, tl * TOP_K), lambda i: (i, 0, 0), memory_space=pltpu.SMEM),
                  pl.BlockSpec((tl * ROW_CHUNKS, LANES), lambda i: (i, 0))],
        out_specs=pl.BlockSpec(memory_space=pl.ANY),
        out_shape=jax.ShapeDtypeStruct((out_rows * ROW_CHUNKS, LANES), xn.dtype),
        scratch_shapes=[pltpu.SemaphoreType.DMA(())],
        compiler_params=_cparams("arbitrary"),
        name="dispatch_rows",
    )(pos_tiles, xn)


def _combine_kernel(pos_ref, pos_next_ref, h_ref, g_ref, fw_ref, ys_ref, o_ref, buf, sems):
    i = pl.program_id(0)
    steps = pl.num_programs(0)
    tokens = h_ref.shape[0]
    slot = lax.rem(i, 2)

    def gather(p_ref, s, start):
        def body(t, carry):
            for k in range(TOP_K):
                cp = pltpu.make_async_copy(_row_tile(ys_ref, p_ref[0, 0, k * tokens + t]),
                                           _row_tile(buf.at[s, k], t * ROW_CHUNKS), sems.at[s])
                cp.start(priority=k % 2) if start else cp.wait()
            return carry
        lax.fori_loop(0, tokens, body, 0, unroll=MOVE_UNROLL)

    @pl.when(i == 0)
    def _():
        gather(pos_ref, 0, True)

    @pl.when(i + 1 < steps)
    def _():
        gather(pos_next_ref, 1 - slot, True)

    gather(pos_ref, slot, False)
    g = g_ref[...]
    y = h_ref[...]
    for k in range(TOP_K):
        y = y + _load_row_tiles(buf.at[slot, k], tokens) * g[:, k:k + 1]
    o_ref[...] = y * lax.rsqrt(jnp.mean(y * y, axis=-1, keepdims=True) + RMS_EPS) * fw_ref[...]


def _combine(h, ys, pos_t, gates_nk, final_w):
    n = h.shape[0]
    tm = COMBINE_ROWS
    steps = n // tm
    smem = lambda imap: pl.BlockSpec((1, 1, tm * TOP_K), imap, memory_space=pltpu.SMEM)
    return pl.pallas_call(
        _combine_kernel,
        grid=(steps,),
        in_specs=[smem(lambda i: (i, 0, 0)),
                  smem(lambda i: (jnp.minimum(i + 1, steps - 1), 0, 0)),
                  pl.BlockSpec((tm, D_MODEL), lambda i: (i, 0)),
                  pl.BlockSpec((tm, TOP_K), lambda i: (i, 0)),
                  pl.BlockSpec((1, D_MODEL), lambda i: (0, 0)),
                  pl.BlockSpec(memory_space=pl.ANY)],
        out_specs=pl.BlockSpec((tm, D_MODEL), lambda i: (i, 0)),
        out_shape=jax.ShapeDtypeStruct((n, D_MODEL), F32),
        scratch_shapes=[pltpu.VMEM((2, TOP_K, tm * ROW_CHUNKS, LANES), F32), pltpu.SemaphoreType.DMA((2,))],
        compiler_params=_cparams("arbitrary"),
        name="combine_norm",
    )(pos_t, pos_t, h, gates_nk, final_w.reshape(1, D_MODEL), ys)


def kernel(x, norm_mix_w, w_in, rel_bias, hgrn_lb_logits, hgrn_norm_w, w_out, norm_ffn_w,
           router_w, router_b, w_gate_up, b_gate_up, w_down, b_down, final_norm_w):
    batch, seq, _ = x.shape
    n = batch * seq
    x2 = x.reshape(n, D_MODEL)

    qa, ka, va, qh, fh, ih, gh = _inproj(x2, norm_mix_w[0], w_in[0])
    attn = _dilated_attention(qa, ka, va, _masked_bias(rel_bias), batch, seq)
    hg = _hgrn(qh, fh, ih, gh, hgrn_lb_logits, hgrn_norm_w[0], batch, seq)
    h, xn, idx_t, gate_t, rank_t, counts = _outproj(attn, hg, x2, w_out[0], norm_ffn_w[0],
                                                    router_w[0], router_b[0])

    tm = FFN_ROWS
    n_blocks = -(-(n * TOP_K) // tm) + N_EXPERTS
    counts = counts.reshape(N_EXPERTS)
    padded = (counts + tm - 1) // tm * tm
    pend = jnp.cumsum(padded)
    pstart = pend - padded
    block_row0 = jnp.arange(n_blocks, dtype=jnp.int32) * tm
    in_block = (block_row0[:, None] >= pstart[None, :]) & (block_row0[:, None] < pend[None, :])
    block_e = jnp.minimum(jnp.sum(pend[None, :] <= block_row0[:, None], axis=1), N_EXPERTS - 1).astype(jnp.int32)
    rows_valid = jnp.sum(jnp.where(in_block, jnp.clip(pstart + counts - block_row0[:, None], 0, tm), 0),
                         axis=1).astype(jnp.int32)

    pos_move, pos_comb = _positions(idx_t, rank_t, pstart.astype(jnp.int32))
    xs = _dispatch_rows(pos_move, xn, n_blocks * tm)
    experts = jnp.arange(N_EXPERTS, dtype=jnp.int32)
    later = jnp.where((experts[None, :] > experts[:, None]) & (counts[None, :] > 0), experts[None, :], N_EXPERTS)
    next_of = jnp.min(later, axis=1)
    next_e = jnp.where(next_of < N_EXPERTS, next_of, -1)[block_e].astype(jnp.int32)
    ys = _expert_ffn(block_e, rows_valid, next_e, xs, w_gate_up[0], b_gate_up[0], w_down[0], b_down[0])
    out = _combine(h, ys, pos_comb, gate_t.T, final_norm_w)
    return out.reshape(batch, seq, D_MODEL)
```

```python
import functools

import numpy as np
import jax
import jax.numpy as jnp
from jax import lax
from jax.experimental import pallas as pl
from jax.experimental.pallas import tpu as pltpu

F32 = jnp.float32
BF16 = jnp.bfloat16
HIGHEST = lax.Precision.HIGHEST

D_MODEL = 1024
ATTN_HEAD_DIM = 64
ATTN_WIDTH = 512
ATTN_HEADS = ATTN_WIDTH // ATTN_HEAD_DIM
DILATION_PATTERNS = ((128, 1), (512, 4), (2048, 16))
ATTN_BLOCK = 128
REL_BUCKETS = 32
REL_MAX_DISTANCE = 2048
HGRN_HEAD_DIM = 128
HGRN_WIDTH = 512
HGRN_HEADS = HGRN_WIDTH // HGRN_HEAD_DIM
HGRN_CHUNK = 64
HGRN_SUB = 8
N_EXPERTS = 32
TOP_K = 4
D_EXPERT = 1024
SWIGLU_LIMIT = 7.0
SWIGLU_ALPHA = 1.702
RMS_EPS = 1e-5
LOG2_E = 1.4426950408889634
IN_PROJ_WIDTH = 3 * ATTN_WIDTH + 4 * HGRN_WIDTH
LANES = 128
HEAD_PAIRS = ATTN_WIDTH // LANES
ROW_CHUNKS = D_MODEL // LANES

INPROJ_ROWS = 512
OUTPROJ_ROWS = 1024
RANK_SPAN = 256
FFN_ROWS = 512
FFN_COLS = 512
COMBINE_ROWS = 256
HGRN_SEQ_BLOCK = 1024
ATTN_UNROLL = 8
MOVE_TOKENS = 2048
MOVE_UNROLL = 8
VMEM_LIMIT = 48 * 1024 * 1024

NT_DIMS = (((1,), (1,)), ((), ()))
TN_DIMS = (((0,), (0,)), ((), ()))


def _cparams(*sem):
    return pltpu.CompilerParams(dimension_semantics=sem, vmem_limit_bytes=VMEM_LIMIT)


def _store_row_tiles(ref, value):
    rows = value.shape[0]
    for c in range(ROW_CHUNKS):
        ref[pl.ds(c, rows, stride=ROW_CHUNKS), :] = value[:, c * LANES:(c + 1) * LANES]


def _load_row_tiles(ref, rows):
    return jnp.concatenate([ref[pl.ds(c, rows, stride=ROW_CHUNKS), :] for c in range(ROW_CHUNKS)], axis=1)


def _inproj_kernel(x_ref, nw_ref, w_ref, qa_ref, ka_ref, va_ref, qh_ref, fh_ref, ih_ref, gh_ref):
    x = x_ref[...]
    xn = x * lax.rsqrt(jnp.mean(x * x, axis=-1, keepdims=True) + RMS_EPS) * nw_ref[...]
    xn = xn.astype(BF16)
    for c, ref in enumerate((qa_ref, ka_ref, va_ref, qh_ref, fh_ref, ih_ref, gh_ref)):
        y = jnp.dot(xn, w_ref[:, c * ATTN_WIDTH:(c + 1) * ATTN_WIDTH], preferred_element_type=F32)
        ref[...] = y.astype(ref.dtype)


def _inproj(x2, norm_w, w_in):
    n = x2.shape[0]
    tm = INPROJ_ROWS
    row = lambda width: pl.BlockSpec((tm, width), lambda i: (i, 0))
    dtypes = (F32, F32, F32, BF16, F32, BF16, F32)
    return pl.pallas_call(
        _inproj_kernel,
        grid=(n // tm,),
        in_specs=[row(D_MODEL),
                  pl.BlockSpec((1, D_MODEL), lambda i: (0, 0)),
                  pl.BlockSpec((D_MODEL, IN_PROJ_WIDTH), lambda i: (0, 0))],
        out_specs=[row(ATTN_WIDTH)] * 7,
        out_shape=[jax.ShapeDtypeStruct((n, ATTN_WIDTH), dt) for dt in dtypes],
        compiler_params=_cparams("arbitrary"),
        name="inproj",
    )(x2, norm_w.reshape(1, D_MODEL), w_in.astype(BF16))


def _t5_causal_bucket(dist):
    n = np.maximum(dist, 0)
    max_exact = REL_BUCKETS // 2
    large = max_exact + (np.log(np.maximum(n, 1) / max_exact)
                         / np.log(REL_MAX_DISTANCE / max_exact)
                         * (REL_BUCKETS - max_exact)).astype(np.int32)
    large = np.minimum(large, REL_BUCKETS - 1)
    return np.where(n < max_exact, n, large).astype(np.int32)


def _bucket_thresholds(max_dist):
    buckets = _t5_causal_bucket(np.arange(max_dist + 1))
    assert np.all(np.diff(buckets) >= 0)
    return [(b, int(np.argmax(buckets >= b))) for b in range(1, REL_BUCKETS) if np.any(buckets >= b)]


def _bias_kernel(rel_ref, o_ref):
    blk = ATTN_BLOCK
    h = pl.program_id(0)
    qi = lax.broadcasted_iota(jnp.int32, (blk, 2 * blk), 0)
    kj = lax.broadcasted_iota(jnp.int32, (blk, 2 * blk), 1)
    dist_sub = blk + qi - kj
    for pi, (window, dilation) in enumerate(DILATION_PATTERNS):
        span = window // dilation
        dist = dist_sub * dilation
        val = jnp.full((blk, 2 * blk), rel_ref[h], F32)
        for b, first_dist in _bucket_thresholds(window):
            val = jnp.where(dist >= first_dist, rel_ref[b * ATTN_HEADS + h], val)
        o_ref[pi, 0] = jnp.where((dist_sub >= 0) & (dist_sub <= span), val * LOG2_E, -1e30)


def _masked_bias(rel_bias):
    blk = ATTN_BLOCK
    npat = len(DILATION_PATTERNS)
    bias = pl.pallas_call(
        _bias_kernel,
        grid=(ATTN_HEADS,),
        in_specs=[pl.BlockSpec(memory_space=pltpu.SMEM)],
        out_specs=pl.BlockSpec((npat, 1, blk, 2 * blk), lambda h: (0, h, 0, 0)),
        out_shape=jax.ShapeDtypeStruct((npat, ATTN_HEADS, blk, 2 * blk), F32),
        compiler_params=_cparams("arbitrary"),
        name="rel_bias",
    )(rel_bias.astype(F32).reshape(REL_BUCKETS * ATTN_HEADS))
    return bias.reshape(npat, HEAD_PAIRS, 2 * blk, 2 * blk)


def _attn_kernel(q_ref, k_ref, v_ref, bias_ref, o_ref, acc_ref, lse_ref, *, seq):
    blk = ATTN_BLOCK
    lane = lax.broadcasted_iota(jnp.int32, (1, LANES), 1)
    first = lane < ATTN_HEAD_DIM
    scale = ATTN_HEAD_DIM ** -0.5 * LOG2_E
    sel0 = jnp.where(first, scale, 0.0)
    sel1 = jnp.where(first, 0.0, scale)
    ones_cols = jnp.ones((2 * blk, LANES), BF16)

    def key_rows(ref, dil, start):
        return ref[0, pl.ds(start, blk, stride=dil), :].astype(BF16)

    def one_block(pi, dil, q_start, kk, vv):
        n_keys = kk.shape[0]
        qb = q_ref[0, pl.ds(q_start, blk, stride=dil), :]
        q2 = jnp.concatenate([qb * sel0, qb * sel1], axis=0).astype(BF16)
        s = lax.dot_general(q2, kk, NT_DIMS, preferred_element_type=F32)
        s = s + bias_ref[pi, 0, :, 2 * blk - n_keys:]
        m = jnp.max(s, axis=-1, keepdims=True)
        p = jnp.exp2(s - m).astype(BF16)
        ov = jnp.dot(p, jnp.concatenate([vv, ones_cols[:n_keys]], axis=1), preferred_element_type=F32)
        z = ov[:, LANES:]
        o = ov[:, :LANES] / z
        lse = m + jnp.log2(z)
        rows = pl.ds(q_start, blk, stride=dil)
        acc_ref[pi, rows, :] = jnp.where(first, o[:blk], o[blk:])
        lse_ref[pi, rows, :] = jnp.where(first, lse[:blk], lse[blk:])

    for pi, (_, dil) in enumerate(DILATION_PATTERNS):
        nb = seq // (dil * blk)

        def residue(r, carry, pi=pi, dil=dil, nb=nb):
            k0, v0 = key_rows(k_ref, dil, r), key_rows(v_ref, dil, r)
            one_block(pi, dil, r, k0, v0)

            def body(j, prev):
                k_prev, v_prev = prev
                q_start = r + j * (blk * dil)
                k_cur, v_cur = key_rows(k_ref, dil, q_start), key_rows(v_ref, dil, q_start)
                one_block(pi, dil, q_start, jnp.concatenate([k_prev, k_cur], axis=0),
                          jnp.concatenate([v_prev, v_cur], axis=0))
                return k_cur, v_cur

            lax.fori_loop(1, nb, body, (k0, v0), unroll=2 * ATTN_UNROLL if dil == 1 else ATTN_UNROLL)
            return carry

        lax.fori_loop(0, dil, residue, 0, unroll=ATTN_UNROLL if nb == 2 else (2 if dil > 1 else 1))

    def merge(i, carry):
        rows = pl.ds(pl.multiple_of(i * blk, blk), blk)
        ls = [lse_ref[pi, rows, :] for pi in range(len(DILATION_PATTERNS))]
        m = functools.reduce(jnp.maximum, ls)
        es = [jnp.exp2(l - m) for l in ls]
        num = sum(e * acc_ref[pi, rows, :] for pi, e in enumerate(es))
        o_ref[0, rows, :] = (num / sum(es)).astype(o_ref.dtype)
        return carry

    lax.fori_loop(0, seq // blk, merge, 0)


def _dilated_attention(qa, ka, va, bias, batch, seq):
    npat = len(DILATION_PATTERNS)
    shp = (batch, seq, ATTN_WIDTH)
    spec = pl.BlockSpec((1, seq, LANES), lambda b, p: (b, 0, p))
    out = pl.pallas_call(
        functools.partial(_attn_kernel, seq=seq),
        grid=(batch, HEAD_PAIRS),
        in_specs=[spec, spec, spec,
                  pl.BlockSpec((npat, 1, 2 * ATTN_BLOCK, 2 * ATTN_BLOCK), lambda b, p: (0, p, 0, 0))],
        out_specs=spec,
        out_shape=jax.ShapeDtypeStruct(shp, BF16),
        scratch_shapes=[pltpu.VMEM((npat, seq, LANES), F32), pltpu.VMEM((npat, seq, LANES), F32)],
        compiler_params=_cparams("arbitrary", "arbitrary"),
        name="dilated_attn",
    )(qa.reshape(shp), ka.reshape(shp), va.reshape(shp), bias)
    return out.reshape(batch * seq, ATTN_WIDTH)


def _hgrn_kernel(q_ref, f_ref, i_ref, g_ref, lbl_ref, nw_ref, o_ref, st_ref, *, n_chunks):
    c, sub, dk = HGRN_CHUNK, HGRN_SUB, HGRN_HEAD_DIM
    nsub = c // sub
    lbl = lbl_ref[...]
    e = jnp.exp(lbl - jnp.max(lbl, axis=0, keepdims=True))
    lb_all = e[0:1] / jnp.sum(e, axis=0, keepdims=True)
    nw = nw_ref[...]

    r64 = lax.broadcasted_iota(jnp.int32, (c, c), 0)
    c64 = lax.broadcasted_iota(jnp.int32, (c, c), 1)
    sub_end = (r64 // sub) * sub + (sub - 1)
    decay_sums = jnp.concatenate([(c64 <= r64), (c64 > r64) & (c64 <= sub_end), (c64 > r64)],
                                 axis=0).astype(BF16)
    col_sub = c64 // sub
    col_in_sub = c64 - (r64 // sub) * sub
    t_iota = lax.broadcasted_iota(jnp.int32, (nsub, sub, dk), 1)
    ones_rhs = jnp.ones((dk, c), BF16)

    @pl.when(pl.program_id(1) == 0)
    def _():
        st_ref[...] = jnp.zeros_like(st_ref)

    def cols_of(hd):
        return slice(hd * dk, (hd + 1) * dk)

    def gates(rows, hd):
        cols = cols_of(hd)
        lb = lb_all[:, cols]
        q = q_ref[0, rows, cols].astype(F32)
        f = lb + (1.0 - lb) * jax.nn.sigmoid(f_ref[0, rows, cols])
        log_f = jnp.log2(f)
        log_hi = log_f.astype(BF16)
        log_lo = (log_f - log_hi.astype(F32)).astype(BF16)
        sums = (jnp.dot(decay_sums, log_hi, preferred_element_type=F32)
                + jnp.dot(decay_sums, log_lo, preferred_element_type=F32))
        return sums, 1.0 - f, q * jax.nn.sigmoid(q) * (dk ** -0.5)

    def products(rows, hd, sums, key, qf):
        cum, to_sub_end, to_chunk_end = sums[:c], sums[c:2 * c], sums[2 * c:]
        v = i_ref[0, rows, cols_of(hd)]
        st = st_ref[hd]
        o = lax.dot_general((qf * jnp.exp2(cum)).astype(BF16), st.astype(BF16), NT_DIMS,
                            preferred_element_type=F32)
        kd = (key * jnp.exp2(to_chunk_end)).astype(BF16)
        st_ref[hd] = (st * jnp.exp2(cum[c - 1:c, :])
                      + lax.dot_general(v, kd, TN_DIMS, preferred_element_type=F32))

        cum3 = cum.reshape(nsub, sub, dk)
        key3 = key.reshape(nsub, sub, dk)
        qf3 = qf.reshape(nsub, sub, dk)
        khat = (key * jnp.exp2(to_sub_end)).astype(BF16)
        qs = []
        for j in range(nsub - 1):
            t0 = sub * (j + 1)
            qs.append(qf[t0:] * jnp.exp2(cum[t0:] - cum[t0 - 1:t0, :]))
        prod = lax.dot_general(jnp.concatenate(qs, axis=0).astype(BF16), khat, NT_DIMS,
                               preferred_element_type=F32)

        ws = []
        for s in range(sub):
            dec = jnp.exp2(jnp.where(t_iota >= s, cum3 - cum3[:, s:s + 1, :], -jnp.inf))
            ws.append((qf3 * key3[:, s:s + 1, :] * dec).reshape(c, dk))
        pair_sums = jnp.dot(jnp.concatenate(ws, axis=0).astype(BF16), ones_rhs,
                            preferred_element_type=F32)
        return o, prod, pair_sums

    def finish(rows, hd, o, prod, pair_sums):
        cols = cols_of(hd)
        a = jnp.zeros((c, c), F32)
        r = 0
        for j in range(nsub - 1):
            t0 = sub * (j + 1)
            pj = jnp.concatenate([jnp.zeros((t0, c), F32), prod[r:r + c - t0, :]], axis=0)
            a = jnp.where(col_sub == j, pj, a)
            r += c - t0
        for s in range(sub):
            a = jnp.where(col_in_sub == s, pair_sums[s * c:(s + 1) * c, :], a)
        o = o + jnp.dot(a.astype(BF16), i_ref[0, rows, cols], preferred_element_type=F32)
        on = o * lax.rsqrt(jnp.mean(o * o, axis=-1, keepdims=True) + RMS_EPS) * nw
        g = g_ref[0, rows, cols]
        o_ref[0, rows, cols] = (on * (g * jax.nn.sigmoid(g))).astype(BF16)

    def body(ci, carry):
        rows = pl.ds(pl.multiple_of(ci * c, c), c)
        heads = range(HGRN_HEADS)
        stage1 = [gates(rows, hd) for hd in heads]
        stage2 = [products(rows, hd, *stage1[hd]) for hd in heads]
        for hd in heads:
            finish(rows, hd, *stage2[hd])
        return carry

    lax.fori_loop(0, n_chunks, body, 0, unroll=4)


def _hgrn(qh, fh, ih, gh, lb_logits, norm_w, batch, seq):
    sb = min(seq, HGRN_SEQ_BLOCK)
    spec = pl.BlockSpec((1, sb, HGRN_WIDTH), lambda b, s: (b, s, 0))
    slots = lb_logits.shape[0]
    shp = (batch, seq, HGRN_WIDTH)
    out = pl.pallas_call(
        functools.partial(_hgrn_kernel, n_chunks=sb // HGRN_CHUNK),
        grid=(batch, seq // sb),
        in_specs=[spec, spec, spec, spec,
                  pl.BlockSpec((slots, HGRN_WIDTH), lambda b, s: (0, 0)),
                  pl.BlockSpec((1, HGRN_HEAD_DIM), lambda b, s: (0, 0))],
        out_specs=spec,
        out_shape=jax.ShapeDtypeStruct(shp, BF16),
        scratch_shapes=[pltpu.VMEM((HGRN_HEADS, HGRN_HEAD_DIM, HGRN_HEAD_DIM), F32)],
        compiler_params=_cparams("arbitrary", "arbitrary"),
        name="hgrn2",
    )(qh.reshape(shp), fh.reshape(shp), ih.reshape(shp), gh.reshape(shp),
      lb_logits.astype(F32), norm_w.reshape(1, HGRN_HEAD_DIM))
    return out.reshape(batch * seq, HGRN_WIDTH)


def _outproj_kernel(attn_ref, hg_ref, x_ref, w_ref, nw_ref, rw_ref, rb_ref,
                    h_ref, xn_ref, idx_ref, gate_ref, rank_ref, cnt_ref, run_ref):
    y = jnp.dot(attn_ref[...], w_ref[0:ATTN_WIDTH, :], preferred_element_type=F32)
    y = y + jnp.dot(hg_ref[...], w_ref[ATTN_WIDTH:, :], preferred_element_type=F32)
    h = x_ref[...] + y
    h_ref[...] = h
    xn = h * lax.rsqrt(jnp.mean(h * h, axis=-1, keepdims=True) + RMS_EPS) * nw_ref[...]
    _store_row_tiles(xn_ref, xn)

    logits = lax.dot_general(rw_ref[...], xn, NT_DIMS, precision=HIGHEST,
                             preferred_element_type=F32) + rb_ref[...]
    eid = lax.broadcasted_iota(jnp.int32, logits.shape, 0)
    vals, idxs = [], []
    for _ in range(TOP_K):
        mx = jnp.max(logits, axis=0, keepdims=True)
        ix = jnp.min(jnp.where(logits == mx, eid, N_EXPERTS), axis=0, keepdims=True)
        vals.append(mx)
        idxs.append(ix)
        logits = jnp.where(eid == ix, -jnp.inf, logits)
    es = [jnp.exp(v - vals[0]) for v in vals]
    den = es[0] + es[1] + es[2] + es[3]
    idx_ref[...] = jnp.concatenate(idxs, axis=0)
    gate_ref[...] = jnp.concatenate([e / den for e in es], axis=0)

    @pl.when(pl.program_id(0) == 0)
    def _():
        run_ref[...] = jnp.zeros_like(run_ref)

    span = RANK_SPAN
    onehots = [(eid == ix).astype(F32) for ix in idxs]
    routed = onehots[0] + onehots[1] + onehots[2] + onehots[3]
    earlier = (lax.broadcasted_iota(jnp.int32, (span, span), 0)
               < lax.broadcasted_iota(jnp.int32, (span, span), 1)).astype(BF16)
    run = run_ref[...]
    ranks = []
    for c0 in range(0, logits.shape[1], span):
        part = routed[:, c0:c0 + span]
        before = jnp.dot(part.astype(BF16), earlier, preferred_element_type=F32) + run
        ranks.append(jnp.concatenate([jnp.sum(o[:, c0:c0 + span] * before, axis=0, keepdims=True)
                                      for o in onehots], axis=0))
        run = run + jnp.sum(part, axis=1, keepdims=True)
    rank_ref[...] = jnp.concatenate(ranks, axis=1).astype(jnp.int32)
    run_ref[...] = run
    cnt_ref[...] = run.astype(jnp.int32)


def _outproj(attn, hg, x2, w_out, norm_w, router_w, router_b):
    n = x2.shape[0]
    tm = OUTPROJ_ROWS
    row = lambda width: pl.BlockSpec((tm, width), lambda i: (i, 0))
    full = lambda a, b: pl.BlockSpec((a, b), lambda i: (0, 0))
    tok = pl.BlockSpec((TOP_K, tm), lambda i: (0, i))
    return pl.pallas_call(
        _outproj_kernel,
        grid=(n // tm,),
        in_specs=[row(ATTN_WIDTH), row(HGRN_WIDTH), row(D_MODEL), full(D_MODEL, D_MODEL),
                  full(1, D_MODEL), full(N_EXPERTS, D_MODEL), full(N_EXPERTS, 1)],
        out_specs=[row(D_MODEL), pl.BlockSpec((tm * ROW_CHUNKS, LANES), lambda i: (i, 0)),
                   tok, tok, tok, full(N_EXPERTS, 1)],
        out_shape=[jax.ShapeDtypeStruct((n, D_MODEL), F32),
                   jax.ShapeDtypeStruct((n * ROW_CHUNKS, LANES), F32),
                   jax.ShapeDtypeStruct((TOP_K, n), jnp.int32),
                   jax.ShapeDtypeStruct((TOP_K, n), F32),
                   jax.ShapeDtypeStruct((TOP_K, n), jnp.int32),
                   jax.ShapeDtypeStruct((N_EXPERTS, 1), jnp.int32)],
        scratch_shapes=[pltpu.VMEM((N_EXPERTS, 1), F32)],
        compiler_params=_cparams("arbitrary"),
        name="outproj_router",
    )(attn, hg, x2, w_out.astype(BF16), norm_w.reshape(1, D_MODEL),
      router_w.T.astype(F32), router_b.astype(F32).reshape(N_EXPERTS, 1))


def _ffn_kernel(be_ref, rv_ref, nx_ref, xs_ref, wgu_hbm, bgu_ref, wdn_hbm, bdn_ref, o_ref,
                wgu_f32, wdn_f32, wgu_bf, wdn_bf, sems):
    i = pl.program_id(0)
    rows_valid = rv_ref[i]
    new_expert = (i == 0) | (be_ref[i] != be_ref[jnp.maximum(i - 1, 0)])

    def weight_copies(e):
        return (pltpu.make_async_copy(wgu_hbm.at[e], wgu_f32, sems.at[0]),
                pltpu.make_async_copy(wdn_hbm.at[e], wdn_f32, sems.at[1]))

    @pl.when(i == 0)
    def _():
        for cp in weight_copies(be_ref[0]):
            cp.start()

    @pl.when(new_expert & (rows_valid > 0))
    def _():
        for cp in weight_copies(be_ref[i]):
            cp.wait()
        step = 128
        for r in range(0, D_MODEL, step):
            wgu_bf[r:r + step, :] = wgu_f32[r:r + step, :].astype(BF16)
        for r in range(0, D_EXPERT, step):
            wdn_bf[r:r + step, :] = wdn_f32[r:r + step, :].astype(BF16)

        @pl.when(nx_ref[i] >= 0)
        def _():
            for cp in weight_copies(nx_ref[i]):
                cp.start()

    @pl.when(rows_valid > 0)
    def _():
        live = lax.broadcasted_iota(jnp.int32, (FFN_ROWS, 1), 0) < rows_valid
        x = jnp.where(live, _load_row_tiles(xs_ref, FFN_ROWS), 0.0).astype(BF16)
        out = bdn_ref[0]
        for c0 in range(0, D_EXPERT, FFN_COLS):
            g_cols = slice(c0, c0 + FFN_COLS)
            u_cols = slice(D_EXPERT + c0, D_EXPERT + c0 + FFN_COLS)
            gate = jnp.dot(x, wgu_bf[:, g_cols], preferred_element_type=F32) + bgu_ref[0, :, g_cols]
            up = jnp.dot(x, wgu_bf[:, u_cols], preferred_element_type=F32) + bgu_ref[0, :, u_cols]
            gate = jnp.minimum(gate, SWIGLU_LIMIT)
            up = jnp.clip(up, -SWIGLU_LIMIT, SWIGLU_LIMIT)
            act = gate * jax.nn.sigmoid(SWIGLU_ALPHA * gate) * (up + 1.0)
            out = out + jnp.dot(act.astype(BF16), wdn_bf[g_cols, :], preferred_element_type=F32)
        _store_row_tiles(o_ref, out)

    @pl.when(rows_valid <= 0)
    def _():
        o_ref[...] = jnp.zeros_like(o_ref)


def _expert_ffn(block_e, rows_valid, next_e, xs, w_gu, b_gu, w_dn, b_dn):
    tm = FFN_ROWS
    tiles = pl.BlockSpec((tm * ROW_CHUNKS, LANES), lambda i, be, rv, nx: (i, 0))
    grid_spec = pltpu.PrefetchScalarGridSpec(
        num_scalar_prefetch=3,
        grid=(xs.shape[0] // (tm * ROW_CHUNKS),),
        in_specs=[tiles,
                  pl.BlockSpec(memory_space=pl.ANY),
                  pl.BlockSpec((1, 1, 2 * D_EXPERT), lambda i, be, rv, nx: (be[i], 0, 0)),
                  pl.BlockSpec(memory_space=pl.ANY),
                  pl.BlockSpec((1, 1, D_MODEL), lambda i, be, rv, nx: (be[i], 0, 0))],
        out_specs=tiles,
        scratch_shapes=[pltpu.VMEM((D_MODEL, 2 * D_EXPERT), F32), pltpu.VMEM((D_EXPERT, D_MODEL), F32),
                        pltpu.VMEM((D_MODEL, 2 * D_EXPERT), BF16), pltpu.VMEM((D_EXPERT, D_MODEL), BF16),
                        pltpu.SemaphoreType.DMA((2,))],
    )
    return pl.pallas_call(
        _ffn_kernel,
        grid_spec=grid_spec,
        out_shape=jax.ShapeDtypeStruct(xs.shape, F32),
        compiler_params=_cparams("arbitrary"),
        name="expert_ffn",
    )(block_e, rows_valid, next_e, xs, w_gu, b_gu.reshape(N_EXPERTS, 1, 2 * D_EXPERT),
      w_dn, b_dn.reshape(N_EXPERTS, 1, D_MODEL))


def _pos_kernel(idx_ref, rank_ref, pstart_ref, move_ref, comb_ref):
    eid = lax.broadcasted_iota(jnp.int32, (N_EXPERTS, idx_ref.shape[1]), 0)
    pstart = pstart_ref[...]
    rows = [jnp.sum(jnp.where(eid == idx_ref[k:k + 1, :], pstart, 0), axis=0, keepdims=True)
            for k in range(TOP_K)]
    pos = (jnp.concatenate(rows, axis=0) + rank_ref[...]) * ROW_CHUNKS
    for k in range(TOP_K):
        move_ref[0, :, k * MOVE_TOKENS:(k + 1) * MOVE_TOKENS] = pos[k:k + 1, :]
        for j in range(MOVE_TOKENS // COMBINE_ROWS):
            comb_ref[j, :, k * COMBINE_ROWS:(k + 1) * COMBINE_ROWS] = pos[k:k + 1, j * COMBINE_ROWS:(j + 1) * COMBINE_ROWS]


def _positions(idx_t, rank_t, pstart):
    n = idx_t.shape[1]
    tl = MOVE_TOKENS
    per = tl // COMBINE_ROWS
    tok = pl.BlockSpec((TOP_K, tl), lambda i: (0, i))
    return pl.pallas_call(
        _pos_kernel,
        grid=(n // tl,),
        in_specs=[tok, tok, pl.BlockSpec((N_EXPERTS, 1), lambda i: (0, 0))],
        out_specs=[pl.BlockSpec((1, 1, TOP_K * tl), lambda i: (i, 0, 0)),
                   pl.BlockSpec((per, 1, TOP_K * COMBINE_ROWS), lambda i: (i, 0, 0))],
        out_shape=[jax.ShapeDtypeStruct((n // tl, 1, TOP_K * tl), jnp.int32),
                   jax.ShapeDtypeStruct((n // COMBINE_ROWS, 1, TOP_K * COMBINE_ROWS), jnp.int32)],
        compiler_params=_cparams("arbitrary"),
        name="dispatch_pos",
    )(idx_t, rank_t, pstart.reshape(N_EXPERTS, 1))


def _row_tile(ref, row_offset):
    return ref.at[pl.ds(pl.multiple_of(row_offset, ROW_CHUNKS), ROW_CHUNKS)]


def _dispatch_kernel(pos_ref, x_ref, xs_ref, sem):
    tokens = x_ref.shape[0] // ROW_CHUNKS

    def copy(t, k):
        return pltpu.make_async_copy(_row_tile(x_ref, t * ROW_CHUNKS),
                                     _row_tile(xs_ref, pos_ref[0, 0, k * tokens + t]), sem)

    def issue(t, carry):
        for k in range(TOP_K):
            copy(t, k).start(priority=k % 2)
        return carry

    def drain(t, carry):
        for k in range(TOP_K):
            copy(t, k).wait()
        return carry

    lax.fori_loop(0, tokens, issue, 0, unroll=MOVE_UNROLL)
    lax.fori_loop(0, tokens, drain, 0, unroll=MOVE_UNROLL)


def _dispatch_rows(pos_tiles, xn, out_rows):
    n = xn.shape[0] // ROW_CHUNKS
    tl = MOVE_TOKENS
    return pl.pallas_call(
        _dispatch_kernel,
        grid=(n // tl,),
        in_specs=[pl.BlockSpec((1, 1, tl * TOP_K), lambda i: (i, 0, 0), memory_space=pltpu.SMEM),
                  pl.BlockSpec((tl * ROW_CHUNKS, LANES), lambda i: (i, 0))],
        out_specs=pl.BlockSpec(memory_space=pl.ANY),
        out_shape=jax.ShapeDtypeStruct((out_rows * ROW_CHUNKS, LANES), xn.dtype),
        scratch_shapes=[pltpu.SemaphoreType.DMA(())],
        compiler_params=_cparams("arbitrary"),
        name="dispatch_rows",
    )(pos_tiles, xn)


def _combine_kernel(pos_ref, pos_next_ref, h_ref, g_ref, fw_ref, ys_ref, o_ref, buf, sems):
    i = pl.program_id(0)
    steps = pl.num_programs(0)
    tokens = h_ref.shape[0]
    slot = lax.rem(i, 2)

    def gather(p_ref, s, start):
        def body(t, carry):
            for k in range(TOP_K):
                cp = pltpu.make_async_copy(_row_tile(ys_ref, p_ref[0, 0, k * tokens + t]),
                                           _row_tile(buf.at[s, k], t * ROW_CHUNKS), sems.at[s])
                cp.start(priority=k % 2) if start else cp.wait()
            return carry
        lax.fori_loop(0, tokens, body, 0, unroll=MOVE_UNROLL)

    @pl.when(i == 0)
    def _():
        gather(pos_ref, 0, True)

    @pl.when(i + 1 < steps)
    def _():
        gather(pos_next_ref, 1 - slot, True)

    gather(pos_ref, slot, False)
    g = g_ref[...]
    y = h_ref[...]
    for k in range(TOP_K):
        y = y + _load_row_tiles(buf.at[slot, k], tokens) * g[:, k:k + 1]
    o_ref[...] = y * lax.rsqrt(jnp.mean(y * y, axis=-1, keepdims=True) + RMS_EPS) * fw_ref[...]


def _combine(h, ys, pos_t, gates_nk, final_w):
    n = h.shape[0]
    tm = COMBINE_ROWS
    steps = n // tm
    smem = lambda imap: pl.BlockSpec((1, 1, tm * TOP_K), imap, memory_space=pltpu.SMEM)
    return pl.pallas_call(
        _combine_kernel,
        grid=(steps,),
        in_specs=[smem(lambda i: (i, 0, 0)),
                  smem(lambda i: (jnp.minimum(i + 1, steps - 1), 0, 0)),
                  pl.BlockSpec((tm, D_MODEL), lambda i: (i, 0)),
                  pl.BlockSpec((tm, TOP_K), lambda i: (i, 0)),
                  pl.BlockSpec((1, D_MODEL), lambda i: (0, 0)),
                  pl.BlockSpec(memory_space=pl.ANY)],
        out_specs=pl.BlockSpec((tm, D_MODEL), lambda i: (i, 0)),
        out_shape=jax.ShapeDtypeStruct((n, D_MODEL), F32),
        scratch_shapes=[pltpu.VMEM((2, TOP_K, tm * ROW_CHUNKS, LANES), F32), pltpu.SemaphoreType.DMA((2,))],
        compiler_params=_cparams("arbitrary"),
        name="combine_norm",
    )(pos_t, pos_t, h, gates_nk, final_w.reshape(1, D_MODEL), ys)


def kernel(x, norm_mix_w, w_in, rel_bias, hgrn_lb_logits, hgrn_norm_w, w_out, norm_ffn_w,
           router_w, router_b, w_gate_up, b_gate_up, w_down, b_down, final_norm_w):
    batch, seq, _ = x.shape
    n = batch * seq
    x2 = x.reshape(n, D_MODEL)

    qa, ka, va, qh, fh, ih, gh = _inproj(x2, norm_mix_w[0], w_in[0])
    attn = _dilated_attention(qa, ka, va, _masked_bias(rel_bias), batch, seq)
    hg = _hgrn(qh, fh, ih, gh, hgrn_lb_logits, hgrn_norm_w[0], batch, seq)
    h, xn, idx_t, gate_t, rank_t, counts = _outproj(attn, hg, x2, w_out[0], norm_ffn_w[0],
                                                    router_w[0], router_b[0])

    tm = FFN_ROWS
    n_blocks = -(-(n * TOP_K) // tm) + N_EXPERTS
    counts = counts.reshape(N_EXPERTS)
    padded = (counts + tm - 1) // tm * tm
    pend = jnp.cumsum(padded)
    pstart = pend - padded
    block_row0 = jnp.arange(n_blocks, dtype=jnp.int32) * tm
    in_block = (block_row0[:, None] >= pstart[None, :]) & (block_row0[:, None] < pend[None, :])
    block_e = jnp.minimum(jnp.sum(pend[None, :] <= block_row0[:, None], axis=1), N_EXPERTS - 1).astype(jnp.int32)
    rows_valid = jnp.sum(jnp.where(in_block, jnp.clip(pstart + counts - block_row0[:, None], 0, tm), 0),
                         axis=1).astype(jnp.int32)

    pos_move, pos_comb = _positions(idx_t, rank_t, pstart.astype(jnp.int32))
    xs = _dispatch_rows(pos_move, xn, n_blocks * tm)
    experts = jnp.arange(N_EXPERTS, dtype=jnp.int32)
    later = jnp.where((experts[None, :] > experts[:, None]) & (counts[None, :] > 0), experts[None, :], N_EXPERTS)
    next_of = jnp.min(later, axis=1)
    next_e = jnp.where(next_of < N_EXPERTS, next_of, -1)[block_e].astype(jnp.int32)
    ys = _expert_ffn(block_e, rows_valid, next_e, xs, w_gate_up[0], b_gate_up[0], w_down[0], b_down[0])
    out = _combine(h, ys, pos_comb, gate_t.T, final_norm_w)
    return out.reshape(batch, seq, D_MODEL)
```

```python
import functools

import numpy as np
import jax
import jax.numpy as jnp
from jax import lax
from jax.experimental import pallas as pl
from jax.experimental.pallas import tpu as pltpu

F32 = jnp.float32
BF16 = jnp.bfloat16

D_MODEL = 1024
ATTN_HEAD_DIM = 64
ATTN_WIDTH = 512
ATTN_HEADS = ATTN_WIDTH // ATTN_HEAD_DIM
DILATION_PATTERNS = ((128, 1), (512, 4), (2048, 16))
ATTN_BLOCK = 128
REL_BUCKETS = 32
REL_MAX_DISTANCE = 2048
HGRN_HEAD_DIM = 128
HGRN_WIDTH = 512
HGRN_HEADS = HGRN_WIDTH // HGRN_HEAD_DIM
HGRN_CHUNK = 64
HGRN_SUB = 8
N_EXPERTS = 32
TOP_K = 4
D_EXPERT = 1024
SWIGLU_LIMIT = 7.0
SWIGLU_ALPHA = 1.702
RMS_EPS = 1e-5
LOG2_E = 1.4426950408889634
IN_PROJ_WIDTH = 3 * ATTN_WIDTH + 4 * HGRN_WIDTH
LANES = 128
HEAD_PAIRS = ATTN_WIDTH // LANES
ROW_CHUNKS = D_MODEL // LANES

INPROJ_ROWS = 512
OUTPROJ_ROWS = 1024
RANK_SPAN = 256
FFN_ROWS = 512
FFN_COLS = 512
COMBINE_ROWS = 256
HGRN_SEQ_BLOCK = 1024
ATTN_UNROLL = 8
MOVE_TOKENS = 2048
MOVE_UNROLL = 8
VMEM_LIMIT = 48 * 1024 * 1024

NT_DIMS = (((1,), (1,)), ((), ()))
TN_DIMS = (((0,), (0,)), ((), ()))


def _cparams(*sem):
    return pltpu.CompilerParams(dimension_semantics=sem, vmem_limit_bytes=VMEM_LIMIT)


def _store_row_tiles(ref, value):
    rows = value.shape[0]
    for c in range(ROW_CHUNKS):
        ref[pl.ds(c, rows, stride=ROW_CHUNKS), :] = value[:, c * LANES:(c + 1) * LANES]


def _load_row_tiles(ref, rows):
    return jnp.concatenate([ref[pl.ds(c, rows, stride=ROW_CHUNKS), :] for c in range(ROW_CHUNKS)], axis=1)


def _inproj_kernel(x_ref, nw_ref, w_ref, qa_ref, ka_ref, va_ref, qh_ref, fh_ref, ih_ref, gh_ref):
    x = x_ref[...]
    xn = x * lax.rsqrt(jnp.mean(x * x, axis=-1, keepdims=True) + RMS_EPS) * nw_ref[...]
    xn = xn.astype(BF16)
    for c, ref in enumerate((qa_ref, ka_ref, va_ref, qh_ref, fh_ref, ih_ref, gh_ref)):
        y = jnp.dot(xn, w_ref[:, c * ATTN_WIDTH:(c + 1) * ATTN_WIDTH], preferred_element_type=F32)
        ref[...] = y.astype(ref.dtype)


def _inproj(x2, norm_w, w_in):
    n = x2.shape[0]
    tm = INPROJ_ROWS
    row = lambda width: pl.BlockSpec((tm, width), lambda i: (i, 0))
    dtypes = (F32, F32, F32, BF16, F32, BF16, F32)
    return pl.pallas_call(
        _inproj_kernel,
        grid=(n // tm,),
        in_specs=[row(D_MODEL),
                  pl.BlockSpec((1, D_MODEL), lambda i: (0, 0)),
                  pl.BlockSpec((D_MODEL, IN_PROJ_WIDTH), lambda i: (0, 0))],
        out_specs=[row(ATTN_WIDTH)] * 7,
        out_shape=[jax.ShapeDtypeStruct((n, ATTN_WIDTH), dt) for dt in dtypes],
        compiler_params=_cparams("arbitrary"),
        name="inproj",
    )(x2, norm_w.reshape(1, D_MODEL), w_in.astype(BF16))


def _t5_causal_bucket(dist):
    n = np.maximum(dist, 0)
    max_exact = REL_BUCKETS // 2
    large = max_exact + (np.log(np.maximum(n, 1) / max_exact)
                         / np.log(REL_MAX_DISTANCE / max_exact)
                         * (REL_BUCKETS - max_exact)).astype(np.int32)
    large = np.minimum(large, REL_BUCKETS - 1)
    return np.where(n < max_exact, n, large).astype(np.int32)


def _bucket_thresholds(max_dist):
    buckets = _t5_causal_bucket(np.arange(max_dist + 1))
    assert np.all(np.diff(buckets) >= 0)
    return [(b, int(np.argmax(buckets >= b))) for b in range(1, REL_BUCKETS) if np.any(buckets >= b)]


def _bias_kernel(rel_ref, o_ref):
    blk = ATTN_BLOCK
    h = pl.program_id(0)
    qi = lax.broadcasted_iota(jnp.int32, (blk, 2 * blk), 0)
    kj = lax.broadcasted_iota(jnp.int32, (blk, 2 * blk), 1)
    dist_sub = blk + qi - kj
    for pi, (window, dilation) in enumerate(DILATION_PATTERNS):
        span = window // dilation
        dist = dist_sub * dilation
        val = jnp.full((blk, 2 * blk), rel_ref[h], F32)
        for b, first_dist in _bucket_thresholds(window):
            val = jnp.where(dist >= first_dist, rel_ref[b * ATTN_HEADS + h], val)
        o_ref[pi, 0] = jnp.where((dist_sub >= 0) & (dist_sub <= span), val * LOG2_E, -1e30)


def _masked_bias(rel_bias):
    blk = ATTN_BLOCK
    npat = len(DILATION_PATTERNS)
    bias = pl.pallas_call(
        _bias_kernel,
        grid=(ATTN_HEADS,),
        in_specs=[pl.BlockSpec(memory_space=pltpu.SMEM)],
        out_specs=pl.BlockSpec((npat, 1, blk, 2 * blk), lambda h: (0, h, 0, 0)),
        out_shape=jax.ShapeDtypeStruct((npat, ATTN_HEADS, blk, 2 * blk), F32),
        compiler_params=_cparams("arbitrary"),
        name="rel_bias",
    )(rel_bias.astype(F32).reshape(REL_BUCKETS * ATTN_HEADS))
    return bias.reshape(npat, HEAD_PAIRS, 2 * blk, 2 * blk)


def _attn_kernel(q_ref, k_ref, v_ref, bias_ref, o_ref, acc_ref, lse_ref, *, seq):
    blk = ATTN_BLOCK
    lane = lax.broadcasted_iota(jnp.int32, (1, LANES), 1)
    first = lane < ATTN_HEAD_DIM
    scale = ATTN_HEAD_DIM ** -0.5 * LOG2_E
    sel0 = jnp.where(first, scale, 0.0)
    sel1 = jnp.where(first, 0.0, scale)
    ones_cols = jnp.ones((2 * blk, LANES), BF16)

    def key_rows(ref, dil, start):
        return ref[0, pl.ds(start, blk, stride=dil), :].astype(BF16)

    def one_block(pi, dil, q_start, kk, vv):
        n_keys = kk.shape[0]
        qb = q_ref[0, pl.ds(q_start, blk, stride=dil), :]
        q2 = jnp.concatenate([qb * sel0, qb * sel1], axis=0).astype(BF16)
        s = lax.dot_general(q2, kk, NT_DIMS, preferred_element_type=F32)
        s = s + bias_ref[pi, 0, :, 2 * blk - n_keys:]
        m = jnp.max(s, axis=-1, keepdims=True)
        p = jnp.exp2(s - m).astype(BF16)
        ov = jnp.dot(p, jnp.concatenate([vv, ones_cols[:n_keys]], axis=1), preferred_element_type=F32)
        z = ov[:, LANES:]
        o = ov[:, :LANES] / z
        lse = m + jnp.log2(z)
        rows = pl.ds(q_start, blk, stride=dil)
        acc_ref[pi, rows, :] = jnp.where(first, o[:blk], o[blk:])
        lse_ref[pi, rows, :] = jnp.where(first, lse[:blk], lse[blk:])

    for pi, (_, dil) in enumerate(DILATION_PATTERNS):
        nb = seq // (dil * blk)

        def residue(r, carry, pi=pi, dil=dil, nb=nb):
            k0, v0 = key_rows(k_ref, dil, r), key_rows(v_ref, dil, r)
            one_block(pi, dil, r, k0, v0)

            def body(j, prev):
                k_prev, v_prev = prev
                q_start = r + j * (blk * dil)
                k_cur, v_cur = key_rows(k_ref, dil, q_start), key_rows(v_ref, dil, q_start)
                one_block(pi, dil, q_start, jnp.concatenate([k_prev, k_cur], axis=0),
                          jnp.concatenate([v_prev, v_cur], axis=0))
                return k_cur, v_cur

            lax.fori_loop(1, nb, body, (k0, v0), unroll=2 * ATTN_UNROLL if dil == 1 else ATTN_UNROLL)
            return carry

        lax.fori_loop(0, dil, residue, 0, unroll=ATTN_UNROLL if nb == 2 else (2 if dil > 1 else 1))

    def merge(i, carry):
        rows = pl.ds(pl.multiple_of(i * blk, blk), blk)
        ls = [lse_ref[pi, rows, :] for pi in range(len(DILATION_PATTERNS))]
        m = functools.reduce(jnp.maximum, ls)
        es = [jnp.exp2(l - m) for l in ls]
        num = sum(e * acc_ref[pi, rows, :] for pi, e in enumerate(es))
        o_ref[0, rows, :] = (num / sum(es)).astype(o_ref.dtype)
        return carry

    lax.fori_loop(0, seq // blk, merge, 0)


def _dilated_attention(qa, ka, va, bias, batch, seq):
    npat = len(DILATION_PATTERNS)
    shp = (batch, seq, ATTN_WIDTH)
    spec = pl.BlockSpec((1, seq, LANES), lambda b, p: (b, 0, p))
    out = pl.pallas_call(
        functools.partial(_attn_kernel, seq=seq),
        grid=(batch, HEAD_PAIRS),
        in_specs=[spec, spec, spec,
                  pl.BlockSpec((npat, 1, 2 * ATTN_BLOCK, 2 * ATTN_BLOCK), lambda b, p: (0, p, 0, 0))],
        out_specs=spec,
        out_shape=jax.ShapeDtypeStruct(shp, BF16),
        scratch_shapes=[pltpu.VMEM((npat, seq, LANES), F32), pltpu.VMEM((npat, seq, LANES), F32)],
        compiler_params=_cparams("arbitrary", "arbitrary"),
        name="dilated_attn",
    )(qa.reshape(shp), ka.reshape(shp), va.reshape(shp), bias)
    return out.reshape(batch * seq, ATTN_WIDTH)


def _hgrn_kernel(q_ref, f_ref, i_ref, g_ref, lbl_ref, nw_ref, o_ref, st_ref, *, n_chunks):
    c, sub, dk = HGRN_CHUNK, HGRN_SUB, HGRN_HEAD_DIM
    nsub = c // sub
    lbl = lbl_ref[...]
    e = jnp.exp(lbl - jnp.max(lbl, axis=0, keepdims=True))
    lb_all = e[0:1] / jnp.sum(e, axis=0, keepdims=True)
    nw = nw_ref[...]

    r64 = lax.broadcasted_iota(jnp.int32, (c, c), 0)
    c64 = lax.broadcasted_iota(jnp.int32, (c, c), 1)
    sub_end = (r64 // sub) * sub + (sub - 1)
    decay_sums = jnp.concatenate([(c64 <= r64), (c64 > r64) & (c64 <= sub_end), (c64 > r64)],
                                 axis=0).astype(BF16)
    col_sub = c64 // sub
    col_in_sub = c64 - (r64 // sub) * sub
    t_iota = lax.broadcasted_iota(jnp.int32, (nsub, sub, dk), 1)
    ones_rhs = jnp.ones((dk, c), BF16)

    @pl.when(pl.program_id(1) == 0)
    def _():
        st_ref[...] = jnp.zeros_like(st_ref)

    def cols_of(hd):
        return slice(hd * dk, (hd + 1) * dk)

    def gates(rows, hd):
        cols = cols_of(hd)
        lb = lb_all[:, cols]
        q = q_ref[0, rows, cols].astype(F32)
        f = lb + (1.0 - lb) * jax.nn.sigmoid(f_ref[0, rows, cols])
        log_f = jnp.log2(f)
        log_hi = log_f.astype(BF16)
        log_lo = (log_f - log_hi.astype(F32)).astype(BF16)
        sums = (jnp.dot(decay_sums, log_hi, preferred_element_type=F32)
                + jnp.dot(decay_sums, log_lo, preferred_element_type=F32))
        return sums, 1.0 - f, q * jax.nn.sigmoid(q) * (dk ** -0.5)

    def products(rows, hd, sums, key, qf):
        cum, to_sub_end, to_chunk_end = sums[:c], sums[c:2 * c], sums[2 * c:]
        v = i_ref[0, rows, cols_of(hd)]
        st = st_ref[hd]
        o = lax.dot_general((qf * jnp.exp2(cum)).astype(BF16), st.astype(BF16), NT_DIMS,
                            preferred_element_type=F32)
        kd = (key * jnp.exp2(to_chunk_end)).astype(BF16)
        st_ref[hd] = (st * jnp.exp2(cum[c - 1:c, :])
                      + lax.dot_general(v, kd, TN_DIMS, preferred_element_type=F32))

        cum3 = cum.reshape(nsub, sub, dk)
        key3 = key.reshape(nsub, sub, dk)
        qf3 = qf.reshape(nsub, sub, dk)
        khat = (key * jnp.exp2(to_sub_end)).astype(BF16)
        qs = []
        for j in range(nsub - 1):
            t0 = sub * (j + 1)
            qs.append(qf[t0:] * jnp.exp2(cum[t0:] - cum[t0 - 1:t0, :]))
        prod = lax.dot_general(jnp.concatenate(qs, axis=0).astype(BF16), khat, NT_DIMS,
                               preferred_element_type=F32)

        ws = []
        for s in range(sub):
            dec = jnp.exp2(jnp.where(t_iota >= s, cum3 - cum3[:, s:s + 1, :], -jnp.inf))
            ws.append((qf3 * key3[:, s:s + 1, :] * dec).reshape(c, dk))
        pair_sums = jnp.dot(jnp.concatenate(ws, axis=0).astype(BF16), ones_rhs,
                            preferred_element_type=F32)
        return o, prod, pair_sums

    def finish(rows, hd, o, prod, pair_sums):
        cols = cols_of(hd)
        a = jnp.zeros((c, c), F32)
        r = 0
        for j in range(nsub - 1):
            t0 = sub * (j + 1)
            pj = jnp.concatenate([jnp.zeros((t0, c), F32), prod[r:r + c - t0, :]], axis=0)
            a = jnp.where(col_sub == j, pj, a)
            r += c - t0
        for s in range(sub):
            a = jnp.where(col_in_sub == s, pair_sums[s * c:(s + 1) * c, :], a)
        o = o + jnp.dot(a.astype(BF16), i_ref[0, rows, cols], preferred_element_type=F32)
        on = o * lax.rsqrt(jnp.mean(o * o, axis=-1, keepdims=True) + RMS_EPS) * nw
        g = g_ref[0, rows, cols]
        o_ref[0, rows, cols] = (on * (g * jax.nn.sigmoid(g))).astype(BF16)

    def body(ci, carry):
        rows = pl.ds(pl.multiple_of(ci * c, c), c)
        heads = range(HGRN_HEADS)
        stage1 = [gates(rows, hd) for hd in heads]
        stage2 = [products(rows, hd, *stage1[hd]) for hd in heads]
        for hd in heads:
            finish(rows, hd, *stage2[hd])
        return carry

    lax.fori_loop(0, n_chunks, body, 0, unroll=4)


def _hgrn(qh, fh, ih, gh, lb_logits, norm_w, batch, seq):
    sb = min(seq, HGRN_SEQ_BLOCK)
    spec = pl.BlockSpec((1, sb, HGRN_WIDTH), lambda b, s: (b, s, 0))
    slots = lb_logits.shape[0]
    shp = (batch, seq, HGRN_WIDTH)
    out = pl.pallas_call(
        functools.partial(_hgrn_kernel, n_chunks=sb // HGRN_CHUNK),
        grid=(batch, seq // sb),
        in_specs=[spec, spec, spec, spec,
                  pl.BlockSpec((slots, HGRN_WIDTH), lambda b, s: (0, 0)),
                  pl.BlockSpec((1, HGRN_HEAD_DIM), lambda b, s: (0, 0))],
        out_specs=spec,
        out_shape=jax.ShapeDtypeStruct(shp, BF16),
        scratch_shapes=[pltpu.VMEM((HGRN_HEADS, HGRN_HEAD_DIM, HGRN_HEAD_DIM), F32)],
        compiler_params=_cparams("arbitrary", "arbitrary"),
        name="hgrn2",
    )(qh.reshape(shp), fh.reshape(shp), ih.reshape(shp), gh.reshape(shp),
      lb_logits.astype(F32), norm_w.reshape(1, HGRN_HEAD_DIM))
    return out.reshape(batch * seq, HGRN_WIDTH)


def _outproj_kernel(attn_ref, hg_ref, x_ref, w_ref, nw_ref, rw_ref, rb_ref,
                    h_ref, xn_ref, idx_ref, gate_ref, rank_ref, cnt_ref, run_ref):
    y = jnp.dot(attn_ref[...], w_ref[0:ATTN_WIDTH, :], preferred_element_type=F32)
    y = y + jnp.dot(hg_ref[...], w_ref[ATTN_WIDTH:, :], preferred_element_type=F32)
    h = x_ref[...] + y
    h_ref[...] = h
    xn = h * lax.rsqrt(jnp.mean(h * h, axis=-1, keepdims=True) + RMS_EPS) * nw_ref[...]
    _store_row_tiles(xn_ref, xn)

    xn_hi = xn.astype(BF16)
    xn_lo = (xn - xn_hi.astype(F32)).astype(BF16)
    rw_hi, rw_lo = rw_ref[0], rw_ref[1]
    nt = lambda a, b: lax.dot_general(a, b, NT_DIMS, preferred_element_type=F32)
    logits = nt(rw_hi, xn_hi) + nt(rw_hi, xn_lo) + nt(rw_lo, xn_hi) + rb_ref[...]
    eid = lax.broadcasted_iota(jnp.int32, logits.shape, 0)
    vals, idxs = [], []
    for _ in range(TOP_K):
        mx = jnp.max(logits, axis=0, keepdims=True)
        ix = jnp.min(jnp.where(logits == mx, eid, N_EXPERTS), axis=0, keepdims=True)
        vals.append(mx)
        idxs.append(ix)
        logits = jnp.where(eid == ix, -jnp.inf, logits)
    es = [jnp.exp(v - vals[0]) for v in vals]
    den = es[0] + es[1] + es[2] + es[3]
    idx_ref[...] = jnp.concatenate(idxs, axis=0)
    gate_ref[...] = jnp.concatenate([e / den for e in es], axis=0)

    @pl.when(pl.program_id(0) == 0)
    def _():
        run_ref[...] = jnp.zeros_like(run_ref)

    span = RANK_SPAN
    onehots = [(eid == ix).astype(F32) for ix in idxs]
    routed = onehots[0] + onehots[1] + onehots[2] + onehots[3]
    earlier = (lax.broadcasted_iota(jnp.int32, (span, span), 0)
               < lax.broadcasted_iota(jnp.int32, (span, span), 1)).astype(BF16)
    run = run_ref[...]
    ranks = []
    for c0 in range(0, logits.shape[1], span):
        part = routed[:, c0:c0 + span]
        before = jnp.dot(part.astype(BF16), earlier, preferred_element_type=F32) + run
        ranks.append(jnp.concatenate([jnp.sum(o[:, c0:c0 + span] * before, axis=0, keepdims=True)
                                      for o in onehots], axis=0))
        run = run + jnp.sum(part, axis=1, keepdims=True)
    rank_ref[...] = jnp.concatenate(ranks, axis=1).astype(jnp.int32)
    run_ref[...] = run
    cnt_ref[...] = run.astype(jnp.int32)


def _outproj(attn, hg, x2, w_out, norm_w, router_w, router_b):
    n = x2.shape[0]
    tm = OUTPROJ_ROWS
    row = lambda width: pl.BlockSpec((tm, width), lambda i: (i, 0))
    full = lambda a, b: pl.BlockSpec((a, b), lambda i: (0, 0))
    tok = pl.BlockSpec((TOP_K, tm), lambda i: (0, i))
    rw = router_w.T.astype(F32)
    rw_hi = rw.astype(BF16)
    rw_split = jnp.stack([rw_hi, (rw - rw_hi.astype(F32)).astype(BF16)])
    return pl.pallas_call(
        _outproj_kernel,
        grid=(n // tm,),
        in_specs=[row(ATTN_WIDTH), row(HGRN_WIDTH), row(D_MODEL), full(D_MODEL, D_MODEL),
                  full(1, D_MODEL), pl.BlockSpec((2, N_EXPERTS, D_MODEL), lambda i: (0, 0, 0)),
                  full(N_EXPERTS, 1)],
        out_specs=[row(D_MODEL), pl.BlockSpec((tm * ROW_CHUNKS, LANES), lambda i: (i, 0)),
                   tok, tok, tok, full(N_EXPERTS, 1)],
        out_shape=[jax.ShapeDtypeStruct((n, D_MODEL), F32),
                   jax.ShapeDtypeStruct((n * ROW_CHUNKS, LANES), F32),
                   jax.ShapeDtypeStruct((TOP_K, n), jnp.int32),
                   jax.ShapeDtypeStruct((TOP_K, n), F32),
                   jax.ShapeDtypeStruct((TOP_K, n), jnp.int32),
                   jax.ShapeDtypeStruct((N_EXPERTS, 1), jnp.int32)],
        scratch_shapes=[pltpu.VMEM((N_EXPERTS, 1), F32)],
        compiler_params=_cparams("arbitrary"),
        name="outproj_router",
    )(attn, hg, x2, w_out.astype(BF16), norm_w.reshape(1, D_MODEL),
      rw_split, router_b.astype(F32).reshape(N_EXPERTS, 1))


def _ffn_kernel(be_ref, rv_ref, nx_ref, xs_ref, wgu_hbm, bgu_ref, wdn_hbm, bdn_ref, o_ref,
                wgu_f32, wdn_f32, wgu_bf, wdn_bf, sems):
    i = pl.program_id(0)
    rows_valid = rv_ref[i]
    new_expert = (i == 0) | (be_ref[i] != be_ref[jnp.maximum(i - 1, 0)])

    def weight_copies(e):
        return (pltpu.make_async_copy(wgu_hbm.at[e], wgu_f32, sems.at[0]),
                pltpu.make_async_copy(wdn_hbm.at[e], wdn_f32, sems.at[1]))

    @pl.when(i == 0)
    def _():
        for cp in weight_copies(be_ref[0]):
            cp.start()

    @pl.when(new_expert & (rows_valid > 0))
    def _():
        for cp in weight_copies(be_ref[i]):
            cp.wait()
        step = 128
        for r in range(0, D_MODEL, step):
            wgu_bf[r:r + step, :] = wgu_f32[r:r + step, :].astype(BF16)
        for r in range(0, D_EXPERT, step):
            wdn_bf[r:r + step, :] = wdn_f32[r:r + step, :].astype(BF16)

        @pl.when(nx_ref[i] >= 0)
        def _():
            for cp in weight_copies(nx_ref[i]):
                cp.start()

    @pl.when(rows_valid > 0)
    def _():
        live = lax.broadcasted_iota(jnp.int32, (FFN_ROWS, 1), 0) < rows_valid
        x = jnp.where(live, _load_row_tiles(xs_ref, FFN_ROWS), 0.0).astype(BF16)
        out = bdn_ref[0]
        for c0 in range(0, D_EXPERT, FFN_COLS):
            g_cols = slice(c0, c0 + FFN_COLS)
            u_cols = slice(D_EXPERT + c0, D_EXPERT + c0 + FFN_COLS)
            gate = jnp.dot(x, wgu_bf[:, g_cols], preferred_element_type=F32) + bgu_ref[0, :, g_cols]
            up = jnp.dot(x, wgu_bf[:, u_cols], preferred_element_type=F32) + bgu_ref[0, :, u_cols]
            gate = jnp.minimum(gate, SWIGLU_LIMIT)
            up = jnp.clip(up, -SWIGLU_LIMIT, SWIGLU_LIMIT)
            act = gate * jax.nn.sigmoid(SWIGLU_ALPHA * gate) * (up + 1.0)
            out = out + jnp.dot(act.astype(BF16), wdn_bf[g_cols, :], preferred_element_type=F32)
        _store_row_tiles(o_ref, out)

    @pl.when(rows_valid <= 0)
    def _():
        o_ref[...] = jnp.zeros_like(o_ref)


def _expert_ffn(block_e, rows_valid, next_e, xs, w_gu, b_gu, w_dn, b_dn):
    tm = FFN_ROWS
    tiles = pl.BlockSpec((tm * ROW_CHUNKS, LANES), lambda i, be, rv, nx: (i, 0))
    grid_spec = pltpu.PrefetchScalarGridSpec(
        num_scalar_prefetch=3,
        grid=(xs.shape[0] // (tm * ROW_CHUNKS),),
        in_specs=[tiles,
                  pl.BlockSpec(memory_space=pl.ANY),
                  pl.BlockSpec((1, 1, 2 * D_EXPERT), lambda i, be, rv, nx: (be[i], 0, 0)),
                  pl.BlockSpec(memory_space=pl.ANY),
                  pl.BlockSpec((1, 1, D_MODEL), lambda i, be, rv, nx: (be[i], 0, 0))],
        out_specs=tiles,
        scratch_shapes=[pltpu.VMEM((D_MODEL, 2 * D_EXPERT), F32), pltpu.VMEM((D_EXPERT, D_MODEL), F32),
                        pltpu.VMEM((D_MODEL, 2 * D_EXPERT), BF16), pltpu.VMEM((D_EXPERT, D_MODEL), BF16),
                        pltpu.SemaphoreType.DMA((2,))],
    )
    return pl.pallas_call(
        _ffn_kernel,
        grid_spec=grid_spec,
        out_shape=jax.ShapeDtypeStruct(xs.shape, F32),
        compiler_params=_cparams("arbitrary"),
        name="expert_ffn",
    )(block_e, rows_valid, next_e, xs, w_gu, b_gu.reshape(N_EXPERTS, 1, 2 * D_EXPERT),
      w_dn, b_dn.reshape(N_EXPERTS, 1, D_MODEL))


def _pos_kernel(idx_ref, rank_ref, pstart_ref, move_ref, comb_ref):
    eid = lax.broadcasted_iota(jnp.int32, (N_EXPERTS, idx_ref.shape[1]), 0)
    pstart = pstart_ref[...]
    rows = [jnp.sum(jnp.where(eid == idx_ref[k:k + 1, :], pstart, 0), axis=0, keepdims=True)
            for k in range(TOP_K)]
    pos = (jnp.concatenate(rows, axis=0) + rank_ref[...]) * ROW_CHUNKS
    for k in range(TOP_K):
        move_ref[0, :, k * MOVE_TOKENS:(k + 1) * MOVE_TOKENS] = pos[k:k + 1, :]
        for j in range(MOVE_TOKENS // COMBINE_ROWS):
            comb_ref[j, :, k * COMBINE_ROWS:(k + 1) * COMBINE_ROWS] = pos[k:k + 1, j * COMBINE_ROWS:(j + 1) * COMBINE_ROWS]


def _positions(idx_t, rank_t, pstart):
    n = idx_t.shape[1]
    tl = MOVE_TOKENS
    per = tl // COMBINE_ROWS
    tok = pl.BlockSpec((TOP_K, tl), lambda i: (0, i))
    return pl.pallas_call(
        _pos_kernel,
        grid=(n // tl,),
        in_specs=[tok, tok, pl.BlockSpec((N_EXPERTS, 1), lambda i: (0, 0))],
        out_specs=[pl.BlockSpec((1, 1, TOP_K * tl), lambda i: (i, 0, 0)),
                   pl.BlockSpec((per, 1, TOP_K * COMBINE_ROWS), lambda i: (i, 0, 0))],
        out_shape=[jax.ShapeDtypeStruct((n // tl, 1, TOP_K * tl), jnp.int32),
                   jax.ShapeDtypeStruct((n // COMBINE_ROWS, 1, TOP_K * COMBINE_ROWS), jnp.int32)],
        compiler_params=_cparams("arbitrary"),
        name="dispatch_pos",
    )(idx_t, rank_t, pstart.reshape(N_EXPERTS, 1))


def _row_tile(ref, row_offset):
    return ref.at[pl.ds(pl.multiple_of(row_offset, ROW_CHUNKS), ROW_CHUNKS)]


def _dispatch_kernel(pos_ref, x_ref, xs_ref, sem):
    tokens = x_ref.shape[0] // ROW_CHUNKS

    def copy(t, k):
        return pltpu.make_async_copy(_row_tile(x_ref, t * ROW_CHUNKS),
                                     _row_tile(xs_ref, pos_ref[0, 0, k * tokens + t]), sem)

    def issue(t, carry):
        for k in range(TOP_K):
            copy(t, k).start(priority=k % 2)
        return carry

    def drain(t, carry):
        for k in range(TOP_K):
            copy(t, k).wait()
        return carry

    lax.fori_loop(0, tokens, issue, 0, unroll=MOVE_UNROLL)
    lax.fori_loop(0, tokens, drain, 0, unroll=MOVE_UNROLL)


def _dispatch_rows(pos_tiles, xn, out_rows):
    n = xn.shape[0] // ROW_CHUNKS
    tl = MOVE_TOKENS
    return pl.pallas_call(
        _dispatch_kernel,
        grid=(n // tl,),
        in_specs=[pl.BlockSpec((1, 1, tl * TOP_K), lambda i: (i, 0, 0), memory_space=pltpu.SMEM),
                  pl.BlockSpec((tl * ROW_CHUNKS, LANES), lambda i: (i, 0))],
        out_specs=pl.BlockSpec(memory_space=pl.ANY),
        out_shape=jax.ShapeDtypeStruct((out_rows * ROW_CHUNKS, LANES), xn.dtype),
        scratch_shapes=[pltpu.SemaphoreType.DMA(())],
        compiler_params=_cparams("arbitrary"),
        name="dispatch_rows",
    )(pos_tiles, xn)


def _combine_kernel(pos_ref, pos_next_ref, h_ref, g_ref, fw_ref, ys_ref, o_ref, buf, sems):
    i = pl.program_id(0)
    steps = pl.num_programs(0)
    tokens = h_ref.shape[0]
    slot = lax.rem(i, 2)

    def gather(p_ref, s, start):
        def body(t, carry):
            for k in range(TOP_K):
                cp = pltpu.make_async_copy(_row_tile(ys_ref, p_ref[0, 0, k * tokens + t]),
                                           _row_tile(buf.at[s, k], t * ROW_CHUNKS), sems.at[s])
                cp.start(priority=k % 2) if start else cp.wait()
            return carry
        lax.fori_loop(0, tokens, body, 0, unroll=MOVE_UNROLL)

    @pl.when(i == 0)
    def _():
        gather(pos_ref, 0, True)

    @pl.when(i + 1 < steps)
    def _():
        gather(pos_next_ref, 1 - slot, True)

    gather(pos_ref, slot, False)
    g = g_ref[...]
    y = h_ref[...]
    for k in range(TOP_K):
        y = y + _load_row_tiles(buf.at[slot, k], tokens) * g[:, k:k + 1]
    o_ref[...] = y * lax.rsqrt(jnp.mean(y * y, axis=-1, keepdims=True) + RMS_EPS) * fw_ref[...]


def _combine(h, ys, pos_t, gates_nk, final_w):
    n = h.shape[0]
    tm = COMBINE_ROWS
    steps = n // tm
    smem = lambda imap: pl.BlockSpec((1, 1, tm * TOP_K), imap, memory_space=pltpu.SMEM)
    return pl.pallas_call(
        _combine_kernel,
        grid=(steps,),
        in_specs=[smem(lambda i: (i, 0, 0)),
                  smem(lambda i: (jnp.minimum(i + 1, steps - 1), 0, 0)),
                  pl.BlockSpec((tm, D_MODEL), lambda i: (i, 0)),
                  pl.BlockSpec((tm, TOP_K), lambda i: (i, 0)),
                  pl.BlockSpec((1, D_MODEL), lambda i: (0, 0)),
                  pl.BlockSpec(memory_space=pl.ANY)],
        out_specs=pl.BlockSpec((tm, D_MODEL), lambda i: (i, 0)),
        out_shape=jax.ShapeDtypeStruct((n, D_MODEL), F32),
        scratch_shapes=[pltpu.VMEM((2, TOP_K, tm * ROW_CHUNKS, LANES), F32), pltpu.SemaphoreType.DMA((2,))],
        compiler_params=_cparams("arbitrary"),
        name="combine_norm",
    )(pos_t, pos_t, h, gates_nk, final_w.reshape(1, D_MODEL), ys)


def kernel(x, norm_mix_w, w_in, rel_bias, hgrn_lb_logits, hgrn_norm_w, w_out, norm_ffn_w,
           router_w, router_b, w_gate_up, b_gate_up, w_down, b_down, final_norm_w):
    batch, seq, _ = x.shape
    n = batch * seq
    x2 = x.reshape(n, D_MODEL)

    qa, ka, va, qh, fh, ih, gh = _inproj(x2, norm_mix_w[0], w_in[0])
    attn = _dilated_attention(qa, ka, va, _masked_bias(rel_bias), batch, seq)
    hg = _hgrn(qh, fh, ih, gh, hgrn_lb_logits, hgrn_norm_w[0], batch, seq)
    h, xn, idx_t, gate_t, rank_t, counts = _outproj(attn, hg, x2, w_out[0], norm_ffn_w[0],
                                                    router_w[0], router_b[0])

    tm = FFN_ROWS
    n_blocks = -(-(n * TOP_K) // tm) + N_EXPERTS
    counts = counts.reshape(N_EXPERTS)
    padded = (counts + tm - 1) // tm * tm
    pend = jnp.cumsum(padded)
    pstart = pend - padded
    block_row0 = jnp.arange(n_blocks, dtype=jnp.int32) * tm
    in_block = (block_row0[:, None] >= pstart[None, :]) & (block_row0[:, None] < pend[None, :])
    block_e = jnp.minimum(jnp.sum(pend[None, :] <= block_row0[:, None], axis=1), N_EXPERTS - 1).astype(jnp.int32)
    rows_valid = jnp.sum(jnp.where(in_block, jnp.clip(pstart + counts - block_row0[:, None], 0, tm), 0),
                         axis=1).astype(jnp.int32)

    pos_move, pos_comb = _positions(idx_t, rank_t, pstart.astype(jnp.int32))
    xs = _dispatch_rows(pos_move, xn, n_blocks * tm)
    experts = jnp.arange(N_EXPERTS, dtype=jnp.int32)
    later = jnp.where((experts[None, :] > experts[:, None]) & (counts[None, :] > 0), experts[None, :], N_EXPERTS)
    next_of = jnp.min(later, axis=1)
    next_e = jnp.where(next_of < N_EXPERTS, next_of, -1)[block_e].astype(jnp.int32)
    ys = _expert_ffn(block_e, rows_valid, next_e, xs, w_gate_up[0], b_gate_up[0], w_down[0], b_down[0])
    out = _combine(h, ys, pos_comb, gate_t.T, final_norm_w)
    return out.reshape(batch, seq, D_MODEL)
```

```python
import functools

import numpy as np
import jax
import jax.numpy as jnp
from jax import lax
from jax.experimental import pallas as pl
from jax.experimental.pallas import tpu as pltpu

F32 = jnp.float32
BF16 = jnp.bfloat16

D_MODEL = 1024
ATTN_HEAD_DIM = 64
ATTN_WIDTH = 512
ATTN_HEADS = ATTN_WIDTH // ATTN_HEAD_DIM
DILATION_PATTERNS = ((128, 1), (512, 4), (2048, 16))
ATTN_BLOCK = 128
REL_BUCKETS = 32
REL_MAX_DISTANCE = 2048
HGRN_HEAD_DIM = 128
HGRN_WIDTH = 512
HGRN_HEADS = HGRN_WIDTH // HGRN_HEAD_DIM
HGRN_CHUNK = 64
HGRN_SUB = 8
N_EXPERTS = 32
TOP_K = 4
D_EXPERT = 1024
SWIGLU_LIMIT = 7.0
SWIGLU_ALPHA = 1.702
RMS_EPS = 1e-5
LOG2_E = 1.4426950408889634
IN_PROJ_WIDTH = 3 * ATTN_WIDTH + 4 * HGRN_WIDTH
LANES = 128
HEAD_PAIRS = ATTN_WIDTH // LANES
ROW_CHUNKS = D_MODEL // LANES

INPROJ_ROWS = 1024
OUTPROJ_ROWS = 1024
RANK_SPAN = 256
FFN_ROWS = 512
FFN_COLS = 512
COMBINE_ROWS = 256
HGRN_SEQ_BLOCK = 1024
ATTN_UNROLL = 8
MOVE_TOKENS = 2048
MOVE_UNROLL = 8
VMEM_LIMIT = 48 * 1024 * 1024

NT_DIMS = (((1,), (1,)), ((), ()))
TN_DIMS = (((0,), (0,)), ((), ()))


def _cparams(*sem):
    return pltpu.CompilerParams(dimension_semantics=sem, vmem_limit_bytes=VMEM_LIMIT)


def _store_row_tiles(ref, value):
    rows = value.shape[0]
    for c in range(ROW_CHUNKS):
        ref[pl.ds(c, rows, stride=ROW_CHUNKS), :] = value[:, c * LANES:(c + 1) * LANES]


def _load_row_tiles(ref, rows):
    return jnp.concatenate([ref[pl.ds(c, rows, stride=ROW_CHUNKS), :] for c in range(ROW_CHUNKS)], axis=1)


def _inproj_kernel(x_ref, nw_ref, w_ref, qa_ref, ka_ref, va_ref, qh_ref, fh_ref, ih_ref, gh_ref):
    x = x_ref[...]
    xn = x * lax.rsqrt(jnp.mean(x * x, axis=-1, keepdims=True) + RMS_EPS) * nw_ref[...]
    xn = xn.astype(BF16)
    for c, ref in enumerate((qa_ref, ka_ref, va_ref, qh_ref, fh_ref, ih_ref, gh_ref)):
        y = jnp.dot(xn, w_ref[:, c * ATTN_WIDTH:(c + 1) * ATTN_WIDTH], preferred_element_type=F32)
        ref[...] = y.astype(ref.dtype)


def _inproj(x2, norm_w, w_in):
    n = x2.shape[0]
    tm = INPROJ_ROWS
    row = lambda width: pl.BlockSpec((tm, width), lambda i: (i, 0))
    dtypes = (F32, F32, F32, BF16, F32, BF16, F32)
    return pl.pallas_call(
        _inproj_kernel,
        grid=(n // tm,),
        in_specs=[row(D_MODEL),
                  pl.BlockSpec((1, D_MODEL), lambda i: (0, 0)),
                  pl.BlockSpec((D_MODEL, IN_PROJ_WIDTH), lambda i: (0, 0))],
        out_specs=[row(ATTN_WIDTH)] * 7,
        out_shape=[jax.ShapeDtypeStruct((n, ATTN_WIDTH), dt) for dt in dtypes],
        compiler_params=_cparams("arbitrary"),
        name="inproj",
    )(x2, norm_w.reshape(1, D_MODEL), w_in.astype(BF16))


def _t5_causal_bucket(dist):
    n = np.maximum(dist, 0)
    max_exact = REL_BUCKETS // 2
    large = max_exact + (np.log(np.maximum(n, 1) / max_exact)
                         / np.log(REL_MAX_DISTANCE / max_exact)
                         * (REL_BUCKETS - max_exact)).astype(np.int32)
    large = np.minimum(large, REL_BUCKETS - 1)
    return np.where(n < max_exact, n, large).astype(np.int32)


def _bucket_thresholds(max_dist):
    buckets = _t5_causal_bucket(np.arange(max_dist + 1))
    assert np.all(np.diff(buckets) >= 0)
    return [(b, int(np.argmax(buckets >= b))) for b in range(1, REL_BUCKETS) if np.any(buckets >= b)]


def _bias_kernel(rel_ref, o_ref):
    blk = ATTN_BLOCK
    h = pl.program_id(0)
    qi = lax.broadcasted_iota(jnp.int32, (blk, 2 * blk), 0)
    kj = lax.broadcasted_iota(jnp.int32, (blk, 2 * blk), 1)
    dist_sub = blk + qi - kj
    for pi, (window, dilation) in enumerate(DILATION_PATTERNS):
        span = window // dilation
        dist = dist_sub * dilation
        val = jnp.full((blk, 2 * blk), rel_ref[h], F32)
        for b, first_dist in _bucket_thresholds(window):
            val = jnp.where(dist >= first_dist, rel_ref[b * ATTN_HEADS + h], val)
        o_ref[pi, 0] = jnp.where((dist_sub >= 0) & (dist_sub <= span), val * LOG2_E, -1e30)


def _masked_bias(rel_bias):
    blk = ATTN_BLOCK
    npat = len(DILATION_PATTERNS)
    bias = pl.pallas_call(
        _bias_kernel,
        grid=(ATTN_HEADS,),
        in_specs=[pl.BlockSpec(memory_space=pltpu.SMEM)],
        out_specs=pl.BlockSpec((npat, 1, blk, 2 * blk), lambda h: (0, h, 0, 0)),
        out_shape=jax.ShapeDtypeStruct((npat, ATTN_HEADS, blk, 2 * blk), F32),
        compiler_params=_cparams("arbitrary"),
        name="rel_bias",
    )(rel_bias.astype(F32).reshape(REL_BUCKETS * ATTN_HEADS))
    return bias.reshape(npat, HEAD_PAIRS, 2 * blk, 2 * blk)


def _attn_kernel(q_ref, k_ref, v_ref, bias_ref, o_ref, acc_ref, lse_ref, *, seq):
    blk = ATTN_BLOCK
    lane = lax.broadcasted_iota(jnp.int32, (1, LANES), 1)
    first = lane < ATTN_HEAD_DIM
    scale = ATTN_HEAD_DIM ** -0.5 * LOG2_E
    sel0 = jnp.where(first, scale, 0.0)
    sel1 = jnp.where(first, 0.0, scale)
    ones_cols = jnp.ones((2 * blk, LANES), BF16)

    def key_rows(ref, dil, start):
        return ref[0, pl.ds(start, blk, stride=dil), :].astype(BF16)

    def one_block(pi, dil, q_start, kk, vv):
        n_keys = kk.shape[0]
        qb = q_ref[0, pl.ds(q_start, blk, stride=dil), :]
        q2 = jnp.concatenate([qb * sel0, qb * sel1], axis=0).astype(BF16)
        s = lax.dot_general(q2, kk, NT_DIMS, preferred_element_type=F32)
        s = s + bias_ref[pi, 0, :, 2 * blk - n_keys:]
        m = jnp.max(s, axis=-1, keepdims=True)
        p = jnp.exp2(s - m).astype(BF16)
        ov = jnp.dot(p, jnp.concatenate([vv, ones_cols[:n_keys]], axis=1), preferred_element_type=F32)
        z = ov[:, LANES:]
        o = ov[:, :LANES] / z
        lse = m + jnp.log2(z)
        rows = pl.ds(q_start, blk, stride=dil)
        acc_ref[pi, rows, :] = jnp.where(first, o[:blk], o[blk:])
        lse_ref[pi, rows, :] = jnp.where(first, lse[:blk], lse[blk:])

    for pi, (_, dil) in enumerate(DILATION_PATTERNS):
        nb = seq // (dil * blk)

        def residue(r, carry, pi=pi, dil=dil, nb=nb):
            k0, v0 = key_rows(k_ref, dil, r), key_rows(v_ref, dil, r)
            one_block(pi, dil, r, k0, v0)

            def body(j, prev):
                k_prev, v_prev = prev
                q_start = r + j * (blk * dil)
                k_cur, v_cur = key_rows(k_ref, dil, q_start), key_rows(v_ref, dil, q_start)
                one_block(pi, dil, q_start, jnp.concatenate([k_prev, k_cur], axis=0),
                          jnp.concatenate([v_prev, v_cur], axis=0))
                return k_cur, v_cur

            lax.fori_loop(1, nb, body, (k0, v0), unroll=2 * ATTN_UNROLL if dil == 1 else ATTN_UNROLL)
            return carry

        lax.fori_loop(0, dil, residue, 0, unroll=ATTN_UNROLL if nb == 2 else (4 if dil > 1 else 1))

    def merge(i, carry):
        rows = pl.ds(pl.multiple_of(i * blk, blk), blk)
        ls = [lse_ref[pi, rows, :] for pi in range(len(DILATION_PATTERNS))]
        m = functools.reduce(jnp.maximum, ls)
        es = [jnp.exp2(l - m) for l in ls]
        num = sum(e * acc_ref[pi, rows, :] for pi, e in enumerate(es))
        o_ref[0, rows, :] = (num / sum(es)).astype(o_ref.dtype)
        return carry

    lax.fori_loop(0, seq // blk, merge, 0)


def _dilated_attention(qa, ka, va, bias, batch, seq):
    npat = len(DILATION_PATTERNS)
    shp = (batch, seq, ATTN_WIDTH)
    spec = pl.BlockSpec((1, seq, LANES), lambda b, p: (b, 0, p))
    out = pl.pallas_call(
        functools.partial(_attn_kernel, seq=seq),
        grid=(batch, HEAD_PAIRS),
        in_specs=[spec, spec, spec,
                  pl.BlockSpec((npat, 1, 2 * ATTN_BLOCK, 2 * ATTN_BLOCK), lambda b, p: (0, p, 0, 0))],
        out_specs=spec,
        out_shape=jax.ShapeDtypeStruct(shp, BF16),
        scratch_shapes=[pltpu.VMEM((npat, seq, LANES), F32), pltpu.VMEM((npat, seq, LANES), F32)],
        compiler_params=_cparams("arbitrary", "arbitrary"),
        name="dilated_attn",
    )(qa.reshape(shp), ka.reshape(shp), va.reshape(shp), bias)
    return out.reshape(batch * seq, ATTN_WIDTH)


def _hgrn_kernel(q_ref, f_ref, i_ref, g_ref, lbl_ref, nw_ref, o_ref, st_ref, *, n_chunks):
    c, sub, dk = HGRN_CHUNK, HGRN_SUB, HGRN_HEAD_DIM
    nsub = c // sub
    lbl = lbl_ref[...]
    e = jnp.exp(lbl - jnp.max(lbl, axis=0, keepdims=True))
    lb_all = e[0:1] / jnp.sum(e, axis=0, keepdims=True)
    nw = nw_ref[...]

    r64 = lax.broadcasted_iota(jnp.int32, (c, c), 0)
    c64 = lax.broadcasted_iota(jnp.int32, (c, c), 1)
    sub_end = (r64 // sub) * sub + (sub - 1)
    decay_sums = jnp.concatenate([(c64 <= r64), (c64 > r64) & (c64 <= sub_end), (c64 > r64)],
                                 axis=0).astype(BF16)
    col_sub = c64 // sub
    col_in_sub = c64 - (r64 // sub) * sub
    t_iota = lax.broadcasted_iota(jnp.int32, (nsub, sub, dk), 1)
    ones_rhs = jnp.ones((dk, c), BF16)

    @pl.when(pl.program_id(1) == 0)
    def _():
        st_ref[...] = jnp.zeros_like(st_ref)

    def cols_of(hd):
        return slice(hd * dk, (hd + 1) * dk)

    def gates(rows, hd):
        cols = cols_of(hd)
        lb = lb_all[:, cols]
        q = q_ref[0, rows, cols].astype(F32)
        f = lb + (1.0 - lb) * jax.nn.sigmoid(f_ref[0, rows, cols])
        log_f = jnp.log2(f)
        log_hi = log_f.astype(BF16)
        log_lo = (log_f - log_hi.astype(F32)).astype(BF16)
        sums = (jnp.dot(decay_sums, log_hi, preferred_element_type=F32)
                + jnp.dot(decay_sums, log_lo, preferred_element_type=F32))
        return sums, 1.0 - f, q * jax.nn.sigmoid(q) * (dk ** -0.5)

    def products(rows, hd, sums, key, qf):
        cum, to_sub_end, to_chunk_end = sums[:c], sums[c:2 * c], sums[2 * c:]
        v = i_ref[0, rows, cols_of(hd)]
        st = st_ref[hd]
        o = lax.dot_general((qf * jnp.exp2(cum)).astype(BF16), st.astype(BF16), NT_DIMS,
                            preferred_element_type=F32)
        kd = (key * jnp.exp2(to_chunk_end)).astype(BF16)
        st_ref[hd] = (st * jnp.exp2(cum[c - 1:c, :])
                      + lax.dot_general(v, kd, TN_DIMS, preferred_element_type=F32))

        cum3 = cum.reshape(nsub, sub, dk)
        key3 = key.reshape(nsub, sub, dk)
        qf3 = qf.reshape(nsub, sub, dk)
        khat = (key * jnp.exp2(to_sub_end)).astype(BF16)
        qs = []
        for j in range(nsub - 1):
            t0 = sub * (j + 1)
            qs.append(qf[t0:] * jnp.exp2(cum[t0:] - cum[t0 - 1:t0, :]))
        prod = lax.dot_general(jnp.concatenate(qs, axis=0).astype(BF16), khat, NT_DIMS,
                               preferred_element_type=F32)

        ws = []
        for s in range(sub):
            dec = jnp.exp2(jnp.where(t_iota >= s, cum3 - cum3[:, s:s + 1, :], -jnp.inf))
            ws.append((qf3 * key3[:, s:s + 1, :] * dec).reshape(c, dk))
        pair_sums = jnp.dot(jnp.concatenate(ws, axis=0).astype(BF16), ones_rhs,
                            preferred_element_type=F32)
        return o, prod, pair_sums

    def finish(rows, hd, o, prod, pair_sums):
        cols = cols_of(hd)
        a = jnp.zeros((c, c), F32)
        r = 0
        for j in range(nsub - 1):
            t0 = sub * (j + 1)
            pj = jnp.concatenate([jnp.zeros((t0, c), F32), prod[r:r + c - t0, :]], axis=0)
            a = jnp.where(col_sub == j, pj, a)
            r += c - t0
        for s in range(sub):
            a = jnp.where(col_in_sub == s, pair_sums[s * c:(s + 1) * c, :], a)
        o = o + jnp.dot(a.astype(BF16), i_ref[0, rows, cols], preferred_element_type=F32)
        on = o * lax.rsqrt(jnp.mean(o * o, axis=-1, keepdims=True) + RMS_EPS) * nw
        g = g_ref[0, rows, cols]
        o_ref[0, rows, cols] = (on * (g * jax.nn.sigmoid(g))).astype(BF16)

    def body(ci, carry):
        rows = pl.ds(pl.multiple_of(ci * c, c), c)
        heads = range(HGRN_HEADS)
        stage1 = [gates(rows, hd) for hd in heads]
        stage2 = [products(rows, hd, *stage1[hd]) for hd in heads]
        for hd in heads:
            finish(rows, hd, *stage2[hd])
        return carry

    lax.fori_loop(0, n_chunks, body, 0, unroll=8)


def _hgrn(qh, fh, ih, gh, lb_logits, norm_w, batch, seq):
    sb = min(seq, HGRN_SEQ_BLOCK)
    spec = pl.BlockSpec((1, sb, HGRN_WIDTH), lambda b, s: (b, s, 0))
    slots = lb_logits.shape[0]
    shp = (batch, seq, HGRN_WIDTH)
    out = pl.pallas_call(
        functools.partial(_hgrn_kernel, n_chunks=sb // HGRN_CHUNK),
        grid=(batch, seq // sb),
        in_specs=[spec, spec, spec, spec,
                  pl.BlockSpec((slots, HGRN_WIDTH), lambda b, s: (0, 0)),
                  pl.BlockSpec((1, HGRN_HEAD_DIM), lambda b, s: (0, 0))],
        out_specs=spec,
        out_shape=jax.ShapeDtypeStruct(shp, BF16),
        scratch_shapes=[pltpu.VMEM((HGRN_HEADS, HGRN_HEAD_DIM, HGRN_HEAD_DIM), F32)],
        compiler_params=_cparams("arbitrary", "arbitrary"),
        name="hgrn2",
    )(qh.reshape(shp), fh.reshape(shp), ih.reshape(shp), gh.reshape(shp),
      lb_logits.astype(F32), norm_w.reshape(1, HGRN_HEAD_DIM))
    return out.reshape(batch * seq, HGRN_WIDTH)


def _outproj_kernel(attn_ref, hg_ref, x_ref, w_ref, nw_ref, rw_ref, rb_ref,
                    h_ref, xn_ref, idx_ref, gate_ref, rank_ref, cnt_ref, run_ref):
    y = jnp.dot(attn_ref[...], w_ref[0:ATTN_WIDTH, :], preferred_element_type=F32)
    y = y + jnp.dot(hg_ref[...], w_ref[ATTN_WIDTH:, :], preferred_element_type=F32)
    h = x_ref[...] + y
    h_ref[...] = h
    xn = h * lax.rsqrt(jnp.mean(h * h, axis=-1, keepdims=True) + RMS_EPS) * nw_ref[...]
    _store_row_tiles(xn_ref, xn)

    xn_hi = xn.astype(BF16)
    xn_lo = (xn - xn_hi.astype(F32)).astype(BF16)
    rw_hi, rw_lo = rw_ref[0], rw_ref[1]
    nt = lambda a, b: lax.dot_general(a, b, NT_DIMS, preferred_element_type=F32)
    logits = nt(rw_hi, xn_hi) + nt(rw_hi, xn_lo) + nt(rw_lo, xn_hi) + rb_ref[...]
    eid = lax.broadcasted_iota(jnp.int32, logits.shape, 0)
    vals, idxs = [], []
    for _ in range(TOP_K):
        mx = jnp.max(logits, axis=0, keepdims=True)
        ix = jnp.min(jnp.where(logits == mx, eid, N_EXPERTS), axis=0, keepdims=True)
        vals.append(mx)
        idxs.append(ix)
        logits = jnp.where(eid == ix, -jnp.inf, logits)
    es = [jnp.exp(v - vals[0]) for v in vals]
    den = es[0] + es[1] + es[2] + es[3]
    idx_ref[...] = jnp.concatenate(idxs, axis=0)
    gate_ref[...] = jnp.concatenate([e / den for e in es], axis=0)

    @pl.when(pl.program_id(0) == 0)
    def _():
        run_ref[...] = jnp.zeros_like(run_ref)

    span = RANK_SPAN
    onehots = [(eid == ix).astype(F32) for ix in idxs]
    routed = onehots[0] + onehots[1] + onehots[2] + onehots[3]
    earlier = (lax.broadcasted_iota(jnp.int32, (span, span), 0)
               < lax.broadcasted_iota(jnp.int32, (span, span), 1)).astype(BF16)
    run = run_ref[...]
    ranks = []
    for c0 in range(0, logits.shape[1], span):
        part = routed[:, c0:c0 + span]
        before = jnp.dot(part.astype(BF16), earlier, preferred_element_type=F32) + run
        ranks.append(jnp.concatenate([jnp.sum(o[:, c0:c0 + span] * before, axis=0, keepdims=True)
                                      for o in onehots], axis=0))
        run = run + jnp.sum(part, axis=1, keepdims=True)
    rank_ref[...] = jnp.concatenate(ranks, axis=1).astype(jnp.int32)
    run_ref[...] = run
    cnt_ref[...] = run.astype(jnp.int32)


def _outproj(attn, hg, x2, w_out, norm_w, router_w, router_b):
    n = x2.shape[0]
    tm = OUTPROJ_ROWS
    row = lambda width: pl.BlockSpec((tm, width), lambda i: (i, 0))
    full = lambda a, b: pl.BlockSpec((a, b), lambda i: (0, 0))
    tok = pl.BlockSpec((TOP_K, tm), lambda i: (0, i))
    rw = router_w.T.astype(F32)
    rw_hi = rw.astype(BF16)
    rw_split = jnp.stack([rw_hi, (rw - rw_hi.astype(F32)).astype(BF16)])
    return pl.pallas_call(
        _outproj_kernel,
        grid=(n // tm,),
        in_specs=[row(ATTN_WIDTH), row(HGRN_WIDTH), row(D_MODEL), full(D_MODEL, D_MODEL),
                  full(1, D_MODEL), pl.BlockSpec((2, N_EXPERTS, D_MODEL), lambda i: (0, 0, 0)),
                  full(N_EXPERTS, 1)],
        out_specs=[row(D_MODEL), pl.BlockSpec((tm * ROW_CHUNKS, LANES), lambda i: (i, 0)),
                   tok, tok, tok, full(N_EXPERTS, 1)],
        out_shape=[jax.ShapeDtypeStruct((n, D_MODEL), F32),
                   jax.ShapeDtypeStruct((n * ROW_CHUNKS, LANES), F32),
                   jax.ShapeDtypeStruct((TOP_K, n), jnp.int32),
                   jax.ShapeDtypeStruct((TOP_K, n), F32),
                   jax.ShapeDtypeStruct((TOP_K, n), jnp.int32),
                   jax.ShapeDtypeStruct((N_EXPERTS, 1), jnp.int32)],
        scratch_shapes=[pltpu.VMEM((N_EXPERTS, 1), F32)],
        compiler_params=_cparams("arbitrary"),
        name="outproj_router",
    )(attn, hg, x2, w_out.astype(BF16), norm_w.reshape(1, D_MODEL),
      rw_split, router_b.astype(F32).reshape(N_EXPERTS, 1))


def _ffn_kernel(be_ref, rv_ref, nx_ref, xs_ref, wgu_hbm, bgu_ref, wdn_hbm, bdn_ref, o_ref,
                wgu_f32, wdn_f32, wgu_bf, wdn_bf, sems):
    i = pl.program_id(0)
    rows_valid = rv_ref[i]
    new_expert = (i == 0) | (be_ref[i] != be_ref[jnp.maximum(i - 1, 0)])

    def weight_copies(e):
        return (pltpu.make_async_copy(wgu_hbm.at[e], wgu_f32, sems.at[0]),
                pltpu.make_async_copy(wdn_hbm.at[e], wdn_f32, sems.at[1]))

    @pl.when(i == 0)
    def _():
        for cp in weight_copies(be_ref[0]):
            cp.start()

    @pl.when(new_expert & (rows_valid > 0))
    def _():
        for cp in weight_copies(be_ref[i]):
            cp.wait()
        step = 128
        for r in range(0, D_MODEL, step):
            wgu_bf[r:r + step, :] = wgu_f32[r:r + step, :].astype(BF16)
        for r in range(0, D_EXPERT, step):
            wdn_bf[r:r + step, :] = wdn_f32[r:r + step, :].astype(BF16)

        @pl.when(nx_ref[i] >= 0)
        def _():
            for cp in weight_copies(nx_ref[i]):
                cp.start()

    @pl.when(rows_valid > 0)
    def _():
        live = lax.broadcasted_iota(jnp.int32, (FFN_ROWS, 1), 0) < rows_valid
        x = jnp.where(live, _load_row_tiles(xs_ref, FFN_ROWS), 0.0).astype(BF16)
        out = bdn_ref[0]
        for c0 in range(0, D_EXPERT, FFN_COLS):
            g_cols = slice(c0, c0 + FFN_COLS)
            u_cols = slice(D_EXPERT + c0, D_EXPERT + c0 + FFN_COLS)
            gate = jnp.dot(x, wgu_bf[:, g_cols], preferred_element_type=F32) + bgu_ref[0, :, g_cols]
            up = jnp.dot(x, wgu_bf[:, u_cols], preferred_element_type=F32) + bgu_ref[0, :, u_cols]
            gate = jnp.minimum(gate, SWIGLU_LIMIT)
            up = jnp.clip(up, -SWIGLU_LIMIT, SWIGLU_LIMIT)
            act = gate * jax.nn.sigmoid(SWIGLU_ALPHA * gate) * (up + 1.0)
            out = out + jnp.dot(act.astype(BF16), wdn_bf[g_cols, :], preferred_element_type=F32)
        _store_row_tiles(o_ref, out)

    @pl.when(rows_valid <= 0)
    def _():
        o_ref[...] = jnp.zeros_like(o_ref)


def _expert_ffn(block_e, rows_valid, next_e, xs, w_gu, b_gu, w_dn, b_dn):
    tm = FFN_ROWS
    tiles = pl.BlockSpec((tm * ROW_CHUNKS, LANES), lambda i, be, rv, nx: (i, 0))
    grid_spec = pltpu.PrefetchScalarGridSpec(
        num_scalar_prefetch=3,
        grid=(xs.shape[0] // (tm * ROW_CHUNKS),),
        in_specs=[tiles,
                  pl.BlockSpec(memory_space=pl.ANY),
                  pl.BlockSpec((1, 1, 2 * D_EXPERT), lambda i, be, rv, nx: (be[i], 0, 0)),
                  pl.BlockSpec(memory_space=pl.ANY),
                  pl.BlockSpec((1, 1, D_MODEL), lambda i, be, rv, nx: (be[i], 0, 0))],
        out_specs=tiles,
        scratch_shapes=[pltpu.VMEM((D_MODEL, 2 * D_EXPERT), F32), pltpu.VMEM((D_EXPERT, D_MODEL), F32),
                        pltpu.VMEM((D_MODEL, 2 * D_EXPERT), BF16), pltpu.VMEM((D_EXPERT, D_MODEL), BF16),
                        pltpu.SemaphoreType.DMA((2,))],
    )
    return pl.pallas_call(
        _ffn_kernel,
        grid_spec=grid_spec,
        out_shape=jax.ShapeDtypeStruct(xs.shape, F32),
        compiler_params=_cparams("arbitrary"),
        name="expert_ffn",
    )(block_e, rows_valid, next_e, xs, w_gu, b_gu.reshape(N_EXPERTS, 1, 2 * D_EXPERT),
      w_dn, b_dn.reshape(N_EXPERTS, 1, D_MODEL))


def _pos_kernel(idx_ref, rank_ref, pstart_ref, move_ref, comb_ref):
    eid = lax.broadcasted_iota(jnp.int32, (N_EXPERTS, idx_ref.shape[1]), 0)
    pstart = pstart_ref[...]
    rows = [jnp.sum(jnp.where(eid == idx_ref[k:k + 1, :], pstart, 0), axis=0, keepdims=True)
            for k in range(TOP_K)]
    pos = (jnp.concatenate(rows, axis=0) + rank_ref[...]) * ROW_CHUNKS
    for k in range(TOP_K):
        move_ref[0, :, k * MOVE_TOKENS:(k + 1) * MOVE_TOKENS] = pos[k:k + 1, :]
        for j in range(MOVE_TOKENS // COMBINE_ROWS):
            comb_ref[j, :, k * COMBINE_ROWS:(k + 1) * COMBINE_ROWS] = pos[k:k + 1, j * COMBINE_ROWS:(j + 1) * COMBINE_ROWS]


def _positions(idx_t, rank_t, pstart):
    n = idx_t.shape[1]
    tl = MOVE_TOKENS
    per = tl // COMBINE_ROWS
    tok = pl.BlockSpec((TOP_K, tl), lambda i: (0, i))
    return pl.pallas_call(
        _pos_kernel,
        grid=(n // tl,),
        in_specs=[tok, tok, pl.BlockSpec((N_EXPERTS, 1), lambda i: (0, 0))],
        out_specs=[pl.BlockSpec((1, 1, TOP_K * tl), lambda i: (i, 0, 0)),
                   pl.BlockSpec((per, 1, TOP_K * COMBINE_ROWS), lambda i: (i, 0, 0))],
        out_shape=[jax.ShapeDtypeStruct((n // tl, 1, TOP_K * tl), jnp.int32),
                   jax.ShapeDtypeStruct((n // COMBINE_ROWS, 1, TOP_K * COMBINE_ROWS), jnp.int32)],
        compiler_params=_cparams("arbitrary"),
        name="dispatch_pos",
    )(idx_t, rank_t, pstart.reshape(N_EXPERTS, 1))


def _row_tile(ref, row_offset):
    return ref.at[pl.ds(pl.multiple_of(row_offset, ROW_CHUNKS), ROW_CHUNKS)]


def _dispatch_kernel(pos_ref, x_ref, xs_ref, sem):
    tokens = x_ref.shape[0] // ROW_CHUNKS

    def copy(t, k):
        return pltpu.make_async_copy(_row_tile(x_ref, t * ROW_CHUNKS),
                                     _row_tile(xs_ref, pos_ref[0, 0, k * tokens + t]), sem)

    def issue(t, carry):
        for k in range(TOP_K):
            copy(t, k).start(priority=k % 2)
        return carry

    def drain(t, carry):
        for k in range(TOP_K):
            copy(t, k).wait()
        return carry

    lax.fori_loop(0, tokens, issue, 0, unroll=MOVE_UNROLL)
    lax.fori_loop(0, tokens, drain, 0, unroll=MOVE_UNROLL)


def _dispatch_rows(pos_tiles, xn, out_rows):
    n = xn.shape[0] // ROW_CHUNKS
    tl = MOVE_TOKENS
    return pl.pallas_call(
        _dispatch_kernel,
        grid=(n // tl,),
        in_specs=[pl.BlockSpec((1, 1, tl * TOP_K), lambda i: (i, 0, 0), memory_space=pltpu.SMEM),
                  pl.BlockSpec((tl * ROW_CHUNKS, LANES), lambda i: (i, 0))],
        out_specs=pl.BlockSpec(memory_space=pl.ANY),
        out_shape=jax.ShapeDtypeStruct((out_rows * ROW_CHUNKS, LANES), xn.dtype),
        scratch_shapes=[pltpu.SemaphoreType.DMA(())],
        compiler_params=_cparams("arbitrary"),
        name="dispatch_rows",
    )(pos_tiles, xn)


def _combine_kernel(pos_ref, pos_next_ref, h_ref, g_ref, fw_ref, ys_ref, o_ref, buf, sems):
    i = pl.program_id(0)
    steps = pl.num_programs(0)
    tokens = h_ref.shape[0]
    slot = lax.rem(i, 2)

    def gather(p_ref, s, start):
        def body(t, carry):
            for k in range(TOP_K):
                cp = pltpu.make_async_copy(_row_tile(ys_ref, p_ref[0, 0, k * tokens + t]),
                                           _row_tile(buf.at[s, k], t * ROW_CHUNKS), sems.at[s])
                cp.start(priority=k % 2) if start else cp.wait()
            return carry
        lax.fori_loop(0, tokens, body, 0, unroll=MOVE_UNROLL)

    @pl.when(i == 0)
    def _():
        gather(pos_ref, 0, True)

    @pl.when(i + 1 < steps)
    def _():
        gather(pos_next_ref, 1 - slot, True)

    gather(pos_ref, slot, False)
    g = g_ref[...]
    y = h_ref[...]
    for k in range(TOP_K):
        y = y + _load_row_tiles(buf.at[slot, k], tokens) * g[:, k:k + 1]
    o_ref[...] = y * lax.rsqrt(jnp.mean(y * y, axis=-1, keepdims=True) + RMS_EPS) * fw_ref[...]


def _combine(h, ys, pos_t, gates_nk, final_w):
    n = h.shape[0]
    tm = COMBINE_ROWS
    steps = n // tm
    smem = lambda imap: pl.BlockSpec((1, 1, tm * TOP_K), imap, memory_space=pltpu.SMEM)
    return pl.pallas_call(
        _combine_kernel,
        grid=(steps,),
        in_specs=[smem(lambda i: (i, 0, 0)),
                  smem(lambda i: (jnp.minimum(i + 1, steps - 1), 0, 0)),
                  pl.BlockSpec((tm, D_MODEL), lambda i: (i, 0)),
                  pl.BlockSpec((tm, TOP_K), lambda i: (i, 0)),
                  pl.BlockSpec((1, D_MODEL), lambda i: (0, 0)),
                  pl.BlockSpec(memory_space=pl.ANY)],
        out_specs=pl.BlockSpec((tm, D_MODEL), lambda i: (i, 0)),
        out_shape=jax.ShapeDtypeStruct((n, D_MODEL), F32),
        scratch_shapes=[pltpu.VMEM((2, TOP_K, tm * ROW_CHUNKS, LANES), F32), pltpu.SemaphoreType.DMA((2,))],
        compiler_params=_cparams("arbitrary"),
        name="combine_norm",
    )(pos_t, pos_t, h, gates_nk, final_w.reshape(1, D_MODEL), ys)


def kernel(x, norm_mix_w, w_in, rel_bias, hgrn_lb_logits, hgrn_norm_w, w_out, norm_ffn_w,
           router_w, router_b, w_gate_up, b_gate_up, w_down, b_down, final_norm_w):
    batch, seq, _ = x.shape
    n = batch * seq
    x2 = x.reshape(n, D_MODEL)

    qa, ka, va, qh, fh, ih, gh = _inproj(x2, norm_mix_w[0], w_in[0])
    attn = _dilated_attention(qa, ka, va, _masked_bias(rel_bias), batch, seq)
    hg = _hgrn(qh, fh, ih, gh, hgrn_lb_logits, hgrn_norm_w[0], batch, seq)
    h, xn, idx_t, gate_t, rank_t, counts = _outproj(attn, hg, x2, w_out[0], norm_ffn_w[0],
                                                    router_w[0], router_b[0])

    tm = FFN_ROWS
    n_blocks = -(-(n * TOP_K) // tm) + N_EXPERTS
    counts = counts.reshape(N_EXPERTS)
    padded = (counts + tm - 1) // tm * tm
    pend = jnp.cumsum(padded)
    pstart = pend - padded
    block_row0 = jnp.arange(n_blocks, dtype=jnp.int32) * tm
    in_block = (block_row0[:, None] >= pstart[None, :]) & (block_row0[:, None] < pend[None, :])
    block_e = jnp.minimum(jnp.sum(pend[None, :] <= block_row0[:, None], axis=1), N_EXPERTS - 1).astype(jnp.int32)
    rows_valid = jnp.sum(jnp.where(in_block, jnp.clip(pstart + counts - block_row0[:, None], 0, tm), 0),
                         axis=1).astype(jnp.int32)

    pos_move, pos_comb = _positions(idx_t, rank_t, pstart.astype(jnp.int32))
    xs = _dispatch_rows(pos_move, xn, n_blocks * tm)
    experts = jnp.arange(N_EXPERTS, dtype=jnp.int32)
    later = jnp.where((experts[None, :] > experts[:, None]) & (counts[None, :] > 0), experts[None, :], N_EXPERTS)
    next_of = jnp.min(later, axis=1)
    next_e = jnp.where(next_of < N_EXPERTS, next_of, -1)[block_e].astype(jnp.int32)
    ys = _expert_ffn(block_e, rows_valid, next_e, xs, w_gate_up[0], b_gate_up[0], w_down[0], b_down[0])
    out = _combine(h, ys, pos_comb, gate_t.T, final_norm_w)
    return out.reshape(batch, seq, D_MODEL)
```
